```python
import math
import jax, jax.numpy as jnp
from jax import lax
import numpy as np

D_MODEL = 1024
BATCH = 4
SEQ = 4096
DEPTH = 1

DN_HEADS = 8
DN_HEAD_DIM = 128
DN_CONV = 4
DN_CHUNK = 64
NSA_HEADS = 8
NSA_KV_HEADS = 2
NSA_HEAD_DIM = 128
NSA_CMP_LEN = 32
NSA_CMP_STRIDE = 16
NSA_CMP_HIDDEN = 256
NSA_SEL_LEN = 64
NSA_SEL_TOPK = 16
NSA_SEL_QBLOCK = 64
NSA_WINDOW = 512
NSA_WIN_QBLOCK = 128
ROPE_THETA = 500000.0
ROPE_DIM = NSA_HEAD_DIM // 4
D_FF = 2816
FFN_CONV = 3
LN_EPS = 1e-5
RMS_EPS = 1e-6
DEEPNORM_ALPHA = (2.0 * DEPTH) ** 0.25
DEEPNORM_BETA = (8.0 * DEPTH) ** -0.25

DN_WIDTH = DN_HEADS * DN_HEAD_DIM
NSA_WIDTH = NSA_HEADS * NSA_HEAD_DIM
NSA_KV_WIDTH = NSA_KV_HEADS * NSA_HEAD_DIM
IN_SIZES = (DN_WIDTH, DN_WIDTH, DN_WIDTH, DN_WIDTH, DN_HEADS, DN_HEADS,
            NSA_WIDTH, NSA_KV_WIDTH, NSA_KV_WIDTH, NSA_KV_WIDTH, NSA_KV_WIDTH,
            NSA_KV_WIDTH, NSA_KV_WIDTH, 3 * NSA_HEADS, D_MODEL, D_MODEL)
D_IN = sum(IN_SIZES)

kernel_name = "hybrid_gdn_nsa_convglu_deepnorm"


def layer_norm(x, g, b):
    xf = x.astype(jnp.float32)
    mu = jnp.mean(xf, -1, keepdims=True)
    var = jnp.mean(jnp.square(xf - mu), -1, keepdims=True)
    return ((xf - mu) * lax.rsqrt(var + LN_EPS) * g + b).astype(x.dtype)


def l2norm(t):
    t = t.astype(jnp.float32)
    return t * lax.rsqrt(jnp.sum(t * t, -1, keepdims=True) + RMS_EPS)


def causal_dwconv(x, w):
    K = w.shape[0]
    S = x.shape[1]
    xp = jnp.pad(x, ((0, 0), (K - 1, 0), (0, 0)))
    return sum(xp[:, i:i + S] * w[i] for i in range(K))


def partial_rope(t, pos):
    half = ROPE_DIM // 2
    inv_freq = ROPE_THETA ** (-jnp.arange(half, dtype=jnp.float32) / half)
    ang = pos.astype(jnp.float32)[:, None] * inv_freq
    cos = jnp.cos(ang)[:, None, :].astype(t.dtype)
    sin = jnp.sin(ang)[:, None, :].astype(t.dtype)
    t1, t2, rest = t[..., :half], t[..., half:ROPE_DIM], t[..., ROPE_DIM:]
    return jnp.concatenate([t1 * cos - t2 * sin, t2 * cos + t1 * sin, rest], -1)


def masked_softmax(s, mask):
    s = jnp.where(mask, s.astype(jnp.float32), -jnp.inf)
    m = jnp.max(s, -1, keepdims=True)
    m = jnp.where(jnp.isfinite(m), m, 0.0)
    p = jnp.exp(s - m)
    return p / jnp.maximum(jnp.sum(p, -1, keepdims=True), jnp.finfo(jnp.float32).tiny)


def gated_delta_rule(q, k, v, g, beta):
    B, S, H, dk = q.shape
    dv = v.shape[-1]
    C = DN_CHUNK
    N = S // C

    def to_chunks(t):
        t = t.reshape((B, N, C, H) + t.shape[3:])
        return jnp.moveaxis(t, (1, 3), (0, 2))

    q, k, v, g, beta = (to_chunks(t) for t in (q, k, v, g, beta))
    g = jnp.cumsum(g, axis=-1)
    incl = jnp.tril(jnp.ones((C, C), bool))
    strict = jnp.tril(jnp.ones((C, C), bool), -1)
    decay = jnp.exp(jnp.where(incl, g[..., :, None] - g[..., None, :], -jnp.inf))
    k_beta = k * beta[..., None]
    v_beta = v * beta[..., None]
    lower = jnp.where(strict, jnp.einsum('nbhid,nbhjd->nbhij', k_beta, k) * decay, 0.0)
    a_mat = lower + jnp.eye(C, dtype=jnp.float32)
    rhs = jnp.concatenate([v_beta, k_beta * jnp.exp(g)[..., None]], -1)
    sol = lax.linalg.triangular_solve(a_mat, rhs, left_side=True, lower=True, unit_diagonal=True)
    u, w = sol[..., :dv], sol[..., dv:]
    qk = jnp.einsum('nbhid,nbhjd->nbhij', q, k) * decay
    q_dec = q * jnp.exp(g)[..., None]
    k_dec = k * jnp.exp(g[..., -1:] - g)[..., None]
    g_last = jnp.exp(g[..., -1])

    def step(state, inp):
        qd, kd, u_c, w_c, qk_c, gl = inp
        v_new = u_c - jnp.einsum('bhck,bhkv->bhcv', w_c, state)
        out = (jnp.einsum('bhck,bhkv->bhcv', qd, state)
               + jnp.einsum('bhij,bhjv->bhiv', qk_c, v_new))
        state = state * gl[..., None, None] + jnp.einsum('bhck,bhcv->bhkv', kd, v_new)
        return state, out

    state0 = jnp.zeros((B, H, dk, dv), jnp.float32)
    _, o = lax.scan(step, state0, (q_dec, k_dec, u, w, qk, g_last))
    return jnp.moveaxis(o, (0, 2), (1, 3)).reshape(B, S, H, dv)


def nsa_attention(q, k_cmp, v_cmp, k_sel, v_sel, k_win, v_win, gates,
                  cmp_pos_k, cmp_pos_v, cmp_k_w1, cmp_k_w2, cmp_v_w1, cmp_v_w2):
    B, S, H, dh = q.shape
    G = k_cmp.shape[2]
    hpg = H // G
    scale = dh ** -0.5
    pos = jnp.arange(S)
    q = partial_rope(q, pos)
    k_sel = partial_rope(k_sel, pos)
    k_win = partial_rope(k_win, pos)
    qg = q.reshape(B, S, G, hpg, dh).transpose(0, 2, 3, 1, 4)

    def heads_first(t):
        return t.transpose(0, 2, 1, 3)

    n_cmp = (S - NSA_CMP_LEN) // NSA_CMP_STRIDE + 1
    tok = np.arange(n_cmp)[:, None] * NSA_CMP_STRIDE + np.arange(NSA_CMP_LEN)[None]
    cmp_end = tok[:, -1]

    def compress(t, pe, w1, w2):
        blk = t[:, tok] + pe[:, None, :]
        blk = blk.transpose(0, 1, 3, 2, 4).reshape(B, n_cmp, G, NSA_CMP_LEN * dh)
        return jax.nn.silu(blk @ w1) @ w2

    kc = partial_rope(compress(k_cmp, cmp_pos_k, cmp_k_w1, cmp_k_w2), jnp.asarray(cmp_end))
    vc = compress(v_cmp, cmp_pos_v, cmp_v_w1, cmp_v_w2)
    s_cmp = jnp.einsum('bghtd,bngd->bghtn', qg, kc) * scale
    cmp_mask = cmp_end[None, :] <= np.arange(S)[:, None]
    p_cmp = masked_softmax(s_cmp, cmp_mask)
    o_cmp = jnp.einsum('bghtn,bngd->bghtd', p_cmp.astype(vc.dtype), vc)

    n_sel = S // NSA_SEL_LEN
    n_top = min(NSA_SEL_TOPK, n_sel)
    sel_start = np.arange(n_sel) * NSA_SEL_LEN
    overlap = ((tok[:, 0][:, None] <= (sel_start + NSA_SEL_LEN - 1)[None])
               & (cmp_end[:, None] >= sel_start[None])).astype(np.float32)
    p_sel = jnp.einsum('bghtn,nj->bgtj', p_cmp, overlap)
    cur = np.arange(S) // NSA_SEL_LEN
    jb = np.arange(n_sel)
    valid = jb[None] <= cur[:, None]
    forced = valid & ((jb[None] == 0) | (jb[None] == cur[:, None]) | (jb[None] == cur[:, None] - 1))
    score = jnp.where(forced, jnp.inf, jnp.where(valid, p_sel, -jnp.inf))
    top_score, top_idx = lax.top_k(score, n_top)
    top_ok = top_score > -jnp.inf

    QB = NSA_SEL_QBLOCK
    nq = S // QB
    kb = heads_first(k_sel).reshape(B, G, n_sel, NSA_SEL_LEN, dh)
    vb = heads_first(v_sel).reshape(B, G, n_sel, NSA_SEL_LEN, dh)
    q_blocks = qg.reshape(B, G, hpg, nq, QB, dh).transpose(3, 0, 1, 2, 4, 5)
    idx_blocks = top_idx.reshape(B, G, nq, QB, n_top).transpose(2, 0, 1, 3, 4)
    ok_blocks = top_ok.reshape(B, G, nq, QB, n_top).transpose(2, 0, 1, 3, 4)
    q_start = jnp.arange(nq) * QB
    bi = jnp.arange(B)[:, None, None, None]
    gi = jnp.arange(G)[None, :, None, None]
    offs = jnp.arange(NSA_SEL_LEN)
    n_keys = n_top * NSA_SEL_LEN

    def sel_block(args):
        qb_, ib, okb, t0 = args
        kg = kb[bi, gi, ib].reshape(B, G, QB, n_keys, dh)
        vg = vb[bi, gi, ib].reshape(B, G, QB, n_keys, dh)
        kpos = ib[..., None] * NSA_SEL_LEN + offs
        qpos = t0 + jnp.arange(QB)
        mask = (okb[..., None] & (kpos <= qpos[:, None, None])).reshape(B, G, 1, QB, n_keys)
        s = jnp.einsum('bghqd,bgqkd->bghqk', qb_, kg) * scale
        p = masked_softmax(s, mask)
        return jnp.einsum('bghqk,bgqkd->bghqd', p.astype(vg.dtype), vg)

    o_sel = lax.map(sel_block, (q_blocks, idx_blocks, ok_blocks, q_start))
    o_sel = o_sel.transpose(1, 2, 3, 0, 4, 5).reshape(B, G, hpg, S, dh)

    W = NSA_WINDOW
    WB = NSA_WIN_QBLOCK
    nb = S // WB
    nw = W // WB

    def band(t):
        tp = jnp.pad(heads_first(t), ((0, 0), (0, 0), (W, 0), (0, 0))).reshape(B, G, nb + nw, WB, dh)
        return jnp.concatenate([tp[:, :, i:i + nb] for i in range(nw + 1)], axis=3)

    kw_b = band(k_win)
    vw_b = band(v_win)
    qw = qg.reshape(B, G, hpg, nb, WB, dh)
    qpos = np.arange(nb)[:, None] * WB + np.arange(WB)[None]
    kpos = np.arange(nb)[:, None] * WB - W + np.arange((nw + 1) * WB)[None]
    rel = qpos[:, :, None] - kpos[:, None, :]
    win_mask = (rel >= 0) & (rel < W) & (kpos[:, None, :] >= 0)
    s_win = jnp.einsum('bghnqd,bgnkd->bghnqk', qw, kw_b) * scale
    p_win = masked_softmax(s_win, win_mask)
    o_win = jnp.einsum('bghnqk,bgnkd->bghnqd', p_win.astype(vw_b.dtype), vw_b).reshape(B, G, hpg, S, dh)

    gt = jax.nn.sigmoid(gates).reshape(B, S, 3, G, hpg).transpose(2, 0, 3, 4, 1)[..., None]
    o = gt[0] * o_cmp + gt[1] * o_sel + gt[2] * o_win
    return o.transpose(0, 3, 1, 2, 4).reshape(B, S, H * dh)


def setup_inputs(seed: int = 0) -> dict:
    key = jax.random.key(seed)
    ks = jax.random.split(key, 24)
    f32 = jnp.float32
    L = DEPTH

    def nrm(k, shape, scale):
        return jax.random.normal(k, shape, f32) * scale

    cmp_in = NSA_CMP_LEN * NSA_HEAD_DIM
    dt = jnp.exp(jax.random.uniform(ks[4], (L, DN_HEADS), f32, math.log(1e-3), math.log(1e-1)))
    return {
        "x": nrm(ks[0], (BATCH, SEQ, D_MODEL), 1.0),
        "w_in": nrm(ks[1], (L, D_MODEL, D_IN), D_MODEL ** -0.5),
        "dn_conv_w": nrm(ks[2], (L, DN_CONV, 3 * DN_WIDTH), DN_CONV ** -0.5),
        "dn_a_log": jnp.log(jax.random.uniform(ks[3], (L, DN_HEADS), f32, 1.0, 16.0)),
        "dn_dt_bias": dt + jnp.log(-jnp.expm1(-dt)),
        "dn_norm_w": 1.0 + nrm(ks[5], (L, DN_HEAD_DIM), 0.02),
        "cmp_pos_k": nrm(ks[6], (L, NSA_CMP_LEN, NSA_HEAD_DIM), 0.1),
        "cmp_pos_v": nrm(ks[7], (L, NSA_CMP_LEN, NSA_HEAD_DIM), 0.1),
        "cmp_k_w1": nrm(ks[8], (L, cmp_in, NSA_CMP_HIDDEN), cmp_in ** -0.5),
        "cmp_k_w2": nrm(ks[9], (L, NSA_CMP_HIDDEN, NSA_HEAD_DIM), NSA_CMP_HIDDEN ** -0.5),
        "cmp_v_w1": nrm(ks[10], (L, cmp_in, NSA_CMP_HIDDEN), cmp_in ** -0.5),
        "cmp_v_w2": nrm(ks[11], (L, NSA_CMP_HIDDEN, NSA_HEAD_DIM), NSA_CMP_HIDDEN ** -0.5),
        "w_branch_dn": nrm(ks[12], (L, DN_WIDTH, D_MODEL), DN_WIDTH ** -0.5),
        "w_branch_nsa": nrm(ks[13], (L, NSA_WIDTH, D_MODEL), NSA_WIDTH ** -0.5),
        "w_out": nrm(ks[14], (L, D_MODEL, D_MODEL), D_MODEL ** -0.5 * DEEPNORM_BETA),
        "ln1_g": 1.0 + nrm(ks[15], (L, D_MODEL), 0.02),
        "ln1_b": nrm(ks[16], (L, D_MODEL), 0.02),
        "ffn_w_up": nrm(ks[17], (L, D_MODEL, 2 * D_FF), D_MODEL ** -0.5),
        "ffn_conv_w": nrm(ks[18], (L, FFN_CONV, 2 * D_FF), FFN_CONV ** -0.5),
        "ffn_conv_b": nrm(ks[19], (L, 2 * D_FF), 0.02),
        "ffn_w_down": nrm(ks[20], (L, D_FF, D_MODEL), D_FF ** -0.5 * DEEPNORM_BETA),
        "ln2_g": 1.0 + nrm(ks[21], (L, D_MODEL), 0.02),
        "ln2_b": nrm(ks[22], (L, D_MODEL), 0.02),
    }


def reference(x, w_in, dn_conv_w, dn_a_log, dn_dt_bias, dn_norm_w,
              cmp_pos_k, cmp_pos_v, cmp_k_w1, cmp_k_w2, cmp_v_w1, cmp_v_w2,
              w_branch_dn, w_branch_nsa, w_out, ln1_g, ln1_b,
              ffn_w_up, ffn_conv_w, ffn_conv_b, ffn_w_down, ln2_g, ln2_b):
    B, S, _ = x.shape
    f32 = jnp.float32
    split_at = np.cumsum(IN_SIZES)[:-1].tolist()
    for layer in range(DEPTH):
        h = x @ w_in[layer]
        (dq, dk, dv, dz, db, da, nq, kc, vc, ksl, vsl, kwn, vwn, ngate, mg_dn, mg_nsa) = \
            jnp.split(h, split_at, axis=-1)

        qkv = jax.nn.silu(causal_dwconv(jnp.concatenate([dq, dk, dv], -1), dn_conv_w[layer]))
        q, k, v = jnp.split(qkv, 3, axis=-1)

        def dn_heads(t):
            return t.reshape(B, S, DN_HEADS, DN_HEAD_DIM)

        q = l2norm(dn_heads(q)) * DN_HEAD_DIM ** -0.5
        k = l2norm(dn_heads(k))
        v = dn_heads(v).astype(f32)
        beta = jax.nn.sigmoid(db.astype(f32))
        g = -jnp.exp(dn_a_log[layer].astype(f32)) * jax.nn.softplus(
            da.astype(f32) + dn_dt_bias[layer].astype(f32))
        o = gated_delta_rule(q, k, v, g, beta)
        o = o * lax.rsqrt(jnp.mean(o * o, -1, keepdims=True) + RMS_EPS)
        o = o * dn_norm_w[layer] * jax.nn.silu(dn_heads(dz).astype(f32))
        y_dn = o.reshape(B, S, DN_WIDTH).astype(x.dtype)

        def kv_heads(t):
            return t.reshape(B, S, NSA_KV_HEADS, NSA_HEAD_DIM)

        y_nsa = nsa_attention(nq.reshape(B, S, NSA_HEADS, NSA_HEAD_DIM),
                              kv_heads(kc), kv_heads(vc), kv_heads(ksl), kv_heads(vsl),
                              kv_heads(kwn), kv_heads(vwn), ngate,
                              cmp_pos_k[layer], cmp_pos_v[layer], cmp_k_w1[layer], cmp_k_w2[layer],
                              cmp_v_w1[layer], cmp_v_w2[layer]).astype(x.dtype)

        y = (jax.nn.sigmoid(mg_dn) * (y_dn @ w_branch_dn[layer])
             + jax.nn.sigmoid(mg_nsa) * (y_nsa @ w_branch_nsa[layer]))
        x = layer_norm(DEEPNORM_ALPHA * x + y @ w_out[layer], ln1_g[layer], ln1_b[layer])

        u = causal_dwconv(x @ ffn_w_up[layer], ffn_conv_w[layer]) + ffn_conv_b[layer]
        u_gate, u_val = jnp.split(u, 2, axis=-1)
        f = (jax.nn.silu(u_gate) * u_val) @ ffn_w_down[layer]
        x = layer_norm(DEEPNORM_ALPHA * x + f, ln2_g[layer], ln2_b[layer])
    return x
```

```python
import functools
import math

import numpy as np
import jax
import jax.numpy as jnp
from jax import lax
from jax.experimental import pallas as pl
from jax.experimental.pallas import tpu as pltpu

F32 = jnp.float32
BF16 = jnp.bfloat16
HI = lax.Precision.HIGHEST

HEADS = 8
HEAD_DIM = 128
KV_GROUPS = 2
HPG = HEADS // KV_GROUPS
DN_CONV = 4
DN_CHUNK = 64
CMP_LEN = 32
CMP_STRIDE = 16
CMP_HIDDEN = 256
SEL_LEN = 64
SEL_TOPK = 16
WINDOW = 512
ROPE_THETA = 500000.0
ROPE_DIM = HEAD_DIM // 4
ROPE_HALF = ROPE_DIM // 2
D_FF = 2816
FFN_CONV = 3
LN_EPS = 1e-5
RMS_EPS = 1e-6
DEEPNORM_ALPHA = 2.0 ** 0.25
ATT_SCALE = HEAD_DIM ** -0.5

LANE = 128
SUBLANE = 8
VMEM_LIMIT = 56 * 1024 * 1024

WIDTH = HEADS * HEAD_DIM
KVW = KV_GROUPS * HEAD_DIM
COL_DQKV = 0
COL_DZ = 3 * WIDTH
COL_NQ = 4 * WIDTH
COL_MG_DN = 5 * WIDTH
COL_MG_NSA = 6 * WIDTH
COL_KC = 7 * WIDTH
COL_VC = COL_KC + KVW
COL_KSL = COL_VC + KVW
COL_VSL = COL_KSL + KVW
COL_KWN = COL_VSL + KVW
COL_VWN = COL_KWN + KVW
COL_SMALL = COL_VWN + KVW
NH = COL_SMALL + LANE
SMALL_BETA = 0
SMALL_DECAY = HEADS
SMALL_GATE = 2 * HEADS

NEG_BIAS = -(2.0 ** 100)


def _cparams(sem):
    return pltpu.CompilerParams(dimension_semantics=sem, vmem_limit_bytes=VMEM_LIMIT)


def _sigmoid(x):
    return 1.0 / (1.0 + jnp.exp(-x))


def _silu(x):
    return x * _sigmoid(x)


def _dot(a, b, precision=None):
    return jnp.dot(a, b, preferred_element_type=F32, precision=precision)


def _dot_nt(a, b):
    return lax.dot_general(a, b, (((1,), (1,)), ((), ())), preferred_element_type=F32)


def _dot_tn(a, b):
    return lax.dot_general(a, b, (((0,), (0,)), ((), ())), preferred_element_type=F32)


def _layer_norm(z, g, b):
    mu = jnp.mean(z, -1, keepdims=True)
    zc = z - mu
    var = jnp.mean(zc * zc, -1, keepdims=True)
    return zc * lax.rsqrt(var + LN_EPS) * g + b


def _inproj_kernel(x_ref, w_ref, o_ref):
    o_ref[...] = _dot(x_ref[...].astype(BF16), w_ref[...])


def _inproj(x2, w):
    T, D = x2.shape
    n_out = w.shape[1]
    tm = 512
    tn = n_out // 3
    return pl.pallas_call(
        _inproj_kernel,
        grid=(n_out // tn, T // tm),
        in_specs=[pl.BlockSpec((tm, D), lambda j, i: (i, 0)),
                  pl.BlockSpec((D, tn), lambda j, i: (0, j))],
        out_specs=pl.BlockSpec((tm, tn), lambda j, i: (i, j)),
        out_shape=jax.ShapeDtypeStruct((T, n_out), F32),
        compiler_params=_cparams(("parallel", "parallel")),
        name="inproj",
    )(x2, w)


def _gdn_prep_kernel(cur_ref, prev_ref, small_ref, cw_ref, alog_ref, dtb_ref,
                     q_ref, k_ref, v_ref, bg_ref, buf_ref):
    i = pl.program_id(1)
    ts = cur_ref.shape[0]
    notfirst = (i > 0).astype(F32)
    buf_ref[0:SUBLANE, :] = prev_ref[...] * notfirst
    buf_ref[SUBLANE:SUBLANE + ts, :] = cur_ref[...]
    for c in range(3 * HEADS):
        sl = slice(c * LANE, (c + 1) * LANE)
        acc = cw_ref[DN_CONV - 1:DN_CONV, sl] * cur_ref[:, sl]
        for s in range(1, DN_CONV):
            acc = acc + cw_ref[DN_CONV - 1 - s:DN_CONV - s, sl] * buf_ref[pl.ds(SUBLANE - s, ts), sl]
        y = _silu(acc)
        if c < 2 * HEADS:
            y = y * lax.rsqrt(jnp.sum(y * y, -1, keepdims=True) + RMS_EPS)
        if c < HEADS:
            q_ref[:, sl] = y * ATT_SCALE
        elif c < 2 * HEADS:
            k_ref[:, slice((c - HEADS) * LANE, (c - HEADS + 1) * LANE)] = y
        else:
            v_ref[:, slice((c - 2 * HEADS) * LANE, (c - 2 * HEADS + 1) * LANE)] = y
    small = small_ref[...]
    beta = _sigmoid(small)
    xs = small + dtb_ref[...]
    softplus = jnp.maximum(xs, 0.0) + jnp.log1p(jnp.exp(-jnp.abs(xs)))
    g = -jnp.exp(alog_ref[...]) * softplus
    lane = lax.broadcasted_iota(jnp.int32, small.shape, 1)
    bg_ref[...] = jnp.where(lane < SMALL_DECAY, beta, g)


def _gdn_prep(h3, conv_w, alog_row, dtb_row):
    B, S, _ = h3.shape
    ts = 256
    wq = 3 * WIDTH
    out_sd = jax.ShapeDtypeStruct((B, S, WIDTH), F32)
    return pl.pallas_call(
        _gdn_prep_kernel,
        grid=(B, S // ts),
        in_specs=[
            pl.BlockSpec((None, ts, wq), lambda b, i: (b, i, 0)),
            pl.BlockSpec((None, SUBLANE, wq), lambda b, i: (b, jnp.maximum(i * (ts // SUBLANE) - 1, 0), 0)),
            pl.BlockSpec((None, ts, LANE), lambda b, i: (b, i, COL_SMALL // LANE)),
            pl.BlockSpec((DN_CONV, wq), lambda b, i: (0, 0)),
            pl.BlockSpec((1, LANE), lambda b, i: (0, 0)),
            pl.BlockSpec((1, LANE), lambda b, i: (0, 0)),
        ],
        out_specs=[
            pl.BlockSpec((None, ts, WIDTH), lambda b, i: (b, i, 0)),
            pl.BlockSpec((None, ts, WIDTH), lambda b, i: (b, i, 0)),
            pl.BlockSpec((None, ts, WIDTH), lambda b, i: (b, i, 0)),
            pl.BlockSpec((None, ts, LANE), lambda b, i: (b, i, 0)),
        ],
        out_shape=[out_sd, out_sd, out_sd, jax.ShapeDtypeStruct((B, S, LANE), F32)],
        scratch_shapes=[pltpu.VMEM((ts + SUBLANE, wq), F32)],
        compiler_params=_cparams(("parallel", "parallel")),
        name="gdn_prep",
    )(h3, h3, h3, conv_w, alog_row, dtb_row)


GDN_CHUNKS_PER_STEP = 2


def _gdn_kernel(q_ref, k_ref, v_ref, bg_ref, z_ref, nw_ref, y_ref, state_ref):
    n = pl.program_id(1)

    @pl.when(n == 0)
    def _():
        state_ref[...] = jnp.zeros(state_ref.shape, F32)

    C = DN_CHUNK
    ri = lax.broadcasted_iota(jnp.int32, (C, C), 0)
    ci = lax.broadcasted_iota(jnp.int32, (C, C), 1)
    incl = ri >= ci
    strict = ri > ci
    tril = jnp.where(incl, 1.0, 0.0).astype(F32)
    triu = jnp.where(ri <= ci, 1.0, 0.0).astype(F32)
    eye = jnp.where(ri == ci, 1.0, 0.0).astype(F32)
    bg = bg_ref[...]
    bg_t = bg.T
    nw = nw_ref[...]
    for c in range(GDN_CHUNKS_PER_STEP):
        rs = slice(c * C, (c + 1) * C)
        gc_cols = _dot(tril, bg[rs, :], HI)
        gc_rows = _dot(bg_t[SMALL_DECAY:SMALL_DECAY + HEADS, rs], triu, HI)
        for h in range(HEADS):
            ls = slice(h * LANE, (h + 1) * LANE)
            qh = q_ref[rs, ls]
            kh = k_ref[rs, ls]
            vh = v_ref[rs, ls]
            beta = bg[rs, SMALL_BETA + h:SMALL_BETA + h + 1]
            gcol = gc_cols[:, SMALL_DECAY + h:SMALL_DECAY + h + 1]
            grow = gc_rows[h:h + 1, :]
            decay = jnp.exp(jnp.where(incl, gcol - grow, -jnp.inf))
            kb = kh * beta
            vb = vh * beta
            khb = kh.astype(BF16)
            lower = jnp.where(strict, _dot_nt(kb.astype(BF16), khb) * decay, 0.0)
            tinv = eye - lower
            ypow = _dot(lower, lower, HI)
            for it in range(5):
                tinv = tinv + _dot(tinv, ypow, HI)
                if it < 4:
                    ypow = _dot(ypow, ypow, HI)
            eg = jnp.exp(gcol)
            u = _dot(tinv, vb, HI)
            w = _dot(tinv, kb * eg, HI)
            qk = _dot_nt(qh.astype(BF16), khb) * decay
            glast = gcol[C - 1:C, :]
            qd = qh * eg
            kd = kh * jnp.exp(glast - gcol)
            state = state_ref[h]
            state_b = state.astype(BF16)
            v_new = u - _dot(w.astype(BF16), state_b)
            o = _dot(qd.astype(BF16), state_b) + _dot(qk.astype(BF16), v_new.astype(BF16))
            state_ref[h] = state * jnp.exp(glast) + _dot_tn(kd.astype(BF16), v_new.astype(BF16))
            o = o * lax.rsqrt(jnp.mean(o * o, -1, keepdims=True) + RMS_EPS)
            y_ref[rs, ls] = (o * nw * _silu(z_ref[rs, ls])).astype(BF16)


def _gdn(q, k, v, bg, h3, norm_w):
    B, S, _ = q.shape
    rows = GDN_CHUNKS_PER_STEP * DN_CHUNK
    tile = pl.BlockSpec((None, rows, WIDTH), lambda b, n: (b, n, 0))
    return pl.pallas_call(
        _gdn_kernel,
        grid=(B, S // rows),
        in_specs=[tile, tile, tile,
                  pl.BlockSpec((None, rows, LANE), lambda b, n: (b, n, 0)),
                  pl.BlockSpec((None, rows, WIDTH), lambda b, n: (b, n, COL_DZ // WIDTH)),
                  pl.BlockSpec((1, LANE), lambda b, n: (0, 0))],
        out_specs=tile,
        out_shape=jax.ShapeDtypeStruct((B, S, WIDTH), BF16),
        scratch_shapes=[pltpu.VMEM((HEADS, HEAD_DIM, HEAD_DIM), F32)],
        compiler_params=_cparams(("parallel", "arbitrary")),
        name="gdn",
    )(q, k, v, bg, h3, norm_w)


def _rope(t, cosf, sinf, lo):
    partner = jnp.where(lo, pltpu.roll(t, LANE - ROPE_HALF, 1), pltpu.roll(t, ROPE_HALF, 1))
    return t * cosf + partner * sinf


def _nsa_prep_kernel(nq_ref, ksl_ref, vsl_ref, kwn_ref, vwn_ref, cos_ref, sin_ref,
                     q_o, ksl_o, vsl_o, kwn_o, vwn_o):
    cosf = cos_ref[...]
    sinf = sin_ref[...]
    lo = lax.broadcasted_iota(jnp.int32, cosf.shape, 1) < ROPE_HALF
    for h in range(HEADS):
        sl = slice(h * LANE, (h + 1) * LANE)
        q_o[:, sl] = (_rope(nq_ref[:, sl], cosf, sinf, lo) * ATT_SCALE).astype(BF16)
    for g in range(KV_GROUPS):
        sl = slice(g * LANE, (g + 1) * LANE)
        ksl_o[:, sl] = _rope(ksl_ref[:, sl], cosf, sinf, lo).astype(BF16)
        kwn_o[:, sl] = _rope(kwn_ref[:, sl], cosf, sinf, lo).astype(BF16)
        vsl_o[:, sl] = vsl_ref[:, sl].astype(BF16)
        vwn_o[:, sl] = vwn_ref[:, sl].astype(BF16)


def _nsa_prep(h3, cos_tab, sin_tab):
    B, S, _ = h3.shape
    ts = 512

    def kv_spec(col):
        return pl.BlockSpec((None, ts, KVW), lambda b, i: (b, i, col // KVW))

    kv_out = pl.BlockSpec((None, ts, KVW), lambda b, i: (b, i, 0))
    kv_sd = jax.ShapeDtypeStruct((B, S, KVW), BF16)
    tab = pl.BlockSpec((ts, LANE), lambda b, i: (i, 0))
    return pl.pallas_call(
        _nsa_prep_kernel,
        grid=(B, S // ts),
        in_specs=[pl.BlockSpec((None, ts, WIDTH), lambda b, i: (b, i, COL_NQ // WIDTH)),
                  kv_spec(COL_KSL), kv_spec(COL_VSL), kv_spec(COL_KWN), kv_spec(COL_VWN), tab, tab],
        out_specs=[pl.BlockSpec((None, ts, WIDTH), lambda b, i: (b, i, 0)), kv_out, kv_out, kv_out, kv_out],
        out_shape=[jax.ShapeDtypeStruct((B, S, WIDTH), BF16), kv_sd, kv_sd, kv_sd, kv_sd],
        compiler_params=_cparams(("parallel", "parallel")),
        name="nsa_prep",
    )(h3, h3, h3, h3, h3, cos_tab, sin_tab)


def _compress_one(t_ref, pe_ref, w1_ref, w2_ref, nc):
    half = CMP_LEN // 2
    y1 = jnp.zeros((nc, CMP_HIDDEN), F32)
    y2 = jnp.zeros((nc, CMP_HIDDEN), F32)
    for l in range(half):
        x = t_ref[pl.ds(l, nc, stride=CMP_STRIDE), :]
        y1 = y1 + _dot((x + pe_ref[l:l + 1, :]).astype(BF16), w1_ref[l * LANE:(l + 1) * LANE, :])
        y2 = y2 + _dot((x + pe_ref[half + l:half + l + 1, :]).astype(BF16),
                       w1_ref[(half + l) * LANE:(half + l + 1) * LANE, :])
    y = y1 + pltpu.roll(y2, nc - 1, 0)
    return _dot(_silu(y).astype(BF16), w2_ref[...])


def _compress_kernel(kc_ref, vc_ref, pek_ref, pev_ref, w1k_ref, w2k_ref, w1v_ref, w2v_ref,
                     cos_ref, sin_ref, kc_o, vc_o):
    nc = kc_o.shape[0]
    kc = _compress_one(kc_ref, pek_ref, w1k_ref, w2k_ref, nc)
    lo = lax.broadcasted_iota(jnp.int32, kc.shape, 1) < ROPE_HALF
    kc_o[...] = _rope(kc, cos_ref[...], sin_ref[...], lo).astype(BF16)
    vc_o[...] = _compress_one(vc_ref, pev_ref, w1v_ref, w2v_ref, nc).astype(BF16)


def _compress(h3, pek, pev, w1k, w2k, w1v, w2v, cos_c, sin_c):
    B, S, _ = h3.shape
    nc = S // CMP_STRIDE
    full = lambda shape: pl.BlockSpec(shape, lambda b, g: tuple(0 for _ in shape))
    out_spec = pl.BlockSpec((None, None, nc, LANE), lambda b, g: (b, g, 0, 0))
    out_sd = jax.ShapeDtypeStruct((B, KV_GROUPS, nc, LANE), BF16)
    cin = CMP_LEN * HEAD_DIM
    return pl.pallas_call(
        _compress_kernel,
        grid=(B, KV_GROUPS),
        in_specs=[pl.BlockSpec((None, S, LANE), lambda b, g: (b, 0, COL_KC // LANE + g)),
                  pl.BlockSpec((None, S, LANE), lambda b, g: (b, 0, COL_VC // LANE + g)),
                  full((CMP_LEN, LANE)), full((CMP_LEN, LANE)),
                  full((cin, CMP_HIDDEN)), full((CMP_HIDDEN, LANE)),
                  full((cin, CMP_HIDDEN)), full((CMP_HIDDEN, LANE)),
                  full((nc, LANE)), full((nc, LANE))],
        out_specs=[out_spec, out_spec],
        out_shape=[out_sd, out_sd],
        compiler_params=_cparams(("parallel", "parallel")),
        name="compress",
    )(h3, h3, pek, pev, w1k, w2k, w1v, w2v, cos_c, sin_c)


def _cmp_attn_kernel(q_ref, kc_ref, vc_ref, ov_ref, o_ref, bias_ref, *, n_sel):
    i = pl.program_id(2)
    tq = q_ref.shape[0]
    nc = kc_ref.shape[0]
    t0 = i * tq
    tpos = t0 + lax.broadcasted_iota(jnp.int32, (tq, 1), 0)
    cmp_end = lax.broadcasted_iota(jnp.int32, (1, nc), 1) * CMP_STRIDE + (CMP_LEN - 1)
    mask = cmp_end <= tpos
    kc = kc_ref[...]
    vc = vc_ref[...]
    psum = jnp.zeros((tq, nc), F32)
    for hh in range(HPG):
        sl = slice(hh * LANE, (hh + 1) * LANE)
        s = jnp.where(mask, _dot_nt(q_ref[:, sl], kc), -jnp.inf)
        m = jnp.max(s, -1, keepdims=True)
        m = jnp.where(m > -jnp.inf, m, 0.0)
        p = jnp.exp(s - m)
        p = p / jnp.maximum(jnp.sum(p, -1, keepdims=True), jnp.finfo(F32).tiny)
        o_ref[:, sl] = _dot(p.astype(BF16), vc)
        psum = psum + p
    p_sel = _dot(psum, ov_ref[...], HI)
    jb = lax.broadcasted_iota(jnp.int32, (tq, LANE), 1)
    cur = tpos >> int(math.log2(SEL_LEN))
    valid = jb <= cur
    forced = valid & ((jb == 0) | (jb == cur) | (jb == cur - 1))
    score = jnp.where(forced, jnp.inf, jnp.where(valid, p_sel, -jnp.inf))
    st = score.T[0:n_sel, :]
    jrow = lax.broadcasted_iota(jnp.int32, (n_sel, tq), 0)
    cnt = jnp.zeros((n_sel, tq), F32)
    for b in range(n_sel):
        rb = st[b:b + 1, :]
        ahead = (rb > st) | ((rb == st) & (jrow > b))
        cnt = cnt + jnp.where(ahead, 1.0, 0.0)
    sel = (cnt < float(SEL_TOPK)) & (st > -jnp.inf)
    bias_t = jnp.where(sel, 0.0, NEG_BIAS)
    if n_sel < LANE:
        bias_t = jnp.concatenate([bias_t, jnp.zeros((LANE - n_sel, tq), F32)], axis=0)
    bias_ref[...] = bias_t.T.astype(BF16)


def _cmp_attn(q_r, kc, vc, overlap):
    B, S, _ = q_r.shape
    nc = kc.shape[2]
    n_sel = S // SEL_LEN
    tq = 256
    gw = HPG * LANE
    return pl.pallas_call(
        functools.partial(_cmp_attn_kernel, n_sel=n_sel),
        grid=(B, KV_GROUPS, S // tq),
        in_specs=[pl.BlockSpec((None, tq, gw), lambda b, g, i: (b, i, g)),
                  pl.BlockSpec((None, None, nc, LANE), lambda b, g, i: (b, g, 0, 0)),
                  pl.BlockSpec((None, None, nc, LANE), lambda b, g, i: (b, g, 0, 0)),
                  pl.BlockSpec((nc, LANE), lambda b, g, i: (0, 0))],
        out_specs=[pl.BlockSpec((None, tq, gw), lambda b, g, i: (b, i, g)),
                   pl.BlockSpec((None, None, tq, LANE), lambda b, g, i: (b, g, i, 0))],
        out_shape=[jax.ShapeDtypeStruct((B, S, WIDTH), F32),
                   jax.ShapeDtypeStruct((B, KV_GROUPS, S, LANE), BF16)],
        compiler_params=_cparams(("parallel", "parallel", "parallel")),
        name="cmp_attn",
    )(q_r, kc, vc, overlap)


def _flash_kernel(*refs, selected, tk):
    if selected:
        q_ref, b_ref, k_ref, v_ref, o_ref, m_ref, l_ref, acc_ref = refs
    else:
        q_ref, k_ref, v_ref, o_ref, m_ref, l_ref, acc_ref = refs
    i = pl.program_id(2)
    tq = q_ref.shape[0]
    rows = HPG * tq
    t0 = i * tq
    parts = []
    for hh in range(HPG):
        qh = q_ref[:, hh * LANE:(hh + 1) * LANE]
        if selected:
            qh = jnp.concatenate([qh, b_ref[...]], axis=1)
        parts.append(qh)
    q_all = jnp.concatenate(parts, axis=0)
    m_ref[...] = jnp.full(m_ref.shape, -jnp.inf, F32)
    l_ref[...] = jnp.zeros(l_ref.shape, F32)
    acc_ref[...] = jnp.zeros(acc_ref.shape, F32)
    qpos = t0 + (lax.broadcasted_iota(jnp.int32, (rows, 1), 0) & (tq - 1))
    hi = (t0 + tq - 1) // tk + 1
    lo = 0 if selected else jnp.maximum(t0 - WINDOW, 0) // tk
    sel_shift = int(math.log2(SEL_LEN))

    def body(j, carry):
        k0 = pl.multiple_of(j * tk, tk)
        kt = k_ref[pl.ds(k0, tk), :]
        if selected:
            blk = (k0 + lax.broadcasted_iota(jnp.int32, (tk, LANE), 0)) >> sel_shift
            onehot = jnp.where(blk == lax.broadcasted_iota(jnp.int32, (tk, LANE), 1), 1.0, 0.0)
            kt = jnp.concatenate([kt, onehot.astype(BF16)], axis=1)
        s = _dot_nt(q_all, kt)
        kpos = k0 + lax.broadcasted_iota(jnp.int32, (1, tk), 1)
        mask = kpos <= qpos
        if not selected:
            mask = mask & (qpos - kpos < WINDOW)
        s = jnp.where(mask, s, -jnp.inf)
        m_old = m_ref[...]
        m_new = jnp.maximum(m_old, jnp.max(s, -1, keepdims=True))
        m_use = jnp.where(m_new > -jnp.inf, m_new, 0.0)
        alpha = jnp.exp(m_old - m_use)
        p = jnp.exp(s - m_use)
        l_ref[...] = alpha * l_ref[...] + jnp.sum(p, -1, keepdims=True)
        acc_ref[...] = alpha * acc_ref[...] + _dot(p.astype(BF16), v_ref[pl.ds(k0, tk), :])
        m_ref[...] = m_new
        return carry

    lax.fori_loop(lo, hi, body, 0)
    out = acc_ref[...] / jnp.maximum(l_ref[...], jnp.finfo(F32).tiny)
    for hh in range(HPG):
        o_ref[:, hh * LANE:(hh + 1) * LANE] = out[hh * tq:(hh + 1) * tq, :]


def _flash(q_r, k_r, v_b, bias, *, tk):
    B, S, _ = q_r.shape
    tq = 128
    gw = HPG * LANE
    selected = bias is not None
    q_spec = pl.BlockSpec((None, tq, gw), lambda b, g, i: (b, i, g))
    kv_spec = pl.BlockSpec((None, S, LANE), lambda b, g, i: (b, 0, g))
    in_specs = [q_spec]
    args = [q_r]
    if selected:
        in_specs.append(pl.BlockSpec((None, None, tq, LANE), lambda b, g, i: (b, g, i, 0)))
        args.append(bias)
    in_specs += [kv_spec, kv_spec]
    args += [k_r, v_b]
    rows = HPG * tq
    return pl.pallas_call(
        functools.partial(_flash_kernel, selected=selected, tk=tk),
        grid=(B, KV_GROUPS, S // tq),
        in_specs=in_specs,
        out_specs=q_spec,
        out_shape=jax.ShapeDtypeStruct((B, S, WIDTH), F32),
        scratch_shapes=[pltpu.VMEM((rows, 1), F32), pltpu.VMEM((rows, 1), F32), pltpu.VMEM((rows, LANE), F32)],
        compiler_params=_cparams(("parallel", "parallel", "parallel")),
        name="flash_sel" if selected else "flash_win",
    )(*args)


def _merge_kernel(ydn_ref, oc_ref, os_ref, ow_ref, small_ref, mgd_ref, mgn_ref, x_ref,
                  wdn_ref, wnsa_ref, wout_ref, g_ref, b_ref, o_ref):
    gates = _sigmoid(small_ref[...])
    cols = []
    for h in range(HEADS):
        sl = slice(h * LANE, (h + 1) * LANE)
        gc = gates[:, SMALL_GATE + h:SMALL_GATE + h + 1]
        gs = gates[:, SMALL_GATE + HEADS + h:SMALL_GATE + HEADS + h + 1]
        gw = gates[:, SMALL_GATE + 2 * HEADS + h:SMALL_GATE + 2 * HEADS + h + 1]
        cols.append((gc * oc_ref[:, sl] + gs * os_ref[:, sl] + gw * ow_ref[:, sl]).astype(BF16))
    y_nsa = jnp.concatenate(cols, axis=1)
    y = (_sigmoid(mgd_ref[...]) * _dot(ydn_ref[...], wdn_ref[...])
         + _sigmoid(mgn_ref[...]) * _dot(y_nsa, wnsa_ref[...]))
    z = DEEPNORM_ALPHA * x_ref[...] + _dot(y.astype(BF16), wout_ref[...])
    o_ref[...] = _layer_norm(z, g_ref[...], b_ref[...])


def _merge(y_dn2, oc2, os2, ow2, h, x2, wdn, wnsa, wout, g, b):
    T, D = x2.shape
    tm = 256
    row = pl.BlockSpec((tm, WIDTH), lambda i: (i, 0))
    full = lambda shape: pl.BlockSpec(shape, lambda i: (0, 0))
    return pl.pallas_call(
        _merge_kernel,
        grid=(T // tm,),
        in_specs=[row, row, row, row,
                  pl.BlockSpec((tm, LANE), lambda i: (i, COL_SMALL // LANE)),
                  pl.BlockSpec((tm, WIDTH), lambda i: (i, COL_MG_DN // WIDTH)),
                  pl.BlockSpec((tm, WIDTH), lambda i: (i, COL_MG_NSA // WIDTH)),
                  pl.BlockSpec((tm, D), lambda i: (i, 0)),
                  full((WIDTH, D)), full((WIDTH, D)), full((D, D)), full((1, D)), full((1, D))],
        out_specs=pl.BlockSpec((tm, D), lambda i: (i, 0)),
        out_shape=jax.ShapeDtypeStruct((T, D), F32),
        compiler_params=_cparams(("parallel",)),
        name="merge",
    )(y_dn2, oc2, os2, ow2, h, h, h, x2, wdn, wnsa, wout, g, b)


def _ffn_up_kernel(x_ref, xp_ref, wg_ref, wv_ref, cwg_ref, cwv_ref, bg_ref, bv_ref, o_ref, *, tiles_per_seq):
    i = pl.program_id(1)
    tm = x_ref.shape[0]
    notfirst = (i % tiles_per_seq != 0).astype(F32)
    xb = x_ref[...].astype(BF16)
    xpb = (xp_ref[...] * notfirst).astype(BF16)
    row = lax.broadcasted_iota(jnp.int32, (tm, 1), 0)

    def conv(w_ref, cw_ref, b_ref):
        u = _dot(xb, w_ref[...])
        p = _dot(xpb, w_ref[...])
        u1 = jnp.where(row == 0, p[SUBLANE - 1:SUBLANE, :], pltpu.roll(u, 1, 0))
        u2 = jnp.where(row == 0, p[SUBLANE - 2:SUBLANE - 1, :],
                       jnp.where(row == 1, p[SUBLANE - 1:SUBLANE, :], pltpu.roll(u, 2, 0)))
        return cw_ref[0:1, :] * u2 + cw_ref[1:2, :] * u1 + cw_ref[2:3, :] * u + b_ref[...]

    o_ref[...] = (_silu(conv(wg_ref, cwg_ref, bg_ref)) * conv(wv_ref, cwv_ref, bv_ref)).astype(BF16)


def _ffn_up(x1, w_up, conv_w, conv_b, seq):
    T, D = x1.shape
    tm = 512
    nj = 2
    tn = D_FF // nj
    return pl.pallas_call(
        functools.partial(_ffn_up_kernel, tiles_per_seq=seq // tm),
        grid=(nj, T // tm),
        in_specs=[pl.BlockSpec((tm, D), lambda j, i: (i, 0)),
                  pl.BlockSpec((SUBLANE, D), lambda j, i: (jnp.maximum(i * (tm // SUBLANE) - 1, 0), 0)),
                  pl.BlockSpec((D, tn), lambda j, i: (0, j)),
                  pl.BlockSpec((D, tn), lambda j, i: (0, j + nj)),
                  pl.BlockSpec((FFN_CONV, tn), lambda j, i: (0, j)),
                  pl.BlockSpec((FFN_CONV, tn), lambda j, i: (0, j + nj)),
                  pl.BlockSpec((1, tn), lambda j, i: (0, j)),
                  pl.BlockSpec((1, tn), lambda j, i: (0, j + nj))],
        out_specs=pl.BlockSpec((tm, tn), lambda j, i: (i, j)),
        out_shape=jax.ShapeDtypeStruct((T, D_FF), BF16),
        compiler_params=_cparams(("parallel", "parallel")),
        name="ffn_up",
    )(x1, x1, w_up, w_up, conv_w, conv_w, conv_b, conv_b)


def _ffn_down_kernel(a_ref, w_ref, x_ref, g_ref, b_ref, o_ref):
    z = DEEPNORM_ALPHA * x_ref[...] + _dot(a_ref[...], w_ref[...])
    o_ref[...] = _layer_norm(z, g_ref[...], b_ref[...])


def _ffn_down(a, w_down, x1, g, b):
    T, D = x1.shape
    tm = 512
    full = lambda shape: pl.BlockSpec(shape, lambda i: (0, 0))
    return pl.pallas_call(
        _ffn_down_kernel,
        grid=(T // tm,),
        in_specs=[pl.BlockSpec((tm, D_FF), lambda i: (i, 0)), full((D_FF, D)),
                  pl.BlockSpec((tm, D), lambda i: (i, 0)), full((1, D)), full((1, D))],
        out_specs=pl.BlockSpec((tm, D), lambda i: (i, 0)),
        out_shape=jax.ShapeDtypeStruct((T, D), F32),
        compiler_params=_cparams(("parallel",)),
        name="ffn_down",
    )(a, w_down, x1, g, b)


def _reorder_w_in(w):
    d = w.shape[0]
    o_db = 4 * WIDTH
    o_nq = o_db + 2 * HEADS
    o_kv = o_nq + WIDTH
    o_gate = o_kv + 6 * KVW
    o_mg = o_gate + 3 * HEADS
    pad = jnp.zeros((d, LANE - 5 * HEADS), w.dtype)
    return jnp.concatenate([
        w[:, 0:o_db], w[:, o_nq:o_kv], w[:, o_mg:o_mg + 2 * WIDTH], w[:, o_kv:o_gate],
        w[:, o_db:o_nq], w[:, o_gate:o_mg], pad], axis=1).astype(BF16)


def _rope_tables(pos):
    inv_freq = ROPE_THETA ** (-jnp.arange(ROPE_HALF, dtype=F32) / ROPE_HALF)
    ang = pos.astype(F32)[:, None] * inv_freq
    cos = jnp.cos(ang)
    sin = jnp.sin(ang)
    p = pos.shape[0]
    cosf = jnp.concatenate([cos, cos, jnp.ones((p, LANE - ROPE_DIM), F32)], axis=1)
    sinf = jnp.concatenate([-sin, sin, jnp.zeros((p, LANE - ROPE_DIM), F32)], axis=1)
    return cosf, sinf


def _overlap_table(nc, n_sel):
    n = np.arange(nc)[:, None]
    j = np.arange(LANE)[None, :]
    ov = (n * CMP_STRIDE <= j * SEL_LEN + SEL_LEN - 1) & (n * CMP_STRIDE + CMP_LEN - 1 >= j * SEL_LEN)
    ov = ov & (j < n_sel) & (n < nc - 1)
    return jnp.asarray(ov.astype(np.float32))


def _lane_row(vals, offset):
    return jnp.zeros((1, LANE), F32).at[0, offset:offset + vals.shape[0]].set(vals.astype(F32))


def kernel(x, w_in, dn_conv_w, dn_a_log, dn_dt_bias, dn_norm_w, cmp_pos_k, cmp_pos_v, cmp_k_w1, cmp_k_w2,
           cmp_v_w1, cmp_v_w2, w_branch_dn, w_branch_nsa, w_out, ln1_g, ln1_b, ffn_w_up, ffn_conv_w,
           ffn_conv_b, ffn_w_down, ln2_g, ln2_b):
    B, S, D = x.shape
    T = B * S
    nc = S // CMP_STRIDE
    n_sel = S // SEL_LEN
    assert D == WIDTH and S % 512 == 0 and n_sel <= LANE and n_sel % SUBLANE == 0
    cos_t, sin_t = _rope_tables(jnp.arange(S))
    cos_c, sin_c = _rope_tables(jnp.arange(nc) * CMP_STRIDE + (CMP_LEN - 1))
    overlap = _overlap_table(nc, n_sel)
    x2 = x.reshape(T, D)
    for layer in range(w_in.shape[0]):
        h = _inproj(x2, _reorder_w_in(w_in[layer]))
        h3 = h.reshape(B, S, NH)

        q, k, v, bg = _gdn_prep(h3, dn_conv_w[layer], _lane_row(dn_a_log[layer], SMALL_DECAY),
                                _lane_row(dn_dt_bias[layer], SMALL_DECAY))
        y_dn = _gdn(q, k, v, bg, h3, dn_norm_w[layer].reshape(1, LANE))

        q_r, ksl_r, vsl_b, kwn_r, vwn_b = _nsa_prep(h3, cos_t, sin_t)
        kc, vc = _compress(h3, cmp_pos_k[layer], cmp_pos_v[layer],
                           cmp_k_w1[layer].astype(BF16), cmp_k_w2[layer].astype(BF16),
                           cmp_v_w1[layer].astype(BF16), cmp_v_w2[layer].astype(BF16), cos_c, sin_c)
        o_cmp, bias = _cmp_attn(q_r, kc, vc, overlap)
        o_sel = _flash(q_r, ksl_r, vsl_b, bias, tk=256)
        o_win = _flash(q_r, kwn_r, vwn_b, None, tk=128)

        x1 = _merge(y_dn.reshape(T, WIDTH), o_cmp.reshape(T, WIDTH), o_sel.reshape(T, WIDTH),
                    o_win.reshape(T, WIDTH), h, x2,
                    w_branch_dn[layer].astype(BF16), w_branch_nsa[layer].astype(BF16),
                    w_out[layer].astype(BF16), ln1_g[layer].reshape(1, D), ln1_b[layer].reshape(1, D))
        a = _ffn_up(x1, ffn_w_up[layer].astype(BF16), ffn_conv_w[layer],
                    ffn_conv_b[layer].reshape(1, 2 * D_FF), S)
        x2 = _ffn_down(a, ffn_w_down[layer].astype(BF16), x1, ln2_g[layer].reshape(1, D), ln2_b[layer].reshape(1, D))
    return x2.reshape(B, S, D)
```

```python
import functools
import math

import numpy as np
import jax
import jax.numpy as jnp
from jax import lax
from jax.experimental import pallas as pl
from jax.experimental.pallas import tpu as pltpu

F32 = jnp.float32
BF16 = jnp.bfloat16
HI = lax.Precision.HIGHEST

HEADS = 8
HEAD_DIM = 128
KV_GROUPS = 2
HPG = HEADS // KV_GROUPS
DN_CONV = 4
DN_CHUNK = 64
CMP_LEN = 32
CMP_STRIDE = 16
CMP_HIDDEN = 256
SEL_LEN = 64
SEL_TOPK = 16
WINDOW = 512
ROPE_THETA = 500000.0
ROPE_DIM = HEAD_DIM // 4
ROPE_HALF = ROPE_DIM // 2
D_FF = 2816
FFN_CONV = 3
LN_EPS = 1e-5
RMS_EPS = 1e-6
DEEPNORM_ALPHA = 2.0 ** 0.25
ATT_SCALE = HEAD_DIM ** -0.5

LANE = 128
SUBLANE = 8
VMEM_LIMIT = 56 * 1024 * 1024

WIDTH = HEADS * HEAD_DIM
KVW = KV_GROUPS * HEAD_DIM
COL_DQKV = 0
COL_DZ = 3 * WIDTH
COL_NQ = 4 * WIDTH
COL_MG_DN = 5 * WIDTH
COL_MG_NSA = 6 * WIDTH
COL_KC = 7 * WIDTH
COL_VC = COL_KC + KVW
COL_KSL = COL_VC + KVW
COL_VSL = COL_KSL + KVW
COL_KWN = COL_VSL + KVW
COL_VWN = COL_KWN + KVW
COL_SMALL = COL_VWN + KVW
NH = COL_SMALL + LANE
SMALL_BETA = 0
SMALL_DECAY = HEADS
SMALL_GATE = 2 * HEADS

NEG_BIAS = -(2.0 ** 100)


def _cparams(sem):
    return pltpu.CompilerParams(dimension_semantics=sem, vmem_limit_bytes=VMEM_LIMIT)


def _sigmoid(x):
    return 1.0 / (1.0 + jnp.exp(-x))


def _silu(x):
    return x * _sigmoid(x)


def _dot(a, b, precision=None):
    return jnp.dot(a, b, preferred_element_type=F32, precision=precision)


def _dot_nt(a, b):
    return lax.dot_general(a, b, (((1,), (1,)), ((), ())), preferred_element_type=F32)


def _dot_tn(a, b):
    return lax.dot_general(a, b, (((0,), (0,)), ((), ())), preferred_element_type=F32)


def _layer_norm(z, g, b):
    mu = jnp.mean(z, -1, keepdims=True)
    zc = z - mu
    var = jnp.mean(zc * zc, -1, keepdims=True)
    return zc * lax.rsqrt(var + LN_EPS) * g + b


def _inproj_kernel(x_ref, w_ref, o_ref):
    o_ref[...] = _dot(x_ref[...].astype(BF16), w_ref[...])


def _inproj(x2, w):
    T, D = x2.shape
    n_out = w.shape[1]
    tm = 512
    tn = n_out // 3
    return pl.pallas_call(
        _inproj_kernel,
        grid=(n_out // tn, T // tm),
        in_specs=[pl.BlockSpec((tm, D), lambda j, i: (i, 0)),
                  pl.BlockSpec((D, tn), lambda j, i: (0, j))],
        out_specs=pl.BlockSpec((tm, tn), lambda j, i: (i, j)),
        out_shape=jax.ShapeDtypeStruct((T, n_out), F32),
        compiler_params=_cparams(("parallel", "parallel")),
        name="inproj",
    )(x2, w)


def _gdn_prep_kernel(cur_ref, prev_ref, small_ref, cw_ref, alog_ref, dtb_ref,
                     q_ref, k_ref, v_ref, bg_ref, buf_ref):
    i = pl.program_id(1)
    ts = cur_ref.shape[0]
    notfirst = (i > 0).astype(F32)
    buf_ref[0:SUBLANE, :] = prev_ref[...] * notfirst
    buf_ref[SUBLANE:SUBLANE + ts, :] = cur_ref[...]
    for c in range(3 * HEADS):
        sl = slice(c * LANE, (c + 1) * LANE)
        acc = cw_ref[DN_CONV - 1:DN_CONV, sl] * cur_ref[:, sl]
        for s in range(1, DN_CONV):
            acc = acc + cw_ref[DN_CONV - 1 - s:DN_CONV - s, sl] * buf_ref[pl.ds(SUBLANE - s, ts), sl]
        y = _silu(acc)
        if c < 2 * HEADS:
            y = y * lax.rsqrt(jnp.sum(y * y, -1, keepdims=True) + RMS_EPS)
        if c < HEADS:
            q_ref[:, sl] = y * ATT_SCALE
        elif c < 2 * HEADS:
            k_ref[:, slice((c - HEADS) * LANE, (c - HEADS + 1) * LANE)] = y
        else:
            v_ref[:, slice((c - 2 * HEADS) * LANE, (c - 2 * HEADS + 1) * LANE)] = y
    small = small_ref[...]
    beta = _sigmoid(small)
    xs = small + dtb_ref[...]
    softplus = jnp.maximum(xs, 0.0) + jnp.log1p(jnp.exp(-jnp.abs(xs)))
    g = -jnp.exp(alog_ref[...]) * softplus
    lane = lax.broadcasted_iota(jnp.int32, small.shape, 1)
    bg_ref[...] = jnp.where(lane < SMALL_DECAY, beta, g)


def _gdn_prep(h3, conv_w, alog_row, dtb_row):
    B, S, _ = h3.shape
    ts = 256
    wq = 3 * WIDTH
    out_sd = jax.ShapeDtypeStruct((B, S, WIDTH), F32)
    return pl.pallas_call(
        _gdn_prep_kernel,
        grid=(B, S // ts),
        in_specs=[
            pl.BlockSpec((None, ts, wq), lambda b, i: (b, i, 0)),
            pl.BlockSpec((None, SUBLANE, wq), lambda b, i: (b, jnp.maximum(i * (ts // SUBLANE) - 1, 0), 0)),
            pl.BlockSpec((None, ts, LANE), lambda b, i: (b, i, COL_SMALL // LANE)),
            pl.BlockSpec((DN_CONV, wq), lambda b, i: (0, 0)),
            pl.BlockSpec((1, LANE), lambda b, i: (0, 0)),
            pl.BlockSpec((1, LANE), lambda b, i: (0, 0)),
        ],
        out_specs=[
            pl.BlockSpec((None, ts, WIDTH), lambda b, i: (b, i, 0)),
            pl.BlockSpec((None, ts, WIDTH), lambda b, i: (b, i, 0)),
            pl.BlockSpec((None, ts, WIDTH), lambda b, i: (b, i, 0)),
            pl.BlockSpec((None, ts, LANE), lambda b, i: (b, i, 0)),
        ],
        out_shape=[out_sd, out_sd, out_sd, jax.ShapeDtypeStruct((B, S, LANE), F32)],
        scratch_shapes=[pltpu.VMEM((ts + SUBLANE, wq), F32)],
        compiler_params=_cparams(("parallel", "parallel")),
        name="gdn_prep",
    )(h3, h3, h3, conv_w, alog_row, dtb_row)


GDN_CHUNKS_PER_STEP = 2


def _split_bf16(a):
    hi = a.astype(BF16)
    return hi, (a - hi.astype(F32)).astype(BF16)


def _dot_split(a, b):
    return _dot(a[0], b[0]) + (_dot(a[0], b[1]) + _dot(a[1], b[0]))


def _gdn_kernel(q_ref, k_ref, v_ref, bg_ref, z_ref, nw_ref, y_ref, state_ref):
    n = pl.program_id(1)

    @pl.when(n == 0)
    def _():
        state_ref[...] = jnp.zeros(state_ref.shape, F32)

    C = DN_CHUNK
    ri = lax.broadcasted_iota(jnp.int32, (C, C), 0)
    ci = lax.broadcasted_iota(jnp.int32, (C, C), 1)
    incl = ri >= ci
    strict = ri > ci
    tril = jnp.where(incl, 1.0, 0.0).astype(F32)
    triu = jnp.where(ri <= ci, 1.0, 0.0).astype(F32)
    eye = jnp.where(ri == ci, 1.0, 0.0).astype(F32)
    bg = bg_ref[...]
    bg_t = bg.T
    nw = nw_ref[...]
    units = [(c, h) for c in range(GDN_CHUNKS_PER_STEP) for h in range(HEADS)]
    gc_cols, gc_rows = [], []
    for c in range(GDN_CHUNKS_PER_STEP):
        rs = slice(c * C, (c + 1) * C)
        gc_cols.append(_dot(tril, bg[rs, :], HI))
        gc_rows.append(_dot(bg_t[SMALL_DECAY:SMALL_DECAY + HEADS, rs], triu, HI))
    gcol, decay, kb_l, vb_l, kh_l, lower = {}, {}, {}, {}, {}, {}
    for (c, h) in units:
        rs = slice(c * C, (c + 1) * C)
        ls = slice(h * LANE, (h + 1) * LANE)
        kh = k_ref[rs, ls]
        beta = bg[rs, SMALL_BETA + h:SMALL_BETA + h + 1]
        gcol[c, h] = gc_cols[c][:, SMALL_DECAY + h:SMALL_DECAY + h + 1]
        decay[c, h] = jnp.exp(jnp.where(incl, gcol[c, h] - gc_rows[c][h:h + 1, :], -jnp.inf))
        kb_l[c, h] = kh * beta
        vb_l[c, h] = v_ref[rs, ls] * beta
        kh_l[c, h] = kh.astype(BF16)
    for u_ in units:
        lower[u_] = jnp.where(strict, _dot_nt(kb_l[u_].astype(BF16), kh_l[u_]) * decay[u_], 0.0)
    tinv = {u_: eye - lower[u_] for u_ in units}
    lsp = {u_: _split_bf16(lower[u_]) for u_ in units}
    ypow = {u_: _dot_split(lsp[u_], lsp[u_]) for u_ in units}
    for it in range(5):
        ysp = {u_: _split_bf16(ypow[u_]) for u_ in units}
        tinv = {u_: tinv[u_] + _dot_split(_split_bf16(tinv[u_]), ysp[u_]) for u_ in units}
        if it < 4:
            ypow = {u_: _dot_split(ysp[u_], ysp[u_]) for u_ in units}
    uw, qk, qd, kd, eglast = {}, {}, {}, {}, {}
    for (c, h) in units:
        rs = slice(c * C, (c + 1) * C)
        ls = slice(h * LANE, (h + 1) * LANE)
        u_ = (c, h)
        eg = jnp.exp(gcol[u_])
        tsp = _split_bf16(tinv[u_])
        rhs = jnp.concatenate([vb_l[u_], kb_l[u_] * eg], axis=1)
        uw[u_] = _dot_split(tsp, _split_bf16(rhs))
        qh = q_ref[rs, ls]
        qk[u_] = (_dot_nt(qh.astype(BF16), kh_l[u_]) * decay[u_]).astype(BF16)
        glast = gcol[u_][C - 1:C, :]
        qd[u_] = (qh * eg).astype(BF16)
        kd[u_] = (k_ref[rs, ls] * jnp.exp(glast - gcol[u_])).astype(BF16)
        eglast[u_] = jnp.exp(glast)
    for c in range(GDN_CHUNKS_PER_STEP):
        rs = slice(c * C, (c + 1) * C)
        for h in range(HEADS):
            ls = slice(h * LANE, (h + 1) * LANE)
            u_ = (c, h)
            state = state_ref[h]
            state_b = state.astype(BF16)
            v_new = uw[u_][:, 0:LANE] - _dot(uw[u_][:, LANE:2 * LANE].astype(BF16), state_b)
            v_new_b = v_new.astype(BF16)
            o = _dot(qd[u_], state_b) + _dot(qk[u_], v_new_b)
            state_ref[h] = state * eglast[u_] + _dot_tn(kd[u_], v_new_b)
            o = o * lax.rsqrt(jnp.mean(o * o, -1, keepdims=True) + RMS_EPS)
            y_ref[rs, ls] = (o * nw * _silu(z_ref[rs, ls])).astype(BF16)


def _gdn(q, k, v, bg, h3, norm_w):
    B, S, _ = q.shape
    rows = GDN_CHUNKS_PER_STEP * DN_CHUNK
    tile = pl.BlockSpec((None, rows, WIDTH), lambda b, n: (b, n, 0))
    return pl.pallas_call(
        _gdn_kernel,
        grid=(B, S // rows),
        in_specs=[tile, tile, tile,
                  pl.BlockSpec((None, rows, LANE), lambda b, n: (b, n, 0)),
                  pl.BlockSpec((None, rows, WIDTH), lambda b, n: (b, n, COL_DZ // WIDTH)),
                  pl.BlockSpec((1, LANE), lambda b, n: (0, 0))],
        out_specs=tile,
        out_shape=jax.ShapeDtypeStruct((B, S, WIDTH), BF16),
        scratch_shapes=[pltpu.VMEM((HEADS, HEAD_DIM, HEAD_DIM), F32)],
        compiler_params=_cparams(("parallel", "arbitrary")),
        name="gdn",
    )(q, k, v, bg, h3, norm_w)


def _rope(t, cosf, sinf, lo):
    partner = jnp.where(lo, pltpu.roll(t, LANE - ROPE_HALF, 1), pltpu.roll(t, ROPE_HALF, 1))
    return t * cosf + partner * sinf


SEL_TK = 256
WIN_TK = 128


def _nsa_prep_kernel(nq_ref, ksl_ref, vsl_ref, kwn_ref, vwn_ref, cos_ref, sin_ref,
                     q_o, qt_o, kaug_o, vslt_o, kwn_o, vwnt_o):
    i = pl.program_id(1)
    ts = nq_ref.shape[0]
    cosf = cos_ref[...]
    sinf = sin_ref[...]
    lane = lax.broadcasted_iota(jnp.int32, cosf.shape, 1)
    lo = lane < ROPE_HALF
    for h in range(HEADS):
        sl = slice(h * LANE, (h + 1) * LANE)
        qh = _rope(nq_ref[:, sl], cosf, sinf, lo) * ATT_SCALE
        q_o[:, sl] = qh.astype(BF16)
        qt_o[sl, :] = qh.T.astype(BF16)
    tpos = i * ts + lax.broadcasted_iota(jnp.int32, cosf.shape, 0)
    onehot = jnp.where((tpos >> int(math.log2(SEL_LEN))) == lane, 1.0, 0.0).astype(BF16)
    for g in range(KV_GROUPS):
        sl = slice(g * LANE, (g + 1) * LANE)
        kaug_o[:, 2 * g * LANE:(2 * g + 1) * LANE] = _rope(ksl_ref[:, sl], cosf, sinf, lo).astype(BF16)
        kaug_o[:, (2 * g + 1) * LANE:(2 * g + 2) * LANE] = onehot
        kwn_o[:, sl] = _rope(kwn_ref[:, sl], cosf, sinf, lo).astype(BF16)
        vs_t = vsl_ref[:, sl].T.astype(BF16)
        for c in range(ts // SEL_TK):
            vslt_o[g, c] = vs_t[:, c * SEL_TK:(c + 1) * SEL_TK]
        vw_t = vwn_ref[:, sl].T.astype(BF16)
        for c in range(ts // WIN_TK):
            vwnt_o[g, c] = vw_t[:, c * WIN_TK:(c + 1) * WIN_TK]


def _nsa_prep(h3, cos_tab, sin_tab):
    B, S, _ = h3.shape
    ts = 512

    def kv_spec(col):
        return pl.BlockSpec((None, ts, KVW), lambda b, i: (b, i, col // KVW))

    tab = pl.BlockSpec((ts, LANE), lambda b, i: (i, 0))
    return pl.pallas_call(
        _nsa_prep_kernel,
        grid=(B, S // ts),
        in_specs=[pl.BlockSpec((None, ts, WIDTH), lambda b, i: (b, i, COL_NQ // WIDTH)),
                  kv_spec(COL_KSL), kv_spec(COL_VSL), kv_spec(COL_KWN), kv_spec(COL_VWN), tab, tab],
        out_specs=[pl.BlockSpec((None, ts, WIDTH), lambda b, i: (b, i, 0)),
                   pl.BlockSpec((None, WIDTH, ts), lambda b, i: (b, 0, i)),
                   pl.BlockSpec((None, ts, 2 * KVW), lambda b, i: (b, i, 0)),
                   pl.BlockSpec((None, KV_GROUPS, ts // SEL_TK, LANE, SEL_TK), lambda b, i: (b, 0, i, 0, 0)),
                   pl.BlockSpec((None, ts, KVW), lambda b, i: (b, i, 0)),
                   pl.BlockSpec((None, KV_GROUPS, ts // WIN_TK, LANE, WIN_TK), lambda b, i: (b, 0, i, 0, 0))],
        out_shape=[jax.ShapeDtypeStruct((B, S, WIDTH), BF16),
                   jax.ShapeDtypeStruct((B, WIDTH, S), BF16),
                   jax.ShapeDtypeStruct((B, S, 2 * KVW), BF16),
                   jax.ShapeDtypeStruct((B, KV_GROUPS, S // SEL_TK, LANE, SEL_TK), BF16),
                   jax.ShapeDtypeStruct((B, S, KVW), BF16),
                   jax.ShapeDtypeStruct((B, KV_GROUPS, S // WIN_TK, LANE, WIN_TK), BF16)],
        compiler_params=_cparams(("parallel", "parallel")),
        name="nsa_prep",
    )(h3, h3, h3, h3, h3, cos_tab, sin_tab)


def _compress_one(t_ref, pe_ref, w1_ref, w2_ref, nc):
    half = CMP_LEN // 2
    y1 = jnp.zeros((nc, CMP_HIDDEN), F32)
    y2 = jnp.zeros((nc, CMP_HIDDEN), F32)
    for l in range(half):
        x = t_ref[pl.ds(l, nc, stride=CMP_STRIDE), :]
        y1 = y1 + _dot((x + pe_ref[l:l + 1, :]).astype(BF16), w1_ref[l * LANE:(l + 1) * LANE, :])
        y2 = y2 + _dot((x + pe_ref[half + l:half + l + 1, :]).astype(BF16),
                       w1_ref[(half + l) * LANE:(half + l + 1) * LANE, :])
    y = y1 + pltpu.roll(y2, nc - 1, 0)
    return _dot(_silu(y).astype(BF16), w2_ref[...])


def _compress_kernel(kc_ref, vc_ref, pek_ref, pev_ref, w1k_ref, w2k_ref, w1v_ref, w2v_ref,
                     cos_ref, sin_ref, kc_o, vc_o):
    nc = kc_o.shape[0]
    kc = _compress_one(kc_ref, pek_ref, w1k_ref, w2k_ref, nc)
    lo = lax.broadcasted_iota(jnp.int32, kc.shape, 1) < ROPE_HALF
    kc_o[...] = _rope(kc, cos_ref[...], sin_ref[...], lo).astype(BF16)
    vc_o[...] = _compress_one(vc_ref, pev_ref, w1v_ref, w2v_ref, nc).astype(BF16)


def _compress(h3, pek, pev, w1k, w2k, w1v, w2v, cos_c, sin_c):
    B, S, _ = h3.shape
    nc = S // CMP_STRIDE
    full = lambda shape: pl.BlockSpec(shape, lambda b, g: tuple(0 for _ in shape))
    out_spec = pl.BlockSpec((None, None, nc, LANE), lambda b, g: (b, g, 0, 0))
    out_sd = jax.ShapeDtypeStruct((B, KV_GROUPS, nc, LANE), BF16)
    cin = CMP_LEN * HEAD_DIM
    return pl.pallas_call(
        _compress_kernel,
        grid=(B, KV_GROUPS),
        in_specs=[pl.BlockSpec((None, S, LANE), lambda b, g: (b, 0, COL_KC // LANE + g)),
                  pl.BlockSpec((None, S, LANE), lambda b, g: (b, 0, COL_VC // LANE + g)),
                  full((CMP_LEN, LANE)), full((CMP_LEN, LANE)),
                  full((cin, CMP_HIDDEN)), full((CMP_HIDDEN, LANE)),
                  full((cin, CMP_HIDDEN)), full((CMP_HIDDEN, LANE)),
                  full((nc, LANE)), full((nc, LANE))],
        out_specs=[out_spec, out_spec],
        out_shape=[out_sd, out_sd],
        compiler_params=_cparams(("parallel", "parallel")),
        name="compress",
    )(h3, h3, pek, pev, w1k, w2k, w1v, w2v, cos_c, sin_c)


def _cmp_attn_kernel(q_ref, kc_ref, vc_ref, ov_ref, o_ref, bias_ref, *, n_sel):
    i = pl.program_id(2)
    tq = q_ref.shape[0]
    nc = kc_ref.shape[0]
    t0 = i * tq
    tpos = t0 + lax.broadcasted_iota(jnp.int32, (tq, 1), 0)
    cmp_end = lax.broadcasted_iota(jnp.int32, (1, nc), 1) * CMP_STRIDE + (CMP_LEN - 1)
    mask = cmp_end <= tpos
    kc = kc_ref[...]
    vc = vc_ref[...]
    psum = jnp.zeros((tq, nc), F32)
    for hh in range(HPG):
        sl = slice(hh * LANE, (hh + 1) * LANE)
        s = jnp.where(mask, _dot_nt(q_ref[:, sl], kc), -jnp.inf)
        m = jnp.max(s, -1, keepdims=True)
        m = jnp.where(m > -jnp.inf, m, 0.0)
        p = jnp.exp(s - m)
        p = p / jnp.maximum(jnp.sum(p, -1, keepdims=True), jnp.finfo(F32).tiny)
        o_ref[:, sl] = _dot(p.astype(BF16), vc)
        psum = psum + p
    p_sel = _dot(psum, ov_ref[...], HI)
    jb = lax.broadcasted_iota(jnp.int32, (tq, LANE), 1)
    cur = tpos >> int(math.log2(SEL_LEN))
    valid = jb <= cur
    forced = valid & ((jb == 0) | (jb == cur) | (jb == cur - 1))
    score = jnp.where(forced, jnp.inf, jnp.where(valid, p_sel, -jnp.inf))
    st = score.T[0:n_sel, :]
    jrow = lax.broadcasted_iota(jnp.int32, (n_sel, tq), 0)
    cnt = jnp.zeros((n_sel, tq), F32)
    for b in range(n_sel):
        rb = st[b:b + 1, :]
        ahead = (rb > st) | ((rb == st) & (jrow > b))
        cnt = cnt + jnp.where(ahead, 1.0, 0.0)
    sel = (cnt < float(SEL_TOPK)) & (st > -jnp.inf)
    bias_t = jnp.where(sel, 0.0, NEG_BIAS)
    if n_sel < LANE:
        bias_t = jnp.concatenate([bias_t, jnp.zeros((LANE - n_sel, tq), F32)], axis=0)
    bias_ref[...] = bias_t.astype(BF16)


def _cmp_attn(q_r, kc, vc, overlap):
    B, S, _ = q_r.shape
    nc = kc.shape[2]
    n_sel = S // SEL_LEN
    tq = 256
    gw = HPG * LANE
    return pl.pallas_call(
        functools.partial(_cmp_attn_kernel, n_sel=n_sel),
        grid=(B, KV_GROUPS, S // tq),
        in_specs=[pl.BlockSpec((None, tq, gw), lambda b, g, i: (b, i, g)),
                  pl.BlockSpec((None, None, nc, LANE), lambda b, g, i: (b, g, 0, 0)),
                  pl.BlockSpec((None, None, nc, LANE), lambda b, g, i: (b, g, 0, 0)),
                  pl.BlockSpec((nc, LANE), lambda b, g, i: (0, 0))],
        out_specs=[pl.BlockSpec((None, tq, gw), lambda b, g, i: (b, i, g)),
                   pl.BlockSpec((None, None, LANE, tq), lambda b, g, i: (b, g, 0, i))],
        out_shape=[jax.ShapeDtypeStruct((B, S, WIDTH), F32),
                   jax.ShapeDtypeStruct((B, KV_GROUPS, LANE, S), BF16)],
        compiler_params=_cparams(("parallel", "parallel", "parallel")),
        name="cmp_attn",
    )(q_r, kc, vc, overlap)


def _flash_kernel(*refs, selected, tk):
    if selected:
        q_ref, b_ref, k_ref, vt_ref, o_ref, m_ref, l_ref, acc_ref = refs
    else:
        q_ref, k_ref, vt_ref, o_ref, m_ref, l_ref, acc_ref = refs
    i = pl.program_id(2)
    tq = q_ref.shape[1]
    cols = HPG * tq
    t0 = i * tq
    parts = []
    for hh in range(HPG):
        qh = q_ref[hh * LANE:(hh + 1) * LANE, :]
        if selected:
            qh = jnp.concatenate([qh, b_ref[...]], axis=0)
        parts.append(qh)
    q_t = jnp.concatenate(parts, axis=1)
    m_ref[...] = jnp.full(m_ref.shape, -jnp.inf, F32)
    l_ref[...] = jnp.zeros(l_ref.shape, F32)
    acc_ref[...] = jnp.zeros(acc_ref.shape, F32)
    qpos = t0 + (lax.broadcasted_iota(jnp.int32, (1, cols), 1) & (tq - 1))

    def step(j, masked):
        k0 = pl.multiple_of(j * tk, tk)
        s = _dot(k_ref[pl.ds(k0, tk), :], q_t)
        if masked:
            kpos = k0 + lax.broadcasted_iota(jnp.int32, (tk, 1), 0)
            mask = kpos <= qpos
            if not selected:
                mask = mask & (qpos - kpos < WINDOW)
            s = jnp.where(mask, s, -jnp.inf)
        m_old = m_ref[...]
        m_new = jnp.maximum(m_old, jnp.max(s, axis=0, keepdims=True))
        alpha = jnp.exp(m_old - m_new)
        p = jnp.exp(s - m_new)
        l_ref[...] = alpha * l_ref[...] + jnp.sum(p, axis=0, keepdims=True)
        acc_ref[...] = alpha * acc_ref[...] + _dot(vt_ref[j], p.astype(BF16))
        m_ref[...] = m_new

    def full_step(j, carry):
        step(j, False)
        return carry

    jd = t0 // tk
    step(jd, True)
    if selected:
        lax.fori_loop(0, jd, full_step, 0)
    else:
        n_full = WINDOW // tk - 1
        lax.fori_loop(jnp.maximum(jd - n_full, 0), jd, full_step, 0)

        @pl.when(jd - n_full - 1 >= 0)
        def _():
            step(jd - n_full - 1, True)

    out_t = acc_ref[...] / jnp.maximum(l_ref[...], jnp.finfo(F32).tiny)
    for hh in range(HPG):
        o_ref[:, hh * LANE:(hh + 1) * LANE] = out_t[:, hh * tq:(hh + 1) * tq].T


def _flash(q_t, k_r, v_t, bias_t, *, tk):
    B, _, S = q_t.shape
    tq = 128
    gw = HPG * LANE
    selected = bias_t is not None
    assert tq == tk or selected
    kw = k_r.shape[2] // KV_GROUPS
    in_specs = [pl.BlockSpec((None, gw, tq), lambda b, g, i: (b, g, i))]
    args = [q_t]
    if selected:
        in_specs.append(pl.BlockSpec((None, None, LANE, tq), lambda b, g, i: (b, g, 0, i)))
        args.append(bias_t)
    in_specs += [pl.BlockSpec((None, S, kw), lambda b, g, i: (b, 0, g)),
                 pl.BlockSpec((None, None, S // tk, LANE, tk), lambda b, g, i: (b, g, 0, 0, 0))]
    args += [k_r, v_t]
    cols = HPG * tq
    return pl.pallas_call(
        functools.partial(_flash_kernel, selected=selected, tk=tk),
        grid=(B, KV_GROUPS, S // tq),
        in_specs=in_specs,
        out_specs=pl.BlockSpec((None, tq, gw), lambda b, g, i: (b, i, g)),
        out_shape=jax.ShapeDtypeStruct((B, S, WIDTH), F32),
        scratch_shapes=[pltpu.VMEM((1, cols), F32), pltpu.VMEM((1, cols), F32), pltpu.VMEM((LANE, cols), F32)],
        compiler_params=_cparams(("parallel", "parallel", "parallel")),
        name="flash_sel" if selected else "flash_win",
    )(*args)


def _merge_kernel(ydn_ref, oc_ref, os_ref, ow_ref, small_ref, mgd_ref, mgn_ref, x_ref,
                  wdn_ref, wnsa_ref, wout_ref, g_ref, b_ref, o_ref):
    gates = _sigmoid(small_ref[...])
    cols = []
    for h in range(HEADS):
        sl = slice(h * LANE, (h + 1) * LANE)
        gc = gates[:, SMALL_GATE + h:SMALL_GATE + h + 1]
        gs = gates[:, SMALL_GATE + HEADS + h:SMALL_GATE + HEADS + h + 1]
        gw = gates[:, SMALL_GATE + 2 * HEADS + h:SMALL_GATE + 2 * HEADS + h + 1]
        cols.append((gc * oc_ref[:, sl] + gs * os_ref[:, sl] + gw * ow_ref[:, sl]).astype(BF16))
    y_nsa = jnp.concatenate(cols, axis=1)
    y = (_sigmoid(mgd_ref[...]) * _dot(ydn_ref[...], wdn_ref[...])
         + _sigmoid(mgn_ref[...]) * _dot(y_nsa, wnsa_ref[...]))
    z = DEEPNORM_ALPHA * x_ref[...] + _dot(y.astype(BF16), wout_ref[...])
    o_ref[...] = _layer_norm(z, g_ref[...], b_ref[...])


def _merge(y_dn2, oc2, os2, ow2, h, x2, wdn, wnsa, wout, g, b):
    T, D = x2.shape
    tm = 256
    row = pl.BlockSpec((tm, WIDTH), lambda i: (i, 0))
    full = lambda shape: pl.BlockSpec(shape, lambda i: (0, 0))
    return pl.pallas_call(
        _merge_kernel,
        grid=(T // tm,),
        in_specs=[row, row, row, row,
                  pl.BlockSpec((tm, LANE), lambda i: (i, COL_SMALL // LANE)),
                  pl.BlockSpec((tm, WIDTH), lambda i: (i, COL_MG_DN // WIDTH)),
                  pl.BlockSpec((tm, WIDTH), lambda i: (i, COL_MG_NSA // WIDTH)),
                  pl.BlockSpec((tm, D), lambda i: (i, 0)),
                  full((WIDTH, D)), full((WIDTH, D)), full((D, D)), full((1, D)), full((1, D))],
        out_specs=pl.BlockSpec((tm, D), lambda i: (i, 0)),
        out_shape=jax.ShapeDtypeStruct((T, D), F32),
        compiler_params=_cparams(("parallel",)),
        name="merge",
    )(y_dn2, oc2, os2, ow2, h, h, h, x2, wdn, wnsa, wout, g, b)


def _ffn_up_kernel(x_ref, xp_ref, wg_ref, wv_ref, cwg_ref, cwv_ref, bg_ref, bv_ref, o_ref, *, tiles_per_seq):
    i = pl.program_id(1)
    tm = x_ref.shape[0]
    notfirst = (i % tiles_per_seq != 0).astype(F32)
    xb = x_ref[...].astype(BF16)
    xpb = (xp_ref[...] * notfirst).astype(BF16)
    row = lax.broadcasted_iota(jnp.int32, (tm, 1), 0)

    def conv(w_ref, cw_ref, b_ref):
        u = _dot(xb, w_ref[...])
        p = _dot(xpb, w_ref[...])
        u1 = jnp.where(row == 0, p[SUBLANE - 1:SUBLANE, :], pltpu.roll(u, 1, 0))
        u2 = jnp.where(row == 0, p[SUBLANE - 2:SUBLANE - 1, :],
                       jnp.where(row == 1, p[SUBLANE - 1:SUBLANE, :], pltpu.roll(u, 2, 0)))
        return cw_ref[0:1, :] * u2 + cw_ref[1:2, :] * u1 + cw_ref[2:3, :] * u + b_ref[...]

    o_ref[...] = (_silu(conv(wg_ref, cwg_ref, bg_ref)) * conv(wv_ref, cwv_ref, bv_ref)).astype(BF16)


def _ffn_up(x1, w_up, conv_w, conv_b, seq):
    T, D = x1.shape
    tm = 512
    nj = 2
    tn = D_FF // nj
    return pl.pallas_call(
        functools.partial(_ffn_up_kernel, tiles_per_seq=seq // tm),
        grid=(nj, T // tm),
        in_specs=[pl.BlockSpec((tm, D), lambda j, i: (i, 0)),
                  pl.BlockSpec((SUBLANE, D), lambda j, i: (jnp.maximum(i * (tm // SUBLANE) - 1, 0), 0)),
                  pl.BlockSpec((D, tn), lambda j, i: (0, j)),
                  pl.BlockSpec((D, tn), lambda j, i: (0, j + nj)),
                  pl.BlockSpec((FFN_CONV, tn), lambda j, i: (0, j)),
                  pl.BlockSpec((FFN_CONV, tn), lambda j, i: (0, j + nj)),
                  pl.BlockSpec((1, tn), lambda j, i: (0, j)),
                  pl.BlockSpec((1, tn), lambda j, i: (0, j + nj))],
        out_specs=pl.BlockSpec((tm, tn), lambda j, i: (i, j)),
        out_shape=jax.ShapeDtypeStruct((T, D_FF), BF16),
        compiler_params=_cparams(("parallel", "parallel")),
        name="ffn_up",
    )(x1, x1, w_up, w_up, conv_w, conv_w, conv_b, conv_b)


def _ffn_down_kernel(a_ref, w_ref, x_ref, g_ref, b_ref, o_ref):
    z = DEEPNORM_ALPHA * x_ref[...] + _dot(a_ref[...], w_ref[...])
    o_ref[...] = _layer_norm(z, g_ref[...], b_ref[...])


def _ffn_down(a, w_down, x1, g, b):
    T, D = x1.shape
    tm = 512
    full = lambda shape: pl.BlockSpec(shape, lambda i: (0, 0))
    return pl.pallas_call(
        _ffn_down_kernel,
        grid=(T // tm,),
        in_specs=[pl.BlockSpec((tm, D_FF), lambda i: (i, 0)), full((D_FF, D)),
                  pl.BlockSpec((tm, D), lambda i: (i, 0)), full((1, D)), full((1, D))],
        out_specs=pl.BlockSpec((tm, D), lambda i: (i, 0)),
        out_shape=jax.ShapeDtypeStruct((T, D), F32),
        compiler_params=_cparams(("parallel",)),
        name="ffn_down",
    )(a, w_down, x1, g, b)


def _reorder_w_in(w):
    d = w.shape[0]
    o_db = 4 * WIDTH
    o_nq = o_db + 2 * HEADS
    o_kv = o_nq + WIDTH
    o_gate = o_kv + 6 * KVW
    o_mg = o_gate + 3 * HEADS
    pad = jnp.zeros((d, LANE - 5 * HEADS), w.dtype)
    return jnp.concatenate([
        w[:, 0:o_db], w[:, o_nq:o_kv], w[:, o_mg:o_mg + 2 * WIDTH], w[:, o_kv:o_gate],
        w[:, o_db:o_nq], w[:, o_gate:o_mg], pad], axis=1).astype(BF16)


def _rope_tables(pos):
    inv_freq = ROPE_THETA ** (-jnp.arange(ROPE_HALF, dtype=F32) / ROPE_HALF)
    ang = pos.astype(F32)[:, None] * inv_freq
    cos = jnp.cos(ang)
    sin = jnp.sin(ang)
    p = pos.shape[0]
    cosf = jnp.concatenate([cos, cos, jnp.ones((p, LANE - ROPE_DIM), F32)], axis=1)
    sinf = jnp.concatenate([-sin, sin, jnp.zeros((p, LANE - ROPE_DIM), F32)], axis=1)
    return cosf, sinf


def _overlap_table(nc, n_sel):
    n = np.arange(nc)[:, None]
    j = np.arange(LANE)[None, :]
    ov = (n * CMP_STRIDE <= j * SEL_LEN + SEL_LEN - 1) & (n * CMP_STRIDE + CMP_LEN - 1 >= j * SEL_LEN)
    ov = ov & (j < n_sel) & (n < nc - 1)
    return jnp.asarray(ov.astype(np.float32))


def _lane_row(vals, offset):
    return jnp.zeros((1, LANE), F32).at[0, offset:offset + vals.shape[0]].set(vals.astype(F32))


def kernel(x, w_in, dn_conv_w, dn_a_log, dn_dt_bias, dn_norm_w, cmp_pos_k, cmp_pos_v, cmp_k_w1, cmp_k_w2,
           cmp_v_w1, cmp_v_w2, w_branch_dn, w_branch_nsa, w_out, ln1_g, ln1_b, ffn_w_up, ffn_conv_w,
           ffn_conv_b, ffn_w_down, ln2_g, ln2_b):
    B, S, D = x.shape
    T = B * S
    nc = S // CMP_STRIDE
    n_sel = S // SEL_LEN
    assert D == WIDTH and S % 512 == 0 and n_sel <= LANE and n_sel % SUBLANE == 0
    cos_t, sin_t = _rope_tables(jnp.arange(S))
    cos_c, sin_c = _rope_tables(jnp.arange(nc) * CMP_STRIDE + (CMP_LEN - 1))
    overlap = _overlap_table(nc, n_sel)
    x2 = x.reshape(T, D)
    for layer in range(w_in.shape[0]):
        h = _inproj(x2, _reorder_w_in(w_in[layer]))
        h3 = h.reshape(B, S, NH)

        q, k, v, bg = _gdn_prep(h3, dn_conv_w[layer], _lane_row(dn_a_log[layer], SMALL_DECAY),
                                _lane_row(dn_dt_bias[layer], SMALL_DECAY))
        y_dn = _gdn(q, k, v, bg, h3, dn_norm_w[layer].reshape(1, LANE))

        q_r, q_t, kaug, vsl_t, kwn_r, vwn_t = _nsa_prep(h3, cos_t, sin_t)
        kc, vc = _compress(h3, cmp_pos_k[layer], cmp_pos_v[layer],
                           cmp_k_w1[layer].astype(BF16), cmp_k_w2[layer].astype(BF16),
                           cmp_v_w1[layer].astype(BF16), cmp_v_w2[layer].astype(BF16), cos_c, sin_c)
        o_cmp, bias = _cmp_attn(q_r, kc, vc, overlap)
        o_sel = _flash(q_t, kaug, vsl_t, bias, tk=SEL_TK)
        o_win = _flash(q_t, kwn_r, vwn_t, None, tk=WIN_TK)

        x1 = _merge(y_dn.reshape(T, WIDTH), o_cmp.reshape(T, WIDTH), o_sel.reshape(T, WIDTH),
                    o_win.reshape(T, WIDTH), h, x2,
                    w_branch_dn[layer].astype(BF16), w_branch_nsa[layer].astype(BF16),
                    w_out[layer].astype(BF16), ln1_g[layer].reshape(1, D), ln1_b[layer].reshape(1, D))
        a = _ffn_up(x1, ffn_w_up[layer].astype(BF16), ffn_conv_w[layer],
                    ffn_conv_b[layer].reshape(1, 2 * D_FF), S)
        x2 = _ffn_down(a, ffn_w_down[layer].astype(BF16), x1, ln2_g[layer].reshape(1, D), ln2_b[layer].reshape(1, D))
    return x2.reshape(B, S, D)
```

```python
import functools
import math

import numpy as np
import jax
import jax.numpy as jnp
from jax import lax
from jax.experimental import pallas as pl
from jax.experimental.pallas import tpu as pltpu

F32 = jnp.float32
BF16 = jnp.bfloat16
HI = lax.Precision.HIGHEST

HEADS = 8
HEAD_DIM = 128
KV_GROUPS = 2
HPG = HEADS // KV_GROUPS
DN_CONV = 4
DN_CHUNK = 64
CMP_LEN = 32
CMP_STRIDE = 16
CMP_HIDDEN = 256
SEL_LEN = 64
SEL_TOPK = 16
WINDOW = 512
ROPE_THETA = 500000.0
ROPE_DIM = HEAD_DIM // 4
ROPE_HALF = ROPE_DIM // 2
D_FF = 2816
FFN_CONV = 3
LN_EPS = 1e-5
RMS_EPS = 1e-6
DEEPNORM_ALPHA = 2.0 ** 0.25
ATT_SCALE = HEAD_DIM ** -0.5

LANE = 128
SUBLANE = 8
VMEM_LIMIT = 56 * 1024 * 1024

WIDTH = HEADS * HEAD_DIM
KVW = KV_GROUPS * HEAD_DIM
COL_DQKV = 0
COL_DZ = 3 * WIDTH
COL_NQ = 4 * WIDTH
COL_MG_DN = 5 * WIDTH
COL_MG_NSA = 6 * WIDTH
COL_KC = 7 * WIDTH
COL_VC = COL_KC + KVW
COL_KSL = COL_VC + KVW
COL_VSL = COL_KSL + KVW
COL_KWN = COL_VSL + KVW
COL_VWN = COL_KWN + KVW
COL_SMALL = COL_VWN + KVW
NH = COL_SMALL + LANE
SMALL_BETA = 0
SMALL_DECAY = HEADS
SMALL_GATE = 2 * HEADS

NEG_BIAS = -(2.0 ** 100)


def _cparams(sem):
    return pltpu.CompilerParams(dimension_semantics=sem, vmem_limit_bytes=VMEM_LIMIT)


def _sigmoid(x):
    return 1.0 / (1.0 + jnp.exp(-x))


def _silu(x):
    return x * _sigmoid(x)


def _dot(a, b, precision=None):
    return jnp.dot(a, b, preferred_element_type=F32, precision=precision)


def _dot_nt(a, b):
    return lax.dot_general(a, b, (((1,), (1,)), ((), ())), preferred_element_type=F32)


def _dot_tn(a, b):
    return lax.dot_general(a, b, (((0,), (0,)), ((), ())), preferred_element_type=F32)


def _layer_norm(z, g, b):
    mu = jnp.mean(z, -1, keepdims=True)
    zc = z - mu
    var = jnp.mean(zc * zc, -1, keepdims=True)
    return zc * lax.rsqrt(var + LN_EPS) * g + b


def _inproj_kernel(x_ref, w_ref, o_ref):
    o_ref[...] = _dot(x_ref[...].astype(BF16), w_ref[...])


def _inproj(x2, w):
    T, D = x2.shape
    n_out = w.shape[1]
    tm = 512
    tn = n_out // 3
    return pl.pallas_call(
        _inproj_kernel,
        grid=(n_out // tn, T // tm),
        in_specs=[pl.BlockSpec((tm, D), lambda j, i: (i, 0)),
                  pl.BlockSpec((D, tn), lambda j, i: (0, j))],
        out_specs=pl.BlockSpec((tm, tn), lambda j, i: (i, j)),
        out_shape=jax.ShapeDtypeStruct((T, n_out), F32),
        compiler_params=_cparams(("parallel", "parallel")),
        name="inproj",
    )(x2, w)


def _gdn_prep_kernel(cur_ref, prev_ref, small_ref, cw_ref, alog_ref, dtb_ref,
                     q_ref, k_ref, v_ref, bg_ref, buf_ref):
    i = pl.program_id(1)
    ts = cur_ref.shape[0]
    notfirst = (i > 0).astype(F32)
    buf_ref[0:SUBLANE, :] = prev_ref[...] * notfirst
    buf_ref[SUBLANE:SUBLANE + ts, :] = cur_ref[...]
    for c in range(3 * HEADS):
        sl = slice(c * LANE, (c + 1) * LANE)
        acc = cw_ref[DN_CONV - 1:DN_CONV, sl] * cur_ref[:, sl]
        for s in range(1, DN_CONV):
            acc = acc + cw_ref[DN_CONV - 1 - s:DN_CONV - s, sl] * buf_ref[pl.ds(SUBLANE - s, ts), sl]
        y = _silu(acc)
        if c < 2 * HEADS:
            y = y * lax.rsqrt(jnp.sum(y * y, -1, keepdims=True) + RMS_EPS)
        if c < HEADS:
            q_ref[:, sl] = y * ATT_SCALE
        elif c < 2 * HEADS:
            k_ref[:, slice((c - HEADS) * LANE, (c - HEADS + 1) * LANE)] = y
        else:
            v_ref[:, slice((c - 2 * HEADS) * LANE, (c - 2 * HEADS + 1) * LANE)] = y
    small = small_ref[...]
    beta = _sigmoid(small)
    xs = small + dtb_ref[...]
    softplus = jnp.maximum(xs, 0.0) + jnp.log1p(jnp.exp(-jnp.abs(xs)))
    g = -jnp.exp(alog_ref[...]) * softplus
    lane = lax.broadcasted_iota(jnp.int32, small.shape, 1)
    bg_ref[...] = jnp.where(lane < SMALL_DECAY, beta, g)


def _gdn_prep(h3, conv_w, alog_row, dtb_row):
    B, S, _ = h3.shape
    ts = 256
    wq = 3 * WIDTH
    out_sd = jax.ShapeDtypeStruct((B, S, WIDTH), F32)
    return pl.pallas_call(
        _gdn_prep_kernel,
        grid=(B, S // ts),
        in_specs=[
            pl.BlockSpec((None, ts, wq), lambda b, i: (b, i, 0)),
            pl.BlockSpec((None, SUBLANE, wq), lambda b, i: (b, jnp.maximum(i * (ts // SUBLANE) - 1, 0), 0)),
            pl.BlockSpec((None, ts, LANE), lambda b, i: (b, i, COL_SMALL // LANE)),
            pl.BlockSpec((DN_CONV, wq), lambda b, i: (0, 0)),
            pl.BlockSpec((1, LANE), lambda b, i: (0, 0)),
            pl.BlockSpec((1, LANE), lambda b, i: (0, 0)),
        ],
        out_specs=[
            pl.BlockSpec((None, ts, WIDTH), lambda b, i: (b, i, 0)),
            pl.BlockSpec((None, ts, WIDTH), lambda b, i: (b, i, 0)),
            pl.BlockSpec((None, ts, WIDTH), lambda b, i: (b, i, 0)),
            pl.BlockSpec((None, ts, LANE), lambda b, i: (b, i, 0)),
        ],
        out_shape=[out_sd, out_sd, out_sd, jax.ShapeDtypeStruct((B, S, LANE), F32)],
        scratch_shapes=[pltpu.VMEM((ts + SUBLANE, wq), F32)],
        compiler_params=_cparams(("parallel", "parallel")),
        name="gdn_prep",
    )(h3, h3, h3, conv_w, alog_row, dtb_row)


GDN_CHUNKS_PER_STEP = 2


def _split_bf16(a):
    hi = a.astype(BF16)
    return hi, (a - hi.astype(F32)).astype(BF16)


def _dot_split(a, b):
    return _dot(a[0], b[0]) + (_dot(a[0], b[1]) + _dot(a[1], b[0]))


def _gdn_kernel(q_ref, k_ref, v_ref, bg_ref, z_ref, nw_ref, y_ref, state_ref):
    n = pl.program_id(1)

    @pl.when(n == 0)
    def _():
        state_ref[...] = jnp.zeros(state_ref.shape, F32)

    C = DN_CHUNK
    ri = lax.broadcasted_iota(jnp.int32, (C, C), 0)
    ci = lax.broadcasted_iota(jnp.int32, (C, C), 1)
    incl = ri >= ci
    strict = ri > ci
    tril = jnp.where(incl, 1.0, 0.0).astype(F32)
    triu = jnp.where(ri <= ci, 1.0, 0.0).astype(F32)
    eye = jnp.where(ri == ci, 1.0, 0.0).astype(F32)
    bg = bg_ref[...]
    bg_t = bg.T
    nw = nw_ref[...]
    units = [(c, h) for c in range(GDN_CHUNKS_PER_STEP) for h in range(HEADS)]
    gc_cols, gc_rows = [], []
    for c in range(GDN_CHUNKS_PER_STEP):
        rs = slice(c * C, (c + 1) * C)
        gc_cols.append(_dot(tril, bg[rs, :], HI))
        gc_rows.append(_dot(bg_t[SMALL_DECAY:SMALL_DECAY + HEADS, rs], triu, HI))
    gcol, decay, kb_l, vb_l, kh_l, lower = {}, {}, {}, {}, {}, {}
    for (c, h) in units:
        rs = slice(c * C, (c + 1) * C)
        ls = slice(h * LANE, (h + 1) * LANE)
        kh = k_ref[rs, ls]
        beta = bg[rs, SMALL_BETA + h:SMALL_BETA + h + 1]
        gcol[c, h] = gc_cols[c][:, SMALL_DECAY + h:SMALL_DECAY + h + 1]
        decay[c, h] = jnp.exp(jnp.where(incl, gcol[c, h] - gc_rows[c][h:h + 1, :], -jnp.inf))
        kb_l[c, h] = kh * beta
        vb_l[c, h] = v_ref[rs, ls] * beta
        kh_l[c, h] = kh.astype(BF16)
    for u_ in units:
        lower[u_] = jnp.where(strict, _dot_nt(kb_l[u_].astype(BF16), kh_l[u_]) * decay[u_], 0.0)
    tinv = {u_: eye - lower[u_] for u_ in units}
    lsp = {u_: _split_bf16(lower[u_]) for u_ in units}
    ypow = {u_: _dot_split(lsp[u_], lsp[u_]) for u_ in units}
    for it in range(5):
        ysp = {u_: _split_bf16(ypow[u_]) for u_ in units}
        tinv = {u_: tinv[u_] + _dot_split(_split_bf16(tinv[u_]), ysp[u_]) for u_ in units}
        if it < 4:
            ypow = {u_: _dot_split(ysp[u_], ysp[u_]) for u_ in units}
    uw, qk, qd, kd, eglast = {}, {}, {}, {}, {}
    for (c, h) in units:
        rs = slice(c * C, (c + 1) * C)
        ls = slice(h * LANE, (h + 1) * LANE)
        u_ = (c, h)
        eg = jnp.exp(gcol[u_])
        tsp = _split_bf16(tinv[u_])
        rhs = jnp.concatenate([vb_l[u_], kb_l[u_] * eg], axis=1)
        uw[u_] = _dot_split(tsp, _split_bf16(rhs))
        qh = q_ref[rs, ls]
        qk[u_] = (_dot_nt(qh.astype(BF16), kh_l[u_]) * decay[u_]).astype(BF16)
        glast = gcol[u_][C - 1:C, :]
        qd[u_] = (qh * eg).astype(BF16)
        kd[u_] = (k_ref[rs, ls] * jnp.exp(glast - gcol[u_])).astype(BF16)
        eglast[u_] = jnp.exp(glast)
    for c in range(GDN_CHUNKS_PER_STEP):
        rs = slice(c * C, (c + 1) * C)
        for h in range(HEADS):
            ls = slice(h * LANE, (h + 1) * LANE)
            u_ = (c, h)
            state = state_ref[h]
            state_b = state.astype(BF16)
            v_new = uw[u_][:, 0:LANE] - _dot(uw[u_][:, LANE:2 * LANE].astype(BF16), state_b)
            v_new_b = v_new.astype(BF16)
            o = _dot(qd[u_], state_b) + _dot(qk[u_], v_new_b)
            state_ref[h] = state * eglast[u_] + _dot_tn(kd[u_], v_new_b)
            o = o * lax.rsqrt(jnp.mean(o * o, -1, keepdims=True) + RMS_EPS)
            y_ref[rs, ls] = (o * nw * _silu(z_ref[rs, ls])).astype(BF16)


def _gdn(q, k, v, bg, h3, norm_w):
    B, S, _ = q.shape
    rows = GDN_CHUNKS_PER_STEP * DN_CHUNK
    tile = pl.BlockSpec((None, rows, WIDTH), lambda b, n: (b, n, 0))
    return pl.pallas_call(
        _gdn_kernel,
        grid=(B, S // rows),
        in_specs=[tile, tile, tile,
                  pl.BlockSpec((None, rows, LANE), lambda b, n: (b, n, 0)),
                  pl.BlockSpec((None, rows, WIDTH), lambda b, n: (b, n, COL_DZ // WIDTH)),
                  pl.BlockSpec((1, LANE), lambda b, n: (0, 0))],
        out_specs=tile,
        out_shape=jax.ShapeDtypeStruct((B, S, WIDTH), BF16),
        scratch_shapes=[pltpu.VMEM((HEADS, HEAD_DIM, HEAD_DIM), F32)],
        compiler_params=_cparams(("parallel", "arbitrary")),
        name="gdn",
    )(q, k, v, bg, h3, norm_w)


def _rope(t, cosf, sinf, lo):
    partner = jnp.where(lo, pltpu.roll(t, LANE - ROPE_HALF, 1), pltpu.roll(t, ROPE_HALF, 1))
    return t * cosf + partner * sinf


SEL_TK = 256
WIN_TK = 256
LOG2E = math.log2(math.e)


def _nsa_prep_kernel(nq_ref, ksl_ref, vsl_ref, kwn_ref, vwn_ref, cos_ref, sin_ref,
                     q_o, qt_o, kaug_o, vslt_o, kwn_o, vwnt_o):
    i = pl.program_id(1)
    ts = nq_ref.shape[0]
    cosf = cos_ref[...]
    sinf = sin_ref[...]
    lane = lax.broadcasted_iota(jnp.int32, cosf.shape, 1)
    lo = lane < ROPE_HALF
    for h in range(HEADS):
        sl = slice(h * LANE, (h + 1) * LANE)
        qh = _rope(nq_ref[:, sl], cosf, sinf, lo) * ATT_SCALE
        q_o[:, sl] = qh.astype(BF16)
        qt_o[sl, :] = (qh * LOG2E).T.astype(BF16)
    tpos = i * ts + lax.broadcasted_iota(jnp.int32, cosf.shape, 0)
    onehot = jnp.where((tpos >> int(math.log2(SEL_LEN))) == lane, 1.0, 0.0).astype(BF16)
    for g in range(KV_GROUPS):
        sl = slice(g * LANE, (g + 1) * LANE)
        kaug_o[:, 2 * g * LANE:(2 * g + 1) * LANE] = _rope(ksl_ref[:, sl], cosf, sinf, lo).astype(BF16)
        kaug_o[:, (2 * g + 1) * LANE:(2 * g + 2) * LANE] = onehot
        kwn_o[:, sl] = _rope(kwn_ref[:, sl], cosf, sinf, lo).astype(BF16)
        vs_t = vsl_ref[:, sl].T.astype(BF16)
        for c in range(ts // SEL_TK):
            vslt_o[g, c] = vs_t[:, c * SEL_TK:(c + 1) * SEL_TK]
        vw_t = vwn_ref[:, sl].T.astype(BF16)
        for c in range(ts // WIN_TK):
            vwnt_o[g, c] = vw_t[:, c * WIN_TK:(c + 1) * WIN_TK]


def _nsa_prep(h3, cos_tab, sin_tab):
    B, S, _ = h3.shape
    ts = 512

    def kv_spec(col):
        return pl.BlockSpec((None, ts, KVW), lambda b, i: (b, i, col // KVW))

    tab = pl.BlockSpec((ts, LANE), lambda b, i: (i, 0))
    return pl.pallas_call(
        _nsa_prep_kernel,
        grid=(B, S // ts),
        in_specs=[pl.BlockSpec((None, ts, WIDTH), lambda b, i: (b, i, COL_NQ // WIDTH)),
                  kv_spec(COL_KSL), kv_spec(COL_VSL), kv_spec(COL_KWN), kv_spec(COL_VWN), tab, tab],
        out_specs=[pl.BlockSpec((None, ts, WIDTH), lambda b, i: (b, i, 0)),
                   pl.BlockSpec((None, WIDTH, ts), lambda b, i: (b, 0, i)),
                   pl.BlockSpec((None, ts, 2 * KVW), lambda b, i: (b, i, 0)),
                   pl.BlockSpec((None, KV_GROUPS, ts // SEL_TK, LANE, SEL_TK), lambda b, i: (b, 0, i, 0, 0)),
                   pl.BlockSpec((None, ts, KVW), lambda b, i: (b, i, 0)),
                   pl.BlockSpec((None, KV_GROUPS, ts // WIN_TK, LANE, WIN_TK), lambda b, i: (b, 0, i, 0, 0))],
        out_shape=[jax.ShapeDtypeStruct((B, S, WIDTH), BF16),
                   jax.ShapeDtypeStruct((B, WIDTH, S), BF16),
                   jax.ShapeDtypeStruct((B, S, 2 * KVW), BF16),
                   jax.ShapeDtypeStruct((B, KV_GROUPS, S // SEL_TK, LANE, SEL_TK), BF16),
                   jax.ShapeDtypeStruct((B, S, KVW), BF16),
                   jax.ShapeDtypeStruct((B, KV_GROUPS, S // WIN_TK, LANE, WIN_TK), BF16)],
        compiler_params=_cparams(("parallel", "parallel")),
        name="nsa_prep",
    )(h3, h3, h3, h3, h3, cos_tab, sin_tab)


def _compress_one(t_ref, pe_ref, w1_ref, w2_ref, nc):
    half = CMP_LEN // 2
    y1 = jnp.zeros((nc, CMP_HIDDEN), F32)
    y2 = jnp.zeros((nc, CMP_HIDDEN), F32)
    for l in range(half):
        x = t_ref[pl.ds(l, nc, stride=CMP_STRIDE), :]
        y1 = y1 + _dot((x + pe_ref[l:l + 1, :]).astype(BF16), w1_ref[l * LANE:(l + 1) * LANE, :])
        y2 = y2 + _dot((x + pe_ref[half + l:half + l + 1, :]).astype(BF16),
                       w1_ref[(half + l) * LANE:(half + l + 1) * LANE, :])
    y = y1 + pltpu.roll(y2, nc - 1, 0)
    return _dot(_silu(y).astype(BF16), w2_ref[...])


def _compress_kernel(kc_ref, vc_ref, pek_ref, pev_ref, w1k_ref, w2k_ref, w1v_ref, w2v_ref,
                     cos_ref, sin_ref, kc_o, vc_o):
    nc = kc_o.shape[0]
    kc = _compress_one(kc_ref, pek_ref, w1k_ref, w2k_ref, nc)
    lo = lax.broadcasted_iota(jnp.int32, kc.shape, 1) < ROPE_HALF
    kc_o[...] = _rope(kc, cos_ref[...], sin_ref[...], lo).astype(BF16)
    vc_o[...] = _compress_one(vc_ref, pev_ref, w1v_ref, w2v_ref, nc).astype(BF16)


def _compress(h3, pek, pev, w1k, w2k, w1v, w2v, cos_c, sin_c):
    B, S, _ = h3.shape
    nc = S // CMP_STRIDE
    full = lambda shape: pl.BlockSpec(shape, lambda b, g: tuple(0 for _ in shape))
    out_spec = pl.BlockSpec((None, None, nc, LANE), lambda b, g: (b, g, 0, 0))
    out_sd = jax.ShapeDtypeStruct((B, KV_GROUPS, nc, LANE), BF16)
    cin = CMP_LEN * HEAD_DIM
    return pl.pallas_call(
        _compress_kernel,
        grid=(B, KV_GROUPS),
        in_specs=[pl.BlockSpec((None, S, LANE), lambda b, g: (b, 0, COL_KC // LANE + g)),
                  pl.BlockSpec((None, S, LANE), lambda b, g: (b, 0, COL_VC // LANE + g)),
                  full((CMP_LEN, LANE)), full((CMP_LEN, LANE)),
                  full((cin, CMP_HIDDEN)), full((CMP_HIDDEN, LANE)),
                  full((cin, CMP_HIDDEN)), full((CMP_HIDDEN, LANE)),
                  full((nc, LANE)), full((nc, LANE))],
        out_specs=[out_spec, out_spec],
        out_shape=[out_sd, out_sd],
        compiler_params=_cparams(("parallel", "parallel")),
        name="compress",
    )(h3, h3, pek, pev, w1k, w2k, w1v, w2v, cos_c, sin_c)


def _cmp_attn_kernel(q_ref, kc_ref, vc_ref, ov_ref, o_ref, bias_ref, *, n_sel):
    i = pl.program_id(2)
    tq = q_ref.shape[0]
    nc = kc_ref.shape[0]
    t0 = i * tq
    tpos = t0 + lax.broadcasted_iota(jnp.int32, (tq, 1), 0)
    cmp_end = lax.broadcasted_iota(jnp.int32, (1, nc), 1) * CMP_STRIDE + (CMP_LEN - 1)
    mask = cmp_end <= tpos
    kc = kc_ref[...]
    vc = vc_ref[...]
    psum = jnp.zeros((tq, nc), F32)
    for hh in range(HPG):
        sl = slice(hh * LANE, (hh + 1) * LANE)
        s = jnp.where(mask, _dot_nt(q_ref[:, sl], kc), -jnp.inf)
        m = jnp.max(s, -1, keepdims=True)
        m = jnp.where(m > -jnp.inf, m, 0.0)
        p = jnp.exp(s - m)
        p = p / jnp.maximum(jnp.sum(p, -1, keepdims=True), jnp.finfo(F32).tiny)
        o_ref[:, sl] = _dot(p.astype(BF16), vc)
        psum = psum + p
    p_sel = _dot(psum, ov_ref[...], HI)
    jb = lax.broadcasted_iota(jnp.int32, (tq, LANE), 1)
    cur = tpos >> int(math.log2(SEL_LEN))
    valid = jb <= cur
    forced = valid & ((jb == 0) | (jb == cur) | (jb == cur - 1))
    score = jnp.where(forced, jnp.inf, jnp.where(valid, p_sel, -jnp.inf))
    st = score.T[0:n_sel, :]
    jrow = lax.broadcasted_iota(jnp.int32, (n_sel, tq), 0)
    cnt = jnp.zeros((n_sel, tq), F32)
    for b in range(n_sel):
        rb = st[b:b + 1, :]
        ahead = (rb > st) | ((rb == st) & (jrow > b))
        cnt = cnt + jnp.where(ahead, 1.0, 0.0)
    sel = (cnt < float(SEL_TOPK)) & (st > -jnp.inf)
    bias_t = jnp.where(sel, 0.0, NEG_BIAS)
    if n_sel < LANE:
        bias_t = jnp.concatenate([bias_t, jnp.zeros((LANE - n_sel, tq), F32)], axis=0)
    bias_ref[...] = bias_t.astype(BF16)


def _cmp_attn(q_r, kc, vc, overlap):
    B, S, _ = q_r.shape
    nc = kc.shape[2]
    n_sel = S // SEL_LEN
    tq = 256
    gw = HPG * LANE
    return pl.pallas_call(
        functools.partial(_cmp_attn_kernel, n_sel=n_sel),
        grid=(B, KV_GROUPS, S // tq),
        in_specs=[pl.BlockSpec((None, tq, gw), lambda b, g, i: (b, i, g)),
                  pl.BlockSpec((None, None, nc, LANE), lambda b, g, i: (b, g, 0, 0)),
                  pl.BlockSpec((None, None, nc, LANE), lambda b, g, i: (b, g, 0, 0)),
                  pl.BlockSpec((nc, LANE), lambda b, g, i: (0, 0))],
        out_specs=[pl.BlockSpec((None, tq, gw), lambda b, g, i: (b, i, g)),
                   pl.BlockSpec((None, None, LANE, tq), lambda b, g, i: (b, g, 0, i))],
        out_shape=[jax.ShapeDtypeStruct((B, S, WIDTH), F32),
                   jax.ShapeDtypeStruct((B, KV_GROUPS, LANE, S), BF16)],
        compiler_params=_cparams(("parallel", "parallel", "parallel")),
        name="cmp_attn",
    )(q_r, kc, vc, overlap)


def _flash_kernel(*refs, selected, tk):
    if selected:
        q_ref, b_ref, k_ref, vt_ref, o_ref, qs_ref, s_ref, m_ref, l_ref, acc_ref = refs
    else:
        q_ref, k_ref, vt_ref, o_ref, qs_ref, s_ref, m_ref, l_ref, acc_ref = refs
    i = pl.program_id(1)
    tq = q_ref.shape[1]
    cols = HPG * tq
    t0 = i * tq
    kw = qs_ref.shape[1]
    for g in range(KV_GROUPS):
        for hh in range(HPG):
            h = g * HPG + hh
            qs_ref[g, 0:LANE, hh * tq:(hh + 1) * tq] = q_ref[h * LANE:(h + 1) * LANE, :]
            if selected:
                qs_ref[g, LANE:2 * LANE, hh * tq:(hh + 1) * tq] = b_ref[g]
    m_ref[...] = jnp.full(m_ref.shape, -jnp.inf, F32)
    l_ref[...] = jnp.zeros(l_ref.shape, F32)
    acc_ref[...] = jnp.zeros(acc_ref.shape, F32)
    qpos = t0 + (lax.broadcasted_iota(jnp.int32, (1, cols), 1) & (tq - 1))

    def scores(j, slot):
        k0 = pl.multiple_of(j * tk, tk)
        for g in range(KV_GROUPS):
            s_ref[slot, g] = _dot(k_ref[pl.ds(k0, tk), g * kw:(g + 1) * kw], qs_ref[g])

    def update(j, slot, masked):
        if masked:
            kpos = j * tk + lax.broadcasted_iota(jnp.int32, (tk, 1), 0)
            mask = kpos <= qpos
            if not selected:
                mask = mask & (qpos - kpos < WINDOW)
        for g in range(KV_GROUPS):
            s = s_ref[slot, g]
            if masked:
                s = jnp.where(mask, s, -jnp.inf)
            m_old = m_ref[g]
            m_new = jnp.maximum(m_old, jnp.max(s, axis=0, keepdims=True))
            alpha = jnp.exp2(m_old - m_new)
            p = jnp.exp2(s - m_new)
            l_ref[g] = alpha * l_ref[g] + jnp.sum(p, axis=0, keepdims=True)
            acc_ref[g] = alpha * acc_ref[g] + _dot(vt_ref[g, j], p.astype(BF16))
            m_ref[g] = m_new

    jd = t0 // tk
    if selected:
        last = jnp.maximum(jd - 1, 0)
        scores(jd, 0)
        scores(0, 1)
        update(jd, 0, True)

        def body(p, carry):
            scores(jnp.minimum(2 * p + 1, last), 0)
            update(2 * p, 1, False)
            scores(jnp.minimum(2 * p + 2, last), 1)
            update(2 * p + 1, 0, False)
            return carry

        lax.fori_loop(0, jd // 2, body, 0)

        @pl.when(jd % 2 == 1)
        def _():
            update(jd - 1, 1, False)
    else:
        n_full = WINDOW // tk - 1
        scores(jd, 0)
        update(jd, 0, True)

        def full_step(j, carry):
            scores(j, 0)
            update(j, 0, False)
            return carry

        lax.fori_loop(jnp.maximum(jd - n_full, 0), jd, full_step, 0)

        @pl.when(jd - n_full - 1 >= 0)
        def _():
            scores(jd - n_full - 1, 0)
            update(jd - n_full - 1, 0, True)

    for g in range(KV_GROUPS):
        out_t = acc_ref[g] / jnp.maximum(l_ref[g], jnp.finfo(F32).tiny)
        for hh in range(HPG):
            h = g * HPG + hh
            o_ref[:, h * LANE:(h + 1) * LANE] = out_t[:, hh * tq:(hh + 1) * tq].T


FLASH_TQ = 256


def _flash(q_t, k_r, v_t, bias_t, *, tk):
    B, _, S = q_t.shape
    tq = FLASH_TQ
    selected = bias_t is not None
    assert tq == tk
    kw = k_r.shape[2] // KV_GROUPS
    in_specs = [pl.BlockSpec((None, WIDTH, tq), lambda b, i: (b, 0, i))]
    args = [q_t]
    if selected:
        in_specs.append(pl.BlockSpec((None, KV_GROUPS, LANE, tq), lambda b, i: (b, 0, 0, i)))
        args.append(bias_t)
    in_specs += [pl.BlockSpec((None, S, KV_GROUPS * kw), lambda b, i: (b, 0, 0)),
                 pl.BlockSpec((None, KV_GROUPS, S // tk, LANE, tk), lambda b, i: (b, 0, 0, 0, 0))]
    args += [k_r, v_t]
    cols = HPG * tq
    return pl.pallas_call(
        functools.partial(_flash_kernel, selected=selected, tk=tk),
        grid=(B, S // tq),
        in_specs=in_specs,
        out_specs=pl.BlockSpec((None, tq, WIDTH), lambda b, i: (b, i, 0)),
        out_shape=jax.ShapeDtypeStruct((B, S, WIDTH), F32),
        scratch_shapes=[pltpu.VMEM((KV_GROUPS, kw, cols), BF16),
                        pltpu.VMEM((2 if selected else 1, KV_GROUPS, tk, cols), F32),
                        pltpu.VMEM((KV_GROUPS, 1, cols), F32), pltpu.VMEM((KV_GROUPS, 1, cols), F32),
                        pltpu.VMEM((KV_GROUPS, LANE, cols), F32)],
        compiler_params=_cparams(("parallel", "parallel")),
        name="flash_sel" if selected else "flash_win",
    )(*args)


def _merge_kernel(ydn_ref, oc_ref, os_ref, ow_ref, small_ref, mgd_ref, mgn_ref, x_ref,
                  wdn_ref, wnsa_ref, wout_ref, g_ref, b_ref, o_ref):
    gates = _sigmoid(small_ref[...])
    cols = []
    for h in range(HEADS):
        sl = slice(h * LANE, (h + 1) * LANE)
        gc = gates[:, SMALL_GATE + h:SMALL_GATE + h + 1]
        gs = gates[:, SMALL_GATE + HEADS + h:SMALL_GATE + HEADS + h + 1]
        gw = gates[:, SMALL_GATE + 2 * HEADS + h:SMALL_GATE + 2 * HEADS + h + 1]
        cols.append((gc * oc_ref[:, sl] + gs * os_ref[:, sl] + gw * ow_ref[:, sl]).astype(BF16))
    y_nsa = jnp.concatenate(cols, axis=1)
    y = (_sigmoid(mgd_ref[...]) * _dot(ydn_ref[...], wdn_ref[...])
         + _sigmoid(mgn_ref[...]) * _dot(y_nsa, wnsa_ref[...]))
    z = DEEPNORM_ALPHA * x_ref[...] + _dot(y.astype(BF16), wout_ref[...])
    o_ref[...] = _layer_norm(z, g_ref[...], b_ref[...])


def _merge(y_dn2, oc2, os2, ow2, h, x2, wdn, wnsa, wout, g, b):
    T, D = x2.shape
    tm = 256
    row = pl.BlockSpec((tm, WIDTH), lambda i: (i, 0))
    full = lambda shape: pl.BlockSpec(shape, lambda i: (0, 0))
    return pl.pallas_call(
        _merge_kernel,
        grid=(T // tm,),
        in_specs=[row, row, row, row,
                  pl.BlockSpec((tm, LANE), lambda i: (i, COL_SMALL // LANE)),
                  pl.BlockSpec((tm, WIDTH), lambda i: (i, COL_MG_DN // WIDTH)),
                  pl.BlockSpec((tm, WIDTH), lambda i: (i, COL_MG_NSA // WIDTH)),
                  pl.BlockSpec((tm, D), lambda i: (i, 0)),
                  full((WIDTH, D)), full((WIDTH, D)), full((D, D)), full((1, D)), full((1, D))],
        out_specs=pl.BlockSpec((tm, D), lambda i: (i, 0)),
        out_shape=jax.ShapeDtypeStruct((T, D), F32),
        compiler_params=_cparams(("parallel",)),
        name="merge",
    )(y_dn2, oc2, os2, ow2, h, h, h, x2, wdn, wnsa, wout, g, b)


def _ffn_up_kernel(x_ref, xp_ref, wg_ref, wv_ref, cwg_ref, cwv_ref, bg_ref, bv_ref, o_ref, *, tiles_per_seq):
    i = pl.program_id(1)
    tm = x_ref.shape[0]
    notfirst = (i % tiles_per_seq != 0).astype(F32)
    xb = x_ref[...].astype(BF16)
    xpb = (xp_ref[...] * notfirst).astype(BF16)
    row = lax.broadcasted_iota(jnp.int32, (tm, 1), 0)

    def conv(w_ref, cw_ref, b_ref):
        u = _dot(xb, w_ref[...])
        p = _dot(xpb, w_ref[...])
        u1 = jnp.where(row == 0, p[SUBLANE - 1:SUBLANE, :], pltpu.roll(u, 1, 0))
        u2 = jnp.where(row == 0, p[SUBLANE - 2:SUBLANE - 1, :],
                       jnp.where(row == 1, p[SUBLANE - 1:SUBLANE, :], pltpu.roll(u, 2, 0)))
        return cw_ref[0:1, :] * u2 + cw_ref[1:2, :] * u1 + cw_ref[2:3, :] * u + b_ref[...]

    o_ref[...] = (_silu(conv(wg_ref, cwg_ref, bg_ref)) * conv(wv_ref, cwv_ref, bv_ref)).astype(BF16)


def _ffn_up(x1, w_up, conv_w, conv_b, seq):
    T, D = x1.shape
    tm = 512
    nj = 2
    tn = D_FF // nj
    return pl.pallas_call(
        functools.partial(_ffn_up_kernel, tiles_per_seq=seq // tm),
        grid=(nj, T // tm),
        in_specs=[pl.BlockSpec((tm, D), lambda j, i: (i, 0)),
                  pl.BlockSpec((SUBLANE, D), lambda j, i: (jnp.maximum(i * (tm // SUBLANE) - 1, 0), 0)),
                  pl.BlockSpec((D, tn), lambda j, i: (0, j)),
                  pl.BlockSpec((D, tn), lambda j, i: (0, j + nj)),
                  pl.BlockSpec((FFN_CONV, tn), lambda j, i: (0, j)),
                  pl.BlockSpec((FFN_CONV, tn), lambda j, i: (0, j + nj)),
                  pl.BlockSpec((1, tn), lambda j, i: (0, j)),
                  pl.BlockSpec((1, tn), lambda j, i: (0, j + nj))],
        out_specs=pl.BlockSpec((tm, tn), lambda j, i: (i, j)),
        out_shape=jax.ShapeDtypeStruct((T, D_FF), BF16),
        compiler_params=_cparams(("parallel", "parallel")),
        name="ffn_up",
    )(x1, x1, w_up, w_up, conv_w, conv_w, conv_b, conv_b)


def _ffn_down_kernel(a_ref, w_ref, x_ref, g_ref, b_ref, o_ref):
    z = DEEPNORM_ALPHA * x_ref[...] + _dot(a_ref[...], w_ref[...])
    o_ref[...] = _layer_norm(z, g_ref[...], b_ref[...])


def _ffn_down(a, w_down, x1, g, b):
    T, D = x1.shape
    tm = 512
    full = lambda shape: pl.BlockSpec(shape, lambda i: (0, 0))
    return pl.pallas_call(
        _ffn_down_kernel,
        grid=(T // tm,),
        in_specs=[pl.BlockSpec((tm, D_FF), lambda i: (i, 0)), full((D_FF, D)),
                  pl.BlockSpec((tm, D), lambda i: (i, 0)), full((1, D)), full((1, D))],
        out_specs=pl.BlockSpec((tm, D), lambda i: (i, 0)),
        out_shape=jax.ShapeDtypeStruct((T, D), F32),
        compiler_params=_cparams(("parallel",)),
        name="ffn_down",
    )(a, w_down, x1, g, b)


def _reorder_w_in(w):
    d = w.shape[0]
    o_db = 4 * WIDTH
    o_nq = o_db + 2 * HEADS
    o_kv = o_nq + WIDTH
    o_gate = o_kv + 6 * KVW
    o_mg = o_gate + 3 * HEADS
    pad = jnp.zeros((d, LANE - 5 * HEADS), w.dtype)
    return jnp.concatenate([
        w[:, 0:o_db], w[:, o_nq:o_kv], w[:, o_mg:o_mg + 2 * WIDTH], w[:, o_kv:o_gate],
        w[:, o_db:o_nq], w[:, o_gate:o_mg], pad], axis=1).astype(BF16)


def _rope_tables(pos):
    inv_freq = ROPE_THETA ** (-jnp.arange(ROPE_HALF, dtype=F32) / ROPE_HALF)
    ang = pos.astype(F32)[:, None] * inv_freq
    cos = jnp.cos(ang)
    sin = jnp.sin(ang)
    p = pos.shape[0]
    cosf = jnp.concatenate([cos, cos, jnp.ones((p, LANE - ROPE_DIM), F32)], axis=1)
    sinf = jnp.concatenate([-sin, sin, jnp.zeros((p, LANE - ROPE_DIM), F32)], axis=1)
    return cosf, sinf


def _overlap_table(nc, n_sel):
    n = np.arange(nc)[:, None]
    j = np.arange(LANE)[None, :]
    ov = (n * CMP_STRIDE <= j * SEL_LEN + SEL_LEN - 1) & (n * CMP_STRIDE + CMP_LEN - 1 >= j * SEL_LEN)
    ov = ov & (j < n_sel) & (n < nc - 1)
    return jnp.asarray(ov.astype(np.float32))


def _lane_row(vals, offset):
    return jnp.zeros((1, LANE), F32).at[0, offset:offset + vals.shape[0]].set(vals.astype(F32))


def kernel(x, w_in, dn_conv_w, dn_a_log, dn_dt_bias, dn_norm_w, cmp_pos_k, cmp_pos_v, cmp_k_w1, cmp_k_w2,
           cmp_v_w1, cmp_v_w2, w_branch_dn, w_branch_nsa, w_out, ln1_g, ln1_b, ffn_w_up, ffn_conv_w,
           ffn_conv_b, ffn_w_down, ln2_g, ln2_b):
    B, S, D = x.shape
    T = B * S
    nc = S // CMP_STRIDE
    n_sel = S // SEL_LEN
    assert D == WIDTH and S % 512 == 0 and n_sel <= LANE and n_sel % SUBLANE == 0
    cos_t, sin_t = _rope_tables(jnp.arange(S))
    cos_c, sin_c = _rope_tables(jnp.arange(nc) * CMP_STRIDE + (CMP_LEN - 1))
    overlap = _overlap_table(nc, n_sel)
    x2 = x.reshape(T, D)
    for layer in range(w_in.shape[0]):
        h = _inproj(x2, _reorder_w_in(w_in[layer]))
        h3 = h.reshape(B, S, NH)

        q, k, v, bg = _gdn_prep(h3, dn_conv_w[layer], _lane_row(dn_a_log[layer], SMALL_DECAY),
                                _lane_row(dn_dt_bias[layer], SMALL_DECAY))
        y_dn = _gdn(q, k, v, bg, h3, dn_norm_w[layer].reshape(1, LANE))

        q_r, q_t, kaug, vsl_t, kwn_r, vwn_t = _nsa_prep(h3, cos_t, sin_t)
        kc, vc = _compress(h3, cmp_pos_k[layer], cmp_pos_v[layer],
                           cmp_k_w1[layer].astype(BF16), cmp_k_w2[layer].astype(BF16),
                           cmp_v_w1[layer].astype(BF16), cmp_v_w2[layer].astype(BF16), cos_c, sin_c)
        o_cmp, bias = _cmp_attn(q_r, kc, vc, overlap)
        o_sel = _flash(q_t, kaug, vsl_t, bias, tk=SEL_TK)
        o_win = _flash(q_t, kwn_r, vwn_t, None, tk=WIN_TK)

        x1 = _merge(y_dn.reshape(T, WIDTH), o_cmp.reshape(T, WIDTH), o_sel.reshape(T, WIDTH),
                    o_win.reshape(T, WIDTH), h, x2,
                    w_branch_dn[layer].astype(BF16), w_branch_nsa[layer].astype(BF16),
                    w_out[layer].astype(BF16), ln1_g[layer].reshape(1, D), ln1_b[layer].reshape(1, D))
        a = _ffn_up(x1, ffn_w_up[layer].astype(BF16), ffn_conv_w[layer],
                    ffn_conv_b[layer].reshape(1, 2 * D_FF), S)
        x2 = _ffn_down(a, ffn_w_down[layer].astype(BF16), x1, ln2_g[layer].reshape(1, D), ln2_b[layer].reshape(1, D))
    return x2.reshape(B, S, D)
```

```python
import functools
import math

import numpy as np
import jax
import jax.numpy as jnp
from jax import lax
from jax.experimental import pallas as pl
from jax.experimental.pallas import tpu as pltpu

F32 = jnp.float32
BF16 = jnp.bfloat16
HI = lax.Precision.HIGHEST

HEADS = 8
HEAD_DIM = 128
KV_GROUPS = 2
HPG = HEADS // KV_GROUPS
DN_CONV = 4
DN_CHUNK = 64
CMP_LEN = 32
CMP_STRIDE = 16
CMP_HIDDEN = 256
SEL_LEN = 64
SEL_TOPK = 16
WINDOW = 512
ROPE_THETA = 500000.0
ROPE_DIM = HEAD_DIM // 4
ROPE_HALF = ROPE_DIM // 2
D_FF = 2816
FFN_CONV = 3
LN_EPS = 1e-5
RMS_EPS = 1e-6
DEEPNORM_ALPHA = 2.0 ** 0.25
ATT_SCALE = HEAD_DIM ** -0.5

LANE = 128
SUBLANE = 8
VMEM_LIMIT = 56 * 1024 * 1024

WIDTH = HEADS * HEAD_DIM
KVW = KV_GROUPS * HEAD_DIM
COL_DQKV = 0
COL_DZ = 3 * WIDTH
COL_NQ = 4 * WIDTH
COL_MG_DN = 5 * WIDTH
COL_MG_NSA = 6 * WIDTH
COL_KC = 7 * WIDTH
COL_VC = COL_KC + KVW
COL_KSL = COL_VC + KVW
COL_VSL = COL_KSL + KVW
COL_KWN = COL_VSL + KVW
COL_VWN = COL_KWN + KVW
COL_SMALL = COL_VWN + KVW
NH = COL_SMALL + LANE
SMALL_BETA = 0
SMALL_DECAY = HEADS
SMALL_GATE = 2 * HEADS

NEG_BIAS = -(2.0 ** 100)


def _cparams(sem):
    return pltpu.CompilerParams(dimension_semantics=sem, vmem_limit_bytes=VMEM_LIMIT)


def _sigmoid(x):
    return 1.0 / (1.0 + jnp.exp(-x))


def _silu(x):
    return x * _sigmoid(x)


def _dot(a, b, precision=None):
    return jnp.dot(a, b, preferred_element_type=F32, precision=precision)


def _dot_nt(a, b):
    return lax.dot_general(a, b, (((1,), (1,)), ((), ())), preferred_element_type=F32)


def _dot_tn(a, b):
    return lax.dot_general(a, b, (((0,), (0,)), ((), ())), preferred_element_type=F32)


def _layer_norm(z, g, b):
    mu = jnp.mean(z, -1, keepdims=True)
    zc = z - mu
    var = jnp.mean(zc * zc, -1, keepdims=True)
    return zc * lax.rsqrt(var + LN_EPS) * g + b


def _inproj_kernel(x_ref, w_ref, o_ref):
    o_ref[...] = _dot(x_ref[...].astype(BF16), w_ref[...])


def _inproj(x2, w):
    T, D = x2.shape
    n_out = w.shape[1]
    tm = 512
    tn = n_out // 3
    return pl.pallas_call(
        _inproj_kernel,
        grid=(n_out // tn, T // tm),
        in_specs=[pl.BlockSpec((tm, D), lambda j, i: (i, 0)),
                  pl.BlockSpec((D, tn), lambda j, i: (0, j))],
        out_specs=pl.BlockSpec((tm, tn), lambda j, i: (i, j)),
        out_shape=jax.ShapeDtypeStruct((T, n_out), F32),
        compiler_params=_cparams(("parallel", "parallel")),
        name="inproj",
    )(x2, w)


def _gdn_prep_kernel(cur_ref, prev_ref, small_ref, cw_ref, alog_ref, dtb_ref,
                     q_ref, k_ref, v_ref, bg_ref, buf_ref):
    i = pl.program_id(1)
    ts = cur_ref.shape[0]
    notfirst = (i > 0).astype(F32)
    buf_ref[0:SUBLANE, :] = prev_ref[...] * notfirst
    buf_ref[SUBLANE:SUBLANE + ts, :] = cur_ref[...]
    for c in range(3 * HEADS):
        sl = slice(c * LANE, (c + 1) * LANE)
        acc = cw_ref[DN_CONV - 1:DN_CONV, sl] * cur_ref[:, sl]
        for s in range(1, DN_CONV):
            acc = acc + cw_ref[DN_CONV - 1 - s:DN_CONV - s, sl] * buf_ref[pl.ds(SUBLANE - s, ts), sl]
        y = _silu(acc)
        if c < 2 * HEADS:
            y = y * lax.rsqrt(jnp.sum(y * y, -1, keepdims=True) + RMS_EPS)
        if c < HEADS:
            q_ref[:, sl] = y * ATT_SCALE
        elif c < 2 * HEADS:
            k_ref[:, slice((c - HEADS) * LANE, (c - HEADS + 1) * LANE)] = y
        else:
            v_ref[:, slice((c - 2 * HEADS) * LANE, (c - 2 * HEADS + 1) * LANE)] = y
    small = small_ref[...]
    beta = _sigmoid(small)
    xs = small + dtb_ref[...]
    softplus = jnp.maximum(xs, 0.0) + jnp.log1p(jnp.exp(-jnp.abs(xs)))
    g = -jnp.exp(alog_ref[...]) * softplus
    lane = lax.broadcasted_iota(jnp.int32, small.shape, 1)
    bg_ref[...] = jnp.where(lane < SMALL_DECAY, beta, g)


def _gdn_prep(h3, conv_w, alog_row, dtb_row):
    B, S, _ = h3.shape
    ts = 256
    wq = 3 * WIDTH
    out_sd = jax.ShapeDtypeStruct((B, S, WIDTH), F32)
    return pl.pallas_call(
        _gdn_prep_kernel,
        grid=(B, S // ts),
        in_specs=[
            pl.BlockSpec((None, ts, wq), lambda b, i: (b, i, 0)),
            pl.BlockSpec((None, SUBLANE, wq), lambda b, i: (b, jnp.maximum(i * (ts // SUBLANE) - 1, 0), 0)),
            pl.BlockSpec((None, ts, LANE), lambda b, i: (b, i, COL_SMALL // LANE)),
            pl.BlockSpec((DN_CONV, wq), lambda b, i: (0, 0)),
            pl.BlockSpec((1, LANE), lambda b, i: (0, 0)),
            pl.BlockSpec((1, LANE), lambda b, i: (0, 0)),
        ],
        out_specs=[
            pl.BlockSpec((None, ts, WIDTH), lambda b, i: (b, i, 0)),
            pl.BlockSpec((None, ts, WIDTH), lambda b, i: (b, i, 0)),
            pl.BlockSpec((None, ts, WIDTH), lambda b, i: (b, i, 0)),
            pl.BlockSpec((None, ts, LANE), lambda b, i: (b, i, 0)),
        ],
        out_shape=[out_sd, out_sd, out_sd, jax.ShapeDtypeStruct((B, S, LANE), F32)],
        scratch_shapes=[pltpu.VMEM((ts + SUBLANE, wq), F32)],
        compiler_params=_cparams(("parallel", "parallel")),
        name="gdn_prep",
    )(h3, h3, h3, conv_w, alog_row, dtb_row)


GDN_CHUNKS_PER_STEP = 2
GDN_BATCH_PER_STEP = 2


GDN_QUAD = 4


def _gdn_kernel(q_ref, k_ref, v_ref, bg_ref, z_ref, nw_ref, y_ref, state_ref):
    n = pl.program_id(1)

    @pl.when(n == 0)
    def _():
        state_ref[...] = jnp.zeros(state_ref.shape, F32)

    C = DN_CHUNK
    QW = GDN_QUAD * C
    QL = GDN_QUAD * LANE
    shift_c = int(math.log2(C))
    r64 = lax.broadcasted_iota(jnp.int32, (C, C), 0)
    c64 = lax.broadcasted_iota(jnp.int32, (C, C), 1)
    tril = jnp.where(r64 >= c64, 1.0, 0.0).astype(F32)
    triu = jnp.where(r64 <= c64, 1.0, 0.0).astype(F32)
    rr = lax.broadcasted_iota(jnp.int32, (C, QW), 0)
    ll = lax.broadcasted_iota(jnp.int32, (C, QW), 1)
    jj = ll & (C - 1)
    hb = ll >> shift_c
    incl_t = rr >= jj
    strict_t = rr > jj
    eye_t = jnp.where(rr == jj, 1.0, 0.0).astype(F32)
    bd_c = jnp.where((lax.broadcasted_iota(jnp.int32, (QW, QW), 0) >> shift_c)
                     == (lax.broadcasted_iota(jnp.int32, (QW, QW), 1) >> shift_c), 1.0, 0.0).astype(BF16)
    bd_l = jnp.where((lax.broadcasted_iota(jnp.int32, (QW, QL), 0) >> shift_c)
                     == (lax.broadcasted_iota(jnp.int32, (QW, QL), 1) >> int(math.log2(LANE))), 1.0, 0.0).astype(BF16)
    zeros_rhs = jnp.zeros((C, 2 * LANE), BF16)
    nw = nw_ref[...]
    nb = q_ref.shape[0]
    quads = [(bb, c, a) for bb in range(nb) for c in range(GDN_CHUNKS_PER_STEP) for a in range(HEADS // GDN_QUAD)]
    gc_cols, gc_rows, bgs = {}, {}, {}
    for bb in range(nb):
        bg = bg_ref[bb]
        bg_t = bg.T
        bgs[bb] = bg
        for c in range(GDN_CHUNKS_PER_STEP):
            rs = slice(c * C, (c + 1) * C)
            gc_cols[bb, c] = _dot(tril, bg[rs, :], HI)
            gc_rows[bb, c] = _dot(bg_t[SMALL_DECAY:SMALL_DECAY + HEADS, rs], triu, HI)
    decay, lower, qk, rhs_bd, qd, kd, eglast = {}, {}, {}, {}, {}, {}, {}
    for u_ in quads:
        bb, c, a = u_
        rs = slice(c * C, (c + 1) * C)
        h0 = a * GDN_QUAD
        gcols = [gc_cols[bb, c][:, SMALL_DECAY + h0 + hh:SMALL_DECAY + h0 + hh + 1] for hh in range(GDN_QUAD)]
        gcol_cat = jnp.broadcast_to(gcols[0], (C, QW))
        for hh in range(1, GDN_QUAD):
            gcol_cat = jnp.where(hb == hh, gcols[hh], gcol_cat)
        grow_cat = jnp.concatenate([gc_rows[bb, c][h0 + hh:h0 + hh + 1, :] for hh in range(GDN_QUAD)], axis=1)
        decay[u_] = jnp.exp(jnp.where(incl_t, gcol_cat - grow_cat, -jnp.inf))
        kq = k_ref[bb, rs, a * QL:(a + 1) * QL]
        qq = q_ref[bb, rs, a * QL:(a + 1) * QL]
        kbs, rows_bd = [], []
        for hh in range(GDN_QUAD):
            h = h0 + hh
            ls = slice(hh * LANE, (hh + 1) * LANE)
            beta = bgs[bb][rs, SMALL_BETA + h:SMALL_BETA + h + 1]
            eg = jnp.exp(gcols[hh])
            glast = gcols[hh][C - 1:C, :]
            kb = kq[:, ls] * beta
            vb = v_ref[bb, rs, h * LANE:(h + 1) * LANE] * beta
            kbs.append(kb)
            rhs_h = jnp.concatenate([vb, kb * eg], axis=1).astype(BF16)
            rows_bd.append(jnp.concatenate([zeros_rhs] * hh + [rhs_h] + [zeros_rhs] * (GDN_QUAD - 1 - hh), axis=1))
            qd[bb, c, h] = (qq[:, ls] * eg).astype(BF16)
            kd[bb, c, h] = (kq[:, ls] * jnp.exp(glast - gcols[hh])).astype(BF16)
            eglast[bb, c, h] = jnp.exp(glast)
        rhs_bd[u_] = jnp.concatenate(rows_bd, axis=0)
        k_bd = jnp.concatenate([kq.astype(BF16)] * GDN_QUAD, axis=0) * bd_l
        lhs = jnp.concatenate([jnp.concatenate(kbs, axis=1), qq], axis=0).astype(BF16)
        aq = _dot_nt(lhs, k_bd)
        lower[u_] = jnp.where(strict_t, aq[0:C] * decay[u_], 0.0)
        qk[u_] = (aq[C:2 * C] * decay[u_]).astype(BF16)

    def block_diag(y_b):
        return jnp.concatenate([y_b] * GDN_QUAD, axis=0) * bd_c

    tinv = {u_: eye_t - lower[u_] for u_ in quads}
    y_b = {u_: lower[u_].astype(BF16) for u_ in quads}
    ypow = {u_: _dot(y_b[u_], block_diag(y_b[u_])) for u_ in quads}
    for it in range(5):
        for u_ in quads:
            yb = ypow[u_].astype(BF16)
            if it < 4:
                r = _dot(jnp.concatenate([tinv[u_].astype(BF16), yb], axis=0), block_diag(yb))
                tinv[u_] = tinv[u_] + r[0:C]
                ypow[u_] = r[C:2 * C]
            else:
                tinv[u_] = tinv[u_] + _dot(tinv[u_].astype(BF16), block_diag(yb))
    uw = {u_: _dot(tinv[u_].astype(BF16), rhs_bd[u_]) for u_ in quads}
    zeros_v = jnp.zeros((C, LANE), BF16)
    chains = [(bb, h) for bb in range(nb) for h in range(HEADS)]
    for c in range(GDN_CHUNKS_PER_STEP):
        rs = slice(c * C, (c + 1) * C)
        states = {s_: state_ref[s_[0] * HEADS + s_[1]] for s_ in chains}
        ws_qs = {}
        for s_ in chains:
            bb, h = s_
            a, hh = divmod(h, GDN_QUAD)
            w_h = uw[bb, c, a][:, hh * 2 * LANE + LANE:(hh + 1) * 2 * LANE]
            ws_qs[s_] = _dot(jnp.concatenate([w_h.astype(BF16), qd[bb, c, h]], axis=0), states[s_].astype(BF16))
        v_new = {}
        for s_ in chains:
            bb, h = s_
            a, hh = divmod(h, GDN_QUAD)
            v_new[s_] = (uw[bb, c, a][:, hh * 2 * LANE:hh * 2 * LANE + LANE] - ws_qs[s_][0:C]).astype(BF16)
        outs = {}
        for s_ in chains:
            bb, h = s_
            a, hh = divmod(h, GDN_QUAD)
            v_pad = jnp.concatenate([zeros_v] * hh + [v_new[s_]] + [zeros_v] * (GDN_QUAD - 1 - hh), axis=0)
            outs[s_] = ws_qs[s_][C:2 * C] + _dot(qk[bb, c, a], v_pad)
            state_ref[bb * HEADS + h] = states[s_] * eglast[bb, c, h] + _dot_tn(kd[bb, c, h], v_new[s_])
        for s_ in chains:
            bb, h = s_
            ls = slice(h * LANE, (h + 1) * LANE)
            o = outs[s_]
            o = o * lax.rsqrt(jnp.mean(o * o, -1, keepdims=True) + RMS_EPS)
            y_ref[bb, rs, ls] = (o * nw * _silu(z_ref[bb, rs, ls])).astype(BF16)


def _gdn(q, k, v, bg, h3, norm_w):
    B, S, _ = q.shape
    rows = GDN_CHUNKS_PER_STEP * DN_CHUNK
    nb = GDN_BATCH_PER_STEP if B % GDN_BATCH_PER_STEP == 0 else 1
    tile = pl.BlockSpec((nb, rows, WIDTH), lambda b, n: (b, n, 0))
    return pl.pallas_call(
        _gdn_kernel,
        grid=(B // nb, S // rows),
        in_specs=[tile, tile, tile,
                  pl.BlockSpec((nb, rows, LANE), lambda b, n: (b, n, 0)),
                  pl.BlockSpec((nb, rows, WIDTH), lambda b, n: (b, n, COL_DZ // WIDTH)),
                  pl.BlockSpec((1, LANE), lambda b, n: (0, 0))],
        out_specs=tile,
        out_shape=jax.ShapeDtypeStruct((B, S, WIDTH), BF16),
        scratch_shapes=[pltpu.VMEM((nb * HEADS, HEAD_DIM, HEAD_DIM), F32)],
        compiler_params=_cparams(("parallel", "arbitrary")),
        name="gdn",
    )(q, k, v, bg, h3, norm_w)


def _rope(t, cosf, sinf, lo):
    partner = jnp.where(lo, pltpu.roll(t, LANE - ROPE_HALF, 1), pltpu.roll(t, ROPE_HALF, 1))
    return t * cosf + partner * sinf


SEL_TK = 256
WIN_TK = 256
LOG2E = math.log2(math.e)


def _nsa_prep_kernel(nq_ref, ksl_ref, vsl_ref, kwn_ref, vwn_ref, cos_ref, sin_ref,
                     q_o, qt_o, kaug_o, vslt_o, kwn_o, vwnt_o):
    i = pl.program_id(1)
    ts = nq_ref.shape[0]
    cosf = cos_ref[...]
    sinf = sin_ref[...]
    lane = lax.broadcasted_iota(jnp.int32, cosf.shape, 1)
    lo = lane < ROPE_HALF
    for h in range(HEADS):
        sl = slice(h * LANE, (h + 1) * LANE)
        qh = _rope(nq_ref[:, sl], cosf, sinf, lo) * ATT_SCALE
        q_o[:, sl] = qh.astype(BF16)
        qt_o[sl, :] = (qh * LOG2E).T.astype(BF16)
    tpos = i * ts + lax.broadcasted_iota(jnp.int32, cosf.shape, 0)
    onehot = jnp.where((tpos >> int(math.log2(SEL_LEN))) == lane, 1.0, 0.0).astype(BF16)
    for g in range(KV_GROUPS):
        sl = slice(g * LANE, (g + 1) * LANE)
        kaug_o[:, 2 * g * LANE:(2 * g + 1) * LANE] = _rope(ksl_ref[:, sl], cosf, sinf, lo).astype(BF16)
        kaug_o[:, (2 * g + 1) * LANE:(2 * g + 2) * LANE] = onehot
        kwn_o[:, sl] = _rope(kwn_ref[:, sl], cosf, sinf, lo).astype(BF16)
        vs_t = vsl_ref[:, sl].T.astype(BF16)
        for c in range(ts // SEL_TK):
            vslt_o[g, c] = vs_t[:, c * SEL_TK:(c + 1) * SEL_TK]
        vw_t = vwn_ref[:, sl].T.astype(BF16)
        for c in range(ts // WIN_TK):
            vwnt_o[g, c] = vw_t[:, c * WIN_TK:(c + 1) * WIN_TK]


def _nsa_prep(h3, cos_tab, sin_tab):
    B, S, _ = h3.shape
    ts = 512

    def kv_spec(col):
        return pl.BlockSpec((None, ts, KVW), lambda b, i: (b, i, col // KVW))

    tab = pl.BlockSpec((ts, LANE), lambda b, i: (i, 0))
    return pl.pallas_call(
        _nsa_prep_kernel,
        grid=(B, S // ts),
        in_specs=[pl.BlockSpec((None, ts, WIDTH), lambda b, i: (b, i, COL_NQ // WIDTH)),
                  kv_spec(COL_KSL), kv_spec(COL_VSL), kv_spec(COL_KWN), kv_spec(COL_VWN), tab, tab],
        out_specs=[pl.BlockSpec((None, ts, WIDTH), lambda b, i: (b, i, 0)),
                   pl.BlockSpec((None, WIDTH, ts), lambda b, i: (b, 0, i)),
                   pl.BlockSpec((None, ts, 2 * KVW), lambda b, i: (b, i, 0)),
                   pl.BlockSpec((None, KV_GROUPS, ts // SEL_TK, LANE, SEL_TK), lambda b, i: (b, 0, i, 0, 0)),
                   pl.BlockSpec((None, ts, KVW), lambda b, i: (b, i, 0)),
                   pl.BlockSpec((None, KV_GROUPS, ts // WIN_TK, LANE, WIN_TK), lambda b, i: (b, 0, i, 0, 0))],
        out_shape=[jax.ShapeDtypeStruct((B, S, WIDTH), BF16),
                   jax.ShapeDtypeStruct((B, WIDTH, S), BF16),
                   jax.ShapeDtypeStruct((B, S, 2 * KVW), BF16),
                   jax.ShapeDtypeStruct((B, KV_GROUPS, S // SEL_TK, LANE, SEL_TK), BF16),
                   jax.ShapeDtypeStruct((B, S, KVW), BF16),
                   jax.ShapeDtypeStruct((B, KV_GROUPS, S // WIN_TK, LANE, WIN_TK), BF16)],
        compiler_params=_cparams(("parallel", "parallel")),
        name="nsa_prep",
    )(h3, h3, h3, h3, h3, cos_tab, sin_tab)


def _compress_one(t_ref, pe_ref, w1_ref, w2_ref, nc):
    half = CMP_LEN // 2
    y1 = jnp.zeros((nc, CMP_HIDDEN), F32)
    y2 = jnp.zeros((nc, CMP_HIDDEN), F32)
    for l in range(half):
        x = t_ref[pl.ds(l, nc, stride=CMP_STRIDE), :]
        y1 = y1 + _dot((x + pe_ref[l:l + 1, :]).astype(BF16), w1_ref[l * LANE:(l + 1) * LANE, :])
        y2 = y2 + _dot((x + pe_ref[half + l:half + l + 1, :]).astype(BF16),
                       w1_ref[(half + l) * LANE:(half + l + 1) * LANE, :])
    y = y1 + pltpu.roll(y2, nc - 1, 0)
    return _dot(_silu(y).astype(BF16), w2_ref[...])


def _compress_kernel(kc_ref, vc_ref, pek_ref, pev_ref, w1k_ref, w2k_ref, w1v_ref, w2v_ref,
                     cos_ref, sin_ref, kc_o, vc_o):
    nc = kc_o.shape[0]
    kc = _compress_one(kc_ref, pek_ref, w1k_ref, w2k_ref, nc)
    lo = lax.broadcasted_iota(jnp.int32, kc.shape, 1) < ROPE_HALF
    kc_o[...] = _rope(kc, cos_ref[...], sin_ref[...], lo).astype(BF16)
    vc_o[...] = _compress_one(vc_ref, pev_ref, w1v_ref, w2v_ref, nc).astype(BF16)


def _compress(h3, pek, pev, w1k, w2k, w1v, w2v, cos_c, sin_c):
    B, S, _ = h3.shape
    nc = S // CMP_STRIDE
    full = lambda shape: pl.BlockSpec(shape, lambda b, g: tuple(0 for _ in shape))
    out_spec = pl.BlockSpec((None, None, nc, LANE), lambda b, g: (b, g, 0, 0))
    out_sd = jax.ShapeDtypeStruct((B, KV_GROUPS, nc, LANE), BF16)
    cin = CMP_LEN * HEAD_DIM
    return pl.pallas_call(
        _compress_kernel,
        grid=(B, KV_GROUPS),
        in_specs=[pl.BlockSpec((None, S, LANE), lambda b, g: (b, 0, COL_KC // LANE + g)),
                  pl.BlockSpec((None, S, LANE), lambda b, g: (b, 0, COL_VC // LANE + g)),
                  full((CMP_LEN, LANE)), full((CMP_LEN, LANE)),
                  full((cin, CMP_HIDDEN)), full((CMP_HIDDEN, LANE)),
                  full((cin, CMP_HIDDEN)), full((CMP_HIDDEN, LANE)),
                  full((nc, LANE)), full((nc, LANE))],
        out_specs=[out_spec, out_spec],
        out_shape=[out_sd, out_sd],
        compiler_params=_cparams(("parallel", "parallel")),
        name="compress",
    )(h3, h3, pek, pev, w1k, w2k, w1v, w2v, cos_c, sin_c)


def _cmp_attn_kernel(q_ref, kc_ref, vc_ref, ov_ref, o_ref, bias_ref, *, n_sel):
    i = pl.program_id(2)
    tq = q_ref.shape[0]
    nc = kc_ref.shape[0]
    t0 = i * tq
    tpos = t0 + lax.broadcasted_iota(jnp.int32, (tq, 1), 0)
    cmp_end = lax.broadcasted_iota(jnp.int32, (1, nc), 1) * CMP_STRIDE + (CMP_LEN - 1)
    mask = cmp_end <= tpos
    kc = kc_ref[...]
    vc = vc_ref[...]
    psum = jnp.zeros((tq, nc), F32)
    for hh in range(HPG):
        sl = slice(hh * LANE, (hh + 1) * LANE)
        s = jnp.where(mask, _dot_nt(q_ref[:, sl], kc), -jnp.inf)
        m = jnp.max(s, -1, keepdims=True)
        m = jnp.where(m > -jnp.inf, m, 0.0)
        p = jnp.exp(s - m)
        p = p / jnp.maximum(jnp.sum(p, -1, keepdims=True), jnp.finfo(F32).tiny)
        o_ref[:, sl] = _dot(p.astype(BF16), vc).astype(BF16)
        psum = psum + p
    ov = ov_ref[...]
    p_hi = psum.astype(BF16)
    r1 = psum - p_hi.astype(F32)
    p_mid = r1.astype(BF16)
    p_lo = (r1 - p_mid.astype(F32)).astype(BF16)
    p_sel = _dot(p_hi, ov) + (_dot(p_mid, ov) + _dot(p_lo, ov))
    jb = lax.broadcasted_iota(jnp.int32, (tq, LANE), 1)
    cur = tpos >> int(math.log2(SEL_LEN))
    valid = jb <= cur
    forced = valid & ((jb == 0) | (jb == cur) | (jb == cur - 1))
    score = jnp.where(forced, jnp.inf, jnp.where(valid, p_sel, -jnp.inf))
    st = score.T
    n_grp = n_sel // SUBLANE
    grp = [st[g * SUBLANE:(g + 1) * SUBLANE, :] for g in range(n_grp)]
    jrow = lax.broadcasted_iota(jnp.int32, (SUBLANE, tq), 0)
    cnt = [jnp.zeros((SUBLANE, tq), F32) for _ in range(n_grp)]
    for b in range(n_sel):
        gb, rb_ = divmod(b, SUBLANE)
        rb = grp[gb][rb_:rb_ + 1, :]
        for g in range(n_grp):
            if g < gb:
                inc = jnp.where(rb > grp[g], 1.0, 0.0)
            elif g > gb:
                inc = jnp.where(rb >= grp[g], 1.0, 0.0)
            else:
                inc = jnp.where(jrow > rb_, jnp.where(rb >= grp[g], 1.0, 0.0), jnp.where(rb > grp[g], 1.0, 0.0))
            cnt[g] = cnt[g] + inc
    rows = [jnp.where((cnt[g] < float(SEL_TOPK)) & (grp[g] > -jnp.inf), 0.0, NEG_BIAS) for g in range(n_grp)]
    if n_sel < LANE:
        rows.append(jnp.zeros((LANE - n_sel, tq), F32))
    bias_ref[...] = jnp.concatenate(rows, axis=0).astype(BF16)


def _cmp_attn(q_r, kc, vc, overlap):
    B, S, _ = q_r.shape
    nc = kc.shape[2]
    n_sel = S // SEL_LEN
    tq = 256
    gw = HPG * LANE
    return pl.pallas_call(
        functools.partial(_cmp_attn_kernel, n_sel=n_sel),
        grid=(B, KV_GROUPS, S // tq),
        in_specs=[pl.BlockSpec((None, tq, gw), lambda b, g, i: (b, i, g)),
                  pl.BlockSpec((None, None, nc, LANE), lambda b, g, i: (b, g, 0, 0)),
                  pl.BlockSpec((None, None, nc, LANE), lambda b, g, i: (b, g, 0, 0)),
                  pl.BlockSpec((nc, LANE), lambda b, g, i: (0, 0))],
        out_specs=[pl.BlockSpec((None, tq, gw), lambda b, g, i: (b, i, g)),
                   pl.BlockSpec((None, None, LANE, tq), lambda b, g, i: (b, g, 0, i))],
        out_shape=[jax.ShapeDtypeStruct((B, S, WIDTH), BF16),
                   jax.ShapeDtypeStruct((B, KV_GROUPS, LANE, S), BF16)],
        compiler_params=_cparams(("parallel", "parallel", "parallel")),
        name="cmp_attn",
    )(q_r, kc, vc, overlap)


def _flash_kernel(*refs, selected, tk):
    if selected:
        q_ref, b_ref, k_ref, vt_ref, o_ref, qs_ref, s_ref, m_ref, l_ref, acc_ref = refs
    else:
        q_ref, k_ref, vt_ref, o_ref, qs_ref, s_ref, m_ref, l_ref, acc_ref = refs
    i = pl.program_id(1)
    tq = q_ref.shape[1]
    cols = HPG * tq
    t0 = i * tq
    kw = qs_ref.shape[1]
    for g in range(KV_GROUPS):
        for hh in range(HPG):
            h = g * HPG + hh
            qs_ref[g, 0:LANE, hh * tq:(hh + 1) * tq] = q_ref[h * LANE:(h + 1) * LANE, :]
            if selected:
                qs_ref[g, LANE:2 * LANE, hh * tq:(hh + 1) * tq] = b_ref[g]
    m_ref[...] = jnp.full(m_ref.shape, -jnp.inf, F32)
    l_ref[...] = jnp.zeros(l_ref.shape, F32)
    acc_ref[...] = jnp.zeros(acc_ref.shape, F32)
    qpos = t0 + (lax.broadcasted_iota(jnp.int32, (1, cols), 1) & (tq - 1))

    def scores(j, slot):
        k0 = pl.multiple_of(j * tk, tk)
        for g in range(KV_GROUPS):
            s_ref[slot, g] = _dot(k_ref[pl.ds(k0, tk), g * kw:(g + 1) * kw], qs_ref[g])

    def update(j, slot, masked):
        if masked:
            kpos = j * tk + lax.broadcasted_iota(jnp.int32, (tk, 1), 0)
            mask = kpos <= qpos
            if not selected:
                mask = mask & (qpos - kpos < WINDOW)
        for g in range(KV_GROUPS):
            s = s_ref[slot, g]
            if masked:
                s = jnp.where(mask, s, -jnp.inf)
            m_old = m_ref[g]
            m_new = jnp.maximum(m_old, jnp.max(s, axis=0, keepdims=True))
            alpha = jnp.exp2(m_old - m_new)
            p = jnp.exp2(s - m_new)
            l_ref[g] = alpha * l_ref[g] + jnp.sum(p, axis=0, keepdims=True)
            acc_ref[g] = alpha * acc_ref[g] + _dot(vt_ref[g, j], p.astype(BF16))
            m_ref[g] = m_new

    jd = t0 // tk
    if selected:
        last = jnp.maximum(jd - 1, 0)
        scores(jd, 0)
        scores(0, 1)
        update(jd, 0, True)

        def body(p, carry):
            scores(jnp.minimum(2 * p + 1, last), 0)
            update(2 * p, 1, False)
            scores(jnp.minimum(2 * p + 2, last), 1)
            update(2 * p + 1, 0, False)
            return carry

        lax.fori_loop(0, jd // 2, body, 0)

        @pl.when(jd % 2 == 1)
        def _():
            update(jd - 1, 1, False)
    else:
        n_full = WINDOW // tk - 1
        scores(jd, 0)
        update(jd, 0, True)

        def full_step(j, carry):
            scores(j, 0)
            update(j, 0, False)
            return carry

        lax.fori_loop(jnp.maximum(jd - n_full, 0), jd, full_step, 0)

        @pl.when(jd - n_full - 1 >= 0)
        def _():
            scores(jd - n_full - 1, 0)
            update(jd - n_full - 1, 0, True)

    for g in range(KV_GROUPS):
        out_t = acc_ref[g] / jnp.maximum(l_ref[g], jnp.finfo(F32).tiny)
        for hh in range(HPG):
            h = g * HPG + hh
            o_ref[:, h * LANE:(h + 1) * LANE] = out_t[:, hh * tq:(hh + 1) * tq].T.astype(BF16)


FLASH_TQ = 256


def _flash(q_t, k_r, v_t, bias_t, *, tk):
    B, _, S = q_t.shape
    tq = FLASH_TQ
    selected = bias_t is not None
    assert tq == tk
    kw = k_r.shape[2] // KV_GROUPS
    in_specs = [pl.BlockSpec((None, WIDTH, tq), lambda b, i: (b, 0, i))]
    args = [q_t]
    if selected:
        in_specs.append(pl.BlockSpec((None, KV_GROUPS, LANE, tq), lambda b, i: (b, 0, 0, i)))
        args.append(bias_t)
    in_specs += [pl.BlockSpec((None, S, KV_GROUPS * kw), lambda b, i: (b, 0, 0)),
                 pl.BlockSpec((None, KV_GROUPS, S // tk, LANE, tk), lambda b, i: (b, 0, 0, 0, 0))]
    args += [k_r, v_t]
    cols = HPG * tq
    return pl.pallas_call(
        functools.partial(_flash_kernel, selected=selected, tk=tk),
        grid=(B, S // tq),
        in_specs=in_specs,
        out_specs=pl.BlockSpec((None, tq, WIDTH), lambda b, i: (b, i, 0)),
        out_shape=jax.ShapeDtypeStruct((B, S, WIDTH), BF16),
        scratch_shapes=[pltpu.VMEM((KV_GROUPS, kw, cols), BF16),
                        pltpu.VMEM((2 if selected else 1, KV_GROUPS, tk, cols), F32),
                        pltpu.VMEM((KV_GROUPS, 1, cols), F32), pltpu.VMEM((KV_GROUPS, 1, cols), F32),
                        pltpu.VMEM((KV_GROUPS, LANE, cols), F32)],
        compiler_params=_cparams(("parallel", "parallel")),
        name="flash_sel" if selected else "flash_win",
    )(*args)


def _merge_kernel(ydn_ref, oc_ref, os_ref, ow_ref, small_ref, mgd_ref, mgn_ref, x_ref,
                  wdn_ref, wnsa_ref, wout_ref, g_ref, b_ref, o_ref):
    gates = _sigmoid(small_ref[...])
    cols = []
    for h in range(HEADS):
        sl = slice(h * LANE, (h + 1) * LANE)
        gc = gates[:, SMALL_GATE + h:SMALL_GATE + h + 1]
        gs = gates[:, SMALL_GATE + HEADS + h:SMALL_GATE + HEADS + h + 1]
        gw = gates[:, SMALL_GATE + 2 * HEADS + h:SMALL_GATE + 2 * HEADS + h + 1]
        cols.append((gc * oc_ref[:, sl] + gs * os_ref[:, sl] + gw * ow_ref[:, sl]).astype(BF16))
    y_nsa = jnp.concatenate(cols, axis=1)
    y = (_sigmoid(mgd_ref[...]) * _dot(ydn_ref[...], wdn_ref[...])
         + _sigmoid(mgn_ref[...]) * _dot(y_nsa, wnsa_ref[...]))
    z = DEEPNORM_ALPHA * x_ref[...] + _dot(y.astype(BF16), wout_ref[...])
    o_ref[...] = _layer_norm(z, g_ref[...], b_ref[...])


def _merge(y_dn2, oc2, os2, ow2, h, x2, wdn, wnsa, wout, g, b):
    T, D = x2.shape
    tm = 256
    row = pl.BlockSpec((tm, WIDTH), lambda i: (i, 0))
    full = lambda shape: pl.BlockSpec(shape, lambda i: (0, 0))
    return pl.pallas_call(
        _merge_kernel,
        grid=(T // tm,),
        in_specs=[row, row, row, row,
                  pl.BlockSpec((tm, LANE), lambda i: (i, COL_SMALL // LANE)),
                  pl.BlockSpec((tm, WIDTH), lambda i: (i, COL_MG_DN // WIDTH)),
                  pl.BlockSpec((tm, WIDTH), lambda i: (i, COL_MG_NSA // WIDTH)),
                  pl.BlockSpec((tm, D), lambda i: (i, 0)),
                  full((WIDTH, D)), full((WIDTH, D)), full((D, D)), full((1, D)), full((1, D))],
        out_specs=pl.BlockSpec((tm, D), lambda i: (i, 0)),
        out_shape=jax.ShapeDtypeStruct((T, D), F32),
        compiler_params=_cparams(("parallel",)),
        name="merge",
    )(y_dn2, oc2, os2, ow2, h, h, h, x2, wdn, wnsa, wout, g, b)


def _ffn_up_kernel(x_ref, xp_ref, wg_ref, wv_ref, cwg_ref, cwv_ref, bg_ref, bv_ref, o_ref, *, tiles_per_seq):
    i = pl.program_id(1)
    tm = x_ref.shape[0]
    notfirst = (i % tiles_per_seq != 0).astype(F32)
    xb = x_ref[...].astype(BF16)
    xpb = (xp_ref[...] * notfirst).astype(BF16)
    row = lax.broadcasted_iota(jnp.int32, (tm, 1), 0)

    def conv(w_ref, cw_ref, b_ref):
        u = _dot(xb, w_ref[...])
        p = _dot(xpb, w_ref[...])
        u1 = jnp.where(row == 0, p[SUBLANE - 1:SUBLANE, :], pltpu.roll(u, 1, 0))
        u2 = jnp.where(row == 0, p[SUBLANE - 2:SUBLANE - 1, :],
                       jnp.where(row == 1, p[SUBLANE - 1:SUBLANE, :], pltpu.roll(u, 2, 0)))
        return cw_ref[0:1, :] * u2 + cw_ref[1:2, :] * u1 + cw_ref[2:3, :] * u + b_ref[...]

    o_ref[...] = (_silu(conv(wg_ref, cwg_ref, bg_ref)) * conv(wv_ref, cwv_ref, bv_ref)).astype(BF16)


def _ffn_up(x1, w_up, conv_w, conv_b, seq):
    T, D = x1.shape
    tm = 512
    nj = 2
    tn = D_FF // nj
    return pl.pallas_call(
        functools.partial(_ffn_up_kernel, tiles_per_seq=seq // tm),
        grid=(nj, T // tm),
        in_specs=[pl.BlockSpec((tm, D), lambda j, i: (i, 0)),
                  pl.BlockSpec((SUBLANE, D), lambda j, i: (jnp.maximum(i * (tm // SUBLANE) - 1, 0), 0)),
                  pl.BlockSpec((D, tn), lambda j, i: (0, j)),
                  pl.BlockSpec((D, tn), lambda j, i: (0, j + nj)),
                  pl.BlockSpec((FFN_CONV, tn), lambda j, i: (0, j)),
                  pl.BlockSpec((FFN_CONV, tn), lambda j, i: (0, j + nj)),
                  pl.BlockSpec((1, tn), lambda j, i: (0, j)),
                  pl.BlockSpec((1, tn), lambda j, i: (0, j + nj))],
        out_specs=pl.BlockSpec((tm, tn), lambda j, i: (i, j)),
        out_shape=jax.ShapeDtypeStruct((T, D_FF), BF16),
        compiler_params=_cparams(("parallel", "parallel")),
        name="ffn_up",
    )(x1, x1, w_up, w_up, conv_w, conv_w, conv_b, conv_b)


def _ffn_down_kernel(a_ref, w_ref, x_ref, g_ref, b_ref, o_ref):
    z = DEEPNORM_ALPHA * x_ref[...] + _dot(a_ref[...], w_ref[...])
    o_ref[...] = _layer_norm(z, g_ref[...], b_ref[...])


def _ffn_down(a, w_down, x1, g, b):
    T, D = x1.shape
    tm = 512
    full = lambda shape: pl.BlockSpec(shape, lambda i: (0, 0))
    return pl.pallas_call(
        _ffn_down_kernel,
        grid=(T // tm,),
        in_specs=[pl.BlockSpec((tm, D_FF), lambda i: (i, 0)), full((D_FF, D)),
                  pl.BlockSpec((tm, D), lambda i: (i, 0)), full((1, D)), full((1, D))],
        out_specs=pl.BlockSpec((tm, D), lambda i: (i, 0)),
        out_shape=jax.ShapeDtypeStruct((T, D), F32),
        compiler_params=_cparams(("parallel",)),
        name="ffn_down",
    )(a, w_down, x1, g, b)


def _reorder_w_in(w):
    d = w.shape[0]
    o_db = 4 * WIDTH
    o_nq = o_db + 2 * HEADS
    o_kv = o_nq + WIDTH
    o_gate = o_kv + 6 * KVW
    o_mg = o_gate + 3 * HEADS
    pad = jnp.zeros((d, LANE - 5 * HEADS), w.dtype)
    return jnp.concatenate([
        w[:, 0:o_db], w[:, o_nq:o_kv], w[:, o_mg:o_mg + 2 * WIDTH], w[:, o_kv:o_gate],
        w[:, o_db:o_nq], w[:, o_gate:o_mg], pad], axis=1).astype(BF16)


def _rope_tables(pos):
    inv_freq = ROPE_THETA ** (-jnp.arange(ROPE_HALF, dtype=F32) / ROPE_HALF)
    ang = pos.astype(F32)[:, None] * inv_freq
    cos = jnp.cos(ang)
    sin = jnp.sin(ang)
    p = pos.shape[0]
    cosf = jnp.concatenate([cos, cos, jnp.ones((p, LANE - ROPE_DIM), F32)], axis=1)
    sinf = jnp.concatenate([-sin, sin, jnp.zeros((p, LANE - ROPE_DIM), F32)], axis=1)
    return cosf, sinf


def _overlap_table(nc, n_sel):
    n = np.arange(nc)[:, None]
    j = np.arange(LANE)[None, :]
    ov = (n * CMP_STRIDE <= j * SEL_LEN + SEL_LEN - 1) & (n * CMP_STRIDE + CMP_LEN - 1 >= j * SEL_LEN)
    ov = ov & (j < n_sel) & (n < nc - 1)
    return jnp.asarray(ov.astype(np.float32)).astype(BF16)


def _lane_row(vals, offset):
    return jnp.zeros((1, LANE), F32).at[0, offset:offset + vals.shape[0]].set(vals.astype(F32))


def kernel(x, w_in, dn_conv_w, dn_a_log, dn_dt_bias, dn_norm_w, cmp_pos_k, cmp_pos_v, cmp_k_w1, cmp_k_w2,
           cmp_v_w1, cmp_v_w2, w_branch_dn, w_branch_nsa, w_out, ln1_g, ln1_b, ffn_w_up, ffn_conv_w,
           ffn_conv_b, ffn_w_down, ln2_g, ln2_b):
    B, S, D = x.shape
    T = B * S
    nc = S // CMP_STRIDE
    n_sel = S // SEL_LEN
    assert D == WIDTH and S % 512 == 0 and n_sel <= LANE and n_sel % SUBLANE == 0
    cos_t, sin_t = _rope_tables(jnp.arange(S))
    cos_c, sin_c = _rope_tables(jnp.arange(nc) * CMP_STRIDE + (CMP_LEN - 1))
    overlap = _overlap_table(nc, n_sel)
    x2 = x.reshape(T, D)
    for layer in range(w_in.shape[0]):
        h = _inproj(x2, _reorder_w_in(w_in[layer]))
        h3 = h.reshape(B, S, NH)

        q, k, v, bg = _gdn_prep(h3, dn_conv_w[layer], _lane_row(dn_a_log[layer], SMALL_DECAY),
                                _lane_row(dn_dt_bias[layer], SMALL_DECAY))
        y_dn = _gdn(q, k, v, bg, h3, dn_norm_w[layer].reshape(1, LANE))

        q_r, q_t, kaug, vsl_t, kwn_r, vwn_t = _nsa_prep(h3, cos_t, sin_t)
        kc, vc = _compress(h3, cmp_pos_k[layer], cmp_pos_v[layer],
                           cmp_k_w1[layer].astype(BF16), cmp_k_w2[layer].astype(BF16),
                           cmp_v_w1[layer].astype(BF16), cmp_v_w2[layer].astype(BF16), cos_c, sin_c)
        o_cmp, bias = _cmp_attn(q_r, kc, vc, overlap)
        o_sel = _flash(q_t, kaug, vsl_t, bias, tk=SEL_TK)
        o_win = _flash(q_t, kwn_r, vwn_t, None, tk=WIN_TK)

        x1 = _merge(y_dn.reshape(T, WIDTH), o_cmp.reshape(T, WIDTH), o_sel.reshape(T, WIDTH),
                    o_win.reshape(T, WIDTH), h, x2,
                    w_branch_dn[layer].astype(BF16), w_branch_nsa[layer].astype(BF16),
                    w_out[layer].astype(BF16), ln1_g[layer].reshape(1, D), ln1_b[layer].reshape(1, D))
        a = _ffn_up(x1, ffn_w_up[layer].astype(BF16), ffn_conv_w[layer],
                    ffn_conv_b[layer].reshape(1, 2 * D_FF), S)
        x2 = _ffn_down(a, ffn_w_down[layer].astype(BF16), x1, ln2_g[layer].reshape(1, D), ln2_b[layer].reshape(1, D))
    return x2.reshape(B, S, D)
```

```python
import functools
import math

import numpy as np
import jax
import jax.numpy as jnp
from jax import lax
from jax.experimental import pallas as pl
from jax.experimental.pallas import tpu as pltpu

F32 = jnp.float32
BF16 = jnp.bfloat16
HI = lax.Precision.HIGHEST

HEADS = 8
HEAD_DIM = 128
KV_GROUPS = 2
HPG = HEADS // KV_GROUPS
DN_CONV = 4
DN_CHUNK = 64
CMP_LEN = 32
CMP_STRIDE = 16
CMP_HIDDEN = 256
SEL_LEN = 64
SEL_TOPK = 16
WINDOW = 512
ROPE_THETA = 500000.0
ROPE_DIM = HEAD_DIM // 4
ROPE_HALF = ROPE_DIM // 2
D_FF = 2816
FFN_CONV = 3
LN_EPS = 1e-5
RMS_EPS = 1e-6
DEEPNORM_ALPHA = 2.0 ** 0.25
ATT_SCALE = HEAD_DIM ** -0.5

LANE = 128
SUBLANE = 8
VMEM_LIMIT = 56 * 1024 * 1024

WIDTH = HEADS * HEAD_DIM
KVW = KV_GROUPS * HEAD_DIM
NA = 3 * WIDTH + LANE
SMALL_BETA = 0
SMALL_DECAY = HEADS
SMALL_GATE = 2 * HEADS
NB = WIDTH + 4 * KVW
NC_COLS = 3 * WIDTH + 2 * KVW
COL_DZ = 0
COL_MG_DN = WIDTH
COL_MG_NSA = 2 * WIDTH
COL_KC = 3 * WIDTH
COL_VC = COL_KC + KVW

NEG_BIAS = -(2.0 ** 100)


def _cparams(sem):
    return pltpu.CompilerParams(dimension_semantics=sem, vmem_limit_bytes=VMEM_LIMIT)


def _sigmoid(x):
    return 1.0 / (1.0 + jnp.exp(-x))


def _silu(x):
    return x * _sigmoid(x)


def _dot(a, b, precision=None):
    return jnp.dot(a, b, preferred_element_type=F32, precision=precision)


def _dot_nt(a, b):
    return lax.dot_general(a, b, (((1,), (1,)), ((), ())), preferred_element_type=F32)


def _dot_tn(a, b):
    return lax.dot_general(a, b, (((0,), (0,)), ((), ())), preferred_element_type=F32)


def _layer_norm(z, g, b):
    mu = jnp.mean(z, -1, keepdims=True)
    zc = z - mu
    var = jnp.mean(zc * zc, -1, keepdims=True)
    return zc * lax.rsqrt(var + LN_EPS) * g + b


PROJ_TM = 512


def _proj_rest_kernel(x_ref, w_ref, o_ref):
    o_ref[...] = _dot(x_ref[...].astype(BF16), w_ref[...])


def _proj_rest(x3, w):
    B, S, D = x3.shape
    n_out = w.shape[1]
    tm = PROJ_TM
    return pl.pallas_call(
        _proj_rest_kernel,
        grid=(B, S // tm),
        in_specs=[pl.BlockSpec((None, tm, D), lambda b, i: (b, i, 0)),
                  pl.BlockSpec((D, n_out), lambda b, i: (0, 0))],
        out_specs=pl.BlockSpec((None, tm, n_out), lambda b, i: (b, i, 0)),
        out_shape=jax.ShapeDtypeStruct((B, S, n_out), F32),
        compiler_params=_cparams(("parallel", "parallel")),
        name="proj_rest",
    )(x3, w)


def _proj_gdn_kernel(x_ref, w_ref, cw_ref, alog_ref, dtb_ref, q_ref, k_ref, v_ref, bg_ref, small_ref, buf_ref):
    i = pl.program_id(1)
    ts = x_ref.shape[0]
    wq = 3 * WIDTH

    @pl.when(i == 0)
    def _():
        buf_ref[0:SUBLANE, :] = jnp.zeros((SUBLANE, wq), F32)

    h = _dot(x_ref[...].astype(BF16), w_ref[...])
    buf_ref[SUBLANE:SUBLANE + ts, :] = h[:, 0:wq]
    small = h[:, wq:wq + LANE]
    for c in range(3 * HEADS):
        sl = slice(c * LANE, (c + 1) * LANE)
        acc = cw_ref[DN_CONV - 1:DN_CONV, sl] * buf_ref[SUBLANE:SUBLANE + ts, sl]
        for s in range(1, DN_CONV):
            acc = acc + cw_ref[DN_CONV - 1 - s:DN_CONV - s, sl] * buf_ref[pl.ds(SUBLANE - s, ts), sl]
        y = _silu(acc)
        if c < 2 * HEADS:
            y = y * lax.rsqrt(jnp.sum(y * y, -1, keepdims=True) + RMS_EPS)
        if c < HEADS:
            q_ref[:, sl] = y * ATT_SCALE
        elif c < 2 * HEADS:
            k_ref[:, slice((c - HEADS) * LANE, (c - HEADS + 1) * LANE)] = y
        else:
            v_ref[:, slice((c - 2 * HEADS) * LANE, (c - 2 * HEADS + 1) * LANE)] = y
    buf_ref[0:SUBLANE, :] = buf_ref[ts:ts + SUBLANE, :]
    small_ref[...] = small
    beta = _sigmoid(small)
    xs = small + dtb_ref[...]
    softplus = jnp.maximum(xs, 0.0) + jnp.log1p(jnp.exp(-jnp.abs(xs)))
    g = -jnp.exp(alog_ref[...]) * softplus
    lane = lax.broadcasted_iota(jnp.int32, small.shape, 1)
    bg_ref[...] = jnp.where(lane < SMALL_DECAY, beta, g)


def _proj_gdn(x3, w_a, conv_w, alog_row, dtb_row):
    B, S, D = x3.shape
    ts = PROJ_TM
    wq = 3 * WIDTH
    out_sd = jax.ShapeDtypeStruct((B, S, WIDTH), F32)
    small_sd = jax.ShapeDtypeStruct((B, S, LANE), F32)
    wide = pl.BlockSpec((None, ts, WIDTH), lambda b, i: (b, i, 0))
    narrow = pl.BlockSpec((None, ts, LANE), lambda b, i: (b, i, 0))
    return pl.pallas_call(
        _proj_gdn_kernel,
        grid=(B, S // ts),
        in_specs=[
            pl.BlockSpec((None, ts, D), lambda b, i: (b, i, 0)),
            pl.BlockSpec((D, NA), lambda b, i: (0, 0)),
            pl.BlockSpec((DN_CONV, wq), lambda b, i: (0, 0)),
            pl.BlockSpec((1, LANE), lambda b, i: (0, 0)),
            pl.BlockSpec((1, LANE), lambda b, i: (0, 0)),
        ],
        out_specs=[wide, wide, wide, narrow, narrow],
        out_shape=[out_sd, out_sd, out_sd, small_sd, small_sd],
        scratch_shapes=[pltpu.VMEM((ts + SUBLANE, wq), F32)],
        compiler_params=_cparams(("parallel", "arbitrary")),
        name="proj_gdn",
    )(x3, w_a, conv_w, alog_row, dtb_row)


GDN_CHUNKS_PER_STEP = 2
GDN_BATCH_PER_STEP = 2


GDN_QUAD = 4


def _gdn_kernel(q_ref, k_ref, v_ref, bg_ref, z_ref, nw_ref, y_ref, state_ref):
    n = pl.program_id(1)

    @pl.when(n == 0)
    def _():
        state_ref[...] = jnp.zeros(state_ref.shape, F32)

    C = DN_CHUNK
    QW = GDN_QUAD * C
    QL = GDN_QUAD * LANE
    shift_c = int(math.log2(C))
    r64 = lax.broadcasted_iota(jnp.int32, (C, C), 0)
    c64 = lax.broadcasted_iota(jnp.int32, (C, C), 1)
    tril = jnp.where(r64 >= c64, 1.0, 0.0).astype(F32)
    triu = jnp.where(r64 <= c64, 1.0, 0.0).astype(F32)
    rr = lax.broadcasted_iota(jnp.int32, (C, QW), 0)
    ll = lax.broadcasted_iota(jnp.int32, (C, QW), 1)
    jj = ll & (C - 1)
    hb = ll >> shift_c
    incl_t = rr >= jj
    strict_t = rr > jj
    eye_t = jnp.where(rr == jj, 1.0, 0.0).astype(F32)
    bd_c = jnp.where((lax.broadcasted_iota(jnp.int32, (QW, QW), 0) >> shift_c)
                     == (lax.broadcasted_iota(jnp.int32, (QW, QW), 1) >> shift_c), 1.0, 0.0).astype(BF16)
    bd_l = jnp.where((lax.broadcasted_iota(jnp.int32, (QW, QL), 0) >> shift_c)
                     == (lax.broadcasted_iota(jnp.int32, (QW, QL), 1) >> int(math.log2(LANE))), 1.0, 0.0).astype(BF16)
    zeros_rhs = jnp.zeros((C, 2 * LANE), BF16)
    nw = nw_ref[...]
    nb = q_ref.shape[0]
    quads = [(bb, c, a) for bb in range(nb) for c in range(GDN_CHUNKS_PER_STEP) for a in range(HEADS // GDN_QUAD)]
    gc_cols, gc_rows, bgs = {}, {}, {}
    for bb in range(nb):
        bg = bg_ref[bb]
        bg_t = bg.T
        bgs[bb] = bg
        for c in range(GDN_CHUNKS_PER_STEP):
            rs = slice(c * C, (c + 1) * C)
            gc_cols[bb, c] = _dot(tril, bg[rs, :], HI)
            gc_rows[bb, c] = _dot(bg_t[SMALL_DECAY:SMALL_DECAY + HEADS, rs], triu, HI)
    decay, lower, qk, rhs_bd, qd, kd, eglast = {}, {}, {}, {}, {}, {}, {}
    for u_ in quads:
        bb, c, a = u_
        rs = slice(c * C, (c + 1) * C)
        h0 = a * GDN_QUAD
        gcols = [gc_cols[bb, c][:, SMALL_DECAY + h0 + hh:SMALL_DECAY + h0 + hh + 1] for hh in range(GDN_QUAD)]
        gcol_cat = jnp.broadcast_to(gcols[0], (C, QW))
        for hh in range(1, GDN_QUAD):
            gcol_cat = jnp.where(hb == hh, gcols[hh], gcol_cat)
        grow_cat = jnp.concatenate([gc_rows[bb, c][h0 + hh:h0 + hh + 1, :] for hh in range(GDN_QUAD)], axis=1)
        decay[u_] = jnp.exp(jnp.where(incl_t, gcol_cat - grow_cat, -jnp.inf))
        kq = k_ref[bb, rs, a * QL:(a + 1) * QL]
        qq = q_ref[bb, rs, a * QL:(a + 1) * QL]
        kbs, rows_bd = [], []
        for hh in range(GDN_QUAD):
            h = h0 + hh
            ls = slice(hh * LANE, (hh + 1) * LANE)
            beta = bgs[bb][rs, SMALL_BETA + h:SMALL_BETA + h + 1]
            eg = jnp.exp(gcols[hh])
            glast = gcols[hh][C - 1:C, :]
            kb = kq[:, ls] * beta
            vb = v_ref[bb, rs, h * LANE:(h + 1) * LANE] * beta
            kbs.append(kb)
            rhs_h = jnp.concatenate([vb, kb * eg], axis=1).astype(BF16)
            rows_bd.append(jnp.concatenate([zeros_rhs] * hh + [rhs_h] + [zeros_rhs] * (GDN_QUAD - 1 - hh), axis=1))
            qd[bb, c, h] = (qq[:, ls] * eg).astype(BF16)
            kd[bb, c, h] = (kq[:, ls] * jnp.exp(glast - gcols[hh])).astype(BF16)
            eglast[bb, c, h] = jnp.exp(glast)
        rhs_bd[u_] = jnp.concatenate(rows_bd, axis=0)
        k_bd = jnp.concatenate([kq.astype(BF16)] * GDN_QUAD, axis=0) * bd_l
        lhs = jnp.concatenate([jnp.concatenate(kbs, axis=1), qq], axis=0).astype(BF16)
        aq = _dot_nt(lhs, k_bd)
        lower[u_] = jnp.where(strict_t, aq[0:C] * decay[u_], 0.0)
        qk[u_] = (aq[C:2 * C] * decay[u_]).astype(BF16)

    def block_diag(y_b):
        return jnp.concatenate([y_b] * GDN_QUAD, axis=0) * bd_c

    tinv = {u_: eye_t - lower[u_] for u_ in quads}
    y_b = {u_: lower[u_].astype(BF16) for u_ in quads}
    ypow = {u_: _dot(y_b[u_], block_diag(y_b[u_])) for u_ in quads}
    for it in range(5):
        for u_ in quads:
            yb = ypow[u_].astype(BF16)
            if it < 4:
                r = _dot(jnp.concatenate([tinv[u_].astype(BF16), yb], axis=0), block_diag(yb))
                tinv[u_] = tinv[u_] + r[0:C]
                ypow[u_] = r[C:2 * C]
            else:
                tinv[u_] = tinv[u_] + _dot(tinv[u_].astype(BF16), block_diag(yb))
    uw = {u_: _dot(tinv[u_].astype(BF16), rhs_bd[u_]) for u_ in quads}
    zeros_v = jnp.zeros((C, LANE), BF16)
    chains = [(bb, h) for bb in range(nb) for h in range(HEADS)]
    for c in range(GDN_CHUNKS_PER_STEP):
        rs = slice(c * C, (c + 1) * C)
        states = {s_: state_ref[s_[0] * HEADS + s_[1]] for s_ in chains}
        ws_qs = {}
        for s_ in chains:
            bb, h = s_
            a, hh = divmod(h, GDN_QUAD)
            w_h = uw[bb, c, a][:, hh * 2 * LANE + LANE:(hh + 1) * 2 * LANE]
            ws_qs[s_] = _dot(jnp.concatenate([w_h.astype(BF16), qd[bb, c, h]], axis=0), states[s_].astype(BF16))
        v_new = {}
        for s_ in chains:
            bb, h = s_
            a, hh = divmod(h, GDN_QUAD)
            v_new[s_] = (uw[bb, c, a][:, hh * 2 * LANE:hh * 2 * LANE + LANE] - ws_qs[s_][0:C]).astype(BF16)
        outs = {}
        for s_ in chains:
            bb, h = s_
            a, hh = divmod(h, GDN_QUAD)
            v_pad = jnp.concatenate([zeros_v] * hh + [v_new[s_]] + [zeros_v] * (GDN_QUAD - 1 - hh), axis=0)
            outs[s_] = ws_qs[s_][C:2 * C] + _dot(qk[bb, c, a], v_pad)
            state_ref[bb * HEADS + h] = states[s_] * eglast[bb, c, h] + _dot_tn(kd[bb, c, h], v_new[s_])
        for s_ in chains:
            bb, h = s_
            ls = slice(h * LANE, (h + 1) * LANE)
            o = outs[s_]
            o = o * lax.rsqrt(jnp.mean(o * o, -1, keepdims=True) + RMS_EPS)
            y_ref[bb, rs, ls] = (o * nw * _silu(z_ref[bb, rs, ls])).astype(BF16)


def _gdn(q, k, v, bg, h3, norm_w):
    B, S, _ = q.shape
    rows = GDN_CHUNKS_PER_STEP * DN_CHUNK
    nb = GDN_BATCH_PER_STEP if B % GDN_BATCH_PER_STEP == 0 else 1
    tile = pl.BlockSpec((nb, rows, WIDTH), lambda b, n: (b, n, 0))
    return pl.pallas_call(
        _gdn_kernel,
        grid=(B // nb, S // rows),
        in_specs=[tile, tile, tile,
                  pl.BlockSpec((nb, rows, LANE), lambda b, n: (b, n, 0)),
                  pl.BlockSpec((nb, rows, WIDTH), lambda b, n: (b, n, COL_DZ // WIDTH)),
                  pl.BlockSpec((1, LANE), lambda b, n: (0, 0))],
        out_specs=tile,
        out_shape=jax.ShapeDtypeStruct((B, S, WIDTH), BF16),
        scratch_shapes=[pltpu.VMEM((nb * HEADS, HEAD_DIM, HEAD_DIM), F32)],
        compiler_params=_cparams(("parallel", "arbitrary")),
        name="gdn",
    )(q, k, v, bg, h3, norm_w)


def _rope(t, cosf, sinf, lo):
    partner = jnp.where(lo, pltpu.roll(t, LANE - ROPE_HALF, 1), pltpu.roll(t, ROPE_HALF, 1))
    return t * cosf + partner * sinf


SEL_TK = 256
WIN_TK = 256
LOG2E = math.log2(math.e)


def _proj_nsa_kernel(x_ref, w_ref, cos_ref, sin_ref, q_o, qt_o, kaug_o, vslt_o, kwn_o, vwnt_o):
    i = pl.program_id(1)
    ts = x_ref.shape[0]
    h = _dot(x_ref[...].astype(BF16), w_ref[...])
    c_ksl, c_vsl, c_kwn, c_vwn = WIDTH, WIDTH + KVW, WIDTH + 2 * KVW, WIDTH + 3 * KVW
    cosf = cos_ref[...]
    sinf = sin_ref[...]
    lane = lax.broadcasted_iota(jnp.int32, cosf.shape, 1)
    lo = lane < ROPE_HALF
    for hd in range(HEADS):
        sl = slice(hd * LANE, (hd + 1) * LANE)
        qh = _rope(h[:, sl], cosf, sinf, lo) * ATT_SCALE
        q_o[:, sl] = qh.astype(BF16)
        qt_o[sl, :] = (qh * LOG2E).T.astype(BF16)
    tpos = i * ts + lax.broadcasted_iota(jnp.int32, cosf.shape, 0)
    onehot = jnp.where((tpos >> int(math.log2(SEL_LEN))) == lane, 1.0, 0.0).astype(BF16)
    for g in range(KV_GROUPS):
        sl = slice(g * LANE, (g + 1) * LANE)
        off = g * LANE
        kaug_o[:, 2 * g * LANE:(2 * g + 1) * LANE] = _rope(h[:, c_ksl + off:c_ksl + off + LANE], cosf, sinf, lo).astype(BF16)
        kaug_o[:, (2 * g + 1) * LANE:(2 * g + 2) * LANE] = onehot
        kwn_o[:, sl] = _rope(h[:, c_kwn + off:c_kwn + off + LANE], cosf, sinf, lo).astype(BF16)
        vs_t = h[:, c_vsl + off:c_vsl + off + LANE].T.astype(BF16)
        for c in range(ts // SEL_TK):
            vslt_o[g, c] = vs_t[:, c * SEL_TK:(c + 1) * SEL_TK]
        vw_t = h[:, c_vwn + off:c_vwn + off + LANE].T.astype(BF16)
        for c in range(ts // WIN_TK):
            vwnt_o[g, c] = vw_t[:, c * WIN_TK:(c + 1) * WIN_TK]


def _proj_nsa(x3, w_b, cos_tab, sin_tab):
    B, S, D = x3.shape
    ts = PROJ_TM
    tab = pl.BlockSpec((ts, LANE), lambda b, i: (i, 0))
    return pl.pallas_call(
        _proj_nsa_kernel,
        grid=(B, S // ts),
        in_specs=[pl.BlockSpec((None, ts, D), lambda b, i: (b, i, 0)),
                  pl.BlockSpec((D, NB), lambda b, i: (0, 0)), tab, tab],
        out_specs=[pl.BlockSpec((None, ts, WIDTH), lambda b, i: (b, i, 0)),
                   pl.BlockSpec((None, WIDTH, ts), lambda b, i: (b, 0, i)),
                   pl.BlockSpec((None, ts, 2 * KVW), lambda b, i: (b, i, 0)),
                   pl.BlockSpec((None, KV_GROUPS, ts // SEL_TK, LANE, SEL_TK), lambda b, i: (b, 0, i, 0, 0)),
                   pl.BlockSpec((None, ts, KVW), lambda b, i: (b, i, 0)),
                   pl.BlockSpec((None, KV_GROUPS, ts // WIN_TK, LANE, WIN_TK), lambda b, i: (b, 0, i, 0, 0))],
        out_shape=[jax.ShapeDtypeStruct((B, S, WIDTH), BF16),
                   jax.ShapeDtypeStruct((B, WIDTH, S), BF16),
                   jax.ShapeDtypeStruct((B, S, 2 * KVW), BF16),
                   jax.ShapeDtypeStruct((B, KV_GROUPS, S // SEL_TK, LANE, SEL_TK), BF16),
                   jax.ShapeDtypeStruct((B, S, KVW), BF16),
                   jax.ShapeDtypeStruct((B, KV_GROUPS, S // WIN_TK, LANE, WIN_TK), BF16)],
        compiler_params=_cparams(("parallel", "parallel")),
        name="proj_nsa",
    )(x3, w_b, cos_tab, sin_tab)


def _compress_one(t_ref, pe_ref, w1_ref, w2_ref, nc):
    half = CMP_LEN // 2
    y1 = jnp.zeros((nc, CMP_HIDDEN), F32)
    y2 = jnp.zeros((nc, CMP_HIDDEN), F32)
    for l in range(half):
        x = t_ref[pl.ds(l, nc, stride=CMP_STRIDE), :]
        y1 = y1 + _dot((x + pe_ref[l:l + 1, :]).astype(BF16), w1_ref[l * LANE:(l + 1) * LANE, :])
        y2 = y2 + _dot((x + pe_ref[half + l:half + l + 1, :]).astype(BF16),
                       w1_ref[(half + l) * LANE:(half + l + 1) * LANE, :])
    y = y1 + pltpu.roll(y2, nc - 1, 0)
    return _dot(_silu(y).astype(BF16), w2_ref[...])


def _compress_kernel(kc_ref, vc_ref, pek_ref, pev_ref, w1k_ref, w2k_ref, w1v_ref, w2v_ref,
                     cos_ref, sin_ref, kc_o, vc_o):
    nc = kc_o.shape[0]
    kc = _compress_one(kc_ref, pek_ref, w1k_ref, w2k_ref, nc)
    lo = lax.broadcasted_iota(jnp.int32, kc.shape, 1) < ROPE_HALF
    kc_o[...] = _rope(kc, cos_ref[...], sin_ref[...], lo).astype(BF16)
    vc_o[...] = _compress_one(vc_ref, pev_ref, w1v_ref, w2v_ref, nc).astype(BF16)


def _compress(h3, pek, pev, w1k, w2k, w1v, w2v, cos_c, sin_c):
    B, S, _ = h3.shape
    nc = S // CMP_STRIDE
    full = lambda shape: pl.BlockSpec(shape, lambda b, g: tuple(0 for _ in shape))
    out_spec = pl.BlockSpec((None, None, nc, LANE), lambda b, g: (b, g, 0, 0))
    out_sd = jax.ShapeDtypeStruct((B, KV_GROUPS, nc, LANE), BF16)
    cin = CMP_LEN * HEAD_DIM
    return pl.pallas_call(
        _compress_kernel,
        grid=(B, KV_GROUPS),
        in_specs=[pl.BlockSpec((None, S, LANE), lambda b, g: (b, 0, COL_KC // LANE + g)),
                  pl.BlockSpec((None, S, LANE), lambda b, g: (b, 0, COL_VC // LANE + g)),
                  full((CMP_LEN, LANE)), full((CMP_LEN, LANE)),
                  full((cin, CMP_HIDDEN)), full((CMP_HIDDEN, LANE)),
                  full((cin, CMP_HIDDEN)), full((CMP_HIDDEN, LANE)),
                  full((nc, LANE)), full((nc, LANE))],
        out_specs=[out_spec, out_spec],
        out_shape=[out_sd, out_sd],
        compiler_params=_cparams(("parallel", "parallel")),
        name="compress",
    )(h3, h3, pek, pev, w1k, w2k, w1v, w2v, cos_c, sin_c)


def _cmp_attn_kernel(q_ref, kc_ref, vc_ref, ov_ref, o_ref, bias_ref, *, n_sel):
    i = pl.program_id(2)
    tq = q_ref.shape[0]
    nc = kc_ref.shape[0]
    t0 = i * tq
    tpos = t0 + lax.broadcasted_iota(jnp.int32, (tq, 1), 0)
    cmp_end = lax.broadcasted_iota(jnp.int32, (1, nc), 1) * CMP_STRIDE + (CMP_LEN - 1)
    mask = cmp_end <= tpos
    kc = kc_ref[...]
    vc = vc_ref[...]
    psum = jnp.zeros((tq, nc), F32)
    for hh in range(HPG):
        sl = slice(hh * LANE, (hh + 1) * LANE)
        s = jnp.where(mask, _dot_nt(q_ref[:, sl], kc), -jnp.inf)
        m = jnp.max(s, -1, keepdims=True)
        m = jnp.where(m > -jnp.inf, m, 0.0)
        p = jnp.exp(s - m)
        p = p / jnp.maximum(jnp.sum(p, -1, keepdims=True), jnp.finfo(F32).tiny)
        o_ref[:, sl] = _dot(p.astype(BF16), vc).astype(BF16)
        psum = psum + p
    ov = ov_ref[...]
    p_hi = psum.astype(BF16)
    r1 = psum - p_hi.astype(F32)
    p_mid = r1.astype(BF16)
    p_lo = (r1 - p_mid.astype(F32)).astype(BF16)
    p_sel = _dot(p_hi, ov) + (_dot(p_mid, ov) + _dot(p_lo, ov))
    jb = lax.broadcasted_iota(jnp.int32, (tq, LANE), 1)
    cur = tpos >> int(math.log2(SEL_LEN))
    valid = jb <= cur
    forced = valid & ((jb == 0) | (jb == cur) | (jb == cur - 1))
    score = jnp.where(forced, jnp.inf, jnp.where(valid, p_sel, -jnp.inf))
    st = score.T
    n_grp = n_sel // SUBLANE
    grp = [st[g * SUBLANE:(g + 1) * SUBLANE, :] for g in range(n_grp)]
    jrow = lax.broadcasted_iota(jnp.int32, (SUBLANE, tq), 0)
    cnt = [jnp.zeros((SUBLANE, tq), F32) for _ in range(n_grp)]
    for b in range(n_sel):
        gb, rb_ = divmod(b, SUBLANE)
        rb = grp[gb][rb_:rb_ + 1, :]
        for g in range(n_grp):
            if g < gb:
                inc = jnp.where(rb > grp[g], 1.0, 0.0)
            elif g > gb:
                inc = jnp.where(rb >= grp[g], 1.0, 0.0)
            else:
                inc = jnp.where(jrow > rb_, jnp.where(rb >= grp[g], 1.0, 0.0), jnp.where(rb > grp[g], 1.0, 0.0))
            cnt[g] = cnt[g] + inc
    rows = [jnp.where((cnt[g] < float(SEL_TOPK)) & (grp[g] > -jnp.inf), 0.0, NEG_BIAS) for g in range(n_grp)]
    if n_sel < LANE:
        rows.append(jnp.zeros((LANE - n_sel, tq), F32))
    bias_ref[...] = jnp.concatenate(rows, axis=0).astype(BF16)


def _cmp_attn(q_r, kc, vc, overlap):
    B, S, _ = q_r.shape
    nc = kc.shape[2]
    n_sel = S // SEL_LEN
    tq = 256
    gw = HPG * LANE
    return pl.pallas_call(
        functools.partial(_cmp_attn_kernel, n_sel=n_sel),
        grid=(B, KV_GROUPS, S // tq),
        in_specs=[pl.BlockSpec((None, tq, gw), lambda b, g, i: (b, i, g)),
                  pl.BlockSpec((None, None, nc, LANE), lambda b, g, i: (b, g, 0, 0)),
                  pl.BlockSpec((None, None, nc, LANE), lambda b, g, i: (b, g, 0, 0)),
                  pl.BlockSpec((nc, LANE), lambda b, g, i: (0, 0))],
        out_specs=[pl.BlockSpec((None, tq, gw), lambda b, g, i: (b, i, g)),
                   pl.BlockSpec((None, None, LANE, tq), lambda b, g, i: (b, g, 0, i))],
        out_shape=[jax.ShapeDtypeStruct((B, S, WIDTH), BF16),
                   jax.ShapeDtypeStruct((B, KV_GROUPS, LANE, S), BF16)],
        compiler_params=_cparams(("parallel", "parallel", "parallel")),
        name="cmp_attn",
    )(q_r, kc, vc, overlap)


def _flash_kernel(*refs, selected, tk):
    if selected:
        q_ref, b_ref, k_ref, vt_ref, o_ref, qs_ref, s_ref, m_ref, l_ref, acc_ref = refs
    else:
        q_ref, k_ref, vt_ref, o_ref, qs_ref, s_ref, m_ref, l_ref, acc_ref = refs
    i = pl.program_id(1)
    tq = q_ref.shape[1]
    cols = HPG * tq
    t0 = i * tq
    kw = qs_ref.shape[1]
    for g in range(KV_GROUPS):
        for hh in range(HPG):
            h = g * HPG + hh
            qs_ref[g, 0:LANE, hh * tq:(hh + 1) * tq] = q_ref[h * LANE:(h + 1) * LANE, :]
            if selected:
                qs_ref[g, LANE:2 * LANE, hh * tq:(hh + 1) * tq] = b_ref[g]
    m_ref[...] = jnp.full(m_ref.shape, -jnp.inf, F32)
    l_ref[...] = jnp.zeros(l_ref.shape, F32)
    acc_ref[...] = jnp.zeros(acc_ref.shape, F32)
    qpos = t0 + (lax.broadcasted_iota(jnp.int32, (1, cols), 1) & (tq - 1))

    def scores(j, slot):
        k0 = pl.multiple_of(j * tk, tk)
        for g in range(KV_GROUPS):
            s_ref[slot, g] = _dot(k_ref[pl.ds(k0, tk), g * kw:(g + 1) * kw], qs_ref[g])

    def update(j, slot, masked):
        if masked:
            kpos = j * tk + lax.broadcasted_iota(jnp.int32, (tk, 1), 0)
            mask = kpos <= qpos
            if not selected:
                mask = mask & (qpos - kpos < WINDOW)
        for g in range(KV_GROUPS):
            s = s_ref[slot, g]
            if masked:
                s = jnp.where(mask, s, -jnp.inf)
            m_old = m_ref[g]
            m_new = jnp.maximum(m_old, jnp.max(s, axis=0, keepdims=True))
            alpha = jnp.exp2(m_old - m_new)
            p = jnp.exp2(s - m_new)
            l_ref[g] = alpha * l_ref[g] + jnp.sum(p, axis=0, keepdims=True)
            acc_ref[g] = alpha * acc_ref[g] + _dot(vt_ref[g, j], p.astype(BF16))
            m_ref[g] = m_new

    jd = t0 // tk
    if selected:
        last = jnp.maximum(jd - 1, 0)
        scores(jd, 0)
        scores(0, 1)
        update(jd, 0, True)

        def body(p, carry):
            scores(jnp.minimum(2 * p + 1, last), 0)
            update(2 * p, 1, False)
            scores(jnp.minimum(2 * p + 2, last), 1)
            update(2 * p + 1, 0, False)
            return carry

        lax.fori_loop(0, jd // 2, body, 0)

        @pl.when(jd % 2 == 1)
        def _():
            update(jd - 1, 1, False)
    else:
        n_full = WINDOW // tk - 1
        scores(jd, 0)
        update(jd, 0, True)

        def full_step(j, carry):
            scores(j, 0)
            update(j, 0, False)
            return carry

        lax.fori_loop(jnp.maximum(jd - n_full, 0), jd, full_step, 0)

        @pl.when(jd - n_full - 1 >= 0)
        def _():
            scores(jd - n_full - 1, 0)
            update(jd - n_full - 1, 0, True)

    for g in range(KV_GROUPS):
        out_t = acc_ref[g] / jnp.maximum(l_ref[g], jnp.finfo(F32).tiny)
        for hh in range(HPG):
            h = g * HPG + hh
            o_ref[:, h * LANE:(h + 1) * LANE] = out_t[:, hh * tq:(hh + 1) * tq].T.astype(BF16)


FLASH_TQ = 256


def _flash(q_t, k_r, v_t, bias_t, *, tk):
    B, _, S = q_t.shape
    tq = FLASH_TQ
    selected = bias_t is not None
    assert tq == tk
    kw = k_r.shape[2] // KV_GROUPS
    in_specs = [pl.BlockSpec((None, WIDTH, tq), lambda b, i: (b, 0, i))]
    args = [q_t]
    if selected:
        in_specs.append(pl.BlockSpec((None, KV_GROUPS, LANE, tq), lambda b, i: (b, 0, 0, i)))
        args.append(bias_t)
    in_specs += [pl.BlockSpec((None, S, KV_GROUPS * kw), lambda b, i: (b, 0, 0)),
                 pl.BlockSpec((None, KV_GROUPS, S // tk, LANE, tk), lambda b, i: (b, 0, 0, 0, 0))]
    args += [k_r, v_t]
    cols = HPG * tq
    return pl.pallas_call(
        functools.partial(_flash_kernel, selected=selected, tk=tk),
        grid=(B, S // tq),
        in_specs=in_specs,
        out_specs=pl.BlockSpec((None, tq, WIDTH), lambda b, i: (b, i, 0)),
        out_shape=jax.ShapeDtypeStruct((B, S, WIDTH), BF16),
        scratch_shapes=[pltpu.VMEM((KV_GROUPS, kw, cols), BF16),
                        pltpu.VMEM((2 if selected else 1, KV_GROUPS, tk, cols), F32),
                        pltpu.VMEM((KV_GROUPS, 1, cols), F32), pltpu.VMEM((KV_GROUPS, 1, cols), F32),
                        pltpu.VMEM((KV_GROUPS, LANE, cols), F32)],
        compiler_params=_cparams(("parallel", "parallel")),
        name="flash_sel" if selected else "flash_win",
    )(*args)


def _merge_kernel(ydn_ref, oc_ref, os_ref, ow_ref, small_ref, mgd_ref, mgn_ref, x_ref,
                  wdn_ref, wnsa_ref, wout_ref, g_ref, b_ref, o_ref):
    gates = _sigmoid(small_ref[...])
    cols = []
    for h in range(HEADS):
        sl = slice(h * LANE, (h + 1) * LANE)
        gc = gates[:, SMALL_GATE + h:SMALL_GATE + h + 1]
        gs = gates[:, SMALL_GATE + HEADS + h:SMALL_GATE + HEADS + h + 1]
        gw = gates[:, SMALL_GATE + 2 * HEADS + h:SMALL_GATE + 2 * HEADS + h + 1]
        cols.append((gc * oc_ref[:, sl] + gs * os_ref[:, sl] + gw * ow_ref[:, sl]).astype(BF16))
    y_nsa = jnp.concatenate(cols, axis=1)
    y = (_sigmoid(mgd_ref[...]) * _dot(ydn_ref[...], wdn_ref[...])
         + _sigmoid(mgn_ref[...]) * _dot(y_nsa, wnsa_ref[...]))
    z = DEEPNORM_ALPHA * x_ref[...] + _dot(y.astype(BF16), wout_ref[...])
    o_ref[...] = _layer_norm(z, g_ref[...], b_ref[...])


def _merge(y_dn2, oc2, os2, ow2, small2, h, x2, wdn, wnsa, wout, g, b):
    T, D = x2.shape
    tm = 256
    row = pl.BlockSpec((tm, WIDTH), lambda i: (i, 0))
    full = lambda shape: pl.BlockSpec(shape, lambda i: (0, 0))
    return pl.pallas_call(
        _merge_kernel,
        grid=(T // tm,),
        in_specs=[row, row, row, row,
                  pl.BlockSpec((tm, LANE), lambda i: (i, 0)),
                  pl.BlockSpec((tm, WIDTH), lambda i: (i, COL_MG_DN // WIDTH)),
                  pl.BlockSpec((tm, WIDTH), lambda i: (i, COL_MG_NSA // WIDTH)),
                  pl.BlockSpec((tm, D), lambda i: (i, 0)),
                  full((WIDTH, D)), full((WIDTH, D)), full((D, D)), full((1, D)), full((1, D))],
        out_specs=pl.BlockSpec((tm, D), lambda i: (i, 0)),
        out_shape=jax.ShapeDtypeStruct((T, D), F32),
        compiler_params=_cparams(("parallel",)),
        name="merge",
    )(y_dn2, oc2, os2, ow2, small2, h, h, x2, wdn, wnsa, wout, g, b)


def _ffn_up_kernel(x_ref, xp_ref, wg_ref, wv_ref, cwg_ref, cwv_ref, bg_ref, bv_ref, o_ref, *, tiles_per_seq):
    i = pl.program_id(1)
    tm = x_ref.shape[0]
    notfirst = (i % tiles_per_seq != 0).astype(F32)
    xb = x_ref[...].astype(BF16)
    xpb = (xp_ref[...] * notfirst).astype(BF16)
    row = lax.broadcasted_iota(jnp.int32, (tm, 1), 0)

    def conv(w_ref, cw_ref, b_ref):
        u = _dot(xb, w_ref[...])
        p = _dot(xpb, w_ref[...])
        u1 = jnp.where(row == 0, p[SUBLANE - 1:SUBLANE, :], pltpu.roll(u, 1, 0))
        u2 = jnp.where(row == 0, p[SUBLANE - 2:SUBLANE - 1, :],
                       jnp.where(row == 1, p[SUBLANE - 1:SUBLANE, :], pltpu.roll(u, 2, 0)))
        return cw_ref[0:1, :] * u2 + cw_ref[1:2, :] * u1 + cw_ref[2:3, :] * u + b_ref[...]

    o_ref[...] = (_silu(conv(wg_ref, cwg_ref, bg_ref)) * conv(wv_ref, cwv_ref, bv_ref)).astype(BF16)


def _ffn_up(x1, w_up, conv_w, conv_b, seq):
    T, D = x1.shape
    tm = 512
    nj = 2
    tn = D_FF // nj
    return pl.pallas_call(
        functools.partial(_ffn_up_kernel, tiles_per_seq=seq // tm),
        grid=(nj, T // tm),
        in_specs=[pl.BlockSpec((tm, D), lambda j, i: (i, 0)),
                  pl.BlockSpec((SUBLANE, D), lambda j, i: (jnp.maximum(i * (tm // SUBLANE) - 1, 0), 0)),
                  pl.BlockSpec((D, tn), lambda j, i: (0, j)),
                  pl.BlockSpec((D, tn), lambda j, i: (0, j + nj)),
                  pl.BlockSpec((FFN_CONV, tn), lambda j, i: (0, j)),
                  pl.BlockSpec((FFN_CONV, tn), lambda j, i: (0, j + nj)),
                  pl.BlockSpec((1, tn), lambda j, i: (0, j)),
                  pl.BlockSpec((1, tn), lambda j, i: (0, j + nj))],
        out_specs=pl.BlockSpec((tm, tn), lambda j, i: (i, j)),
        out_shape=jax.ShapeDtypeStruct((T, D_FF), BF16),
        compiler_params=_cparams(("parallel", "parallel")),
        name="ffn_up",
    )(x1, x1, w_up, w_up, conv_w, conv_w, conv_b, conv_b)


def _ffn_down_kernel(a_ref, w_ref, x_ref, g_ref, b_ref, o_ref):
    z = DEEPNORM_ALPHA * x_ref[...] + _dot(a_ref[...], w_ref[...])
    o_ref[...] = _layer_norm(z, g_ref[...], b_ref[...])


def _ffn_down(a, w_down, x1, g, b):
    T, D = x1.shape
    tm = 512
    full = lambda shape: pl.BlockSpec(shape, lambda i: (0, 0))
    return pl.pallas_call(
        _ffn_down_kernel,
        grid=(T // tm,),
        in_specs=[pl.BlockSpec((tm, D_FF), lambda i: (i, 0)), full((D_FF, D)),
                  pl.BlockSpec((tm, D), lambda i: (i, 0)), full((1, D)), full((1, D))],
        out_specs=pl.BlockSpec((tm, D), lambda i: (i, 0)),
        out_shape=jax.ShapeDtypeStruct((T, D), F32),
        compiler_params=_cparams(("parallel",)),
        name="ffn_down",
    )(a, w_down, x1, g, b)


def _split_w_in(w):
    d = w.shape[0]
    o_dz = 3 * WIDTH
    o_db = 4 * WIDTH
    o_nq = o_db + 2 * HEADS
    o_kc = o_nq + WIDTH
    o_ksl = o_kc + 2 * KVW
    o_gate = o_ksl + 4 * KVW
    o_mg = o_gate + 3 * HEADS
    pad = jnp.zeros((d, LANE - 5 * HEADS), w.dtype)
    w_a = jnp.concatenate([w[:, 0:o_dz], w[:, o_db:o_nq], w[:, o_gate:o_mg], pad], axis=1)
    w_b = jnp.concatenate([w[:, o_nq:o_kc], w[:, o_ksl:o_gate]], axis=1)
    w_c = jnp.concatenate([w[:, o_dz:o_db], w[:, o_mg:o_mg + 2 * WIDTH], w[:, o_kc:o_ksl]], axis=1)
    return w_a.astype(BF16), w_b.astype(BF16), w_c.astype(BF16)


def _rope_tables(pos):
    inv_freq = ROPE_THETA ** (-jnp.arange(ROPE_HALF, dtype=F32) / ROPE_HALF)
    ang = pos.astype(F32)[:, None] * inv_freq
    cos = jnp.cos(ang)
    sin = jnp.sin(ang)
    p = pos.shape[0]
    cosf = jnp.concatenate([cos, cos, jnp.ones((p, LANE - ROPE_DIM), F32)], axis=1)
    sinf = jnp.concatenate([-sin, sin, jnp.zeros((p, LANE - ROPE_DIM), F32)], axis=1)
    return cosf, sinf


def _overlap_table(nc, n_sel):
    n = np.arange(nc)[:, None]
    j = np.arange(LANE)[None, :]
    ov = (n * CMP_STRIDE <= j * SEL_LEN + SEL_LEN - 1) & (n * CMP_STRIDE + CMP_LEN - 1 >= j * SEL_LEN)
    ov = ov & (j < n_sel) & (n < nc - 1)
    return jnp.asarray(ov.astype(np.float32)).astype(BF16)


def _lane_row(vals, offset):
    return jnp.zeros((1, LANE), F32).at[0, offset:offset + vals.shape[0]].set(vals.astype(F32))


def kernel(x, w_in, dn_conv_w, dn_a_log, dn_dt_bias, dn_norm_w, cmp_pos_k, cmp_pos_v, cmp_k_w1, cmp_k_w2,
           cmp_v_w1, cmp_v_w2, w_branch_dn, w_branch_nsa, w_out, ln1_g, ln1_b, ffn_w_up, ffn_conv_w,
           ffn_conv_b, ffn_w_down, ln2_g, ln2_b):
    B, S, D = x.shape
    T = B * S
    nc = S // CMP_STRIDE
    n_sel = S // SEL_LEN
    assert D == WIDTH and S % 512 == 0 and n_sel <= LANE and n_sel % SUBLANE == 0
    cos_t, sin_t = _rope_tables(jnp.arange(S))
    cos_c, sin_c = _rope_tables(jnp.arange(nc) * CMP_STRIDE + (CMP_LEN - 1))
    overlap = _overlap_table(nc, n_sel)
    x2 = x.reshape(T, D)
    for layer in range(w_in.shape[0]):
        w_a, w_b, w_c = _split_w_in(w_in[layer])
        x3 = x2.reshape(B, S, D)
        h3 = _proj_rest(x3, w_c)

        q, k, v, bg, small = _proj_gdn(x3, w_a, dn_conv_w[layer], _lane_row(dn_a_log[layer], SMALL_DECAY),
                                       _lane_row(dn_dt_bias[layer], SMALL_DECAY))
        y_dn = _gdn(q, k, v, bg, h3, dn_norm_w[layer].reshape(1, LANE))

        q_r, q_t, kaug, vsl_t, kwn_r, vwn_t = _proj_nsa(x3, w_b, cos_t, sin_t)
        kc, vc = _compress(h3, cmp_pos_k[layer], cmp_pos_v[layer],
                           cmp_k_w1[layer].astype(BF16), cmp_k_w2[layer].astype(BF16),
                           cmp_v_w1[layer].astype(BF16), cmp_v_w2[layer].astype(BF16), cos_c, sin_c)
        o_cmp, bias = _cmp_attn(q_r, kc, vc, overlap)
        o_sel = _flash(q_t, kaug, vsl_t, bias, tk=SEL_TK)
        o_win = _flash(q_t, kwn_r, vwn_t, None, tk=WIN_TK)

        x1 = _merge(y_dn.reshape(T, WIDTH), o_cmp.reshape(T, WIDTH), o_sel.reshape(T, WIDTH),
                    o_win.reshape(T, WIDTH), small.reshape(T, LANE), h3.reshape(T, NC_COLS), x2,
                    w_branch_dn[layer].astype(BF16), w_branch_nsa[layer].astype(BF16),
                    w_out[layer].astype(BF16), ln1_g[layer].reshape(1, D), ln1_b[layer].reshape(1, D))
        a = _ffn_up(x1, ffn_w_up[layer].astype(BF16), ffn_conv_w[layer],
                    ffn_conv_b[layer].reshape(1, 2 * D_FF), S)
        x2 = _ffn_down(a, ffn_w_down[layer].astype(BF16), x1, ln2_g[layer].reshape(1, D), ln2_b[layer].reshape(1, D))
    return x2.reshape(B, S, D)
```

```python
import functools
import math

import numpy as np
import jax
import jax.numpy as jnp
from jax import lax
from jax.experimental import pallas as pl
from jax.experimental.pallas import tpu as pltpu

F32 = jnp.float32
BF16 = jnp.bfloat16
HI = lax.Precision.HIGHEST

HEADS = 8
HEAD_DIM = 128
KV_GROUPS = 2
HPG = HEADS // KV_GROUPS
DN_CONV = 4
DN_CHUNK = 64
CMP_LEN = 32
CMP_STRIDE = 16
CMP_HIDDEN = 256
SEL_LEN = 64
SEL_TOPK = 16
WINDOW = 512
ROPE_THETA = 500000.0
ROPE_DIM = HEAD_DIM // 4
ROPE_HALF = ROPE_DIM // 2
D_FF = 2816
FFN_CONV = 3
LN_EPS = 1e-5
RMS_EPS = 1e-6
DEEPNORM_ALPHA = 2.0 ** 0.25
ATT_SCALE = HEAD_DIM ** -0.5

LANE = 128
SUBLANE = 8
VMEM_LIMIT = 56 * 1024 * 1024

WIDTH = HEADS * HEAD_DIM
KVW = KV_GROUPS * HEAD_DIM
NA = 3 * WIDTH + LANE
SMALL_BETA = 0
SMALL_DECAY = HEADS
SMALL_GATE = 2 * HEADS
NB = WIDTH + 4 * KVW
NC_COLS = 3 * WIDTH + 2 * KVW
COL_DZ = 0
COL_MG_DN = WIDTH
COL_MG_NSA = 2 * WIDTH
COL_KC = 3 * WIDTH
COL_VC = COL_KC + KVW

NEG_BIAS = -(2.0 ** 100)


def _cparams(sem):
    return pltpu.CompilerParams(dimension_semantics=sem, vmem_limit_bytes=VMEM_LIMIT)


def _sigmoid(x):
    return 1.0 / (1.0 + jnp.exp(-x))


def _silu(x):
    return x * _sigmoid(x)


def _dot(a, b, precision=None):
    return jnp.dot(a, b, preferred_element_type=F32, precision=precision)


def _dot_nt(a, b):
    return lax.dot_general(a, b, (((1,), (1,)), ((), ())), preferred_element_type=F32)


def _dot_tn(a, b):
    return lax.dot_general(a, b, (((0,), (0,)), ((), ())), preferred_element_type=F32)


def _layer_norm(z, g, b):
    mu = jnp.mean(z, -1, keepdims=True)
    zc = z - mu
    var = jnp.mean(zc * zc, -1, keepdims=True)
    return zc * lax.rsqrt(var + LN_EPS) * g + b


PROJ_TM = 512


def _proj_rest_kernel(x_ref, w_ref, o_ref):
    o_ref[...] = _dot(x_ref[...].astype(BF16), w_ref[...])


def _proj_rest(x3, w):
    B, S, D = x3.shape
    n_out = w.shape[1]
    tm = PROJ_TM
    return pl.pallas_call(
        _proj_rest_kernel,
        grid=(B, S // tm),
        in_specs=[pl.BlockSpec((None, tm, D), lambda b, i: (b, i, 0)),
                  pl.BlockSpec((D, n_out), lambda b, i: (0, 0))],
        out_specs=pl.BlockSpec((None, tm, n_out), lambda b, i: (b, i, 0)),
        out_shape=jax.ShapeDtypeStruct((B, S, n_out), F32),
        compiler_params=_cparams(("parallel", "parallel")),
        name="proj_rest",
    )(x3, w)


def _proj_gdn_kernel(x_ref, w_ref, cw_ref, alog_ref, dtb_ref, q_ref, k_ref, v_ref, bg_ref, small_ref, buf_ref):
    i = pl.program_id(1)
    ts = x_ref.shape[0]
    wq = 3 * WIDTH

    @pl.when(i == 0)
    def _():
        buf_ref[0:SUBLANE, :] = jnp.zeros((SUBLANE, wq), F32)

    h = _dot(x_ref[...].astype(BF16), w_ref[...])
    buf_ref[SUBLANE:SUBLANE + ts, :] = h[:, 0:wq]
    small = h[:, wq:wq + LANE]
    for c in range(3 * HEADS):
        sl = slice(c * LANE, (c + 1) * LANE)
        acc = cw_ref[DN_CONV - 1:DN_CONV, sl] * buf_ref[SUBLANE:SUBLANE + ts, sl]
        for s in range(1, DN_CONV):
            acc = acc + cw_ref[DN_CONV - 1 - s:DN_CONV - s, sl] * buf_ref[pl.ds(SUBLANE - s, ts), sl]
        y = _silu(acc)
        if c < 2 * HEADS:
            y = y * lax.rsqrt(jnp.sum(y * y, -1, keepdims=True) + RMS_EPS)
        if c < HEADS:
            q_ref[:, sl] = y * ATT_SCALE
        elif c < 2 * HEADS:
            k_ref[:, slice((c - HEADS) * LANE, (c - HEADS + 1) * LANE)] = y
        else:
            v_ref[:, slice((c - 2 * HEADS) * LANE, (c - 2 * HEADS + 1) * LANE)] = y
    buf_ref[0:SUBLANE, :] = buf_ref[ts:ts + SUBLANE, :]
    small_ref[...] = small
    beta = _sigmoid(small)
    xs = small + dtb_ref[...]
    softplus = jnp.maximum(xs, 0.0) + jnp.log1p(jnp.exp(-jnp.abs(xs)))
    g = -jnp.exp(alog_ref[...]) * softplus
    lane = lax.broadcasted_iota(jnp.int32, small.shape, 1)
    bg_ref[...] = jnp.where(lane < SMALL_DECAY, beta, g)


def _proj_gdn(x3, w_a, conv_w, alog_row, dtb_row):
    B, S, D = x3.shape
    ts = PROJ_TM
    wq = 3 * WIDTH
    out_sd = jax.ShapeDtypeStruct((B, S, WIDTH), F32)
    small_sd = jax.ShapeDtypeStruct((B, S, LANE), F32)
    wide = pl.BlockSpec((None, ts, WIDTH), lambda b, i: (b, i, 0))
    narrow = pl.BlockSpec((None, ts, LANE), lambda b, i: (b, i, 0))
    return pl.pallas_call(
        _proj_gdn_kernel,
        grid=(B, S // ts),
        in_specs=[
            pl.BlockSpec((None, ts, D), lambda b, i: (b, i, 0)),
            pl.BlockSpec((D, NA), lambda b, i: (0, 0)),
            pl.BlockSpec((DN_CONV, wq), lambda b, i: (0, 0)),
            pl.BlockSpec((1, LANE), lambda b, i: (0, 0)),
            pl.BlockSpec((1, LANE), lambda b, i: (0, 0)),
        ],
        out_specs=[wide, wide, wide, narrow, narrow],
        out_shape=[out_sd, out_sd, out_sd, small_sd, small_sd],
        scratch_shapes=[pltpu.VMEM((ts + SUBLANE, wq), F32)],
        compiler_params=_cparams(("parallel", "arbitrary")),
        name="proj_gdn",
    )(x3, w_a, conv_w, alog_row, dtb_row)


GDN_CHUNKS_PER_STEP = 2
GDN_BATCH_PER_STEP = 2


GDN_QUAD = 4


def _gdn_kernel(q_ref, k_ref, v_ref, bg_ref, z_ref, nw_ref, y_ref, state_ref):
    n = pl.program_id(1)

    @pl.when(n == 0)
    def _():
        state_ref[...] = jnp.zeros(state_ref.shape, F32)

    C = DN_CHUNK
    QW = GDN_QUAD * C
    QL = GDN_QUAD * LANE
    shift_c = int(math.log2(C))
    r64 = lax.broadcasted_iota(jnp.int32, (C, C), 0)
    c64 = lax.broadcasted_iota(jnp.int32, (C, C), 1)
    tril = jnp.where(r64 >= c64, 1.0, 0.0).astype(F32)
    triu = jnp.where(r64 <= c64, 1.0, 0.0).astype(F32)
    rr = lax.broadcasted_iota(jnp.int32, (C, QW), 0)
    ll = lax.broadcasted_iota(jnp.int32, (C, QW), 1)
    jj = ll & (C - 1)
    hb = ll >> shift_c
    incl_t = rr >= jj
    strict_t = rr > jj
    eye_t = jnp.where(rr == jj, 1.0, 0.0).astype(F32)
    bd_c = jnp.where((lax.broadcasted_iota(jnp.int32, (QW, QW), 0) >> shift_c)
                     == (lax.broadcasted_iota(jnp.int32, (QW, QW), 1) >> shift_c), 1.0, 0.0).astype(BF16)
    bd_l = jnp.where((lax.broadcasted_iota(jnp.int32, (QW, QL), 0) >> shift_c)
                     == (lax.broadcasted_iota(jnp.int32, (QW, QL), 1) >> int(math.log2(LANE))), 1.0, 0.0).astype(BF16)
    zeros_rhs = jnp.zeros((C, 2 * LANE), BF16)
    nw = nw_ref[...]
    nb = q_ref.shape[0]
    quads = [(bb, c, a) for bb in range(nb) for c in range(GDN_CHUNKS_PER_STEP) for a in range(HEADS // GDN_QUAD)]
    gc_cols, gc_rows, bgs = {}, {}, {}
    for bb in range(nb):
        bg = bg_ref[bb]
        bg_t = bg.T
        bgs[bb] = bg
        for c in range(GDN_CHUNKS_PER_STEP):
            rs = slice(c * C, (c + 1) * C)
            gc_cols[bb, c] = _dot(tril, bg[rs, :], HI)
            gc_rows[bb, c] = _dot(bg_t[SMALL_DECAY:SMALL_DECAY + HEADS, rs], triu, HI)
    decay, lower, qk, rhs_bd, qd, kd, eglast = {}, {}, {}, {}, {}, {}, {}
    for u_ in quads:
        bb, c, a = u_
        rs = slice(c * C, (c + 1) * C)
        h0 = a * GDN_QUAD
        gcols = [gc_cols[bb, c][:, SMALL_DECAY + h0 + hh:SMALL_DECAY + h0 + hh + 1] for hh in range(GDN_QUAD)]
        gcol_cat = jnp.broadcast_to(gcols[0], (C, QW))
        for hh in range(1, GDN_QUAD):
            gcol_cat = jnp.where(hb == hh, gcols[hh], gcol_cat)
        grow_cat = jnp.concatenate([gc_rows[bb, c][h0 + hh:h0 + hh + 1, :] for hh in range(GDN_QUAD)], axis=1)
        decay[u_] = jnp.exp(jnp.where(incl_t, gcol_cat - grow_cat, -jnp.inf))
        kq = k_ref[bb, rs, a * QL:(a + 1) * QL]
        qq = q_ref[bb, rs, a * QL:(a + 1) * QL]
        kbs, rows_bd = [], []
        for hh in range(GDN_QUAD):
            h = h0 + hh
            ls = slice(hh * LANE, (hh + 1) * LANE)
            beta = bgs[bb][rs, SMALL_BETA + h:SMALL_BETA + h + 1]
            eg = jnp.exp(gcols[hh])
            glast = gcols[hh][C - 1:C, :]
            kb = kq[:, ls] * beta
            vb = v_ref[bb, rs, h * LANE:(h + 1) * LANE] * beta
            kbs.append(kb)
            rhs_h = jnp.concatenate([vb, kb * eg], axis=1).astype(BF16)
            rows_bd.append(jnp.concatenate([zeros_rhs] * hh + [rhs_h] + [zeros_rhs] * (GDN_QUAD - 1 - hh), axis=1))
            qd[bb, c, h] = (qq[:, ls] * eg).astype(BF16)
            kd[bb, c, h] = (kq[:, ls] * jnp.exp(glast - gcols[hh])).astype(BF16)
            eglast[bb, c, h] = jnp.exp(glast)
        rhs_bd[u_] = jnp.concatenate(rows_bd, axis=0)
        k_bd = jnp.concatenate([kq.astype(BF16)] * GDN_QUAD, axis=0) * bd_l
        lhs = jnp.concatenate([jnp.concatenate(kbs, axis=1), qq], axis=0).astype(BF16)
        aq = _dot_nt(lhs, k_bd)
        lower[u_] = jnp.where(strict_t, aq[0:C] * decay[u_], 0.0)
        qk[u_] = (aq[C:2 * C] * decay[u_]).astype(BF16)

    def block_diag(y_b):
        return jnp.concatenate([y_b] * GDN_QUAD, axis=0) * bd_c

    tinv = {u_: eye_t - lower[u_] for u_ in quads}
    y_b = {u_: lower[u_].astype(BF16) for u_ in quads}
    ypow = {u_: _dot(y_b[u_], block_diag(y_b[u_])) for u_ in quads}
    for it in range(5):
        for u_ in quads:
            yb = ypow[u_].astype(BF16)
            if it < 4:
                r = _dot(jnp.concatenate([tinv[u_].astype(BF16), yb], axis=0), block_diag(yb))
                tinv[u_] = tinv[u_] + r[0:C]
                ypow[u_] = r[C:2 * C]
            else:
                tinv[u_] = tinv[u_] + _dot(tinv[u_].astype(BF16), block_diag(yb))
    uw = {u_: _dot(tinv[u_].astype(BF16), rhs_bd[u_]) for u_ in quads}
    zeros_v = jnp.zeros((C, LANE), BF16)
    chains = [(bb, h) for bb in range(nb) for h in range(HEADS)]
    for c in range(GDN_CHUNKS_PER_STEP):
        rs = slice(c * C, (c + 1) * C)
        states = {s_: state_ref[s_[0] * HEADS + s_[1]] for s_ in chains}
        ws_qs = {}
        for s_ in chains:
            bb, h = s_
            a, hh = divmod(h, GDN_QUAD)
            w_h = uw[bb, c, a][:, hh * 2 * LANE + LANE:(hh + 1) * 2 * LANE]
            ws_qs[s_] = _dot(jnp.concatenate([w_h.astype(BF16), qd[bb, c, h]], axis=0), states[s_].astype(BF16))
        v_new = {}
        for s_ in chains:
            bb, h = s_
            a, hh = divmod(h, GDN_QUAD)
            v_new[s_] = (uw[bb, c, a][:, hh * 2 * LANE:hh * 2 * LANE + LANE] - ws_qs[s_][0:C]).astype(BF16)
        outs = {}
        for s_ in chains:
            bb, h = s_
            a, hh = divmod(h, GDN_QUAD)
            v_pad = jnp.concatenate([zeros_v] * hh + [v_new[s_]] + [zeros_v] * (GDN_QUAD - 1 - hh), axis=0)
            outs[s_] = ws_qs[s_][C:2 * C] + _dot(qk[bb, c, a], v_pad)
            state_ref[bb * HEADS + h] = states[s_] * eglast[bb, c, h] + _dot_tn(kd[bb, c, h], v_new[s_])
        for s_ in chains:
            bb, h = s_
            ls = slice(h * LANE, (h + 1) * LANE)
            o = outs[s_]
            o = o * lax.rsqrt(jnp.mean(o * o, -1, keepdims=True) + RMS_EPS)
            y_ref[bb, rs, ls] = (o * nw * _silu(z_ref[bb, rs, ls])).astype(BF16)


def _gdn(q, k, v, bg, h3, norm_w):
    B, S, _ = q.shape
    rows = GDN_CHUNKS_PER_STEP * DN_CHUNK
    nb = GDN_BATCH_PER_STEP if B % GDN_BATCH_PER_STEP == 0 else 1
    tile = pl.BlockSpec((nb, rows, WIDTH), lambda b, n: (b, n, 0))
    return pl.pallas_call(
        _gdn_kernel,
        grid=(B // nb, S // rows),
        in_specs=[tile, tile, tile,
                  pl.BlockSpec((nb, rows, LANE), lambda b, n: (b, n, 0)),
                  pl.BlockSpec((nb, rows, WIDTH), lambda b, n: (b, n, COL_DZ // WIDTH)),
                  pl.BlockSpec((1, LANE), lambda b, n: (0, 0))],
        out_specs=tile,
        out_shape=jax.ShapeDtypeStruct((B, S, WIDTH), BF16),
        scratch_shapes=[pltpu.VMEM((nb * HEADS, HEAD_DIM, HEAD_DIM), F32)],
        compiler_params=_cparams(("parallel", "arbitrary")),
        name="gdn",
    )(q, k, v, bg, h3, norm_w)


def _rope(t, cosf, sinf, lo):
    partner = jnp.where(lo, pltpu.roll(t, LANE - ROPE_HALF, 1), pltpu.roll(t, ROPE_HALF, 1))
    return t * cosf + partner * sinf


SEL_TK = 256
WIN_TK = 256
SEL_VROWS = LANE + 16
LOG2E = math.log2(math.e)


def _proj_nsa_kernel(x_ref, w_ref, cos_ref, sin_ref, q_o, qt_o, kaug_o, vslt_o, kwn_o, vwnt_o):
    i = pl.program_id(1)
    ts = x_ref.shape[0]
    h = _dot(x_ref[...].astype(BF16), w_ref[...])
    c_ksl, c_vsl, c_kwn, c_vwn = WIDTH, WIDTH + KVW, WIDTH + 2 * KVW, WIDTH + 3 * KVW
    cosf = cos_ref[...]
    sinf = sin_ref[...]
    lane = lax.broadcasted_iota(jnp.int32, cosf.shape, 1)
    lo = lane < ROPE_HALF
    for hd in range(HEADS):
        sl = slice(hd * LANE, (hd + 1) * LANE)
        qh = _rope(h[:, sl], cosf, sinf, lo) * ATT_SCALE
        q_o[:, sl] = qh.astype(BF16)
        qt_o[sl, :] = (qh * LOG2E).T.astype(BF16)
    tpos = i * ts + lax.broadcasted_iota(jnp.int32, cosf.shape, 0)
    onehot = jnp.where((tpos >> int(math.log2(SEL_LEN))) == lane, 1.0, 0.0).astype(BF16)
    for g in range(KV_GROUPS):
        sl = slice(g * LANE, (g + 1) * LANE)
        off = g * LANE
        kaug_o[:, 2 * g * LANE:(2 * g + 1) * LANE] = _rope(h[:, c_ksl + off:c_ksl + off + LANE], cosf, sinf, lo).astype(BF16)
        kaug_o[:, (2 * g + 1) * LANE:(2 * g + 2) * LANE] = onehot
        kwn_o[:, sl] = _rope(h[:, c_kwn + off:c_kwn + off + LANE], cosf, sinf, lo).astype(BF16)
        vs_t = h[:, c_vsl + off:c_vsl + off + LANE].T.astype(BF16)
        for c in range(ts // SEL_TK):
            ones_rows = jnp.where(lax.broadcasted_iota(jnp.int32, (SEL_VROWS - LANE, SEL_TK), 0) == 0, 1.0, 0.0)
            vslt_o[g, c] = jnp.concatenate([vs_t[:, c * SEL_TK:(c + 1) * SEL_TK], ones_rows.astype(BF16)], axis=0)
        vw_t = h[:, c_vwn + off:c_vwn + off + LANE].T.astype(BF16)
        for c in range(ts // WIN_TK):
            vwnt_o[g, c] = vw_t[:, c * WIN_TK:(c + 1) * WIN_TK]


def _proj_nsa(x3, w_b, cos_tab, sin_tab):
    B, S, D = x3.shape
    ts = PROJ_TM
    tab = pl.BlockSpec((ts, LANE), lambda b, i: (i, 0))
    return pl.pallas_call(
        _proj_nsa_kernel,
        grid=(B, S // ts),
        in_specs=[pl.BlockSpec((None, ts, D), lambda b, i: (b, i, 0)),
                  pl.BlockSpec((D, NB), lambda b, i: (0, 0)), tab, tab],
        out_specs=[pl.BlockSpec((None, ts, WIDTH), lambda b, i: (b, i, 0)),
                   pl.BlockSpec((None, WIDTH, ts), lambda b, i: (b, 0, i)),
                   pl.BlockSpec((None, ts, 2 * KVW), lambda b, i: (b, i, 0)),
                   pl.BlockSpec((None, KV_GROUPS, ts // SEL_TK, SEL_VROWS, SEL_TK), lambda b, i: (b, 0, i, 0, 0)),
                   pl.BlockSpec((None, ts, KVW), lambda b, i: (b, i, 0)),
                   pl.BlockSpec((None, KV_GROUPS, ts // WIN_TK, LANE, WIN_TK), lambda b, i: (b, 0, i, 0, 0))],
        out_shape=[jax.ShapeDtypeStruct((B, S, WIDTH), BF16),
                   jax.ShapeDtypeStruct((B, WIDTH, S), BF16),
                   jax.ShapeDtypeStruct((B, S, 2 * KVW), BF16),
                   jax.ShapeDtypeStruct((B, KV_GROUPS, S // SEL_TK, SEL_VROWS, SEL_TK), BF16),
                   jax.ShapeDtypeStruct((B, S, KVW), BF16),
                   jax.ShapeDtypeStruct((B, KV_GROUPS, S // WIN_TK, LANE, WIN_TK), BF16)],
        compiler_params=_cparams(("parallel", "parallel")),
        name="proj_nsa",
    )(x3, w_b, cos_tab, sin_tab)


def _compress_one(t_ref, pe_ref, w1_ref, w2_ref, nc):
    half = CMP_LEN // 2
    y1 = jnp.zeros((nc, CMP_HIDDEN), F32)
    y2 = jnp.zeros((nc, CMP_HIDDEN), F32)
    for l in range(half):
        x = t_ref[pl.ds(l, nc, stride=CMP_STRIDE), :]
        y1 = y1 + _dot((x + pe_ref[l:l + 1, :]).astype(BF16), w1_ref[l * LANE:(l + 1) * LANE, :])
        y2 = y2 + _dot((x + pe_ref[half + l:half + l + 1, :]).astype(BF16),
                       w1_ref[(half + l) * LANE:(half + l + 1) * LANE, :])
    y = y1 + pltpu.roll(y2, nc - 1, 0)
    return _dot(_silu(y).astype(BF16), w2_ref[...])


def _compress_kernel(kc_ref, vc_ref, pek_ref, pev_ref, w1k_ref, w2k_ref, w1v_ref, w2v_ref,
                     cos_ref, sin_ref, kc_o, vc_o):
    nc = kc_o.shape[0]
    kc = _compress_one(kc_ref, pek_ref, w1k_ref, w2k_ref, nc)
    lo = lax.broadcasted_iota(jnp.int32, kc.shape, 1) < ROPE_HALF
    kc_o[...] = _rope(kc, cos_ref[...], sin_ref[...], lo).astype(BF16)
    vc_o[...] = _compress_one(vc_ref, pev_ref, w1v_ref, w2v_ref, nc).astype(BF16)


def _compress(h3, pek, pev, w1k, w2k, w1v, w2v, cos_c, sin_c):
    B, S, _ = h3.shape
    nc = S // CMP_STRIDE
    full = lambda shape: pl.BlockSpec(shape, lambda b, g: tuple(0 for _ in shape))
    out_spec = pl.BlockSpec((None, None, nc, LANE), lambda b, g: (b, g, 0, 0))
    out_sd = jax.ShapeDtypeStruct((B, KV_GROUPS, nc, LANE), BF16)
    cin = CMP_LEN * HEAD_DIM
    return pl.pallas_call(
        _compress_kernel,
        grid=(B, KV_GROUPS),
        in_specs=[pl.BlockSpec((None, S, LANE), lambda b, g: (b, 0, COL_KC // LANE + g)),
                  pl.BlockSpec((None, S, LANE), lambda b, g: (b, 0, COL_VC // LANE + g)),
                  full((CMP_LEN, LANE)), full((CMP_LEN, LANE)),
                  full((cin, CMP_HIDDEN)), full((CMP_HIDDEN, LANE)),
                  full((cin, CMP_HIDDEN)), full((CMP_HIDDEN, LANE)),
                  full((nc, LANE)), full((nc, LANE))],
        out_specs=[out_spec, out_spec],
        out_shape=[out_sd, out_sd],
        compiler_params=_cparams(("parallel", "parallel")),
        name="compress",
    )(h3, h3, pek, pev, w1k, w2k, w1v, w2v, cos_c, sin_c)


def _cmp_attn_kernel(q_ref, kc_ref, vc_ref, ov_ref, o_ref, bias_ref, *, n_sel):
    i = pl.program_id(2)
    tq = q_ref.shape[0]
    nc = kc_ref.shape[0]
    t0 = i * tq
    tpos = t0 + lax.broadcasted_iota(jnp.int32, (tq, 1), 0)
    cmp_end = lax.broadcasted_iota(jnp.int32, (1, nc), 1) * CMP_STRIDE + (CMP_LEN - 1)
    mask = cmp_end <= tpos
    kc = kc_ref[...]
    vc = vc_ref[...]
    psum = jnp.zeros((tq, nc), F32)
    for hh in range(HPG):
        sl = slice(hh * LANE, (hh + 1) * LANE)
        s = jnp.where(mask, _dot_nt(q_ref[:, sl], kc), -jnp.inf)
        m = jnp.max(s, -1, keepdims=True)
        m = jnp.where(m > -jnp.inf, m, 0.0)
        p = jnp.exp(s - m)
        p = p / jnp.maximum(jnp.sum(p, -1, keepdims=True), jnp.finfo(F32).tiny)
        o_ref[:, sl] = _dot(p.astype(BF16), vc).astype(BF16)
        psum = psum + p
    ov = ov_ref[...]
    p_hi = psum.astype(BF16)
    r1 = psum - p_hi.astype(F32)
    p_mid = r1.astype(BF16)
    p_lo = (r1 - p_mid.astype(F32)).astype(BF16)
    p_sel = _dot(p_hi, ov) + (_dot(p_mid, ov) + _dot(p_lo, ov))
    jb = lax.broadcasted_iota(jnp.int32, (tq, LANE), 1)
    cur = tpos >> int(math.log2(SEL_LEN))
    valid = jb <= cur
    forced = valid & ((jb == 0) | (jb == cur) | (jb == cur - 1))
    score = jnp.where(forced, jnp.inf, jnp.where(valid, p_sel, -jnp.inf))
    st = score.T
    n_grp = n_sel // SUBLANE
    grp = [st[g * SUBLANE:(g + 1) * SUBLANE, :] for g in range(n_grp)]
    jrow = lax.broadcasted_iota(jnp.int32, (SUBLANE, tq), 0)
    cnt = [jnp.zeros((SUBLANE, tq), F32) for _ in range(n_grp)]
    for b in range(n_sel):
        gb, rb_ = divmod(b, SUBLANE)
        rb = grp[gb][rb_:rb_ + 1, :]
        for g in range(n_grp):
            if g < gb:
                inc = jnp.where(rb > grp[g], 1.0, 0.0)
            elif g > gb:
                inc = jnp.where(rb >= grp[g], 1.0, 0.0)
            else:
                inc = jnp.where(jrow > rb_, jnp.where(rb >= grp[g], 1.0, 0.0), jnp.where(rb > grp[g], 1.0, 0.0))
            cnt[g] = cnt[g] + inc
    rows = [jnp.where((cnt[g] < float(SEL_TOPK)) & (grp[g] > -jnp.inf), 0.0, NEG_BIAS) for g in range(n_grp)]
    if n_sel < LANE:
        rows.append(jnp.zeros((LANE - n_sel, tq), F32))
    bias_ref[...] = jnp.concatenate(rows, axis=0).astype(BF16)


def _cmp_attn(q_r, kc, vc, overlap):
    B, S, _ = q_r.shape
    nc = kc.shape[2]
    n_sel = S // SEL_LEN
    tq = 256
    gw = HPG * LANE
    return pl.pallas_call(
        functools.partial(_cmp_attn_kernel, n_sel=n_sel),
        grid=(B, KV_GROUPS, S // tq),
        in_specs=[pl.BlockSpec((None, tq, gw), lambda b, g, i: (b, i, g)),
                  pl.BlockSpec((None, None, nc, LANE), lambda b, g, i: (b, g, 0, 0)),
                  pl.BlockSpec((None, None, nc, LANE), lambda b, g, i: (b, g, 0, 0)),
                  pl.BlockSpec((nc, LANE), lambda b, g, i: (0, 0))],
        out_specs=[pl.BlockSpec((None, tq, gw), lambda b, g, i: (b, i, g)),
                   pl.BlockSpec((None, None, LANE, tq), lambda b, g, i: (b, g, 0, i))],
        out_shape=[jax.ShapeDtypeStruct((B, S, WIDTH), BF16),
                   jax.ShapeDtypeStruct((B, KV_GROUPS, LANE, S), BF16)],
        compiler_params=_cparams(("parallel", "parallel", "parallel")),
        name="cmp_attn",
    )(q_r, kc, vc, overlap)


FLASH_TQ = 256


def _store_heads(o_ref, out_t, g, tq):
    for hh in range(HPG):
        h = g * HPG + hh
        o_ref[:, h * LANE:(h + 1) * LANE] = out_t[:, hh * tq:(hh + 1) * tq].T.astype(BF16)


def _flash_sel_kernel(q_ref, b_ref, k_ref, vt_ref, o_ref, qs_ref, s_ref, m_ref, acc_ref, *, tk):
    i = pl.program_id(1)
    tq = q_ref.shape[1]
    cols = HPG * tq
    t0 = i * tq
    kw = qs_ref.shape[1]
    for g in range(KV_GROUPS):
        for hh in range(HPG):
            h = g * HPG + hh
            qs_ref[g, 0:LANE, hh * tq:(hh + 1) * tq] = q_ref[h * LANE:(h + 1) * LANE, :]
            qs_ref[g, LANE:2 * LANE, hh * tq:(hh + 1) * tq] = b_ref[g]
    m_ref[...] = jnp.full(m_ref.shape, -jnp.inf, F32)
    acc_ref[...] = jnp.zeros(acc_ref.shape, F32)
    qpos = t0 + (lax.broadcasted_iota(jnp.int32, (1, cols), 1) & (tq - 1))

    def scores(j, slot):
        k0 = pl.multiple_of(j * tk, tk)
        for g in range(KV_GROUPS):
            s_ref[slot, g] = _dot(k_ref[pl.ds(k0, tk), g * kw:(g + 1) * kw], qs_ref[g])

    def update(j, slot, masked):
        if masked:
            mask = j * tk + lax.broadcasted_iota(jnp.int32, (tk, 1), 0) <= qpos
        for g in range(KV_GROUPS):
            s = s_ref[slot, g]
            if masked:
                s = jnp.where(mask, s, -jnp.inf)
            m_old = m_ref[g]
            m_new = jnp.maximum(m_old, jnp.max(s, axis=0, keepdims=True))
            alpha = jnp.exp2(m_old - m_new)
            p = jnp.exp2(s - m_new)
            acc_ref[g] = alpha * acc_ref[g] + _dot(vt_ref[g, j], p.astype(BF16))
            m_ref[g] = m_new

    jd = t0 // tk
    last = jnp.maximum(jd - 1, 0)
    scores(jd, 0)
    scores(0, 1)
    update(jd, 0, True)

    def body(p, carry):
        scores(jnp.minimum(2 * p + 1, last), 0)
        update(2 * p, 1, False)
        scores(jnp.minimum(2 * p + 2, last), 1)
        update(2 * p + 1, 0, False)
        return carry

    lax.fori_loop(0, jd // 2, body, 0)

    @pl.when(jd % 2 == 1)
    def _():
        update(jd - 1, 1, False)

    for g in range(KV_GROUPS):
        _store_heads(o_ref, acc_ref[g, 0:LANE] / jnp.maximum(acc_ref[g, LANE:LANE + 1], jnp.finfo(F32).tiny), g, tq)


def _flash_sel(q_t, kaug, v_t, bias_t):
    B, _, S = q_t.shape
    tq = tk = FLASH_TQ
    assert tk == SEL_TK
    kw = kaug.shape[2] // KV_GROUPS
    cols = HPG * tq
    return pl.pallas_call(
        functools.partial(_flash_sel_kernel, tk=tk),
        grid=(B, S // tq),
        in_specs=[pl.BlockSpec((None, WIDTH, tq), lambda b, i: (b, 0, i)),
                  pl.BlockSpec((None, KV_GROUPS, LANE, tq), lambda b, i: (b, 0, 0, i)),
                  pl.BlockSpec((None, S, KV_GROUPS * kw), lambda b, i: (b, 0, 0)),
                  pl.BlockSpec((None, KV_GROUPS, S // tk, SEL_VROWS, tk), lambda b, i: (b, 0, 0, 0, 0))],
        out_specs=pl.BlockSpec((None, tq, WIDTH), lambda b, i: (b, i, 0)),
        out_shape=jax.ShapeDtypeStruct((B, S, WIDTH), BF16),
        scratch_shapes=[pltpu.VMEM((KV_GROUPS, kw, cols), BF16),
                        pltpu.VMEM((2, KV_GROUPS, tk, cols), F32),
                        pltpu.VMEM((KV_GROUPS, 1, cols), F32),
                        pltpu.VMEM((KV_GROUPS, SEL_VROWS, cols), F32)],
        compiler_params=_cparams(("parallel", "parallel")),
        name="flash_sel",
    )(q_t, bias_t, kaug, v_t)


def _flash_win_kernel(q_ref, k_ref, vt_ref, wb_ref, o_ref, s_ref, *, tk):
    i = pl.program_id(1)
    tq = q_ref.shape[1]
    cols = HPG * tq
    n_t = WINDOW // tk + 1
    jt = [jnp.maximum(i - d, 0) for d in range(n_t)]
    for g in range(KV_GROUPS):
        q_g = jnp.concatenate([q_ref[(g * HPG + hh) * LANE:(g * HPG + hh + 1) * LANE, :] for hh in range(HPG)], axis=1)
        for d in range(n_t):
            k0 = pl.multiple_of(jt[d] * tk, tk)
            s_ref[g, d] = _dot(k_ref[pl.ds(k0, tk), g * LANE:(g + 1) * LANE], q_g)
    static_bias = {0: 0, n_t - 1: 1}

    def tile(g, d):
        s = s_ref[g, d]
        return s + wb_ref[static_bias[d]] if d in static_bias else s

    for g in range(KV_GROUPS):
        m = jnp.max(tile(g, 0), axis=0, keepdims=True)
        for d in range(1, n_t):
            m = jnp.maximum(m, jnp.where(i >= d, jnp.max(tile(g, d), axis=0, keepdims=True), -jnp.inf))
        l = jnp.zeros((1, cols), F32)
        acc = jnp.zeros((LANE, cols), F32)
        for d in range(n_t):
            p = jnp.exp2(tile(g, d) - (m if d == 0 else jnp.where(i >= d, m, jnp.inf)))
            l = l + jnp.sum(p, axis=0, keepdims=True)
            acc = acc + _dot(vt_ref[g, jt[d]], p.astype(BF16))
        _store_heads(o_ref, acc / jnp.maximum(l, jnp.finfo(F32).tiny), g, tq)


def _window_bias(tq):
    r = np.arange(tq)[:, None]
    c = (np.arange(HPG * tq) % tq)[None, :]
    neg = np.float32(-np.inf)
    return jnp.asarray(np.stack([np.where(r <= c, np.float32(0), neg), np.where(c < r, np.float32(0), neg)]))


def _flash_win(q_t, k_r, v_t):
    B, _, S = q_t.shape
    tq = tk = FLASH_TQ
    assert tk == WIN_TK and WINDOW == 2 * tk
    cols = HPG * tq
    return pl.pallas_call(
        functools.partial(_flash_win_kernel, tk=tk),
        grid=(B, S // tq),
        in_specs=[pl.BlockSpec((None, WIDTH, tq), lambda b, i: (b, 0, i)),
                  pl.BlockSpec((None, S, KVW), lambda b, i: (b, 0, 0)),
                  pl.BlockSpec((None, KV_GROUPS, S // tk, LANE, tk), lambda b, i: (b, 0, 0, 0, 0)),
                  pl.BlockSpec((2, tk, cols), lambda b, i: (0, 0, 0))],
        out_specs=pl.BlockSpec((None, tq, WIDTH), lambda b, i: (b, i, 0)),
        out_shape=jax.ShapeDtypeStruct((B, S, WIDTH), BF16),
        scratch_shapes=[pltpu.VMEM((KV_GROUPS, WINDOW // tk + 1, tk, cols), F32)],
        compiler_params=_cparams(("parallel", "parallel")),
        name="flash_win",
    )(q_t, k_r, v_t, _window_bias(tq))


def _merge_kernel(ydn_ref, oc_ref, os_ref, ow_ref, small_ref, mgd_ref, mgn_ref, x_ref,
                  wdn_ref, wnsa_ref, wout_ref, g_ref, b_ref, o_ref):
    gates = _sigmoid(small_ref[...])
    cols = []
    for h in range(HEADS):
        sl = slice(h * LANE, (h + 1) * LANE)
        gc = gates[:, SMALL_GATE + h:SMALL_GATE + h + 1]
        gs = gates[:, SMALL_GATE + HEADS + h:SMALL_GATE + HEADS + h + 1]
        gw = gates[:, SMALL_GATE + 2 * HEADS + h:SMALL_GATE + 2 * HEADS + h + 1]
        cols.append((gc * oc_ref[:, sl] + gs * os_ref[:, sl] + gw * ow_ref[:, sl]).astype(BF16))
    y_nsa = jnp.concatenate(cols, axis=1)
    y = (_sigmoid(mgd_ref[...]) * _dot(ydn_ref[...], wdn_ref[...])
         + _sigmoid(mgn_ref[...]) * _dot(y_nsa, wnsa_ref[...]))
    z = DEEPNORM_ALPHA * x_ref[...] + _dot(y.astype(BF16), wout_ref[...])
    o_ref[...] = _layer_norm(z, g_ref[...], b_ref[...])


def _merge(y_dn2, oc2, os2, ow2, small2, h, x2, wdn, wnsa, wout, g, b):
    T, D = x2.shape
    tm = 256
    row = pl.BlockSpec((tm, WIDTH), lambda i: (i, 0))
    full = lambda shape: pl.BlockSpec(shape, lambda i: (0, 0))
    return pl.pallas_call(
        _merge_kernel,
        grid=(T // tm,),
        in_specs=[row, row, row, row,
                  pl.BlockSpec((tm, LANE), lambda i: (i, 0)),
                  pl.BlockSpec((tm, WIDTH), lambda i: (i, COL_MG_DN // WIDTH)),
                  pl.BlockSpec((tm, WIDTH), lambda i: (i, COL_MG_NSA // WIDTH)),
                  pl.BlockSpec((tm, D), lambda i: (i, 0)),
                  full((WIDTH, D)), full((WIDTH, D)), full((D, D)), full((1, D)), full((1, D))],
        out_specs=pl.BlockSpec((tm, D), lambda i: (i, 0)),
        out_shape=jax.ShapeDtypeStruct((T, D), F32),
        compiler_params=_cparams(("parallel",)),
        name="merge",
    )(y_dn2, oc2, os2, ow2, small2, h, h, x2, wdn, wnsa, wout, g, b)


def _ffn_up_kernel(x_ref, xp_ref, wg_ref, wv_ref, cwg_ref, cwv_ref, bg_ref, bv_ref, o_ref, *, tiles_per_seq):
    i = pl.program_id(1)
    tm = x_ref.shape[0]
    notfirst = (i % tiles_per_seq != 0).astype(F32)
    xb = x_ref[...].astype(BF16)
    xpb = (xp_ref[...] * notfirst).astype(BF16)
    row = lax.broadcasted_iota(jnp.int32, (tm, 1), 0)

    def conv(w_ref, cw_ref, b_ref):
        u = _dot(xb, w_ref[...])
        p = _dot(xpb, w_ref[...])
        u1 = jnp.where(row == 0, p[SUBLANE - 1:SUBLANE, :], pltpu.roll(u, 1, 0))
        u2 = jnp.where(row == 0, p[SUBLANE - 2:SUBLANE - 1, :],
                       jnp.where(row == 1, p[SUBLANE - 1:SUBLANE, :], pltpu.roll(u, 2, 0)))
        return cw_ref[0:1, :] * u2 + cw_ref[1:2, :] * u1 + cw_ref[2:3, :] * u + b_ref[...]

    o_ref[...] = (_silu(conv(wg_ref, cwg_ref, bg_ref)) * conv(wv_ref, cwv_ref, bv_ref)).astype(BF16)


def _ffn_up(x1, w_up, conv_w, conv_b, seq):
    T, D = x1.shape
    tm = 512
    nj = 2
    tn = D_FF // nj
    return pl.pallas_call(
        functools.partial(_ffn_up_kernel, tiles_per_seq=seq // tm),
        grid=(nj, T // tm),
        in_specs=[pl.BlockSpec((tm, D), lambda j, i: (i, 0)),
                  pl.BlockSpec((SUBLANE, D), lambda j, i: (jnp.maximum(i * (tm // SUBLANE) - 1, 0), 0)),
                  pl.BlockSpec((D, tn), lambda j, i: (0, j)),
                  pl.BlockSpec((D, tn), lambda j, i: (0, j + nj)),
                  pl.BlockSpec((FFN_CONV, tn), lambda j, i: (0, j)),
                  pl.BlockSpec((FFN_CONV, tn), lambda j, i: (0, j + nj)),
                  pl.BlockSpec((1, tn), lambda j, i: (0, j)),
                  pl.BlockSpec((1, tn), lambda j, i: (0, j + nj))],
        out_specs=pl.BlockSpec((tm, tn), lambda j, i: (i, j)),
        out_shape=jax.ShapeDtypeStruct((T, D_FF), BF16),
        compiler_params=_cparams(("parallel", "parallel")),
        name="ffn_up",
    )(x1, x1, w_up, w_up, conv_w, conv_w, conv_b, conv_b)


def _ffn_down_kernel(a_ref, w_ref, x_ref, g_ref, b_ref, o_ref):
    z = DEEPNORM_ALPHA * x_ref[...] + _dot(a_ref[...], w_ref[...])
    o_ref[...] = _layer_norm(z, g_ref[...], b_ref[...])


def _ffn_down(a, w_down, x1, g, b):
    T, D = x1.shape
    tm = 512
    full = lambda shape: pl.BlockSpec(shape, lambda i: (0, 0))
    return pl.pallas_call(
        _ffn_down_kernel,
        grid=(T // tm,),
        in_specs=[pl.BlockSpec((tm, D_FF), lambda i: (i, 0)), full((D_FF, D)),
                  pl.BlockSpec((tm, D), lambda i: (i, 0)), full((1, D)), full((1, D))],
        out_specs=pl.BlockSpec((tm, D), lambda i: (i, 0)),
        out_shape=jax.ShapeDtypeStruct((T, D), F32),
        compiler_params=_cparams(("parallel",)),
        name="ffn_down",
    )(a, w_down, x1, g, b)


def _split_w_in(w):
    d = w.shape[0]
    o_dz = 3 * WIDTH
    o_db = 4 * WIDTH
    o_nq = o_db + 2 * HEADS
    o_kc = o_nq + WIDTH
    o_ksl = o_kc + 2 * KVW
    o_gate = o_ksl + 4 * KVW
    o_mg = o_gate + 3 * HEADS
    pad = jnp.zeros((d, LANE - 5 * HEADS), w.dtype)
    w_a = jnp.concatenate([w[:, 0:o_dz], w[:, o_db:o_nq], w[:, o_gate:o_mg], pad], axis=1)
    w_b = jnp.concatenate([w[:, o_nq:o_kc], w[:, o_ksl:o_gate]], axis=1)
    w_c = jnp.concatenate([w[:, o_dz:o_db], w[:, o_mg:o_mg + 2 * WIDTH], w[:, o_kc:o_ksl]], axis=1)
    return w_a.astype(BF16), w_b.astype(BF16), w_c.astype(BF16)


def _rope_tables(pos):
    inv_freq = ROPE_THETA ** (-jnp.arange(ROPE_HALF, dtype=F32) / ROPE_HALF)
    ang = pos.astype(F32)[:, None] * inv_freq
    cos = jnp.cos(ang)
    sin = jnp.sin(ang)
    p = pos.shape[0]
    cosf = jnp.concatenate([cos, cos, jnp.ones((p, LANE - ROPE_DIM), F32)], axis=1)
    sinf = jnp.concatenate([-sin, sin, jnp.zeros((p, LANE - ROPE_DIM), F32)], axis=1)
    return cosf, sinf


def _overlap_table(nc, n_sel):
    n = np.arange(nc)[:, None]
    j = np.arange(LANE)[None, :]
    ov = (n * CMP_STRIDE <= j * SEL_LEN + SEL_LEN - 1) & (n * CMP_STRIDE + CMP_LEN - 1 >= j * SEL_LEN)
    ov = ov & (j < n_sel) & (n < nc - 1)
    return jnp.asarray(ov.astype(np.float32)).astype(BF16)


def _lane_row(vals, offset):
    return jnp.zeros((1, LANE), F32).at[0, offset:offset + vals.shape[0]].set(vals.astype(F32))


def kernel(x, w_in, dn_conv_w, dn_a_log, dn_dt_bias, dn_norm_w, cmp_pos_k, cmp_pos_v, cmp_k_w1, cmp_k_w2,
           cmp_v_w1, cmp_v_w2, w_branch_dn, w_branch_nsa, w_out, ln1_g, ln1_b, ffn_w_up, ffn_conv_w,
           ffn_conv_b, ffn_w_down, ln2_g, ln2_b):
    B, S, D = x.shape
    T = B * S
    nc = S // CMP_STRIDE
    n_sel = S // SEL_LEN
    assert D == WIDTH and S % 512 == 0 and n_sel <= LANE and n_sel % SUBLANE == 0
    cos_t, sin_t = _rope_tables(jnp.arange(S))
    cos_c, sin_c = _rope_tables(jnp.arange(nc) * CMP_STRIDE + (CMP_LEN - 1))
    overlap = _overlap_table(nc, n_sel)
    x2 = x.reshape(T, D)
    for layer in range(w_in.shape[0]):
        w_a, w_b, w_c = _split_w_in(w_in[layer])
        x3 = x2.reshape(B, S, D)
        h3 = _proj_rest(x3, w_c)

        q, k, v, bg, small = _proj_gdn(x3, w_a, dn_conv_w[layer], _lane_row(dn_a_log[layer], SMALL_DECAY),
                                       _lane_row(dn_dt_bias[layer], SMALL_DECAY))
        y_dn = _gdn(q, k, v, bg, h3, dn_norm_w[layer].reshape(1, LANE))

        q_r, q_t, kaug, vsl_t, kwn_r, vwn_t = _proj_nsa(x3, w_b, cos_t, sin_t)
        kc, vc = _compress(h3, cmp_pos_k[layer], cmp_pos_v[layer],
                           cmp_k_w1[layer].astype(BF16), cmp_k_w2[layer].astype(BF16),
                           cmp_v_w1[layer].astype(BF16), cmp_v_w2[layer].astype(BF16), cos_c, sin_c)
        o_cmp, bias = _cmp_attn(q_r, kc, vc, overlap)
        o_sel = _flash_sel(q_t, kaug, vsl_t, bias)
        o_win = _flash_win(q_t, kwn_r, vwn_t)

        x1 = _merge(y_dn.reshape(T, WIDTH), o_cmp.reshape(T, WIDTH), o_sel.reshape(T, WIDTH),
                    o_win.reshape(T, WIDTH), small.reshape(T, LANE), h3.reshape(T, NC_COLS), x2,
                    w_branch_dn[layer].astype(BF16), w_branch_nsa[layer].astype(BF16),
                    w_out[layer].astype(BF16), ln1_g[layer].reshape(1, D), ln1_b[layer].reshape(1, D))
        a = _ffn_up(x1, ffn_w_up[layer].astype(BF16), ffn_conv_w[layer],
                    ffn_conv_b[layer].reshape(1, 2 * D_FF), S)
        x2 = _ffn_down(a, ffn_w_down[layer].astype(BF16), x1, ln2_g[layer].reshape(1, D), ln2_b[layer].reshape(1, D))
    return x2.reshape(B, S, D)
```

```python
import functools
import math

import numpy as np
import jax
import jax.numpy as jnp
from jax import lax
from jax.experimental import pallas as pl
from jax.experimental.pallas import tpu as pltpu

F32 = jnp.float32
BF16 = jnp.bfloat16
HI = lax.Precision.HIGHEST

HEADS = 8
HEAD_DIM = 128
KV_GROUPS = 2
HPG = HEADS // KV_GROUPS
DN_CONV = 4
DN_CHUNK = 64
CMP_LEN = 32
CMP_STRIDE = 16
CMP_HIDDEN = 256
SEL_LEN = 64
SEL_TOPK = 16
WINDOW = 512
ROPE_THETA = 500000.0
ROPE_DIM = HEAD_DIM // 4
ROPE_HALF = ROPE_DIM // 2
D_FF = 2816
FFN_CONV = 3
LN_EPS = 1e-5
RMS_EPS = 1e-6
DEEPNORM_ALPHA = 2.0 ** 0.25
ATT_SCALE = HEAD_DIM ** -0.5

LANE = 128
SUBLANE = 8
VMEM_LIMIT = 56 * 1024 * 1024

WIDTH = HEADS * HEAD_DIM
KVW = KV_GROUPS * HEAD_DIM
NA = 3 * WIDTH + LANE
SMALL_BETA = 0
SMALL_DECAY = HEADS
SMALL_GATE = 2 * HEADS
NB = WIDTH + 4 * KVW
NC_COLS = 3 * WIDTH + 2 * KVW
COL_DZ = 0
COL_MG_DN = WIDTH
COL_MG_NSA = 2 * WIDTH
COL_KC = 3 * WIDTH
COL_VC = COL_KC + KVW

NEG_BIAS = -(2.0 ** 100)


def _cparams(sem):
    return pltpu.CompilerParams(dimension_semantics=sem, vmem_limit_bytes=VMEM_LIMIT)


def _sigmoid(x):
    return 1.0 / (1.0 + jnp.exp(-x))


def _silu(x):
    return x * _sigmoid(x)


def _dot(a, b, precision=None):
    return jnp.dot(a, b, preferred_element_type=F32, precision=precision)


def _dot_nt(a, b):
    return lax.dot_general(a, b, (((1,), (1,)), ((), ())), preferred_element_type=F32)


def _dot_tn(a, b):
    return lax.dot_general(a, b, (((0,), (0,)), ((), ())), preferred_element_type=F32)


def _layer_norm(z, g, b):
    mu = jnp.mean(z, -1, keepdims=True)
    zc = z - mu
    var = jnp.mean(zc * zc, -1, keepdims=True)
    return zc * lax.rsqrt(var + LN_EPS) * g + b


PROJ_TM = 512


def _proj_rest_kernel(x_ref, w_ref, g_ref, c_ref):
    h = _dot(x_ref[...].astype(BF16), w_ref[...])
    g_ref[...] = h[:, 0:COL_KC].astype(BF16)
    c_ref[...] = h[:, COL_KC:NC_COLS]


def _proj_rest(x3, w):
    B, S, D = x3.shape
    tm = PROJ_TM
    return pl.pallas_call(
        _proj_rest_kernel,
        grid=(B, S // tm),
        in_specs=[pl.BlockSpec((None, tm, D), lambda b, i: (b, i, 0)),
                  pl.BlockSpec((D, NC_COLS), lambda b, i: (0, 0))],
        out_specs=[pl.BlockSpec((None, tm, COL_KC), lambda b, i: (b, i, 0)),
                   pl.BlockSpec((None, tm, NC_COLS - COL_KC), lambda b, i: (b, i, 0))],
        out_shape=[jax.ShapeDtypeStruct((B, S, COL_KC), BF16),
                   jax.ShapeDtypeStruct((B, S, NC_COLS - COL_KC), F32)],
        compiler_params=_cparams(("parallel", "parallel")),
        name="proj_rest",
    )(x3, w)


def _proj_gdn_kernel(x_ref, w_ref, cw_ref, alog_ref, dtb_ref, q_ref, k_ref, v_ref, bg_ref, small_ref, buf_ref):
    i = pl.program_id(1)
    ts = x_ref.shape[0]
    wq = 3 * WIDTH

    @pl.when(i == 0)
    def _():
        buf_ref[0:SUBLANE, :] = jnp.zeros((SUBLANE, wq), F32)

    h = _dot(x_ref[...].astype(BF16), w_ref[...])
    buf_ref[SUBLANE:SUBLANE + ts, :] = h[:, 0:wq]
    small = h[:, wq:wq + LANE]
    for c in range(3 * HEADS):
        sl = slice(c * LANE, (c + 1) * LANE)
        acc = cw_ref[DN_CONV - 1:DN_CONV, sl] * buf_ref[SUBLANE:SUBLANE + ts, sl]
        for s in range(1, DN_CONV):
            acc = acc + cw_ref[DN_CONV - 1 - s:DN_CONV - s, sl] * buf_ref[pl.ds(SUBLANE - s, ts), sl]
        y = _silu(acc)
        if c < 2 * HEADS:
            y = y * lax.rsqrt(jnp.sum(y * y, -1, keepdims=True) + RMS_EPS)
        if c < HEADS:
            q_ref[:, sl] = (y * ATT_SCALE).astype(BF16)
        elif c < 2 * HEADS:
            k_ref[:, slice((c - HEADS) * LANE, (c - HEADS + 1) * LANE)] = y.astype(BF16)
        else:
            v_ref[:, slice((c - 2 * HEADS) * LANE, (c - 2 * HEADS + 1) * LANE)] = y.astype(BF16)
    buf_ref[0:SUBLANE, :] = buf_ref[ts:ts + SUBLANE, :]
    small_ref[...] = small
    beta = _sigmoid(small)
    xs = small + dtb_ref[...]
    softplus = jnp.maximum(xs, 0.0) + jnp.log1p(jnp.exp(-jnp.abs(xs)))
    g = -jnp.exp(alog_ref[...]) * softplus
    lane = lax.broadcasted_iota(jnp.int32, small.shape, 1)
    bg_ref[...] = jnp.where(lane < SMALL_DECAY, beta, g)


def _proj_gdn(x3, w_a, conv_w, alog_row, dtb_row):
    B, S, D = x3.shape
    ts = PROJ_TM
    wq = 3 * WIDTH
    out_sd = jax.ShapeDtypeStruct((B, S, WIDTH), BF16)
    small_sd = jax.ShapeDtypeStruct((B, S, LANE), F32)
    wide = pl.BlockSpec((None, ts, WIDTH), lambda b, i: (b, i, 0))
    narrow = pl.BlockSpec((None, ts, LANE), lambda b, i: (b, i, 0))
    return pl.pallas_call(
        _proj_gdn_kernel,
        grid=(B, S // ts),
        in_specs=[
            pl.BlockSpec((None, ts, D), lambda b, i: (b, i, 0)),
            pl.BlockSpec((D, NA), lambda b, i: (0, 0)),
            pl.BlockSpec((DN_CONV, wq), lambda b, i: (0, 0)),
            pl.BlockSpec((1, LANE), lambda b, i: (0, 0)),
            pl.BlockSpec((1, LANE), lambda b, i: (0, 0)),
        ],
        out_specs=[wide, wide, wide, narrow, narrow],
        out_shape=[out_sd, out_sd, out_sd, small_sd, small_sd],
        scratch_shapes=[pltpu.VMEM((ts + SUBLANE, wq), F32)],
        compiler_params=_cparams(("parallel", "arbitrary")),
        name="proj_gdn",
    )(x3, w_a, conv_w, alog_row, dtb_row)


GDN_CHUNKS_PER_STEP = 2
GDN_BATCH_PER_STEP = 2


GDN_QUAD = 4


def _gdn_kernel(q_ref, k_ref, v_ref, bg_ref, z_ref, nw_ref, y_ref, state_ref):
    n = pl.program_id(1)

    @pl.when(n == 0)
    def _():
        state_ref[...] = jnp.zeros(state_ref.shape, F32)

    C = DN_CHUNK
    QW = GDN_QUAD * C
    QL = GDN_QUAD * LANE
    shift_c = int(math.log2(C))
    r64 = lax.broadcasted_iota(jnp.int32, (C, C), 0)
    c64 = lax.broadcasted_iota(jnp.int32, (C, C), 1)
    tril = jnp.where(r64 >= c64, 1.0, 0.0).astype(F32)
    triu = jnp.where(r64 <= c64, 1.0, 0.0).astype(F32)
    rr = lax.broadcasted_iota(jnp.int32, (C, QW), 0)
    ll = lax.broadcasted_iota(jnp.int32, (C, QW), 1)
    jj = ll & (C - 1)
    hb = ll >> shift_c
    incl_t = rr >= jj
    strict_t = rr > jj
    eye_t = jnp.where(rr == jj, 1.0, 0.0).astype(F32)
    bd_c = jnp.where((lax.broadcasted_iota(jnp.int32, (QW, QW), 0) >> shift_c)
                     == (lax.broadcasted_iota(jnp.int32, (QW, QW), 1) >> shift_c), 1.0, 0.0).astype(BF16)
    bd_l = jnp.where((lax.broadcasted_iota(jnp.int32, (QW, QL), 0) >> shift_c)
                     == (lax.broadcasted_iota(jnp.int32, (QW, QL), 1) >> int(math.log2(LANE))), 1.0, 0.0).astype(BF16)
    zeros_rhs = jnp.zeros((C, 2 * LANE), BF16)
    nw = nw_ref[...]
    nb = q_ref.shape[0]
    quads = [(bb, c, a) for bb in range(nb) for c in range(GDN_CHUNKS_PER_STEP) for a in range(HEADS // GDN_QUAD)]
    gc_cols, gc_rows, bgs = {}, {}, {}
    for bb in range(nb):
        bg = bg_ref[bb]
        bg_t = bg.T
        bgs[bb] = bg
        for c in range(GDN_CHUNKS_PER_STEP):
            rs = slice(c * C, (c + 1) * C)
            gc_cols[bb, c] = _dot(tril, bg[rs, :], HI)
            gc_rows[bb, c] = _dot(bg_t[SMALL_DECAY:SMALL_DECAY + HEADS, rs], triu, HI)
    decay, lower, qk, rhs_bd, qd, kd, eglast = {}, {}, {}, {}, {}, {}, {}
    for u_ in quads:
        bb, c, a = u_
        rs = slice(c * C, (c + 1) * C)
        h0 = a * GDN_QUAD
        gcols = [gc_cols[bb, c][:, SMALL_DECAY + h0 + hh:SMALL_DECAY + h0 + hh + 1] for hh in range(GDN_QUAD)]
        gcol_cat = jnp.broadcast_to(gcols[0], (C, QW))
        for hh in range(1, GDN_QUAD):
            gcol_cat = jnp.where(hb == hh, gcols[hh], gcol_cat)
        grow_cat = jnp.concatenate([gc_rows[bb, c][h0 + hh:h0 + hh + 1, :] for hh in range(GDN_QUAD)], axis=1)
        decay[u_] = jnp.exp(jnp.where(incl_t, gcol_cat - grow_cat, -jnp.inf))
        kq = k_ref[bb, rs, a * QL:(a + 1) * QL].astype(F32)
        qq = q_ref[bb, rs, a * QL:(a + 1) * QL].astype(F32)
        kbs, rows_bd = [], []
        for hh in range(GDN_QUAD):
            h = h0 + hh
            ls = slice(hh * LANE, (hh + 1) * LANE)
            beta = bgs[bb][rs, SMALL_BETA + h:SMALL_BETA + h + 1]
            eg = jnp.exp(gcols[hh])
            glast = gcols[hh][C - 1:C, :]
            kb = kq[:, ls] * beta
            vb = v_ref[bb, rs, h * LANE:(h + 1) * LANE].astype(F32) * beta
            kbs.append(kb)
            rhs_h = jnp.concatenate([vb, kb * eg], axis=1).astype(BF16)
            rows_bd.append(jnp.concatenate([zeros_rhs] * hh + [rhs_h] + [zeros_rhs] * (GDN_QUAD - 1 - hh), axis=1))
            qd[bb, c, h] = (qq[:, ls] * eg).astype(BF16)
            kd[bb, c, h] = (kq[:, ls] * jnp.exp(glast - gcols[hh])).astype(BF16)
            eglast[bb, c, h] = jnp.exp(glast)
        rhs_bd[u_] = jnp.concatenate(rows_bd, axis=0)
        k_bd = jnp.concatenate([kq.astype(BF16)] * GDN_QUAD, axis=0) * bd_l
        lhs = jnp.concatenate([jnp.concatenate(kbs, axis=1), qq], axis=0).astype(BF16)
        aq = _dot_nt(lhs, k_bd)
        lower[u_] = jnp.where(strict_t, aq[0:C] * decay[u_], 0.0)
        qk[u_] = (aq[C:2 * C] * decay[u_]).astype(BF16)

    def block_diag(y_b):
        return jnp.concatenate([y_b] * GDN_QUAD, axis=0) * bd_c

    tinv = {u_: eye_t - lower[u_] for u_ in quads}
    y_b = {u_: lower[u_].astype(BF16) for u_ in quads}
    ypow = {u_: _dot(y_b[u_], block_diag(y_b[u_])) for u_ in quads}
    for it in range(5):
        for u_ in quads:
            yb = ypow[u_].astype(BF16)
            if it < 4:
                r = _dot(jnp.concatenate([tinv[u_].astype(BF16), yb], axis=0), block_diag(yb))
                tinv[u_] = tinv[u_] + r[0:C]
                ypow[u_] = r[C:2 * C]
            else:
                tinv[u_] = tinv[u_] + _dot(tinv[u_].astype(BF16), block_diag(yb))
    uw = {u_: _dot(tinv[u_].astype(BF16), rhs_bd[u_]) for u_ in quads}
    zeros_v = jnp.zeros((C, LANE), BF16)
    chains = [(bb, h) for bb in range(nb) for h in range(HEADS)]
    for c in range(GDN_CHUNKS_PER_STEP):
        rs = slice(c * C, (c + 1) * C)
        states = {s_: state_ref[s_[0] * HEADS + s_[1]] for s_ in chains}
        ws_qs = {}
        for s_ in chains:
            bb, h = s_
            a, hh = divmod(h, GDN_QUAD)
            w_h = uw[bb, c, a][:, hh * 2 * LANE + LANE:(hh + 1) * 2 * LANE]
            ws_qs[s_] = _dot(jnp.concatenate([w_h.astype(BF16), qd[bb, c, h]], axis=0), states[s_].astype(BF16))
        v_new = {}
        for s_ in chains:
            bb, h = s_
            a, hh = divmod(h, GDN_QUAD)
            v_new[s_] = (uw[bb, c, a][:, hh * 2 * LANE:hh * 2 * LANE + LANE] - ws_qs[s_][0:C]).astype(BF16)
        outs = {}
        for s_ in chains:
            bb, h = s_
            a, hh = divmod(h, GDN_QUAD)
            v_pad = jnp.concatenate([zeros_v] * hh + [v_new[s_]] + [zeros_v] * (GDN_QUAD - 1 - hh), axis=0)
            outs[s_] = ws_qs[s_][C:2 * C] + _dot(qk[bb, c, a], v_pad)
            state_ref[bb * HEADS + h] = states[s_] * eglast[bb, c, h] + _dot_tn(kd[bb, c, h], v_new[s_])
        for s_ in chains:
            bb, h = s_
            ls = slice(h * LANE, (h + 1) * LANE)
            o = outs[s_]
            o = o * lax.rsqrt(jnp.mean(o * o, -1, keepdims=True) + RMS_EPS)
            y_ref[bb, rs, ls] = (o * nw * _silu(z_ref[bb, rs, ls].astype(F32))).astype(BF16)


def _gdn(q, k, v, bg, h3, norm_w):
    B, S, _ = q.shape
    rows = GDN_CHUNKS_PER_STEP * DN_CHUNK
    nb = GDN_BATCH_PER_STEP if B % GDN_BATCH_PER_STEP == 0 else 1
    tile = pl.BlockSpec((nb, rows, WIDTH), lambda b, n: (b, n, 0))
    return pl.pallas_call(
        _gdn_kernel,
        grid=(B // nb, S // rows),
        in_specs=[tile, tile, tile,
                  pl.BlockSpec((nb, rows, LANE), lambda b, n: (b, n, 0)),
                  pl.BlockSpec((nb, rows, WIDTH), lambda b, n: (b, n, COL_DZ // WIDTH)),
                  pl.BlockSpec((1, LANE), lambda b, n: (0, 0))],
        out_specs=tile,
        out_shape=jax.ShapeDtypeStruct((B, S, WIDTH), BF16),
        scratch_shapes=[pltpu.VMEM((nb * HEADS, HEAD_DIM, HEAD_DIM), F32)],
        compiler_params=_cparams(("parallel", "arbitrary")),
        name="gdn",
    )(q, k, v, bg, h3, norm_w)


def _rope(t, cosf, sinf, lo):
    partner = jnp.where(lo, pltpu.roll(t, LANE - ROPE_HALF, 1), pltpu.roll(t, ROPE_HALF, 1))
    return t * cosf + partner * sinf


SEL_TK = 256
WIN_TK = 256
SEL_VROWS = LANE + 16
LOG2E = math.log2(math.e)


def _proj_nsa_kernel(x_ref, w_ref, cos_ref, sin_ref, qt_o, kaug_o, vslt_o, kwn_o, vwnt_o):
    i = pl.program_id(1)
    ts = x_ref.shape[0]
    h = _dot(x_ref[...].astype(BF16), w_ref[...])
    c_ksl, c_vsl, c_kwn, c_vwn = WIDTH, WIDTH + KVW, WIDTH + 2 * KVW, WIDTH + 3 * KVW
    cosf = cos_ref[...]
    sinf = sin_ref[...]
    lane = lax.broadcasted_iota(jnp.int32, cosf.shape, 1)
    lo = lane < ROPE_HALF
    for hd in range(HEADS):
        sl = slice(hd * LANE, (hd + 1) * LANE)
        qt_o[sl, :] = (_rope(h[:, sl], cosf, sinf, lo) * (ATT_SCALE * LOG2E)).T.astype(BF16)
    tpos = i * ts + lax.broadcasted_iota(jnp.int32, cosf.shape, 0)
    onehot = jnp.where((tpos >> int(math.log2(SEL_LEN))) == lane, 1.0, 0.0).astype(BF16)
    for g in range(KV_GROUPS):
        sl = slice(g * LANE, (g + 1) * LANE)
        off = g * LANE
        kaug_o[:, 2 * g * LANE:(2 * g + 1) * LANE] = _rope(h[:, c_ksl + off:c_ksl + off + LANE], cosf, sinf, lo).astype(BF16)
        kaug_o[:, (2 * g + 1) * LANE:(2 * g + 2) * LANE] = onehot
        kwn_o[:, sl] = _rope(h[:, c_kwn + off:c_kwn + off + LANE], cosf, sinf, lo).astype(BF16)
        vs_t = h[:, c_vsl + off:c_vsl + off + LANE].T.astype(BF16)
        for c in range(ts // SEL_TK):
            ones_rows = jnp.where(lax.broadcasted_iota(jnp.int32, (SEL_VROWS - LANE, SEL_TK), 0) == 0, 1.0, 0.0)
            vslt_o[g, c] = jnp.concatenate([vs_t[:, c * SEL_TK:(c + 1) * SEL_TK], ones_rows.astype(BF16)], axis=0)
        vw_t = h[:, c_vwn + off:c_vwn + off + LANE].T.astype(BF16)
        for c in range(ts // WIN_TK):
            vwnt_o[g, c] = vw_t[:, c * WIN_TK:(c + 1) * WIN_TK]


def _proj_nsa(x3, w_b, cos_tab, sin_tab):
    B, S, D = x3.shape
    ts = PROJ_TM
    tab = pl.BlockSpec((ts, LANE), lambda b, i: (i, 0))
    return pl.pallas_call(
        _proj_nsa_kernel,
        grid=(B, S // ts),
        in_specs=[pl.BlockSpec((None, ts, D), lambda b, i: (b, i, 0)),
                  pl.BlockSpec((D, NB), lambda b, i: (0, 0)), tab, tab],
        out_specs=[pl.BlockSpec((None, WIDTH, ts), lambda b, i: (b, 0, i)),
                   pl.BlockSpec((None, ts, 2 * KVW), lambda b, i: (b, i, 0)),
                   pl.BlockSpec((None, KV_GROUPS, ts // SEL_TK, SEL_VROWS, SEL_TK), lambda b, i: (b, 0, i, 0, 0)),
                   pl.BlockSpec((None, ts, KVW), lambda b, i: (b, i, 0)),
                   pl.BlockSpec((None, KV_GROUPS, ts // WIN_TK, LANE, WIN_TK), lambda b, i: (b, 0, i, 0, 0))],
        out_shape=[jax.ShapeDtypeStruct((B, WIDTH, S), BF16),
                   jax.ShapeDtypeStruct((B, S, 2 * KVW), BF16),
                   jax.ShapeDtypeStruct((B, KV_GROUPS, S // SEL_TK, SEL_VROWS, SEL_TK), BF16),
                   jax.ShapeDtypeStruct((B, S, KVW), BF16),
                   jax.ShapeDtypeStruct((B, KV_GROUPS, S // WIN_TK, LANE, WIN_TK), BF16)],
        compiler_params=_cparams(("parallel", "parallel")),
        name="proj_nsa",
    )(x3, w_b, cos_tab, sin_tab)


def _compress_one(t_ref, pe_ref, w1_ref, w2_ref, nc):
    half = CMP_LEN // 2
    y1 = jnp.zeros((nc, CMP_HIDDEN), F32)
    y2 = jnp.zeros((nc, CMP_HIDDEN), F32)
    for l in range(half):
        x = t_ref[pl.ds(l, nc, stride=CMP_STRIDE), :]
        y1 = y1 + _dot((x + pe_ref[l:l + 1, :]).astype(BF16), w1_ref[l * LANE:(l + 1) * LANE, :])
        y2 = y2 + _dot((x + pe_ref[half + l:half + l + 1, :]).astype(BF16),
                       w1_ref[(half + l) * LANE:(half + l + 1) * LANE, :])
    y = y1 + pltpu.roll(y2, nc - 1, 0)
    return _dot(_silu(y).astype(BF16), w2_ref[...])


def _compress_kernel(kc_ref, vc_ref, pek_ref, pev_ref, w1k_ref, w2k_ref, w1v_ref, w2v_ref,
                     cos_ref, sin_ref, kc_o, vc_o):
    nc = kc_o.shape[0]
    kc = _compress_one(kc_ref, pek_ref, w1k_ref, w2k_ref, nc)
    lo = lax.broadcasted_iota(jnp.int32, kc.shape, 1) < ROPE_HALF
    kc_o[...] = _rope(kc, cos_ref[...], sin_ref[...], lo).astype(BF16)
    vc_o[...] = _compress_one(vc_ref, pev_ref, w1v_ref, w2v_ref, nc).astype(BF16)


def _compress(h3, pek, pev, w1k, w2k, w1v, w2v, cos_c, sin_c):
    B, S, _ = h3.shape
    nc = S // CMP_STRIDE
    full = lambda shape: pl.BlockSpec(shape, lambda b, g: tuple(0 for _ in shape))
    out_spec = pl.BlockSpec((None, None, nc, LANE), lambda b, g: (b, g, 0, 0))
    out_sd = jax.ShapeDtypeStruct((B, KV_GROUPS, nc, LANE), BF16)
    cin = CMP_LEN * HEAD_DIM
    return pl.pallas_call(
        _compress_kernel,
        grid=(B, KV_GROUPS),
        in_specs=[pl.BlockSpec((None, S, LANE), lambda b, g: (b, 0, g)),
                  pl.BlockSpec((None, S, LANE), lambda b, g: (b, 0, KV_GROUPS + g)),
                  full((CMP_LEN, LANE)), full((CMP_LEN, LANE)),
                  full((cin, CMP_HIDDEN)), full((CMP_HIDDEN, LANE)),
                  full((cin, CMP_HIDDEN)), full((CMP_HIDDEN, LANE)),
                  full((nc, LANE)), full((nc, LANE))],
        out_specs=[out_spec, out_spec],
        out_shape=[out_sd, out_sd],
        compiler_params=_cparams(("parallel", "parallel")),
        name="compress",
    )(h3, h3, pek, pev, w1k, w2k, w1v, w2v, cos_c, sin_c)


def _cmp_attn_kernel(q_ref, kc_ref, vc_ref, ovt_ref, o_ref, bias_ref, st_ref, cnt_ref, *, n_sel):
    i = pl.program_id(2)
    tq = q_ref.shape[1]
    cols = HPG * tq
    nc = kc_ref.shape[0]
    t0 = i * tq
    q_g = jnp.concatenate([q_ref[hh * LANE:(hh + 1) * LANE, :] for hh in range(HPG)], axis=1)
    tpos = t0 + (lax.broadcasted_iota(jnp.int32, (1, cols), 1) & (tq - 1))
    cmp_end = lax.broadcasted_iota(jnp.int32, (nc, 1), 0) * CMP_STRIDE + (CMP_LEN - 1)
    s = jnp.where(cmp_end <= tpos, _dot(kc_ref[...], q_g), -jnp.inf)
    m = jnp.max(s, axis=0, keepdims=True)
    m = jnp.where(m > -jnp.inf, m, 0.0)
    p = jnp.exp2(s - m)
    p = p / jnp.maximum(jnp.sum(p, axis=0, keepdims=True), jnp.finfo(F32).tiny)
    vct = vc_ref[...].astype(F32).T.astype(BF16)
    _store_heads(o_ref, _dot(vct, p.astype(BF16)), 0, tq)
    psum = p[:, 0:tq]
    for hh in range(1, HPG):
        psum = psum + p[:, hh * tq:(hh + 1) * tq]
    ovt = ovt_ref[...]
    p_hi = psum.astype(BF16)
    r1 = psum - p_hi.astype(F32)
    p_mid = r1.astype(BF16)
    p_lo = (r1 - p_mid.astype(F32)).astype(BF16)
    p_sel = _dot(ovt, p_hi) + (_dot(ovt, p_mid) + _dot(ovt, p_lo))
    jb = lax.broadcasted_iota(jnp.int32, (LANE, tq), 0)
    cur = (t0 + lax.broadcasted_iota(jnp.int32, (1, tq), 1)) >> int(math.log2(SEL_LEN))
    valid = jb <= cur
    forced = valid & ((jb == 0) | (jb == cur) | (jb == cur - 1))
    st = jnp.where(forced, jnp.inf, jnp.where(valid, p_sel, -jnp.inf))
    n_grp = n_sel // SUBLANE
    st_ref[...] = st
    cnt_ref[...] = jnp.zeros(cnt_ref.shape, F32)
    jrow = lax.broadcasted_iota(jnp.int32, (SUBLANE, tq), 0)
    cur_max = (t0 + tq - 1) >> int(math.log2(SEL_LEN))
    for gb in range(n_grp):
        @pl.when(cur_max >= gb * SUBLANE)
        def _():
            grp = [st_ref[g * SUBLANE:(g + 1) * SUBLANE, :] for g in range(n_grp)]
            cnt = [cnt_ref[g * SUBLANE:(g + 1) * SUBLANE, :] for g in range(n_grp)]
            for rb_ in range(SUBLANE):
                rb = grp[gb][rb_:rb_ + 1, :]
                for g in range(n_grp):
                    if g < gb:
                        inc = jnp.where(rb > grp[g], 1.0, 0.0)
                    elif g > gb:
                        inc = jnp.where(rb >= grp[g], 1.0, 0.0)
                    else:
                        inc = jnp.where(jrow > rb_, jnp.where(rb >= grp[g], 1.0, 0.0), jnp.where(rb > grp[g], 1.0, 0.0))
                    cnt[g] = cnt[g] + inc
            for g in range(n_grp):
                cnt_ref[g * SUBLANE:(g + 1) * SUBLANE, :] = cnt[g]

    keep = (cnt_ref[...] < float(SEL_TOPK)) & (st_ref[0:n_sel, :] > -jnp.inf)
    bias = jnp.where(keep, 0.0, NEG_BIAS)
    if n_sel < LANE:
        bias = jnp.concatenate([bias, jnp.zeros((LANE - n_sel, tq), F32)], axis=0)
    bias_ref[...] = bias.astype(BF16)


def _cmp_attn(q_t, kc, vc, overlap_t):
    B, _, S = q_t.shape
    nc = kc.shape[2]
    n_sel = S // SEL_LEN
    tq = 256
    gw = HPG * LANE
    return pl.pallas_call(
        functools.partial(_cmp_attn_kernel, n_sel=n_sel),
        grid=(B, KV_GROUPS, S // tq),
        in_specs=[pl.BlockSpec((None, gw, tq), lambda b, g, i: (b, g, i)),
                  pl.BlockSpec((None, None, nc, LANE), lambda b, g, i: (b, g, 0, 0)),
                  pl.BlockSpec((None, None, nc, LANE), lambda b, g, i: (b, g, 0, 0)),
                  pl.BlockSpec((LANE, nc), lambda b, g, i: (0, 0))],
        out_specs=[pl.BlockSpec((None, tq, gw), lambda b, g, i: (b, i, g)),
                   pl.BlockSpec((None, None, LANE, tq), lambda b, g, i: (b, g, 0, i))],
        out_shape=[jax.ShapeDtypeStruct((B, S, WIDTH), BF16),
                   jax.ShapeDtypeStruct((B, KV_GROUPS, LANE, S), BF16)],
        scratch_shapes=[pltpu.VMEM((LANE, tq), F32), pltpu.VMEM((n_sel, tq), F32)],
        compiler_params=_cparams(("parallel", "parallel", "parallel")),
        name="cmp_attn",
    )(q_t, kc, vc, overlap_t)


FLASH_TQ = 256


def _store_heads(o_ref, out_t, g, tq):
    for hh in range(HPG):
        h = g * HPG + hh
        o_ref[:, h * LANE:(h + 1) * LANE] = out_t[:, hh * tq:(hh + 1) * tq].T.astype(BF16)


def _flash_sel_kernel(q_ref, b_ref, k_ref, vt_ref, o_ref, qs_ref, s_ref, m_ref, acc_ref, *, tk):
    i = pl.program_id(1)
    tq = q_ref.shape[1]
    cols = HPG * tq
    t0 = i * tq
    kw = qs_ref.shape[1]
    for g in range(KV_GROUPS):
        for hh in range(HPG):
            h = g * HPG + hh
            qs_ref[g, 0:LANE, hh * tq:(hh + 1) * tq] = q_ref[h * LANE:(h + 1) * LANE, :]
            qs_ref[g, LANE:2 * LANE, hh * tq:(hh + 1) * tq] = b_ref[g]
    m_ref[...] = jnp.full(m_ref.shape, -jnp.inf, F32)
    acc_ref[...] = jnp.zeros(acc_ref.shape, F32)
    qpos = t0 + (lax.broadcasted_iota(jnp.int32, (1, cols), 1) & (tq - 1))

    def scores(j, slot):
        k0 = pl.multiple_of(j * tk, tk)
        for g in range(KV_GROUPS):
            s_ref[slot, g] = _dot(k_ref[pl.ds(k0, tk), g * kw:(g + 1) * kw], qs_ref[g])

    def update(j, slot, masked):
        if masked:
            mask = j * tk + lax.broadcasted_iota(jnp.int32, (tk, 1), 0) <= qpos
        for g in range(KV_GROUPS):
            s = s_ref[slot, g]
            if masked:
                s = jnp.where(mask, s, -jnp.inf)
            m_old = m_ref[g]
            m_new = jnp.maximum(m_old, jnp.max(s, axis=0, keepdims=True))
            alpha = jnp.exp2(m_old - m_new)
            p = jnp.exp2(s - m_new)
            acc_ref[g] = alpha * acc_ref[g] + _dot(vt_ref[g, j], p.astype(BF16))
            m_ref[g] = m_new

    jd = t0 // tk
    last = jnp.maximum(jd - 1, 0)
    scores(jd, 0)
    scores(0, 1)
    update(jd, 0, True)

    def body(p, carry):
        scores(jnp.minimum(2 * p + 1, last), 0)
        update(2 * p, 1, False)
        scores(jnp.minimum(2 * p + 2, last), 1)
        update(2 * p + 1, 0, False)
        return carry

    lax.fori_loop(0, jd // 2, body, 0)

    @pl.when(jd % 2 == 1)
    def _():
        update(jd - 1, 1, False)

    for g in range(KV_GROUPS):
        _store_heads(o_ref, acc_ref[g, 0:LANE] / jnp.maximum(acc_ref[g, LANE:LANE + 1], jnp.finfo(F32).tiny), g, tq)


def _flash_sel(q_t, kaug, v_t, bias_t):
    B, _, S = q_t.shape
    tq = tk = FLASH_TQ
    assert tk == SEL_TK
    kw = kaug.shape[2] // KV_GROUPS
    cols = HPG * tq
    return pl.pallas_call(
        functools.partial(_flash_sel_kernel, tk=tk),
        grid=(B, S // tq),
        in_specs=[pl.BlockSpec((None, WIDTH, tq), lambda b, i: (b, 0, i)),
                  pl.BlockSpec((None, KV_GROUPS, LANE, tq), lambda b, i: (b, 0, 0, i)),
                  pl.BlockSpec((None, S, KV_GROUPS * kw), lambda b, i: (b, 0, 0)),
                  pl.BlockSpec((None, KV_GROUPS, S // tk, SEL_VROWS, tk), lambda b, i: (b, 0, 0, 0, 0))],
        out_specs=pl.BlockSpec((None, tq, WIDTH), lambda b, i: (b, i, 0)),
        out_shape=jax.ShapeDtypeStruct((B, S, WIDTH), BF16),
        scratch_shapes=[pltpu.VMEM((KV_GROUPS, kw, cols), BF16),
                        pltpu.VMEM((2, KV_GROUPS, tk, cols), F32),
                        pltpu.VMEM((KV_GROUPS, 1, cols), F32),
                        pltpu.VMEM((KV_GROUPS, SEL_VROWS, cols), F32)],
        compiler_params=_cparams(("parallel", "parallel")),
        name="flash_sel",
    )(q_t, bias_t, kaug, v_t)


def _flash_win_kernel(q_ref, k_ref, vt_ref, wb_ref, o_ref, s_ref, *, tk):
    i = pl.program_id(1)
    tq = q_ref.shape[1]
    cols = HPG * tq
    n_t = WINDOW // tk + 1
    jt = [jnp.maximum(i - d, 0) for d in range(n_t)]
    for g in range(KV_GROUPS):
        q_g = jnp.concatenate([q_ref[(g * HPG + hh) * LANE:(g * HPG + hh + 1) * LANE, :] for hh in range(HPG)], axis=1)
        for d in range(n_t):
            k0 = pl.multiple_of(jt[d] * tk, tk)
            s_ref[g, d] = _dot(k_ref[pl.ds(k0, tk), g * LANE:(g + 1) * LANE], q_g)
    static_bias = {0: 0, n_t - 1: 1}

    def tile(g, d):
        s = s_ref[g, d]
        return s + wb_ref[static_bias[d]] if d in static_bias else s

    for g in range(KV_GROUPS):
        m = jnp.max(tile(g, 0), axis=0, keepdims=True)
        for d in range(1, n_t):
            m = jnp.maximum(m, jnp.where(i >= d, jnp.max(tile(g, d), axis=0, keepdims=True), -jnp.inf))
        l = jnp.zeros((1, cols), F32)
        acc = jnp.zeros((LANE, cols), F32)
        for d in range(n_t):
            p = jnp.exp2(tile(g, d) - (m if d == 0 else jnp.where(i >= d, m, jnp.inf)))
            l = l + jnp.sum(p, axis=0, keepdims=True)
            acc = acc + _dot(vt_ref[g, jt[d]], p.astype(BF16))
        _store_heads(o_ref, acc / jnp.maximum(l, jnp.finfo(F32).tiny), g, tq)


def _window_bias(tq):
    r = np.arange(tq)[:, None]
    c = (np.arange(HPG * tq) % tq)[None, :]
    neg = np.float32(-np.inf)
    return jnp.asarray(np.stack([np.where(r <= c, np.float32(0), neg), np.where(c < r, np.float32(0), neg)]))


def _flash_win(q_t, k_r, v_t):
    B, _, S = q_t.shape
    tq = tk = FLASH_TQ
    assert tk == WIN_TK and WINDOW == 2 * tk
    cols = HPG * tq
    return pl.pallas_call(
        functools.partial(_flash_win_kernel, tk=tk),
        grid=(B, S // tq),
        in_specs=[pl.BlockSpec((None, WIDTH, tq), lambda b, i: (b, 0, i)),
                  pl.BlockSpec((None, S, KVW), lambda b, i: (b, 0, 0)),
                  pl.BlockSpec((None, KV_GROUPS, S // tk, LANE, tk), lambda b, i: (b, 0, 0, 0, 0)),
                  pl.BlockSpec((2, tk, cols), lambda b, i: (0, 0, 0))],
        out_specs=pl.BlockSpec((None, tq, WIDTH), lambda b, i: (b, i, 0)),
        out_shape=jax.ShapeDtypeStruct((B, S, WIDTH), BF16),
        scratch_shapes=[pltpu.VMEM((KV_GROUPS, WINDOW // tk + 1, tk, cols), F32)],
        compiler_params=_cparams(("parallel", "parallel")),
        name="flash_win",
    )(q_t, k_r, v_t, _window_bias(tq))


def _merge_kernel(ydn_ref, oc_ref, os_ref, ow_ref, small_ref, mgd_ref, mgn_ref, x_ref,
                  wdn_ref, wnsa_ref, wout_ref, g_ref, b_ref, o_ref):
    gates = _sigmoid(small_ref[...])
    cols = []
    for h in range(HEADS):
        sl = slice(h * LANE, (h + 1) * LANE)
        gc = gates[:, SMALL_GATE + h:SMALL_GATE + h + 1]
        gs = gates[:, SMALL_GATE + HEADS + h:SMALL_GATE + HEADS + h + 1]
        gw = gates[:, SMALL_GATE + 2 * HEADS + h:SMALL_GATE + 2 * HEADS + h + 1]
        cols.append((gc * oc_ref[:, sl] + gs * os_ref[:, sl] + gw * ow_ref[:, sl]).astype(BF16))
    y_nsa = jnp.concatenate(cols, axis=1)
    y = (_sigmoid(mgd_ref[...].astype(F32)) * _dot(ydn_ref[...], wdn_ref[...])
         + _sigmoid(mgn_ref[...].astype(F32)) * _dot(y_nsa, wnsa_ref[...]))
    z = DEEPNORM_ALPHA * x_ref[...] + _dot(y.astype(BF16), wout_ref[...])
    o_ref[...] = _layer_norm(z, g_ref[...], b_ref[...])


def _merge(y_dn2, oc2, os2, ow2, small2, h, x2, wdn, wnsa, wout, g, b):
    T, D = x2.shape
    tm = 256
    row = pl.BlockSpec((tm, WIDTH), lambda i: (i, 0))
    full = lambda shape: pl.BlockSpec(shape, lambda i: (0, 0))
    return pl.pallas_call(
        _merge_kernel,
        grid=(T // tm,),
        in_specs=[row, row, row, row,
                  pl.BlockSpec((tm, LANE), lambda i: (i, 0)),
                  pl.BlockSpec((tm, WIDTH), lambda i: (i, COL_MG_DN // WIDTH)),
                  pl.BlockSpec((tm, WIDTH), lambda i: (i, COL_MG_NSA // WIDTH)),
                  pl.BlockSpec((tm, D), lambda i: (i, 0)),
                  full((WIDTH, D)), full((WIDTH, D)), full((D, D)), full((1, D)), full((1, D))],
        out_specs=pl.BlockSpec((tm, D), lambda i: (i, 0)),
        out_shape=jax.ShapeDtypeStruct((T, D), F32),
        compiler_params=_cparams(("parallel",)),
        name="merge",
    )(y_dn2, oc2, os2, ow2, small2, h, h, x2, wdn, wnsa, wout, g, b)


def _ffn_up_kernel(x_ref, xp_ref, wg_ref, wv_ref, cwg_ref, cwv_ref, bg_ref, bv_ref, o_ref, *, tiles_per_seq):
    i = pl.program_id(1)
    tm = x_ref.shape[0]
    notfirst = (i % tiles_per_seq != 0).astype(F32)
    xb = x_ref[...].astype(BF16)
    xpb = (xp_ref[...] * notfirst).astype(BF16)
    row = lax.broadcasted_iota(jnp.int32, (tm, 1), 0)

    def conv(w_ref, cw_ref, b_ref):
        u = _dot(xb, w_ref[...])
        p = _dot(xpb, w_ref[...])
        u1 = jnp.where(row == 0, p[SUBLANE - 1:SUBLANE, :], pltpu.roll(u, 1, 0))
        u2 = jnp.where(row == 0, p[SUBLANE - 2:SUBLANE - 1, :],
                       jnp.where(row == 1, p[SUBLANE - 1:SUBLANE, :], pltpu.roll(u, 2, 0)))
        return cw_ref[0:1, :] * u2 + cw_ref[1:2, :] * u1 + cw_ref[2:3, :] * u + b_ref[...]

    o_ref[...] = (_silu(conv(wg_ref, cwg_ref, bg_ref)) * conv(wv_ref, cwv_ref, bv_ref)).astype(BF16)


def _ffn_up(x1, w_up, conv_w, conv_b, seq):
    T, D = x1.shape
    tm = 512
    nj = 2
    tn = D_FF // nj
    return pl.pallas_call(
        functools.partial(_ffn_up_kernel, tiles_per_seq=seq // tm),
        grid=(nj, T // tm),
        in_specs=[pl.BlockSpec((tm, D), lambda j, i: (i, 0)),
                  pl.BlockSpec((SUBLANE, D), lambda j, i: (jnp.maximum(i * (tm // SUBLANE) - 1, 0), 0)),
                  pl.BlockSpec((D, tn), lambda j, i: (0, j)),
                  pl.BlockSpec((D, tn), lambda j, i: (0, j + nj)),
                  pl.BlockSpec((FFN_CONV, tn), lambda j, i: (0, j)),
                  pl.BlockSpec((FFN_CONV, tn), lambda j, i: (0, j + nj)),
                  pl.BlockSpec((1, tn), lambda j, i: (0, j)),
                  pl.BlockSpec((1, tn), lambda j, i: (0, j + nj))],
        out_specs=pl.BlockSpec((tm, tn), lambda j, i: (i, j)),
        out_shape=jax.ShapeDtypeStruct((T, D_FF), BF16),
        compiler_params=_cparams(("parallel", "parallel")),
        name="ffn_up",
    )(x1, x1, w_up, w_up, conv_w, conv_w, conv_b, conv_b)


def _ffn_down_kernel(a_ref, w_ref, x_ref, g_ref, b_ref, o_ref):
    z = DEEPNORM_ALPHA * x_ref[...] + _dot(a_ref[...], w_ref[...])
    o_ref[...] = _layer_norm(z, g_ref[...], b_ref[...])


def _ffn_down(a, w_down, x1, g, b):
    T, D = x1.shape
    tm = 512
    full = lambda shape: pl.BlockSpec(shape, lambda i: (0, 0))
    return pl.pallas_call(
        _ffn_down_kernel,
        grid=(T // tm,),
        in_specs=[pl.BlockSpec((tm, D_FF), lambda i: (i, 0)), full((D_FF, D)),
                  pl.BlockSpec((tm, D), lambda i: (i, 0)), full((1, D)), full((1, D))],
        out_specs=pl.BlockSpec((tm, D), lambda i: (i, 0)),
        out_shape=jax.ShapeDtypeStruct((T, D), F32),
        compiler_params=_cparams(("parallel",)),
        name="ffn_down",
    )(a, w_down, x1, g, b)


def _split_w_in(w):
    d = w.shape[0]
    o_dz = 3 * WIDTH
    o_db = 4 * WIDTH
    o_nq = o_db + 2 * HEADS
    o_kc = o_nq + WIDTH
    o_ksl = o_kc + 2 * KVW
    o_gate = o_ksl + 4 * KVW
    o_mg = o_gate + 3 * HEADS
    pad = jnp.zeros((d, LANE - 5 * HEADS), w.dtype)
    w_a = jnp.concatenate([w[:, 0:o_dz], w[:, o_db:o_nq], w[:, o_gate:o_mg], pad], axis=1)
    w_b = jnp.concatenate([w[:, o_nq:o_kc], w[:, o_ksl:o_gate]], axis=1)
    w_c = jnp.concatenate([w[:, o_dz:o_db], w[:, o_mg:o_mg + 2 * WIDTH], w[:, o_kc:o_ksl]], axis=1)
    return w_a.astype(BF16), w_b.astype(BF16), w_c.astype(BF16)


def _rope_tables(pos):
    inv_freq = ROPE_THETA ** (-jnp.arange(ROPE_HALF, dtype=F32) / ROPE_HALF)
    ang = pos.astype(F32)[:, None] * inv_freq
    cos = jnp.cos(ang)
    sin = jnp.sin(ang)
    p = pos.shape[0]
    cosf = jnp.concatenate([cos, cos, jnp.ones((p, LANE - ROPE_DIM), F32)], axis=1)
    sinf = jnp.concatenate([-sin, sin, jnp.zeros((p, LANE - ROPE_DIM), F32)], axis=1)
    return cosf, sinf


def _overlap_table(nc, n_sel):
    n = np.arange(nc)[None, :]
    j = np.arange(LANE)[:, None]
    ov = (n * CMP_STRIDE <= j * SEL_LEN + SEL_LEN - 1) & (n * CMP_STRIDE + CMP_LEN - 1 >= j * SEL_LEN)
    ov = ov & (j < n_sel) & (n < nc - 1)
    return jnp.asarray(ov.astype(np.float32)).astype(BF16)


def _lane_row(vals, offset):
    return jnp.zeros((1, LANE), F32).at[0, offset:offset + vals.shape[0]].set(vals.astype(F32))


def kernel(x, w_in, dn_conv_w, dn_a_log, dn_dt_bias, dn_norm_w, cmp_pos_k, cmp_pos_v, cmp_k_w1, cmp_k_w2,
           cmp_v_w1, cmp_v_w2, w_branch_dn, w_branch_nsa, w_out, ln1_g, ln1_b, ffn_w_up, ffn_conv_w,
           ffn_conv_b, ffn_w_down, ln2_g, ln2_b):
    B, S, D = x.shape
    T = B * S
    nc = S // CMP_STRIDE
    n_sel = S // SEL_LEN
    assert D == WIDTH and S % 512 == 0 and n_sel <= LANE and n_sel % SUBLANE == 0
    cos_t, sin_t = _rope_tables(jnp.arange(S))
    cos_c, sin_c = _rope_tables(jnp.arange(nc) * CMP_STRIDE + (CMP_LEN - 1))
    overlap = _overlap_table(nc, n_sel)
    x2 = x.reshape(T, D)
    for layer in range(w_in.shape[0]):
        w_a, w_b, w_c = _split_w_in(w_in[layer])
        x3 = x2.reshape(B, S, D)
        h3, kv_cmp = _proj_rest(x3, w_c)

        q, k, v, bg, small = _proj_gdn(x3, w_a, dn_conv_w[layer], _lane_row(dn_a_log[layer], SMALL_DECAY),
                                       _lane_row(dn_dt_bias[layer], SMALL_DECAY))
        y_dn = _gdn(q, k, v, bg, h3, dn_norm_w[layer].reshape(1, LANE))

        q_t, kaug, vsl_t, kwn_r, vwn_t = _proj_nsa(x3, w_b, cos_t, sin_t)
        kc, vc = _compress(kv_cmp, cmp_pos_k[layer], cmp_pos_v[layer],
                           cmp_k_w1[layer].astype(BF16), cmp_k_w2[layer].astype(BF16),
                           cmp_v_w1[layer].astype(BF16), cmp_v_w2[layer].astype(BF16), cos_c, sin_c)
        o_cmp, bias = _cmp_attn(q_t, kc, vc, overlap)
        o_sel = _flash_sel(q_t, kaug, vsl_t, bias)
        o_win = _flash_win(q_t, kwn_r, vwn_t)

        x1 = _merge(y_dn.reshape(T, WIDTH), o_cmp.reshape(T, WIDTH), o_sel.reshape(T, WIDTH),
                    o_win.reshape(T, WIDTH), small.reshape(T, LANE), h3.reshape(T, COL_KC), x2,
                    w_branch_dn[layer].astype(BF16), w_branch_nsa[layer].astype(BF16),
                    w_out[layer].astype(BF16), ln1_g[layer].reshape(1, D), ln1_b[layer].reshape(1, D))
        a = _ffn_up(x1, ffn_w_up[layer].astype(BF16), ffn_conv_w[layer],
                    ffn_conv_b[layer].reshape(1, 2 * D_FF), S)
        x2 = _ffn_down(a, ffn_w_down[layer].astype(BF16), x1, ln2_g[layer].reshape(1, D), ln2_b[layer].reshape(1, D))
    return x2.reshape(B, S, D)
```

```python
import functools
import math

import numpy as np
import jax
import jax.numpy as jnp
from jax import lax
from jax.experimental import pallas as pl
from jax.experimental.pallas import tpu as pltpu

F32 = jnp.float32
BF16 = jnp.bfloat16
HI = lax.Precision.HIGHEST

HEADS = 8
HEAD_DIM = 128
KV_GROUPS = 2
HPG = HEADS // KV_GROUPS
DN_CONV = 4
DN_CHUNK = 64
CMP_LEN = 32
CMP_STRIDE = 16
CMP_HIDDEN = 256
SEL_LEN = 64
SEL_TOPK = 16
WINDOW = 512
ROPE_THETA = 500000.0
ROPE_DIM = HEAD_DIM // 4
ROPE_HALF = ROPE_DIM // 2
D_FF = 2816
FFN_CONV = 3
LN_EPS = 1e-5
RMS_EPS = 1e-6
DEEPNORM_ALPHA = 2.0 ** 0.25
ATT_SCALE = HEAD_DIM ** -0.5

LANE = 128
SUBLANE = 8
VMEM_LIMIT = 56 * 1024 * 1024

WIDTH = HEADS * HEAD_DIM
KVW = KV_GROUPS * HEAD_DIM
NA = 3 * WIDTH + LANE
SMALL_BETA = 0
SMALL_DECAY = HEADS
SMALL_GATE = 2 * HEADS
NB = WIDTH + 4 * KVW
NC_COLS = 3 * WIDTH + 2 * KVW
COL_DZ = 0
COL_MG_DN = WIDTH
COL_MG_NSA = 2 * WIDTH
COL_KC = 3 * WIDTH
COL_VC = COL_KC + KVW

NEG_BIAS = -(2.0 ** 100)


def _cparams(sem):
    return pltpu.CompilerParams(dimension_semantics=sem, vmem_limit_bytes=VMEM_LIMIT)


def _sigmoid(x):
    return 1.0 / (1.0 + jnp.exp(-x))


def _silu(x):
    return x * _sigmoid(x)


def _dot(a, b, precision=None):
    return jnp.dot(a, b, preferred_element_type=F32, precision=precision)


def _dot_nt(a, b):
    return lax.dot_general(a, b, (((1,), (1,)), ((), ())), preferred_element_type=F32)


def _dot_tn(a, b):
    return lax.dot_general(a, b, (((0,), (0,)), ((), ())), preferred_element_type=F32)


def _layer_norm(z, g, b):
    mu = jnp.mean(z, -1, keepdims=True)
    zc = z - mu
    var = jnp.mean(zc * zc, -1, keepdims=True)
    return zc * lax.rsqrt(var + LN_EPS) * g + b


PROJ_TM = 512


def _proj_rest_kernel(x_ref, w_ref, g_ref, c_ref):
    h = _dot(x_ref[...].astype(BF16), w_ref[...])
    g_ref[...] = h[:, 0:COL_KC].astype(BF16)
    c_ref[...] = h[:, COL_KC:NC_COLS]


def _proj_rest(x3, w):
    B, S, D = x3.shape
    tm = PROJ_TM
    return pl.pallas_call(
        _proj_rest_kernel,
        grid=(B, S // tm),
        in_specs=[pl.BlockSpec((None, tm, D), lambda b, i: (b, i, 0)),
                  pl.BlockSpec((D, NC_COLS), lambda b, i: (0, 0))],
        out_specs=[pl.BlockSpec((None, tm, COL_KC), lambda b, i: (b, i, 0)),
                   pl.BlockSpec((None, tm, NC_COLS - COL_KC), lambda b, i: (b, i, 0))],
        out_shape=[jax.ShapeDtypeStruct((B, S, COL_KC), BF16),
                   jax.ShapeDtypeStruct((B, S, NC_COLS - COL_KC), F32)],
        compiler_params=_cparams(("parallel", "parallel")),
        name="proj_rest",
    )(x3, w)


def _proj_gdn_kernel(x_ref, w_ref, cw_ref, alog_ref, dtb_ref, q_ref, k_ref, v_ref, bg_ref, small_ref,
                     xs_ref, ys_ref, carry_ref):
    i = pl.program_id(1)
    ts = x_ref.shape[0]
    nv = ts // SUBLANE
    wq = 3 * WIDTH
    n_slab = x_ref.shape[1] // LANE

    @pl.when(i == 0)
    def _():
        carry_ref[...] = jnp.zeros(carry_ref.shape, F32)

    for sp in range(SUBLANE):
        for c in range(n_slab):
            xs_ref[c, pl.ds(sp, nv, stride=SUBLANE), :] = x_ref[sp * nv:(sp + 1) * nv, c * LANE:(c + 1) * LANE]
    xb = jnp.concatenate([xs_ref[c] for c in range(n_slab)], axis=1).astype(BF16)
    h = _dot(xb, w_ref[...])
    sub = lax.broadcasted_iota(jnp.int32, (SUBLANE, 1), 0)

    def token_order(c, dst_ref, dst_sl, dtype):
        for sp in range(SUBLANE):
            dst_ref[sp * nv:(sp + 1) * nv, dst_sl] = ys_ref[c, pl.ds(sp, nv, stride=SUBLANE), :].astype(dtype)

    for c in range(3 * HEADS):
        sl = slice(c * LANE, (c + 1) * LANE)
        col = h[:, sl]
        wrap = [jnp.where(sub == 0, carry_ref[SUBLANE - k:SUBLANE - k + 1, sl],
                          pltpu.roll(col[(nv - k) * SUBLANE:(nv - k + 1) * SUBLANE, :], 1, 0))
                for k in range(1, DN_CONV)]
        acc = cw_ref[DN_CONV - 1:DN_CONV, sl] * col
        for s in range(1, DN_CONV):
            shifted = jnp.concatenate(wrap[0:s][::-1] + [col[0:(nv - s) * SUBLANE, :]], axis=0)
            acc = acc + cw_ref[DN_CONV - 1 - s:DN_CONV - s, sl] * shifted
        y = _silu(acc)
        if c < 2 * HEADS:
            y = y * lax.rsqrt(jnp.sum(y * y, -1, keepdims=True) + RMS_EPS)
        if c < HEADS:
            y = y * ATT_SCALE
        ys_ref[c] = y
        dst = (q_ref, k_ref, v_ref)[c // HEADS]
        token_order(c, dst, slice((c % HEADS) * LANE, (c % HEADS + 1) * LANE), BF16)
    for k in range(1, DN_CONV):
        carry_ref[SUBLANE - k:SUBLANE - k + 1, :] = h[(nv - k + 1) * SUBLANE - 1:(nv - k + 1) * SUBLANE, 0:wq]
    small = h[:, wq:wq + LANE]
    beta = _sigmoid(small)
    xs = small + dtb_ref[...]
    softplus = jnp.maximum(xs, 0.0) + jnp.log1p(jnp.exp(-jnp.abs(xs)))
    g = -jnp.exp(alog_ref[...]) * softplus
    lane = lax.broadcasted_iota(jnp.int32, small.shape, 1)
    ys_ref[3 * HEADS] = small
    token_order(3 * HEADS, small_ref, slice(0, LANE), F32)
    ys_ref[3 * HEADS + 1] = jnp.where(lane < SMALL_DECAY, beta, g)
    token_order(3 * HEADS + 1, bg_ref, slice(0, LANE), F32)


def _proj_gdn(x3, w_a, conv_w, alog_row, dtb_row):
    B, S, D = x3.shape
    ts = PROJ_TM
    wq = 3 * WIDTH
    out_sd = jax.ShapeDtypeStruct((B, S, WIDTH), BF16)
    small_sd = jax.ShapeDtypeStruct((B, S, LANE), F32)
    wide = pl.BlockSpec((None, ts, WIDTH), lambda b, i: (b, i, 0))
    narrow = pl.BlockSpec((None, ts, LANE), lambda b, i: (b, i, 0))
    return pl.pallas_call(
        _proj_gdn_kernel,
        grid=(B, S // ts),
        in_specs=[
            pl.BlockSpec((None, ts, D), lambda b, i: (b, i, 0)),
            pl.BlockSpec((D, NA), lambda b, i: (0, 0)),
            pl.BlockSpec((DN_CONV, wq), lambda b, i: (0, 0)),
            pl.BlockSpec((1, LANE), lambda b, i: (0, 0)),
            pl.BlockSpec((1, LANE), lambda b, i: (0, 0)),
        ],
        out_specs=[wide, wide, wide, narrow, narrow],
        out_shape=[out_sd, out_sd, out_sd, small_sd, small_sd],
        scratch_shapes=[pltpu.VMEM((D // LANE, ts, LANE), F32),
                        pltpu.VMEM((3 * HEADS + 2, ts, LANE), F32),
                        pltpu.VMEM((SUBLANE, wq), F32)],
        compiler_params=_cparams(("parallel", "arbitrary")),
        name="proj_gdn",
    )(x3, w_a, conv_w, alog_row, dtb_row)


GDN_CHUNKS_PER_STEP = 2
GDN_BATCH_PER_STEP = 2


GDN_QUAD = 4


def _gdn_kernel(q_ref, k_ref, v_ref, bg_ref, z_ref, nw_ref, y_ref, state_ref):
    n = pl.program_id(1)

    @pl.when(n == 0)
    def _():
        state_ref[...] = jnp.zeros(state_ref.shape, F32)

    C = DN_CHUNK
    QW = GDN_QUAD * C
    QL = GDN_QUAD * LANE
    shift_c = int(math.log2(C))
    r64 = lax.broadcasted_iota(jnp.int32, (C, C), 0)
    c64 = lax.broadcasted_iota(jnp.int32, (C, C), 1)
    tril = jnp.where(r64 >= c64, 1.0, 0.0).astype(F32)
    triu = jnp.where(r64 <= c64, 1.0, 0.0).astype(F32)
    rr = lax.broadcasted_iota(jnp.int32, (C, QW), 0)
    ll = lax.broadcasted_iota(jnp.int32, (C, QW), 1)
    jj = ll & (C - 1)
    hb = ll >> shift_c
    incl_t = rr >= jj
    strict_t = rr > jj
    eye_t = jnp.where(rr == jj, 1.0, 0.0).astype(F32)
    bd_c = jnp.where((lax.broadcasted_iota(jnp.int32, (QW, QW), 0) >> shift_c)
                     == (lax.broadcasted_iota(jnp.int32, (QW, QW), 1) >> shift_c), 1.0, 0.0).astype(BF16)
    bd_l = jnp.where((lax.broadcasted_iota(jnp.int32, (QW, QL), 0) >> shift_c)
                     == (lax.broadcasted_iota(jnp.int32, (QW, QL), 1) >> int(math.log2(LANE))), 1.0, 0.0).astype(BF16)
    zeros_rhs = jnp.zeros((C, 2 * LANE), BF16)
    nw = nw_ref[...]
    nb = q_ref.shape[0]
    quads = [(bb, c, a) for bb in range(nb) for c in range(GDN_CHUNKS_PER_STEP) for a in range(HEADS // GDN_QUAD)]
    gc_cols, gc_rows, bgs = {}, {}, {}
    for bb in range(nb):
        bg = bg_ref[bb]
        bg_t = bg.T
        bgs[bb] = bg
        for c in range(GDN_CHUNKS_PER_STEP):
            rs = slice(c * C, (c + 1) * C)
            gc_cols[bb, c] = _dot(tril, bg[rs, :], HI)
            gc_rows[bb, c] = _dot(bg_t[SMALL_DECAY:SMALL_DECAY + HEADS, rs], triu, HI)
    decay, lower, qk, rhs_bd, qd, kd, eglast = {}, {}, {}, {}, {}, {}, {}
    for u_ in quads:
        bb, c, a = u_
        rs = slice(c * C, (c + 1) * C)
        h0 = a * GDN_QUAD
        gcols = [gc_cols[bb, c][:, SMALL_DECAY + h0 + hh:SMALL_DECAY + h0 + hh + 1] for hh in range(GDN_QUAD)]
        gcol_cat = jnp.broadcast_to(gcols[0], (C, QW))
        for hh in range(1, GDN_QUAD):
            gcol_cat = jnp.where(hb == hh, gcols[hh], gcol_cat)
        grow_cat = jnp.concatenate([gc_rows[bb, c][h0 + hh:h0 + hh + 1, :] for hh in range(GDN_QUAD)], axis=1)
        decay[u_] = jnp.exp(jnp.where(incl_t, gcol_cat - grow_cat, -jnp.inf))
        kq = k_ref[bb, rs, a * QL:(a + 1) * QL].astype(F32)
        qq = q_ref[bb, rs, a * QL:(a + 1) * QL].astype(F32)
        kbs, rows_bd = [], []
        for hh in range(GDN_QUAD):
            h = h0 + hh
            ls = slice(hh * LANE, (hh + 1) * LANE)
            beta = bgs[bb][rs, SMALL_BETA + h:SMALL_BETA + h + 1]
            eg = jnp.exp(gcols[hh])
            glast = gcols[hh][C - 1:C, :]
            kb = kq[:, ls] * beta
            vb = v_ref[bb, rs, h * LANE:(h + 1) * LANE].astype(F32) * beta
            kbs.append(kb)
            rhs_h = jnp.concatenate([vb, kb * eg], axis=1).astype(BF16)
            rows_bd.append(jnp.concatenate([zeros_rhs] * hh + [rhs_h] + [zeros_rhs] * (GDN_QUAD - 1 - hh), axis=1))
            qd[bb, c, h] = (qq[:, ls] * eg).astype(BF16)
            kd[bb, c, h] = (kq[:, ls] * jnp.exp(glast - gcols[hh])).astype(BF16)
            eglast[bb, c, h] = jnp.exp(glast)
        rhs_bd[u_] = jnp.concatenate(rows_bd, axis=0)
        k_bd = jnp.concatenate([kq.astype(BF16)] * GDN_QUAD, axis=0) * bd_l
        lhs = jnp.concatenate([jnp.concatenate(kbs, axis=1), qq], axis=0).astype(BF16)
        aq = _dot_nt(lhs, k_bd)
        lower[u_] = jnp.where(strict_t, aq[0:C] * decay[u_], 0.0)
        qk[u_] = (aq[C:2 * C] * decay[u_]).astype(BF16)

    def block_diag(y_b):
        return jnp.concatenate([y_b] * GDN_QUAD, axis=0) * bd_c

    tinv = {u_: eye_t - lower[u_] for u_ in quads}
    y_b = {u_: lower[u_].astype(BF16) for u_ in quads}
    ypow = {u_: _dot(y_b[u_], block_diag(y_b[u_])) for u_ in quads}
    for it in range(5):
        for u_ in quads:
            yb = ypow[u_].astype(BF16)
            if it < 4:
                r = _dot(jnp.concatenate([tinv[u_].astype(BF16), yb], axis=0), block_diag(yb))
                tinv[u_] = tinv[u_] + r[0:C]
                ypow[u_] = r[C:2 * C]
            else:
                tinv[u_] = tinv[u_] + _dot(tinv[u_].astype(BF16), block_diag(yb))
    uw = {u_: _dot(tinv[u_].astype(BF16), rhs_bd[u_]) for u_ in quads}
    zeros_v = jnp.zeros((C, LANE), BF16)
    chains = [(bb, h) for bb in range(nb) for h in range(HEADS)]
    for c in range(GDN_CHUNKS_PER_STEP):
        rs = slice(c * C, (c + 1) * C)
        states = {s_: state_ref[s_[0] * HEADS + s_[1]] for s_ in chains}
        ws_qs = {}
        for s_ in chains:
            bb, h = s_
            a, hh = divmod(h, GDN_QUAD)
            w_h = uw[bb, c, a][:, hh * 2 * LANE + LANE:(hh + 1) * 2 * LANE]
            ws_qs[s_] = _dot(jnp.concatenate([w_h.astype(BF16), qd[bb, c, h]], axis=0), states[s_].astype(BF16))
        v_new = {}
        for s_ in chains:
            bb, h = s_
            a, hh = divmod(h, GDN_QUAD)
            v_new[s_] = (uw[bb, c, a][:, hh * 2 * LANE:hh * 2 * LANE + LANE] - ws_qs[s_][0:C]).astype(BF16)
        outs = {}
        for s_ in chains:
            bb, h = s_
            a, hh = divmod(h, GDN_QUAD)
            v_pad = jnp.concatenate([zeros_v] * hh + [v_new[s_]] + [zeros_v] * (GDN_QUAD - 1 - hh), axis=0)
            outs[s_] = ws_qs[s_][C:2 * C] + _dot(qk[bb, c, a], v_pad)
            state_ref[bb * HEADS + h] = states[s_] * eglast[bb, c, h] + _dot_tn(kd[bb, c, h], v_new[s_])
        for s_ in chains:
            bb, h = s_
            ls = slice(h * LANE, (h + 1) * LANE)
            o = outs[s_]
            o = o * lax.rsqrt(jnp.mean(o * o, -1, keepdims=True) + RMS_EPS)
            y_ref[bb, rs, ls] = (o * nw * _silu(z_ref[bb, rs, ls].astype(F32))).astype(BF16)


def _gdn(q, k, v, bg, h3, norm_w):
    B, S, _ = q.shape
    rows = GDN_CHUNKS_PER_STEP * DN_CHUNK
    nb = GDN_BATCH_PER_STEP if B % GDN_BATCH_PER_STEP == 0 else 1
    tile = pl.BlockSpec((nb, rows, WIDTH), lambda b, n: (b, n, 0))
    return pl.pallas_call(
        _gdn_kernel,
        grid=(B // nb, S // rows),
        in_specs=[tile, tile, tile,
                  pl.BlockSpec((nb, rows, LANE), lambda b, n: (b, n, 0)),
                  pl.BlockSpec((nb, rows, WIDTH), lambda b, n: (b, n, COL_DZ // WIDTH)),
                  pl.BlockSpec((1, LANE), lambda b, n: (0, 0))],
        out_specs=tile,
        out_shape=jax.ShapeDtypeStruct((B, S, WIDTH), BF16),
        scratch_shapes=[pltpu.VMEM((nb * HEADS, HEAD_DIM, HEAD_DIM), F32)],
        compiler_params=_cparams(("parallel", "arbitrary")),
        name="gdn",
    )(q, k, v, bg, h3, norm_w)


def _rope(t, cosf, sinf, lo):
    partner = jnp.where(lo, pltpu.roll(t, LANE - ROPE_HALF, 1), pltpu.roll(t, ROPE_HALF, 1))
    return t * cosf + partner * sinf


SEL_TK = 256
WIN_TK = 256
SEL_VROWS = LANE + 16
LOG2E = math.log2(math.e)


def _proj_nsa_kernel(x_ref, w_ref, cos_ref, sin_ref, qt_o, kaug_o, vslt_o, kwn_o, vwnt_o):
    i = pl.program_id(1)
    ts = x_ref.shape[0]
    h = _dot(x_ref[...].astype(BF16), w_ref[...])
    c_ksl, c_vsl, c_kwn, c_vwn = WIDTH, WIDTH + KVW, WIDTH + 2 * KVW, WIDTH + 3 * KVW
    cosf = cos_ref[...]
    sinf = sin_ref[...]
    lane = lax.broadcasted_iota(jnp.int32, cosf.shape, 1)
    lo = lane < ROPE_HALF
    for hd in range(HEADS):
        sl = slice(hd * LANE, (hd + 1) * LANE)
        qt_o[sl, :] = (_rope(h[:, sl], cosf, sinf, lo) * (ATT_SCALE * LOG2E)).T.astype(BF16)
    tpos = i * ts + lax.broadcasted_iota(jnp.int32, cosf.shape, 0)
    onehot = jnp.where((tpos >> int(math.log2(SEL_LEN))) == lane, 1.0, 0.0).astype(BF16)
    for g in range(KV_GROUPS):
        sl = slice(g * LANE, (g + 1) * LANE)
        off = g * LANE
        kaug_o[:, 2 * g * LANE:(2 * g + 1) * LANE] = _rope(h[:, c_ksl + off:c_ksl + off + LANE], cosf, sinf, lo).astype(BF16)
        kaug_o[:, (2 * g + 1) * LANE:(2 * g + 2) * LANE] = onehot
        kwn_o[:, sl] = _rope(h[:, c_kwn + off:c_kwn + off + LANE], cosf, sinf, lo).astype(BF16)
        vs_t = h[:, c_vsl + off:c_vsl + off + LANE].T.astype(BF16)
        for c in range(ts // SEL_TK):
            ones_rows = jnp.where(lax.broadcasted_iota(jnp.int32, (SEL_VROWS - LANE, SEL_TK), 0) == 0, 1.0, 0.0)
            vslt_o[g, c] = jnp.concatenate([vs_t[:, c * SEL_TK:(c + 1) * SEL_TK], ones_rows.astype(BF16)], axis=0)
        vw_t = h[:, c_vwn + off:c_vwn + off + LANE].T.astype(BF16)
        for c in range(ts // WIN_TK):
            vwnt_o[g, c] = vw_t[:, c * WIN_TK:(c + 1) * WIN_TK]


def _proj_nsa(x3, w_b, cos_tab, sin_tab):
    B, S, D = x3.shape
    ts = PROJ_TM
    tab = pl.BlockSpec((ts, LANE), lambda b, i: (i, 0))
    return pl.pallas_call(
        _proj_nsa_kernel,
        grid=(B, S // ts),
        in_specs=[pl.BlockSpec((None, ts, D), lambda b, i: (b, i, 0)),
                  pl.BlockSpec((D, NB), lambda b, i: (0, 0)), tab, tab],
        out_specs=[pl.BlockSpec((None, WIDTH, ts), lambda b, i: (b, 0, i)),
                   pl.BlockSpec((None, ts, 2 * KVW), lambda b, i: (b, i, 0)),
                   pl.BlockSpec((None, KV_GROUPS, ts // SEL_TK, SEL_VROWS, SEL_TK), lambda b, i: (b, 0, i, 0, 0)),
                   pl.BlockSpec((None, ts, KVW), lambda b, i: (b, i, 0)),
                   pl.BlockSpec((None, KV_GROUPS, ts // WIN_TK, LANE, WIN_TK), lambda b, i: (b, 0, i, 0, 0))],
        out_shape=[jax.ShapeDtypeStruct((B, WIDTH, S), BF16),
                   jax.ShapeDtypeStruct((B, S, 2 * KVW), BF16),
                   jax.ShapeDtypeStruct((B, KV_GROUPS, S // SEL_TK, SEL_VROWS, SEL_TK), BF16),
                   jax.ShapeDtypeStruct((B, S, KVW), BF16),
                   jax.ShapeDtypeStruct((B, KV_GROUPS, S // WIN_TK, LANE, WIN_TK), BF16)],
        compiler_params=_cparams(("parallel", "parallel")),
        name="proj_nsa",
    )(x3, w_b, cos_tab, sin_tab)


def _compress_one(t_ref, pe_ref, w1_ref, w2_ref, nc):
    half = CMP_LEN // 2
    y1 = jnp.zeros((nc, CMP_HIDDEN), F32)
    y2 = jnp.zeros((nc, CMP_HIDDEN), F32)
    for l in range(half):
        x = t_ref[pl.ds(l, nc, stride=CMP_STRIDE), :]
        y1 = y1 + _dot((x + pe_ref[l:l + 1, :]).astype(BF16), w1_ref[l * LANE:(l + 1) * LANE, :])
        y2 = y2 + _dot((x + pe_ref[half + l:half + l + 1, :]).astype(BF16),
                       w1_ref[(half + l) * LANE:(half + l + 1) * LANE, :])
    y = y1 + pltpu.roll(y2, nc - 1, 0)
    return _dot(_silu(y).astype(BF16), w2_ref[...])


def _compress_kernel(kc_ref, vc_ref, pek_ref, pev_ref, w1k_ref, w2k_ref, w1v_ref, w2v_ref,
                     cos_ref, sin_ref, kc_o, vc_o):
    nc = kc_o.shape[0]
    kc = _compress_one(kc_ref, pek_ref, w1k_ref, w2k_ref, nc)
    lo = lax.broadcasted_iota(jnp.int32, kc.shape, 1) < ROPE_HALF
    kc_o[...] = _rope(kc, cos_ref[...], sin_ref[...], lo).astype(BF16)
    vc_o[...] = _compress_one(vc_ref, pev_ref, w1v_ref, w2v_ref, nc).astype(BF16)


def _compress(h3, pek, pev, w1k, w2k, w1v, w2v, cos_c, sin_c):
    B, S, _ = h3.shape
    nc = S // CMP_STRIDE
    full = lambda shape: pl.BlockSpec(shape, lambda b, g: tuple(0 for _ in shape))
    out_spec = pl.BlockSpec((None, None, nc, LANE), lambda b, g: (b, g, 0, 0))
    out_sd = jax.ShapeDtypeStruct((B, KV_GROUPS, nc, LANE), BF16)
    cin = CMP_LEN * HEAD_DIM
    return pl.pallas_call(
        _compress_kernel,
        grid=(B, KV_GROUPS),
        in_specs=[pl.BlockSpec((None, S, LANE), lambda b, g: (b, 0, g)),
                  pl.BlockSpec((None, S, LANE), lambda b, g: (b, 0, KV_GROUPS + g)),
                  full((CMP_LEN, LANE)), full((CMP_LEN, LANE)),
                  full((cin, CMP_HIDDEN)), full((CMP_HIDDEN, LANE)),
                  full((cin, CMP_HIDDEN)), full((CMP_HIDDEN, LANE)),
                  full((nc, LANE)), full((nc, LANE))],
        out_specs=[out_spec, out_spec],
        out_shape=[out_sd, out_sd],
        compiler_params=_cparams(("parallel", "parallel")),
        name="compress",
    )(h3, h3, pek, pev, w1k, w2k, w1v, w2v, cos_c, sin_c)


def _cmp_attn_kernel(q_ref, kc_ref, vc_ref, ovt_ref, o_ref, bias_ref, st_ref, cnt_ref, *, n_sel):
    i = pl.program_id(2)
    tq = q_ref.shape[1]
    cols = HPG * tq
    nc = kc_ref.shape[0]
    t0 = i * tq
    q_g = jnp.concatenate([q_ref[hh * LANE:(hh + 1) * LANE, :] for hh in range(HPG)], axis=1)
    tpos = t0 + (lax.broadcasted_iota(jnp.int32, (1, cols), 1) & (tq - 1))
    cmp_end = lax.broadcasted_iota(jnp.int32, (nc, 1), 0) * CMP_STRIDE + (CMP_LEN - 1)
    s = jnp.where(cmp_end <= tpos, _dot(kc_ref[...], q_g), -jnp.inf)
    m = jnp.max(s, axis=0, keepdims=True)
    m = jnp.where(m > -jnp.inf, m, 0.0)
    p = jnp.exp2(s - m)
    p = p / jnp.maximum(jnp.sum(p, axis=0, keepdims=True), jnp.finfo(F32).tiny)
    vct = vc_ref[...].astype(F32).T.astype(BF16)
    _store_heads(o_ref, _dot(vct, p.astype(BF16)), 0, tq)
    psum = p[:, 0:tq]
    for hh in range(1, HPG):
        psum = psum + p[:, hh * tq:(hh + 1) * tq]
    ovt = ovt_ref[...]
    p_hi = psum.astype(BF16)
    r1 = psum - p_hi.astype(F32)
    p_mid = r1.astype(BF16)
    p_lo = (r1 - p_mid.astype(F32)).astype(BF16)
    p_sel = _dot(ovt, p_hi) + (_dot(ovt, p_mid) + _dot(ovt, p_lo))
    jb = lax.broadcasted_iota(jnp.int32, (LANE, tq), 0)
    cur = (t0 + lax.broadcasted_iota(jnp.int32, (1, tq), 1)) >> int(math.log2(SEL_LEN))
    valid = jb <= cur
    forced = valid & ((jb == 0) | (jb == cur) | (jb == cur - 1))
    st = jnp.where(forced, jnp.inf, jnp.where(valid, p_sel, -jnp.inf))
    n_grp = n_sel // SUBLANE
    st_ref[...] = st
    cnt_ref[...] = jnp.zeros(cnt_ref.shape, F32)
    jrow = lax.broadcasted_iota(jnp.int32, (SUBLANE, tq), 0)
    cur_max = (t0 + tq - 1) >> int(math.log2(SEL_LEN))
    for gb in range(n_grp):
        @pl.when(cur_max >= gb * SUBLANE)
        def _():
            grp = [st_ref[g * SUBLANE:(g + 1) * SUBLANE, :] for g in range(n_grp)]
            cnt = [cnt_ref[g * SUBLANE:(g + 1) * SUBLANE, :] for g in range(n_grp)]
            for rb_ in range(SUBLANE):
                rb = grp[gb][rb_:rb_ + 1, :]
                for g in range(n_grp):
                    if g < gb:
                        inc = jnp.where(rb > grp[g], 1.0, 0.0)
                    elif g > gb:
                        inc = jnp.where(rb >= grp[g], 1.0, 0.0)
                    else:
                        inc = jnp.where(jrow > rb_, jnp.where(rb >= grp[g], 1.0, 0.0), jnp.where(rb > grp[g], 1.0, 0.0))
                    cnt[g] = cnt[g] + inc
            for g in range(n_grp):
                cnt_ref[g * SUBLANE:(g + 1) * SUBLANE, :] = cnt[g]

    keep = (cnt_ref[...] < float(SEL_TOPK)) & (st_ref[0:n_sel, :] > -jnp.inf)
    bias = jnp.where(keep, 0.0, NEG_BIAS)
    if n_sel < LANE:
        bias = jnp.concatenate([bias, jnp.zeros((LANE - n_sel, tq), F32)], axis=0)
    bias_ref[...] = bias.astype(BF16)


def _cmp_attn(q_t, kc, vc, overlap_t):
    B, _, S = q_t.shape
    nc = kc.shape[2]
    n_sel = S // SEL_LEN
    tq = 256
    gw = HPG * LANE
    return pl.pallas_call(
        functools.partial(_cmp_attn_kernel, n_sel=n_sel),
        grid=(B, KV_GROUPS, S // tq),
        in_specs=[pl.BlockSpec((None, gw, tq), lambda b, g, i: (b, g, i)),
                  pl.BlockSpec((None, None, nc, LANE), lambda b, g, i: (b, g, 0, 0)),
                  pl.BlockSpec((None, None, nc, LANE), lambda b, g, i: (b, g, 0, 0)),
                  pl.BlockSpec((LANE, nc), lambda b, g, i: (0, 0))],
        out_specs=[pl.BlockSpec((None, tq, gw), lambda b, g, i: (b, i, g)),
                   pl.BlockSpec((None, None, LANE, tq), lambda b, g, i: (b, g, 0, i))],
        out_shape=[jax.ShapeDtypeStruct((B, S, WIDTH), BF16),
                   jax.ShapeDtypeStruct((B, KV_GROUPS, LANE, S), BF16)],
        scratch_shapes=[pltpu.VMEM((LANE, tq), F32), pltpu.VMEM((n_sel, tq), F32)],
        compiler_params=_cparams(("parallel", "parallel", "parallel")),
        name="cmp_attn",
    )(q_t, kc, vc, overlap_t)


FLASH_TQ = 256


def _store_heads(o_ref, out_t, g, tq):
    for hh in range(HPG):
        h = g * HPG + hh
        o_ref[:, h * LANE:(h + 1) * LANE] = out_t[:, hh * tq:(hh + 1) * tq].T.astype(BF16)


def _flash_sel_kernel(q_ref, b_ref, k_ref, vt_ref, o_ref, qs_ref, s_ref, m_ref, acc_ref, *, tk):
    i = pl.program_id(1)
    tq = q_ref.shape[1]
    cols = HPG * tq
    t0 = i * tq
    kw = qs_ref.shape[1]
    for g in range(KV_GROUPS):
        for hh in range(HPG):
            h = g * HPG + hh
            qs_ref[g, 0:LANE, hh * tq:(hh + 1) * tq] = q_ref[h * LANE:(h + 1) * LANE, :]
            qs_ref[g, LANE:2 * LANE, hh * tq:(hh + 1) * tq] = b_ref[g]
    m_ref[...] = jnp.full(m_ref.shape, -jnp.inf, F32)
    acc_ref[...] = jnp.zeros(acc_ref.shape, F32)
    qpos = t0 + (lax.broadcasted_iota(jnp.int32, (1, cols), 1) & (tq - 1))

    def scores(j, slot):
        k0 = pl.multiple_of(j * tk, tk)
        for g in range(KV_GROUPS):
            s_ref[slot, g] = _dot(k_ref[pl.ds(k0, tk), g * kw:(g + 1) * kw], qs_ref[g])

    def update(j, slot, masked):
        if masked:
            mask = j * tk + lax.broadcasted_iota(jnp.int32, (tk, 1), 0) <= qpos
        for g in range(KV_GROUPS):
            s = s_ref[slot, g]
            if masked:
                s = jnp.where(mask, s, -jnp.inf)
            m_old = m_ref[g]
            m_new = jnp.maximum(m_old, jnp.max(s, axis=0, keepdims=True))
            alpha = jnp.exp2(m_old - m_new)
            p = jnp.exp2(s - m_new)
            acc_ref[g] = alpha * acc_ref[g] + _dot(vt_ref[g, j], p.astype(BF16))
            m_ref[g] = m_new

    jd = t0 // tk
    last = jnp.maximum(jd - 1, 0)
    scores(jd, 0)
    scores(0, 1)
    update(jd, 0, True)

    def body(p, carry):
        scores(jnp.minimum(2 * p + 1, last), 0)
        update(2 * p, 1, False)
        scores(jnp.minimum(2 * p + 2, last), 1)
        update(2 * p + 1, 0, False)
        return carry

    lax.fori_loop(0, jd // 2, body, 0)

    @pl.when(jd % 2 == 1)
    def _():
        update(jd - 1, 1, False)

    for g in range(KV_GROUPS):
        _store_heads(o_ref, acc_ref[g, 0:LANE] / jnp.maximum(acc_ref[g, LANE:LANE + 1], jnp.finfo(F32).tiny), g, tq)


def _flash_sel(q_t, kaug, v_t, bias_t):
    B, _, S = q_t.shape
    tq = tk = FLASH_TQ
    assert tk == SEL_TK
    kw = kaug.shape[2] // KV_GROUPS
    cols = HPG * tq
    return pl.pallas_call(
        functools.partial(_flash_sel_kernel, tk=tk),
        grid=(B, S // tq),
        in_specs=[pl.BlockSpec((None, WIDTH, tq), lambda b, i: (b, 0, i)),
                  pl.BlockSpec((None, KV_GROUPS, LANE, tq), lambda b, i: (b, 0, 0, i)),
                  pl.BlockSpec((None, S, KV_GROUPS * kw), lambda b, i: (b, 0, 0)),
                  pl.BlockSpec((None, KV_GROUPS, S // tk, SEL_VROWS, tk), lambda b, i: (b, 0, 0, 0, 0))],
        out_specs=pl.BlockSpec((None, tq, WIDTH), lambda b, i: (b, i, 0)),
        out_shape=jax.ShapeDtypeStruct((B, S, WIDTH), BF16),
        scratch_shapes=[pltpu.VMEM((KV_GROUPS, kw, cols), BF16),
                        pltpu.VMEM((2, KV_GROUPS, tk, cols), F32),
                        pltpu.VMEM((KV_GROUPS, 1, cols), F32),
                        pltpu.VMEM((KV_GROUPS, SEL_VROWS, cols), F32)],
        compiler_params=_cparams(("parallel", "parallel")),
        name="flash_sel",
    )(q_t, bias_t, kaug, v_t)


def _flash_win_kernel(q_ref, k_ref, vt_ref, wb_ref, o_ref, s_ref, *, tk):
    i = pl.program_id(1)
    tq = q_ref.shape[1]
    cols = HPG * tq
    n_t = WINDOW // tk + 1
    jt = [jnp.maximum(i - d, 0) for d in range(n_t)]
    for g in range(KV_GROUPS):
        q_g = jnp.concatenate([q_ref[(g * HPG + hh) * LANE:(g * HPG + hh + 1) * LANE, :] for hh in range(HPG)], axis=1)
        for d in range(n_t):
            k0 = pl.multiple_of(jt[d] * tk, tk)
            s_ref[g, d] = _dot(k_ref[pl.ds(k0, tk), g * LANE:(g + 1) * LANE], q_g)
    static_bias = {0: 0, n_t - 1: 1}

    def tile(g, d):
        s = s_ref[g, d]
        return s + wb_ref[static_bias[d]] if d in static_bias else s

    for g in range(KV_GROUPS):
        m = jnp.max(tile(g, 0), axis=0, keepdims=True)
        for d in range(1, n_t):
            m = jnp.maximum(m, jnp.where(i >= d, jnp.max(tile(g, d), axis=0, keepdims=True), -jnp.inf))
        l = jnp.zeros((1, cols), F32)
        acc = jnp.zeros((LANE, cols), F32)
        for d in range(n_t):
            p = jnp.exp2(tile(g, d) - (m if d == 0 else jnp.where(i >= d, m, jnp.inf)))
            l = l + jnp.sum(p, axis=0, keepdims=True)
            acc = acc + _dot(vt_ref[g, jt[d]], p.astype(BF16))
        _store_heads(o_ref, acc / jnp.maximum(l, jnp.finfo(F32).tiny), g, tq)


def _window_bias(tq):
    r = np.arange(tq)[:, None]
    c = (np.arange(HPG * tq) % tq)[None, :]
    neg = np.float32(-np.inf)
    return jnp.asarray(np.stack([np.where(r <= c, np.float32(0), neg), np.where(c < r, np.float32(0), neg)]))


def _flash_win(q_t, k_r, v_t):
    B, _, S = q_t.shape
    tq = tk = FLASH_TQ
    assert tk == WIN_TK and WINDOW == 2 * tk
    cols = HPG * tq
    return pl.pallas_call(
        functools.partial(_flash_win_kernel, tk=tk),
        grid=(B, S // tq),
        in_specs=[pl.BlockSpec((None, WIDTH, tq), lambda b, i: (b, 0, i)),
                  pl.BlockSpec((None, S, KVW), lambda b, i: (b, 0, 0)),
                  pl.BlockSpec((None, KV_GROUPS, S // tk, LANE, tk), lambda b, i: (b, 0, 0, 0, 0)),
                  pl.BlockSpec((2, tk, cols), lambda b, i: (0, 0, 0))],
        out_specs=pl.BlockSpec((None, tq, WIDTH), lambda b, i: (b, i, 0)),
        out_shape=jax.ShapeDtypeStruct((B, S, WIDTH), BF16),
        scratch_shapes=[pltpu.VMEM((KV_GROUPS, WINDOW // tk + 1, tk, cols), F32)],
        compiler_params=_cparams(("parallel", "parallel")),
        name="flash_win",
    )(q_t, k_r, v_t, _window_bias(tq))


def _merge_kernel(ydn_ref, oc_ref, os_ref, ow_ref, small_ref, mgd_ref, mgn_ref, x_ref,
                  wdn_ref, wnsa_ref, wout_ref, g_ref, b_ref, o_ref):
    gates = _sigmoid(small_ref[...])
    cols = []
    for h in range(HEADS):
        sl = slice(h * LANE, (h + 1) * LANE)
        gc = gates[:, SMALL_GATE + h:SMALL_GATE + h + 1]
        gs = gates[:, SMALL_GATE + HEADS + h:SMALL_GATE + HEADS + h + 1]
        gw = gates[:, SMALL_GATE + 2 * HEADS + h:SMALL_GATE + 2 * HEADS + h + 1]
        cols.append((gc * oc_ref[:, sl] + gs * os_ref[:, sl] + gw * ow_ref[:, sl]).astype(BF16))
    y_nsa = jnp.concatenate(cols, axis=1)
    y = (_sigmoid(mgd_ref[...].astype(F32)) * _dot(ydn_ref[...], wdn_ref[...])
         + _sigmoid(mgn_ref[...].astype(F32)) * _dot(y_nsa, wnsa_ref[...]))
    z = DEEPNORM_ALPHA * x_ref[...] + _dot(y.astype(BF16), wout_ref[...])
    o_ref[...] = _layer_norm(z, g_ref[...], b_ref[...])


def _merge(y_dn2, oc2, os2, ow2, small2, h, x2, wdn, wnsa, wout, g, b):
    T, D = x2.shape
    tm = 512
    row = pl.BlockSpec((tm, WIDTH), lambda i: (i, 0))
    full = lambda shape: pl.BlockSpec(shape, lambda i: (0, 0))
    return pl.pallas_call(
        _merge_kernel,
        grid=(T // tm,),
        in_specs=[row, row, row, row,
                  pl.BlockSpec((tm, LANE), lambda i: (i, 0)),
                  pl.BlockSpec((tm, WIDTH), lambda i: (i, COL_MG_DN // WIDTH)),
                  pl.BlockSpec((tm, WIDTH), lambda i: (i, COL_MG_NSA // WIDTH)),
                  pl.BlockSpec((tm, D), lambda i: (i, 0)),
                  full((WIDTH, D)), full((WIDTH, D)), full((D, D)), full((1, D)), full((1, D))],
        out_specs=pl.BlockSpec((tm, D), lambda i: (i, 0)),
        out_shape=jax.ShapeDtypeStruct((T, D), F32),
        compiler_params=_cparams(("parallel",)),
        name="merge",
    )(y_dn2, oc2, os2, ow2, small2, h, h, x2, wdn, wnsa, wout, g, b)


FFN_TM = 512


def _ffn_up_kernel(x_ref, xp_ref, wg_ref, wv_ref, cwg_ref, cwv_ref, bg_ref, bv_ref, o_ref, xs_ref, *, tiles_per_seq):
    i = pl.program_id(1)
    tm = x_ref.shape[0]
    nv = tm // SUBLANE
    n_slab = x_ref.shape[1] // LANE
    for sp in range(SUBLANE):
        for c in range(n_slab):
            xs_ref[c, pl.ds(sp, nv, stride=SUBLANE), :] = x_ref[sp * nv:(sp + 1) * nv, c * LANE:(c + 1) * LANE]
    xb = jnp.concatenate([xs_ref[c] for c in range(n_slab)], axis=1).astype(BF16)
    notfirst = (i % tiles_per_seq != 0).astype(F32)
    xpb = (xp_ref[...] * notfirst).astype(BF16)
    sub = lax.broadcasted_iota(jnp.int32, (SUBLANE, 1), 0)

    def conv(w_ref, cw_ref, b_ref):
        u = _dot(xb, w_ref[...])
        p = _dot(xpb, w_ref[...])

        def wrap(k):
            blk = u[(nv - k) * SUBLANE:(nv - k + 1) * SUBLANE, :]
            return jnp.where(sub == 0, p[SUBLANE - k:SUBLANE - k + 1, :], pltpu.roll(blk, 1, 0))

        w1, w2 = wrap(1), wrap(2)
        u1 = jnp.concatenate([w1, u[0:(nv - 1) * SUBLANE, :]], axis=0)
        u2 = jnp.concatenate([w2, w1, u[0:(nv - 2) * SUBLANE, :]], axis=0)
        return cw_ref[0:1, :] * u2 + cw_ref[1:2, :] * u1 + cw_ref[2:3, :] * u + b_ref[...]

    o_ref[...] = (_silu(conv(wg_ref, cwg_ref, bg_ref)) * conv(wv_ref, cwv_ref, bv_ref)).astype(BF16)


def _ffn_up(x1, w_up, conv_w, conv_b, seq):
    T, D = x1.shape
    tm = FFN_TM
    nj = 2
    tn = D_FF // nj
    return pl.pallas_call(
        functools.partial(_ffn_up_kernel, tiles_per_seq=seq // tm),
        grid=(nj, T // tm),
        in_specs=[pl.BlockSpec((tm, D), lambda j, i: (i, 0)),
                  pl.BlockSpec((SUBLANE, D), lambda j, i: (jnp.maximum(i * (tm // SUBLANE) - 1, 0), 0)),
                  pl.BlockSpec((D, tn), lambda j, i: (0, j)),
                  pl.BlockSpec((D, tn), lambda j, i: (0, j + nj)),
                  pl.BlockSpec((FFN_CONV, tn), lambda j, i: (0, j)),
                  pl.BlockSpec((FFN_CONV, tn), lambda j, i: (0, j + nj)),
                  pl.BlockSpec((1, tn), lambda j, i: (0, j)),
                  pl.BlockSpec((1, tn), lambda j, i: (0, j + nj))],
        out_specs=pl.BlockSpec((tm, tn), lambda j, i: (i, j)),
        out_shape=jax.ShapeDtypeStruct((T, D_FF), BF16),
        scratch_shapes=[pltpu.VMEM((D // LANE, tm, LANE), F32)],
        compiler_params=_cparams(("parallel", "parallel")),
        name="ffn_up",
    )(x1, x1, w_up, w_up, conv_w, conv_w, conv_b, conv_b)


def _ffn_down_kernel(a_ref, w_ref, x_ref, g_ref, b_ref, o_ref, fs_ref):
    tm, d = x_ref.shape
    nv = tm // SUBLANE
    n_slab = d // LANE
    f = _dot(a_ref[...], w_ref[...])
    for c in range(n_slab):
        fs_ref[c] = f[:, c * LANE:(c + 1) * LANE]
    for sp in range(SUBLANE):
        rows = slice(sp * nv, (sp + 1) * nv)
        f_nat = jnp.concatenate([fs_ref[c, pl.ds(sp, nv, stride=SUBLANE), :] for c in range(n_slab)], axis=1)
        z = DEEPNORM_ALPHA * x_ref[rows, :] + f_nat
        o_ref[rows, :] = _layer_norm(z, g_ref[...], b_ref[...])


def _ffn_down(a, w_down, x1, g, b):
    T, D = x1.shape
    tm = FFN_TM
    full = lambda shape: pl.BlockSpec(shape, lambda i: (0, 0))
    return pl.pallas_call(
        _ffn_down_kernel,
        grid=(T // tm,),
        in_specs=[pl.BlockSpec((tm, D_FF), lambda i: (i, 0)), full((D_FF, D)),
                  pl.BlockSpec((tm, D), lambda i: (i, 0)), full((1, D)), full((1, D))],
        out_specs=pl.BlockSpec((tm, D), lambda i: (i, 0)),
        out_shape=jax.ShapeDtypeStruct((T, D), F32),
        scratch_shapes=[pltpu.VMEM((D // LANE, tm, LANE), F32)],
        compiler_params=_cparams(("parallel",)),
        name="ffn_down",
    )(a, w_down, x1, g, b)


def _split_w_in(w):
    d = w.shape[0]
    o_dz = 3 * WIDTH
    o_db = 4 * WIDTH
    o_nq = o_db + 2 * HEADS
    o_kc = o_nq + WIDTH
    o_ksl = o_kc + 2 * KVW
    o_gate = o_ksl + 4 * KVW
    o_mg = o_gate + 3 * HEADS
    pad = jnp.zeros((d, LANE - 5 * HEADS), w.dtype)
    w_a = jnp.concatenate([w[:, 0:o_dz], w[:, o_db:o_nq], w[:, o_gate:o_mg], pad], axis=1)
    w_b = jnp.concatenate([w[:, o_nq:o_kc], w[:, o_ksl:o_gate]], axis=1)
    w_c = jnp.concatenate([w[:, o_dz:o_db], w[:, o_mg:o_mg + 2 * WIDTH], w[:, o_kc:o_ksl]], axis=1)
    return w_a.astype(BF16), w_b.astype(BF16), w_c.astype(BF16)


def _rope_tables(pos):
    inv_freq = ROPE_THETA ** (-jnp.arange(ROPE_HALF, dtype=F32) / ROPE_HALF)
    ang = pos.astype(F32)[:, None] * inv_freq
    cos = jnp.cos(ang)
    sin = jnp.sin(ang)
    p = pos.shape[0]
    cosf = jnp.concatenate([cos, cos, jnp.ones((p, LANE - ROPE_DIM), F32)], axis=1)
    sinf = jnp.concatenate([-sin, sin, jnp.zeros((p, LANE - ROPE_DIM), F32)], axis=1)
    return cosf, sinf


def _overlap_table(nc, n_sel):
    n = np.arange(nc)[None, :]
    j = np.arange(LANE)[:, None]
    ov = (n * CMP_STRIDE <= j * SEL_LEN + SEL_LEN - 1) & (n * CMP_STRIDE + CMP_LEN - 1 >= j * SEL_LEN)
    ov = ov & (j < n_sel) & (n < nc - 1)
    return jnp.asarray(ov.astype(np.float32)).astype(BF16)


def _lane_row(vals, offset):
    return jnp.zeros((1, LANE), F32).at[0, offset:offset + vals.shape[0]].set(vals.astype(F32))


def kernel(x, w_in, dn_conv_w, dn_a_log, dn_dt_bias, dn_norm_w, cmp_pos_k, cmp_pos_v, cmp_k_w1, cmp_k_w2,
           cmp_v_w1, cmp_v_w2, w_branch_dn, w_branch_nsa, w_out, ln1_g, ln1_b, ffn_w_up, ffn_conv_w,
           ffn_conv_b, ffn_w_down, ln2_g, ln2_b):
    B, S, D = x.shape
    T = B * S
    nc = S // CMP_STRIDE
    n_sel = S // SEL_LEN
    assert D == WIDTH and S % 512 == 0 and n_sel <= LANE and n_sel % SUBLANE == 0
    cos_t, sin_t = _rope_tables(jnp.arange(S))
    cos_c, sin_c = _rope_tables(jnp.arange(nc) * CMP_STRIDE + (CMP_LEN - 1))
    overlap = _overlap_table(nc, n_sel)
    x2 = x.reshape(T, D)
    for layer in range(w_in.shape[0]):
        w_a, w_b, w_c = _split_w_in(w_in[layer])
        x3 = x2.reshape(B, S, D)
        h3, kv_cmp = _proj_rest(x3, w_c)

        q, k, v, bg, small = _proj_gdn(x3, w_a, dn_conv_w[layer], _lane_row(dn_a_log[layer], SMALL_DECAY),
                                       _lane_row(dn_dt_bias[layer], SMALL_DECAY))
        y_dn = _gdn(q, k, v, bg, h3, dn_norm_w[layer].reshape(1, LANE))

        q_t, kaug, vsl_t, kwn_r, vwn_t = _proj_nsa(x3, w_b, cos_t, sin_t)
        kc, vc = _compress(kv_cmp, cmp_pos_k[layer], cmp_pos_v[layer],
                           cmp_k_w1[layer].astype(BF16), cmp_k_w2[layer].astype(BF16),
                           cmp_v_w1[layer].astype(BF16), cmp_v_w2[layer].astype(BF16), cos_c, sin_c)
        o_cmp, bias = _cmp_attn(q_t, kc, vc, overlap)
        o_sel = _flash_sel(q_t, kaug, vsl_t, bias)
        o_win = _flash_win(q_t, kwn_r, vwn_t)

        x1 = _merge(y_dn.reshape(T, WIDTH), o_cmp.reshape(T, WIDTH), o_sel.reshape(T, WIDTH),
                    o_win.reshape(T, WIDTH), small.reshape(T, LANE), h3.reshape(T, COL_KC), x2,
                    w_branch_dn[layer].astype(BF16), w_branch_nsa[layer].astype(BF16),
                    w_out[layer].astype(BF16), ln1_g[layer].reshape(1, D), ln1_b[layer].reshape(1, D))
        a = _ffn_up(x1, ffn_w_up[layer].astype(BF16), ffn_conv_w[layer],
                    ffn_conv_b[layer].reshape(1, 2 * D_FF), S)
        x2 = _ffn_down(a, ffn_w_down[layer].astype(BF16), x1, ln2_g[layer].reshape(1, D), ln2_b[layer].reshape(1, D))
    return x2.reshape(B, S, D)
```

```python
import functools
import math

import numpy as np
import jax
import jax.numpy as jnp
from jax import lax
from jax.experimental import pallas as pl
from jax.experimental.pallas import tpu as pltpu

F32 = jnp.float32
BF16 = jnp.bfloat16
HI = lax.Precision.HIGHEST

HEADS = 8
HEAD_DIM = 128
KV_GROUPS = 2
HPG = HEADS // KV_GROUPS
DN_CONV = 4
DN_CHUNK = 64
CMP_LEN = 32
CMP_STRIDE = 16
CMP_HIDDEN = 256
SEL_LEN = 64
SEL_TOPK = 16
WINDOW = 512
ROPE_THETA = 500000.0
ROPE_DIM = HEAD_DIM // 4
ROPE_HALF = ROPE_DIM // 2
D_FF = 2816
FFN_CONV = 3
LN_EPS = 1e-5
RMS_EPS = 1e-6
DEEPNORM_ALPHA = 2.0 ** 0.25
ATT_SCALE = HEAD_DIM ** -0.5

LANE = 128
SUBLANE = 8
VMEM_LIMIT = 56 * 1024 * 1024

WIDTH = HEADS * HEAD_DIM
KVW = KV_GROUPS * HEAD_DIM
NA = 3 * WIDTH + LANE
SMALL_BETA = 0
SMALL_DECAY = HEADS
SMALL_GATE = 2 * HEADS
NB = WIDTH + 4 * KVW
NC_COLS = 3 * WIDTH + 2 * KVW
COL_DZ = 0
COL_MG_DN = WIDTH
COL_MG_NSA = 2 * WIDTH
COL_KC = 3 * WIDTH
COL_VC = COL_KC + KVW

NEG_BIAS = -(2.0 ** 100)


def _cparams(sem):
    return pltpu.CompilerParams(dimension_semantics=sem, vmem_limit_bytes=VMEM_LIMIT)


def _sigmoid(x):
    return 1.0 / (1.0 + jnp.exp(-x))


def _silu(x):
    return x * _sigmoid(x)


def _dot(a, b, precision=None):
    return jnp.dot(a, b, preferred_element_type=F32, precision=precision)


def _dot_nt(a, b):
    return lax.dot_general(a, b, (((1,), (1,)), ((), ())), preferred_element_type=F32)


def _dot_tn(a, b):
    return lax.dot_general(a, b, (((0,), (0,)), ((), ())), preferred_element_type=F32)


def _layer_norm(z, g, b):
    mu = jnp.mean(z, -1, keepdims=True)
    zc = z - mu
    var = jnp.mean(zc * zc, -1, keepdims=True)
    return zc * lax.rsqrt(var + LN_EPS) * g + b


PROJ_TM = 512


def _proj_rest_kernel(x_ref, w_ref, g_ref, c_ref):
    h = _dot(x_ref[...].astype(BF16), w_ref[...])
    g_ref[...] = h[:, 0:COL_KC].astype(BF16)
    c_ref[...] = h[:, COL_KC:NC_COLS]


def _proj_rest(x3, w):
    B, S, D = x3.shape
    tm = PROJ_TM
    return pl.pallas_call(
        _proj_rest_kernel,
        grid=(B, S // tm),
        in_specs=[pl.BlockSpec((None, tm, D), lambda b, i: (b, i, 0)),
                  pl.BlockSpec((D, NC_COLS), lambda b, i: (0, 0))],
        out_specs=[pl.BlockSpec((None, tm, COL_KC), lambda b, i: (b, i, 0)),
                   pl.BlockSpec((None, tm, NC_COLS - COL_KC), lambda b, i: (b, i, 0))],
        out_shape=[jax.ShapeDtypeStruct((B, S, COL_KC), BF16),
                   jax.ShapeDtypeStruct((B, S, NC_COLS - COL_KC), F32)],
        compiler_params=_cparams(("parallel", "parallel")),
        name="proj_rest",
    )(x3, w)


def _proj_gdn_kernel(x_ref, w_ref, cw_ref, alog_ref, dtb_ref, q_ref, k_ref, v_ref, bg_ref, small_ref,
                     xs_ref, ys_ref, carry_ref):
    i = pl.program_id(1)
    ts = x_ref.shape[0]
    nv = ts // SUBLANE
    wq = 3 * WIDTH
    n_slab = x_ref.shape[1] // LANE

    @pl.when(i == 0)
    def _():
        carry_ref[...] = jnp.zeros(carry_ref.shape, F32)

    for sp in range(SUBLANE):
        for c in range(n_slab):
            xs_ref[c, pl.ds(sp, nv, stride=SUBLANE), :] = x_ref[sp * nv:(sp + 1) * nv, c * LANE:(c + 1) * LANE]
    xb = jnp.concatenate([xs_ref[c] for c in range(n_slab)], axis=1).astype(BF16)
    h = _dot(xb, w_ref[...])
    sub = lax.broadcasted_iota(jnp.int32, (SUBLANE, 1), 0)

    def token_order(c, dst_ref, dst_sl, dtype):
        for sp in range(SUBLANE):
            dst_ref[sp * nv:(sp + 1) * nv, dst_sl] = ys_ref[c, pl.ds(sp, nv, stride=SUBLANE), :].astype(dtype)

    for c in range(3 * HEADS):
        sl = slice(c * LANE, (c + 1) * LANE)
        col = h[:, sl]
        wrap = [jnp.where(sub == 0, carry_ref[SUBLANE - k:SUBLANE - k + 1, sl],
                          pltpu.roll(col[(nv - k) * SUBLANE:(nv - k + 1) * SUBLANE, :], 1, 0))
                for k in range(1, DN_CONV)]
        acc = cw_ref[DN_CONV - 1:DN_CONV, sl] * col
        for s in range(1, DN_CONV):
            shifted = jnp.concatenate(wrap[0:s][::-1] + [col[0:(nv - s) * SUBLANE, :]], axis=0)
            acc = acc + cw_ref[DN_CONV - 1 - s:DN_CONV - s, sl] * shifted
        y = _silu(acc)
        if c < 2 * HEADS:
            y = y * lax.rsqrt(jnp.sum(y * y, -1, keepdims=True) + RMS_EPS)
        if c < HEADS:
            y = y * ATT_SCALE
        ys_ref[c] = y
        dst = (q_ref, k_ref, v_ref)[c // HEADS]
        token_order(c, dst, slice((c % HEADS) * LANE, (c % HEADS + 1) * LANE), BF16)
    for k in range(1, DN_CONV):
        carry_ref[SUBLANE - k:SUBLANE - k + 1, :] = h[(nv - k + 1) * SUBLANE - 1:(nv - k + 1) * SUBLANE, 0:wq]
    small = h[:, wq:wq + LANE]
    beta = _sigmoid(small)
    xs = small + dtb_ref[...]
    softplus = jnp.maximum(xs, 0.0) + jnp.log1p(jnp.exp(-jnp.abs(xs)))
    g = -jnp.exp(alog_ref[...]) * softplus
    lane = lax.broadcasted_iota(jnp.int32, small.shape, 1)
    ys_ref[3 * HEADS] = small
    token_order(3 * HEADS, small_ref, slice(0, LANE), F32)
    ys_ref[3 * HEADS + 1] = jnp.where(lane < SMALL_DECAY, beta, g)
    token_order(3 * HEADS + 1, bg_ref, slice(0, LANE), F32)


def _proj_gdn(x3, w_a, conv_w, alog_row, dtb_row):
    B, S, D = x3.shape
    ts = PROJ_TM
    wq = 3 * WIDTH
    out_sd = jax.ShapeDtypeStruct((B, S, WIDTH), BF16)
    small_sd = jax.ShapeDtypeStruct((B, S, LANE), F32)
    wide = pl.BlockSpec((None, ts, WIDTH), lambda b, i: (b, i, 0))
    narrow = pl.BlockSpec((None, ts, LANE), lambda b, i: (b, i, 0))
    return pl.pallas_call(
        _proj_gdn_kernel,
        grid=(B, S // ts),
        in_specs=[
            pl.BlockSpec((None, ts, D), lambda b, i: (b, i, 0)),
            pl.BlockSpec((D, NA), lambda b, i: (0, 0)),
            pl.BlockSpec((DN_CONV, wq), lambda b, i: (0, 0)),
            pl.BlockSpec((1, LANE), lambda b, i: (0, 0)),
            pl.BlockSpec((1, LANE), lambda b, i: (0, 0)),
        ],
        out_specs=[wide, wide, wide, narrow, narrow],
        out_shape=[out_sd, out_sd, out_sd, small_sd, small_sd],
        scratch_shapes=[pltpu.VMEM((D // LANE, ts, LANE), F32),
                        pltpu.VMEM((3 * HEADS + 2, ts, LANE), F32),
                        pltpu.VMEM((SUBLANE, wq), F32)],
        compiler_params=_cparams(("parallel", "arbitrary")),
        name="proj_gdn",
    )(x3, w_a, conv_w, alog_row, dtb_row)


GDN_CHUNKS_PER_STEP = 4
GDN_BATCH_PER_STEP = 2


GDN_QUAD = 4


def _gdn_kernel(q_ref, k_ref, v_ref, bg_ref, z_ref, nw_ref, y_ref, state_ref):
    n = pl.program_id(1)

    @pl.when(n == 0)
    def _():
        state_ref[...] = jnp.zeros(state_ref.shape, F32)

    C = DN_CHUNK
    QW = GDN_QUAD * C
    QL = GDN_QUAD * LANE
    shift_c = int(math.log2(C))
    r64 = lax.broadcasted_iota(jnp.int32, (C, C), 0)
    c64 = lax.broadcasted_iota(jnp.int32, (C, C), 1)
    tril = jnp.where(r64 >= c64, 1.0, 0.0).astype(F32)
    triu = jnp.where(r64 <= c64, 1.0, 0.0).astype(F32)
    rr = lax.broadcasted_iota(jnp.int32, (C, QW), 0)
    ll = lax.broadcasted_iota(jnp.int32, (C, QW), 1)
    jj = ll & (C - 1)
    hb = ll >> shift_c
    incl_t = rr >= jj
    strict_t = rr > jj
    eye_t = jnp.where(rr == jj, 1.0, 0.0).astype(F32)
    bd_c = jnp.where((lax.broadcasted_iota(jnp.int32, (QW, QW), 0) >> shift_c)
                     == (lax.broadcasted_iota(jnp.int32, (QW, QW), 1) >> shift_c), 1.0, 0.0).astype(BF16)
    bd_l = jnp.where((lax.broadcasted_iota(jnp.int32, (QW, QL), 0) >> shift_c)
                     == (lax.broadcasted_iota(jnp.int32, (QW, QL), 1) >> int(math.log2(LANE))), 1.0, 0.0).astype(BF16)
    zeros_rhs = jnp.zeros((C, 2 * LANE), BF16)
    nw = nw_ref[...]
    nb = q_ref.shape[0]
    quads = [(bb, c, a) for bb in range(nb) for c in range(GDN_CHUNKS_PER_STEP) for a in range(HEADS // GDN_QUAD)]
    gc_cols, gc_rows, bgs = {}, {}, {}
    for bb in range(nb):
        bg = bg_ref[bb]
        bg_t = bg.T
        bgs[bb] = bg
        for c in range(GDN_CHUNKS_PER_STEP):
            rs = slice(c * C, (c + 1) * C)
            gc_cols[bb, c] = _dot(tril, bg[rs, :], HI)
            gc_rows[bb, c] = _dot(bg_t[SMALL_DECAY:SMALL_DECAY + HEADS, rs], triu, HI)
    decay, lower, qk, rhs_bd, qd, kd, eglast = {}, {}, {}, {}, {}, {}, {}
    for u_ in quads:
        bb, c, a = u_
        rs = slice(c * C, (c + 1) * C)
        h0 = a * GDN_QUAD
        gcols = [gc_cols[bb, c][:, SMALL_DECAY + h0 + hh:SMALL_DECAY + h0 + hh + 1] for hh in range(GDN_QUAD)]
        gcol_cat = jnp.broadcast_to(gcols[0], (C, QW))
        for hh in range(1, GDN_QUAD):
            gcol_cat = jnp.where(hb == hh, gcols[hh], gcol_cat)
        grow_cat = jnp.concatenate([gc_rows[bb, c][h0 + hh:h0 + hh + 1, :] for hh in range(GDN_QUAD)], axis=1)
        decay[u_] = jnp.exp(jnp.where(incl_t, gcol_cat - grow_cat, -jnp.inf))
        kq = k_ref[bb, rs, a * QL:(a + 1) * QL].astype(F32)
        qq = q_ref[bb, rs, a * QL:(a + 1) * QL].astype(F32)
        kbs, rows_bd = [], []
        for hh in range(GDN_QUAD):
            h = h0 + hh
            ls = slice(hh * LANE, (hh + 1) * LANE)
            beta = bgs[bb][rs, SMALL_BETA + h:SMALL_BETA + h + 1]
            eg = jnp.exp(gcols[hh])
            glast = gcols[hh][C - 1:C, :]
            kb = kq[:, ls] * beta
            vb = v_ref[bb, rs, h * LANE:(h + 1) * LANE].astype(F32) * beta
            kbs.append(kb)
            rhs_h = jnp.concatenate([vb, kb * eg], axis=1).astype(BF16)
            rows_bd.append(jnp.concatenate([zeros_rhs] * hh + [rhs_h] + [zeros_rhs] * (GDN_QUAD - 1 - hh), axis=1))
            qd[bb, c, h] = (qq[:, ls] * eg).astype(BF16)
            kd[bb, c, h] = (kq[:, ls] * jnp.exp(glast - gcols[hh])).astype(BF16)
            eglast[bb, c, h] = jnp.exp(glast)
        rhs_bd[u_] = jnp.concatenate(rows_bd, axis=0)
        k_bd = jnp.concatenate([kq.astype(BF16)] * GDN_QUAD, axis=0) * bd_l
        lhs = jnp.concatenate([jnp.concatenate(kbs, axis=1), qq], axis=0).astype(BF16)
        aq = _dot_nt(lhs, k_bd)
        lower[u_] = jnp.where(strict_t, aq[0:C] * decay[u_], 0.0)
        qk[u_] = (aq[C:2 * C] * decay[u_]).astype(BF16)

    def block_diag(y_b):
        return jnp.concatenate([y_b] * GDN_QUAD, axis=0) * bd_c

    tinv = {u_: eye_t - lower[u_] for u_ in quads}
    y_b = {u_: lower[u_].astype(BF16) for u_ in quads}
    ypow = {u_: _dot(y_b[u_], block_diag(y_b[u_])) for u_ in quads}
    for it in range(5):
        for u_ in quads:
            yb = ypow[u_].astype(BF16)
            if it < 4:
                r = _dot(jnp.concatenate([tinv[u_].astype(BF16), yb], axis=0), block_diag(yb))
                tinv[u_] = tinv[u_] + r[0:C]
                ypow[u_] = r[C:2 * C]
            else:
                tinv[u_] = tinv[u_] + _dot(tinv[u_].astype(BF16), block_diag(yb))
    uw = {u_: _dot(tinv[u_].astype(BF16), rhs_bd[u_]) for u_ in quads}
    zeros_v = jnp.zeros((C, LANE), BF16)
    chains = [(bb, h) for bb in range(nb) for h in range(HEADS)]
    for c in range(GDN_CHUNKS_PER_STEP):
        rs = slice(c * C, (c + 1) * C)
        states = {s_: state_ref[s_[0] * HEADS + s_[1]] for s_ in chains}
        ws_qs = {}
        for s_ in chains:
            bb, h = s_
            a, hh = divmod(h, GDN_QUAD)
            w_h = uw[bb, c, a][:, hh * 2 * LANE + LANE:(hh + 1) * 2 * LANE]
            ws_qs[s_] = _dot(jnp.concatenate([w_h.astype(BF16), qd[bb, c, h]], axis=0), states[s_].astype(BF16))
        v_new = {}
        for s_ in chains:
            bb, h = s_
            a, hh = divmod(h, GDN_QUAD)
            v_new[s_] = (uw[bb, c, a][:, hh * 2 * LANE:hh * 2 * LANE + LANE] - ws_qs[s_][0:C]).astype(BF16)
        outs = {}
        for s_ in chains:
            bb, h = s_
            a, hh = divmod(h, GDN_QUAD)
            v_pad = jnp.concatenate([zeros_v] * hh + [v_new[s_]] + [zeros_v] * (GDN_QUAD - 1 - hh), axis=0)
            outs[s_] = ws_qs[s_][C:2 * C] + _dot(qk[bb, c, a], v_pad)
            state_ref[bb * HEADS + h] = states[s_] * eglast[bb, c, h] + _dot_tn(kd[bb, c, h], v_new[s_])
        for s_ in chains:
            bb, h = s_
            ls = slice(h * LANE, (h + 1) * LANE)
            o = outs[s_]
            o = o * lax.rsqrt(jnp.mean(o * o, -1, keepdims=True) + RMS_EPS)
            y_ref[bb, rs, ls] = (o * nw * _silu(z_ref[bb, rs, ls].astype(F32))).astype(BF16)


def _gdn(q, k, v, bg, h3, norm_w):
    B, S, _ = q.shape
    rows = GDN_CHUNKS_PER_STEP * DN_CHUNK
    nb = GDN_BATCH_PER_STEP if B % GDN_BATCH_PER_STEP == 0 else 1
    tile = pl.BlockSpec((nb, rows, WIDTH), lambda b, n: (b, n, 0))
    return pl.pallas_call(
        _gdn_kernel,
        grid=(B // nb, S // rows),
        in_specs=[tile, tile, tile,
                  pl.BlockSpec((nb, rows, LANE), lambda b, n: (b, n, 0)),
                  pl.BlockSpec((nb, rows, WIDTH), lambda b, n: (b, n, COL_DZ // WIDTH)),
                  pl.BlockSpec((1, LANE), lambda b, n: (0, 0))],
        out_specs=tile,
        out_shape=jax.ShapeDtypeStruct((B, S, WIDTH), BF16),
        scratch_shapes=[pltpu.VMEM((nb * HEADS, HEAD_DIM, HEAD_DIM), F32)],
        compiler_params=_cparams(("parallel", "arbitrary")),
        name="gdn",
    )(q, k, v, bg, h3, norm_w)


def _rope(t, cosf, sinf, lo):
    partner = jnp.where(lo, pltpu.roll(t, LANE - ROPE_HALF, 1), pltpu.roll(t, ROPE_HALF, 1))
    return t * cosf + partner * sinf


SEL_TK = 256
WIN_TK = 256
SEL_VROWS = LANE + 16
LOG2E = math.log2(math.e)


def _proj_nsa_kernel(x_ref, w_ref, cos_ref, sin_ref, qt_o, kaug_o, vslt_o, kwn_o, vwnt_o):
    i = pl.program_id(1)
    ts = x_ref.shape[0]
    h = _dot(x_ref[...].astype(BF16), w_ref[...])
    c_ksl, c_vsl, c_kwn, c_vwn = WIDTH, WIDTH + KVW, WIDTH + 2 * KVW, WIDTH + 3 * KVW
    cosf = cos_ref[...]
    sinf = sin_ref[...]
    lane = lax.broadcasted_iota(jnp.int32, cosf.shape, 1)
    lo = lane < ROPE_HALF
    for hd in range(HEADS):
        sl = slice(hd * LANE, (hd + 1) * LANE)
        qt_o[sl, :] = (_rope(h[:, sl], cosf, sinf, lo) * (ATT_SCALE * LOG2E)).T.astype(BF16)
    tpos = i * ts + lax.broadcasted_iota(jnp.int32, cosf.shape, 0)
    onehot = jnp.where((tpos >> int(math.log2(SEL_LEN))) == lane, 1.0, 0.0).astype(BF16)
    for g in range(KV_GROUPS):
        sl = slice(g * LANE, (g + 1) * LANE)
        off = g * LANE
        kaug_o[:, 2 * g * LANE:(2 * g + 1) * LANE] = _rope(h[:, c_ksl + off:c_ksl + off + LANE], cosf, sinf, lo).astype(BF16)
        kaug_o[:, (2 * g + 1) * LANE:(2 * g + 2) * LANE] = onehot
        kwn_o[:, sl] = _rope(h[:, c_kwn + off:c_kwn + off + LANE], cosf, sinf, lo).astype(BF16)
        vs_t = h[:, c_vsl + off:c_vsl + off + LANE].T.astype(BF16)
        for c in range(ts // SEL_TK):
            ones_rows = jnp.where(lax.broadcasted_iota(jnp.int32, (SEL_VROWS - LANE, SEL_TK), 0) == 0, 1.0, 0.0)
            vslt_o[g, c] = jnp.concatenate([vs_t[:, c * SEL_TK:(c + 1) * SEL_TK], ones_rows.astype(BF16)], axis=0)
        vw_t = h[:, c_vwn + off:c_vwn + off + LANE].T.astype(BF16)
        for c in range(ts // WIN_TK):
            vwnt_o[g, c] = vw_t[:, c * WIN_TK:(c + 1) * WIN_TK]


def _proj_nsa(x3, w_b, cos_tab, sin_tab):
    B, S, D = x3.shape
    ts = PROJ_TM
    tab = pl.BlockSpec((ts, LANE), lambda b, i: (i, 0))
    return pl.pallas_call(
        _proj_nsa_kernel,
        grid=(B, S // ts),
        in_specs=[pl.BlockSpec((None, ts, D), lambda b, i: (b, i, 0)),
                  pl.BlockSpec((D, NB), lambda b, i: (0, 0)), tab, tab],
        out_specs=[pl.BlockSpec((None, WIDTH, ts), lambda b, i: (b, 0, i)),
                   pl.BlockSpec((None, ts, 2 * KVW), lambda b, i: (b, i, 0)),
                   pl.BlockSpec((None, KV_GROUPS, ts // SEL_TK, SEL_VROWS, SEL_TK), lambda b, i: (b, 0, i, 0, 0)),
                   pl.BlockSpec((None, ts, KVW), lambda b, i: (b, i, 0)),
                   pl.BlockSpec((None, KV_GROUPS, ts // WIN_TK, LANE, WIN_TK), lambda b, i: (b, 0, i, 0, 0))],
        out_shape=[jax.ShapeDtypeStruct((B, WIDTH, S), BF16),
                   jax.ShapeDtypeStruct((B, S, 2 * KVW), BF16),
                   jax.ShapeDtypeStruct((B, KV_GROUPS, S // SEL_TK, SEL_VROWS, SEL_TK), BF16),
                   jax.ShapeDtypeStruct((B, S, KVW), BF16),
                   jax.ShapeDtypeStruct((B, KV_GROUPS, S // WIN_TK, LANE, WIN_TK), BF16)],
        compiler_params=_cparams(("parallel", "parallel")),
        name="proj_nsa",
    )(x3, w_b, cos_tab, sin_tab)


def _compress_one(t_ref, pe_ref, w1_ref, w2_ref, nc):
    half = CMP_LEN // 2
    y1 = jnp.zeros((nc, CMP_HIDDEN), F32)
    y2 = jnp.zeros((nc, CMP_HIDDEN), F32)
    for l in range(half):
        x = t_ref[pl.ds(l, nc, stride=CMP_STRIDE), :]
        y1 = y1 + _dot((x + pe_ref[l:l + 1, :]).astype(BF16), w1_ref[l * LANE:(l + 1) * LANE, :])
        y2 = y2 + _dot((x + pe_ref[half + l:half + l + 1, :]).astype(BF16),
                       w1_ref[(half + l) * LANE:(half + l + 1) * LANE, :])
    y = y1 + pltpu.roll(y2, nc - 1, 0)
    return _dot(_silu(y).astype(BF16), w2_ref[...])


def _compress_kernel(kc_ref, vc_ref, pek_ref, pev_ref, w1k_ref, w2k_ref, w1v_ref, w2v_ref,
                     cos_ref, sin_ref, kc_o, vc_o):
    nc = kc_o.shape[0]
    kc = _compress_one(kc_ref, pek_ref, w1k_ref, w2k_ref, nc)
    lo = lax.broadcasted_iota(jnp.int32, kc.shape, 1) < ROPE_HALF
    kc_o[...] = _rope(kc, cos_ref[...], sin_ref[...], lo).astype(BF16)
    vc_o[...] = _compress_one(vc_ref, pev_ref, w1v_ref, w2v_ref, nc).astype(BF16)


def _compress(h3, pek, pev, w1k, w2k, w1v, w2v, cos_c, sin_c):
    B, S, _ = h3.shape
    nc = S // CMP_STRIDE
    full = lambda shape: pl.BlockSpec(shape, lambda b, g: tuple(0 for _ in shape))
    out_spec = pl.BlockSpec((None, None, nc, LANE), lambda b, g: (b, g, 0, 0))
    out_sd = jax.ShapeDtypeStruct((B, KV_GROUPS, nc, LANE), BF16)
    cin = CMP_LEN * HEAD_DIM
    return pl.pallas_call(
        _compress_kernel,
        grid=(B, KV_GROUPS),
        in_specs=[pl.BlockSpec((None, S, LANE), lambda b, g: (b, 0, g)),
                  pl.BlockSpec((None, S, LANE), lambda b, g: (b, 0, KV_GROUPS + g)),
                  full((CMP_LEN, LANE)), full((CMP_LEN, LANE)),
                  full((cin, CMP_HIDDEN)), full((CMP_HIDDEN, LANE)),
                  full((cin, CMP_HIDDEN)), full((CMP_HIDDEN, LANE)),
                  full((nc, LANE)), full((nc, LANE))],
        out_specs=[out_spec, out_spec],
        out_shape=[out_sd, out_sd],
        compiler_params=_cparams(("parallel", "parallel")),
        name="compress",
    )(h3, h3, pek, pev, w1k, w2k, w1v, w2v, cos_c, sin_c)


def _cmp_attn_kernel(q_ref, kc_ref, vc_ref, ovt_ref, o_ref, bias_ref, st_ref, cnt_ref, *, n_sel):
    i = pl.program_id(2)
    tq = q_ref.shape[1]
    cols = HPG * tq
    nc = kc_ref.shape[0]
    t0 = i * tq
    q_g = jnp.concatenate([q_ref[hh * LANE:(hh + 1) * LANE, :] for hh in range(HPG)], axis=1)
    tpos = t0 + (lax.broadcasted_iota(jnp.int32, (1, cols), 1) & (tq - 1))
    cmp_end = lax.broadcasted_iota(jnp.int32, (nc, 1), 0) * CMP_STRIDE + (CMP_LEN - 1)
    s = jnp.where(cmp_end <= tpos, _dot(kc_ref[...], q_g), -jnp.inf)
    m = jnp.max(s, axis=0, keepdims=True)
    m = jnp.where(m > -jnp.inf, m, 0.0)
    p = jnp.exp2(s - m)
    p = p / jnp.maximum(jnp.sum(p, axis=0, keepdims=True), jnp.finfo(F32).tiny)
    vct = vc_ref[...].astype(F32).T.astype(BF16)
    _store_heads(o_ref, _dot(vct, p.astype(BF16)), 0, tq)
    psum = p[:, 0:tq]
    for hh in range(1, HPG):
        psum = psum + p[:, hh * tq:(hh + 1) * tq]
    ovt = ovt_ref[...]
    p_hi = psum.astype(BF16)
    r1 = psum - p_hi.astype(F32)
    p_mid = r1.astype(BF16)
    p_lo = (r1 - p_mid.astype(F32)).astype(BF16)
    p_sel = _dot(ovt, p_hi) + (_dot(ovt, p_mid) + _dot(ovt, p_lo))
    jb = lax.broadcasted_iota(jnp.int32, (LANE, tq), 0)
    cur = (t0 + lax.broadcasted_iota(jnp.int32, (1, tq), 1)) >> int(math.log2(SEL_LEN))
    valid = jb <= cur
    forced = valid & ((jb == 0) | (jb == cur) | (jb == cur - 1))
    st = jnp.where(forced, jnp.inf, jnp.where(valid, p_sel, -jnp.inf))
    n_grp = n_sel // SUBLANE
    st_ref[...] = st
    cnt_ref[...] = jnp.zeros(cnt_ref.shape, F32)
    jrow = lax.broadcasted_iota(jnp.int32, (SUBLANE, tq), 0)
    cur_max = (t0 + tq - 1) >> int(math.log2(SEL_LEN))
    for gb in range(n_grp):
        @pl.when(cur_max >= gb * SUBLANE)
        def _():
            grp = [st_ref[g * SUBLANE:(g + 1) * SUBLANE, :] for g in range(n_grp)]
            cnt = [cnt_ref[g * SUBLANE:(g + 1) * SUBLANE, :] for g in range(n_grp)]
            for rb_ in range(SUBLANE):
                rb = grp[gb][rb_:rb_ + 1, :]
                for g in range(n_grp):
                    if g < gb:
                        inc = jnp.where(rb > grp[g], 1.0, 0.0)
                    elif g > gb:
                        inc = jnp.where(rb >= grp[g], 1.0, 0.0)
                    else:
                        inc = jnp.where(jrow > rb_, jnp.where(rb >= grp[g], 1.0, 0.0), jnp.where(rb > grp[g], 1.0, 0.0))
                    cnt[g] = cnt[g] + inc
            for g in range(n_grp):
                cnt_ref[g * SUBLANE:(g + 1) * SUBLANE, :] = cnt[g]

    keep = (cnt_ref[...] < float(SEL_TOPK)) & (st_ref[0:n_sel, :] > -jnp.inf)
    bias = jnp.where(keep, 0.0, NEG_BIAS)
    if n_sel < LANE:
        bias = jnp.concatenate([bias, jnp.zeros((LANE - n_sel, tq), F32)], axis=0)
    bias_ref[...] = bias.astype(BF16)


def _cmp_attn(q_t, kc, vc, overlap_t):
    B, _, S = q_t.shape
    nc = kc.shape[2]
    n_sel = S // SEL_LEN
    tq = 256
    gw = HPG * LANE
    return pl.pallas_call(
        functools.partial(_cmp_attn_kernel, n_sel=n_sel),
        grid=(B, KV_GROUPS, S // tq),
        in_specs=[pl.BlockSpec((None, gw, tq), lambda b, g, i: (b, g, i)),
                  pl.BlockSpec((None, None, nc, LANE), lambda b, g, i: (b, g, 0, 0)),
                  pl.BlockSpec((None, None, nc, LANE), lambda b, g, i: (b, g, 0, 0)),
                  pl.BlockSpec((LANE, nc), lambda b, g, i: (0, 0))],
        out_specs=[pl.BlockSpec((None, tq, gw), lambda b, g, i: (b, i, g)),
                   pl.BlockSpec((None, None, LANE, tq), lambda b, g, i: (b, g, 0, i))],
        out_shape=[jax.ShapeDtypeStruct((B, S, WIDTH), BF16),
                   jax.ShapeDtypeStruct((B, KV_GROUPS, LANE, S), BF16)],
        scratch_shapes=[pltpu.VMEM((LANE, tq), F32), pltpu.VMEM((n_sel, tq), F32)],
        compiler_params=_cparams(("parallel", "parallel", "parallel")),
        name="cmp_attn",
    )(q_t, kc, vc, overlap_t)


FLASH_TQ = 256


def _store_heads(o_ref, out_t, g, tq):
    for hh in range(HPG):
        h = g * HPG + hh
        o_ref[:, h * LANE:(h + 1) * LANE] = out_t[:, hh * tq:(hh + 1) * tq].T.astype(BF16)


def _flash_sel_kernel(q_ref, b_ref, k_ref, vt_ref, o_ref, qs_ref, s_ref, m_ref, acc_ref, *, tk):
    i = pl.program_id(1)
    tq = q_ref.shape[1]
    cols = HPG * tq
    t0 = i * tq
    kw = qs_ref.shape[1]
    for g in range(KV_GROUPS):
        for hh in range(HPG):
            h = g * HPG + hh
            qs_ref[g, 0:LANE, hh * tq:(hh + 1) * tq] = q_ref[h * LANE:(h + 1) * LANE, :]
            qs_ref[g, LANE:2 * LANE, hh * tq:(hh + 1) * tq] = b_ref[g]
    m_ref[...] = jnp.full(m_ref.shape, -jnp.inf, F32)
    acc_ref[...] = jnp.zeros(acc_ref.shape, F32)
    qpos = t0 + (lax.broadcasted_iota(jnp.int32, (1, cols), 1) & (tq - 1))

    def scores(j, slot):
        k0 = pl.multiple_of(j * tk, tk)
        for g in range(KV_GROUPS):
            s_ref[slot, g] = _dot(k_ref[pl.ds(k0, tk), g * kw:(g + 1) * kw], qs_ref[g])

    def update(j, slot, masked):
        if masked:
            mask = j * tk + lax.broadcasted_iota(jnp.int32, (tk, 1), 0) <= qpos
        for g in range(KV_GROUPS):
            s = s_ref[slot, g]
            if masked:
                s = jnp.where(mask, s, -jnp.inf)
            m_old = m_ref[g]
            m_new = jnp.maximum(m_old, jnp.max(s, axis=0, keepdims=True))
            alpha = jnp.exp2(m_old - m_new)
            p = jnp.exp2(s - m_new)
            acc_ref[g] = alpha * acc_ref[g] + _dot(vt_ref[g, j], p.astype(BF16))
            m_ref[g] = m_new

    jd = t0 // tk
    last = jnp.maximum(jd - 1, 0)
    scores(jd, 0)
    scores(0, 1)
    update(jd, 0, True)

    def body(p, carry):
        scores(jnp.minimum(2 * p + 1, last), 0)
        update(2 * p, 1, False)
        scores(jnp.minimum(2 * p + 2, last), 1)
        update(2 * p + 1, 0, False)
        return carry

    lax.fori_loop(0, jd // 2, body, 0)

    @pl.when(jd % 2 == 1)
    def _():
        update(jd - 1, 1, False)

    for g in range(KV_GROUPS):
        _store_heads(o_ref, acc_ref[g, 0:LANE] / jnp.maximum(acc_ref[g, LANE:LANE + 1], jnp.finfo(F32).tiny), g, tq)


def _flash_sel(q_t, kaug, v_t, bias_t):
    B, _, S = q_t.shape
    tq = tk = FLASH_TQ
    assert tk == SEL_TK
    kw = kaug.shape[2] // KV_GROUPS
    cols = HPG * tq
    return pl.pallas_call(
        functools.partial(_flash_sel_kernel, tk=tk),
        grid=(B, S // tq),
        in_specs=[pl.BlockSpec((None, WIDTH, tq), lambda b, i: (b, 0, i)),
                  pl.BlockSpec((None, KV_GROUPS, LANE, tq), lambda b, i: (b, 0, 0, i)),
                  pl.BlockSpec((None, S, KV_GROUPS * kw), lambda b, i: (b, 0, 0)),
                  pl.BlockSpec((None, KV_GROUPS, S // tk, SEL_VROWS, tk), lambda b, i: (b, 0, 0, 0, 0))],
        out_specs=pl.BlockSpec((None, tq, WIDTH), lambda b, i: (b, i, 0)),
        out_shape=jax.ShapeDtypeStruct((B, S, WIDTH), BF16),
        scratch_shapes=[pltpu.VMEM((KV_GROUPS, kw, cols), BF16),
                        pltpu.VMEM((2, KV_GROUPS, tk, cols), F32),
                        pltpu.VMEM((KV_GROUPS, 1, cols), F32),
                        pltpu.VMEM((KV_GROUPS, SEL_VROWS, cols), F32)],
        compiler_params=_cparams(("parallel", "parallel")),
        name="flash_sel",
    )(q_t, bias_t, kaug, v_t)


def _flash_win_kernel(q_ref, k_ref, vt_ref, wb_ref, o_ref, s_ref, *, tk):
    i = pl.program_id(1)
    tq = q_ref.shape[1]
    cols = HPG * tq
    n_t = WINDOW // tk + 1
    jt = [jnp.maximum(i - d, 0) for d in range(n_t)]
    for g in range(KV_GROUPS):
        q_g = jnp.concatenate([q_ref[(g * HPG + hh) * LANE:(g * HPG + hh + 1) * LANE, :] for hh in range(HPG)], axis=1)
        for d in range(n_t):
            k0 = pl.multiple_of(jt[d] * tk, tk)
            s_ref[g, d] = _dot(k_ref[pl.ds(k0, tk), g * LANE:(g + 1) * LANE], q_g)
    static_bias = {0: 0, n_t - 1: 1}

    def tile(g, d):
        s = s_ref[g, d]
        return s + wb_ref[static_bias[d]] if d in static_bias else s

    for g in range(KV_GROUPS):
        m = jnp.max(tile(g, 0), axis=0, keepdims=True)
        for d in range(1, n_t):
            m = jnp.maximum(m, jnp.where(i >= d, jnp.max(tile(g, d), axis=0, keepdims=True), -jnp.inf))
        l = jnp.zeros((1, cols), F32)
        acc = jnp.zeros((LANE, cols), F32)
        for d in range(n_t):
            p = jnp.exp2(tile(g, d) - (m if d == 0 else jnp.where(i >= d, m, jnp.inf)))
            l = l + jnp.sum(p, axis=0, keepdims=True)
            acc = acc + _dot(vt_ref[g, jt[d]], p.astype(BF16))
        _store_heads(o_ref, acc / jnp.maximum(l, jnp.finfo(F32).tiny), g, tq)


def _window_bias(tq):
    r = np.arange(tq)[:, None]
    c = (np.arange(HPG * tq) % tq)[None, :]
    neg = np.float32(-np.inf)
    return jnp.asarray(np.stack([np.where(r <= c, np.float32(0), neg), np.where(c < r, np.float32(0), neg)]))


def _flash_win(q_t, k_r, v_t):
    B, _, S = q_t.shape
    tq = tk = FLASH_TQ
    assert tk == WIN_TK and WINDOW == 2 * tk
    cols = HPG * tq
    return pl.pallas_call(
        functools.partial(_flash_win_kernel, tk=tk),
        grid=(B, S // tq),
        in_specs=[pl.BlockSpec((None, WIDTH, tq), lambda b, i: (b, 0, i)),
                  pl.BlockSpec((None, S, KVW), lambda b, i: (b, 0, 0)),
                  pl.BlockSpec((None, KV_GROUPS, S // tk, LANE, tk), lambda b, i: (b, 0, 0, 0, 0)),
                  pl.BlockSpec((2, tk, cols), lambda b, i: (0, 0, 0))],
        out_specs=pl.BlockSpec((None, tq, WIDTH), lambda b, i: (b, i, 0)),
        out_shape=jax.ShapeDtypeStruct((B, S, WIDTH), BF16),
        scratch_shapes=[pltpu.VMEM((KV_GROUPS, WINDOW // tk + 1, tk, cols), F32)],
        compiler_params=_cparams(("parallel", "parallel")),
        name="flash_win",
    )(q_t, k_r, v_t, _window_bias(tq))


def _merge_kernel(ydn_ref, oc_ref, os_ref, ow_ref, small_ref, mgd_ref, mgn_ref, x_ref,
                  wdn_ref, wnsa_ref, wout_ref, g_ref, b_ref, o_ref):
    gates = _sigmoid(small_ref[...])
    cols = []
    for h in range(HEADS):
        sl = slice(h * LANE, (h + 1) * LANE)
        gc = gates[:, SMALL_GATE + h:SMALL_GATE + h + 1]
        gs = gates[:, SMALL_GATE + HEADS + h:SMALL_GATE + HEADS + h + 1]
        gw = gates[:, SMALL_GATE + 2 * HEADS + h:SMALL_GATE + 2 * HEADS + h + 1]
        cols.append((gc * oc_ref[:, sl] + gs * os_ref[:, sl] + gw * ow_ref[:, sl]).astype(BF16))
    y_nsa = jnp.concatenate(cols, axis=1)
    y = (_sigmoid(mgd_ref[...].astype(F32)) * _dot(ydn_ref[...], wdn_ref[...])
         + _sigmoid(mgn_ref[...].astype(F32)) * _dot(y_nsa, wnsa_ref[...]))
    z = DEEPNORM_ALPHA * x_ref[...] + _dot(y.astype(BF16), wout_ref[...])
    o_ref[...] = _layer_norm(z, g_ref[...], b_ref[...])


def _merge(y_dn2, oc2, os2, ow2, small2, h, x2, wdn, wnsa, wout, g, b):
    T, D = x2.shape
    tm = 512
    row = pl.BlockSpec((tm, WIDTH), lambda i: (i, 0))
    full = lambda shape: pl.BlockSpec(shape, lambda i: (0, 0))
    return pl.pallas_call(
        _merge_kernel,
        grid=(T // tm,),
        in_specs=[row, row, row, row,
                  pl.BlockSpec((tm, LANE), lambda i: (i, 0)),
                  pl.BlockSpec((tm, WIDTH), lambda i: (i, COL_MG_DN // WIDTH)),
                  pl.BlockSpec((tm, WIDTH), lambda i: (i, COL_MG_NSA // WIDTH)),
                  pl.BlockSpec((tm, D), lambda i: (i, 0)),
                  full((WIDTH, D)), full((WIDTH, D)), full((D, D)), full((1, D)), full((1, D))],
        out_specs=pl.BlockSpec((tm, D), lambda i: (i, 0)),
        out_shape=jax.ShapeDtypeStruct((T, D), F32),
        compiler_params=_cparams(("parallel",)),
        name="merge",
    )(y_dn2, oc2, os2, ow2, small2, h, h, x2, wdn, wnsa, wout, g, b)


FFN_TM = 512


def _ffn_up_kernel(x_ref, xp_ref, wg_ref, wv_ref, cwg_ref, cwv_ref, bg_ref, bv_ref, o_ref, xs_ref, *, tiles_per_seq):
    i = pl.program_id(1)
    tm = x_ref.shape[0]
    nv = tm // SUBLANE
    n_slab = x_ref.shape[1] // LANE
    for sp in range(SUBLANE):
        for c in range(n_slab):
            xs_ref[c, pl.ds(sp, nv, stride=SUBLANE), :] = x_ref[sp * nv:(sp + 1) * nv, c * LANE:(c + 1) * LANE]
    xb = jnp.concatenate([xs_ref[c] for c in range(n_slab)], axis=1).astype(BF16)
    notfirst = (i % tiles_per_seq != 0).astype(F32)
    xpb = (xp_ref[...] * notfirst).astype(BF16)
    sub = lax.broadcasted_iota(jnp.int32, (SUBLANE, 1), 0)

    def conv(w_ref, cw_ref, b_ref):
        u = _dot(xb, w_ref[...])
        p = _dot(xpb, w_ref[...])

        def wrap(k):
            blk = u[(nv - k) * SUBLANE:(nv - k + 1) * SUBLANE, :]
            return jnp.where(sub == 0, p[SUBLANE - k:SUBLANE - k + 1, :], pltpu.roll(blk, 1, 0))

        w1, w2 = wrap(1), wrap(2)
        u1 = jnp.concatenate([w1, u[0:(nv - 1) * SUBLANE, :]], axis=0)
        u2 = jnp.concatenate([w2, w1, u[0:(nv - 2) * SUBLANE, :]], axis=0)
        return cw_ref[0:1, :] * u2 + cw_ref[1:2, :] * u1 + cw_ref[2:3, :] * u + b_ref[...]

    o_ref[...] = (_silu(conv(wg_ref, cwg_ref, bg_ref)) * conv(wv_ref, cwv_ref, bv_ref)).astype(BF16)


def _ffn_up(x1, w_up, conv_w, conv_b, seq):
    T, D = x1.shape
    tm = FFN_TM
    nj = 2
    tn = D_FF // nj
    return pl.pallas_call(
        functools.partial(_ffn_up_kernel, tiles_per_seq=seq // tm),
        grid=(nj, T // tm),
        in_specs=[pl.BlockSpec((tm, D), lambda j, i: (i, 0)),
                  pl.BlockSpec((SUBLANE, D), lambda j, i: (jnp.maximum(i * (tm // SUBLANE) - 1, 0), 0)),
                  pl.BlockSpec((D, tn), lambda j, i: (0, j)),
                  pl.BlockSpec((D, tn), lambda j, i: (0, j + nj)),
                  pl.BlockSpec((FFN_CONV, tn), lambda j, i: (0, j)),
                  pl.BlockSpec((FFN_CONV, tn), lambda j, i: (0, j + nj)),
                  pl.BlockSpec((1, tn), lambda j, i: (0, j)),
                  pl.BlockSpec((1, tn), lambda j, i: (0, j + nj))],
        out_specs=pl.BlockSpec((tm, tn), lambda j, i: (i, j)),
        out_shape=jax.ShapeDtypeStruct((T, D_FF), BF16),
        scratch_shapes=[pltpu.VMEM((D // LANE, tm, LANE), F32)],
        compiler_params=_cparams(("parallel", "parallel")),
        name="ffn_up",
    )(x1, x1, w_up, w_up, conv_w, conv_w, conv_b, conv_b)


def _ffn_down_kernel(a_ref, w_ref, x_ref, g_ref, b_ref, o_ref, fs_ref):
    tm, d = x_ref.shape
    nv = tm // SUBLANE
    n_slab = d // LANE
    f = _dot(a_ref[...], w_ref[...])
    for c in range(n_slab):
        fs_ref[c] = f[:, c * LANE:(c + 1) * LANE]
    for sp in range(SUBLANE):
        rows = slice(sp * nv, (sp + 1) * nv)
        f_nat = jnp.concatenate([fs_ref[c, pl.ds(sp, nv, stride=SUBLANE), :] for c in range(n_slab)], axis=1)
        z = DEEPNORM_ALPHA * x_ref[rows, :] + f_nat
        o_ref[rows, :] = _layer_norm(z, g_ref[...], b_ref[...])


def _ffn_down(a, w_down, x1, g, b):
    T, D = x1.shape
    tm = FFN_TM
    full = lambda shape: pl.BlockSpec(shape, lambda i: (0, 0))
    return pl.pallas_call(
        _ffn_down_kernel,
        grid=(T // tm,),
        in_specs=[pl.BlockSpec((tm, D_FF), lambda i: (i, 0)), full((D_FF, D)),
                  pl.BlockSpec((tm, D), lambda i: (i, 0)), full((1, D)), full((1, D))],
        out_specs=pl.BlockSpec((tm, D), lambda i: (i, 0)),
        out_shape=jax.ShapeDtypeStruct((T, D), F32),
        scratch_shapes=[pltpu.VMEM((D // LANE, tm, LANE), F32)],
        compiler_params=_cparams(("parallel",)),
        name="ffn_down",
    )(a, w_down, x1, g, b)


def _split_w_in_kernel(w_ref, a_ref, b_ref, c_ref):
    o_dz = 3 * WIDTH
    o_db = 4 * WIDTH
    o_nq = o_db + 2 * HEADS
    o_kc = o_nq + WIDTH
    o_ksl = o_kc + 2 * KVW
    o_gate = o_ksl + 4 * KVW
    o_mg = o_gate + 3 * HEADS
    rows = w_ref.shape[0]
    pad = jnp.zeros((rows, LANE - 5 * HEADS), F32)
    a_ref[:, 0:o_dz] = w_ref[:, 0:o_dz].astype(BF16)
    a_ref[:, o_dz:NA] = jnp.concatenate([w_ref[:, o_db:o_nq], w_ref[:, o_gate:o_mg], pad], axis=1).astype(BF16)
    b_ref[:, 0:WIDTH] = w_ref[:, o_nq:o_kc].astype(BF16)
    b_ref[:, WIDTH:NB] = w_ref[:, o_ksl:o_gate].astype(BF16)
    c_ref[:, 0:WIDTH] = w_ref[:, o_dz:o_db].astype(BF16)
    c_ref[:, WIDTH:3 * WIDTH] = w_ref[:, o_mg:o_mg + 2 * WIDTH].astype(BF16)
    c_ref[:, 3 * WIDTH:NC_COLS] = w_ref[:, o_kc:o_ksl].astype(BF16)


def _split_w_in(w):
    d, n_in = w.shape
    tr = 128
    return pl.pallas_call(
        _split_w_in_kernel,
        grid=(d // tr,),
        in_specs=[pl.BlockSpec((tr, n_in), lambda i: (i, 0))],
        out_specs=[pl.BlockSpec((tr, NA), lambda i: (i, 0)), pl.BlockSpec((tr, NB), lambda i: (i, 0)),
                   pl.BlockSpec((tr, NC_COLS), lambda i: (i, 0))],
        out_shape=[jax.ShapeDtypeStruct((d, NA), BF16), jax.ShapeDtypeStruct((d, NB), BF16),
                   jax.ShapeDtypeStruct((d, NC_COLS), BF16)],
        compiler_params=_cparams(("parallel",)),
        name="split_w_in",
    )(w)


def _rope_tables(pos):
    pos = np.asarray(pos)
    inv_freq = np.float32(ROPE_THETA) ** (-(np.arange(ROPE_HALF, dtype=np.float32) / np.float32(ROPE_HALF)))
    ang = pos.astype(np.float32)[:, None] * inv_freq.astype(np.float32)
    cos = np.cos(ang.astype(np.float64)).astype(np.float32)
    sin = np.sin(ang.astype(np.float64)).astype(np.float32)
    p = pos.shape[0]
    cosf = np.concatenate([cos, cos, np.ones((p, LANE - ROPE_DIM), np.float32)], axis=1)
    sinf = np.concatenate([-sin, sin, np.zeros((p, LANE - ROPE_DIM), np.float32)], axis=1)
    return jnp.asarray(cosf), jnp.asarray(sinf)


def _overlap_table(nc, n_sel):
    n = np.arange(nc)[None, :]
    j = np.arange(LANE)[:, None]
    ov = (n * CMP_STRIDE <= j * SEL_LEN + SEL_LEN - 1) & (n * CMP_STRIDE + CMP_LEN - 1 >= j * SEL_LEN)
    ov = ov & (j < n_sel) & (n < nc - 1)
    return jnp.asarray(ov.astype(np.float32)).astype(BF16)


def _lane_row(vals, offset):
    return jnp.zeros((1, LANE), F32).at[0, offset:offset + vals.shape[0]].set(vals.astype(F32))


def kernel(x, w_in, dn_conv_w, dn_a_log, dn_dt_bias, dn_norm_w, cmp_pos_k, cmp_pos_v, cmp_k_w1, cmp_k_w2,
           cmp_v_w1, cmp_v_w2, w_branch_dn, w_branch_nsa, w_out, ln1_g, ln1_b, ffn_w_up, ffn_conv_w,
           ffn_conv_b, ffn_w_down, ln2_g, ln2_b):
    B, S, D = x.shape
    T = B * S
    nc = S // CMP_STRIDE
    n_sel = S // SEL_LEN
    assert D == WIDTH and S % 512 == 0 and n_sel <= LANE and n_sel % SUBLANE == 0
    cos_t, sin_t = _rope_tables(np.arange(S))
    cos_c, sin_c = _rope_tables(np.arange(nc) * CMP_STRIDE + (CMP_LEN - 1))
    overlap = _overlap_table(nc, n_sel)
    x2 = x.reshape(T, D)
    for layer in range(w_in.shape[0]):
        w_a, w_b, w_c = _split_w_in(w_in[layer])
        x3 = x2.reshape(B, S, D)
        h3, kv_cmp = _proj_rest(x3, w_c)

        q, k, v, bg, small = _proj_gdn(x3, w_a, dn_conv_w[layer], _lane_row(dn_a_log[layer], SMALL_DECAY),
                                       _lane_row(dn_dt_bias[layer], SMALL_DECAY))
        y_dn = _gdn(q, k, v, bg, h3, dn_norm_w[layer].reshape(1, LANE))

        q_t, kaug, vsl_t, kwn_r, vwn_t = _proj_nsa(x3, w_b, cos_t, sin_t)
        kc, vc = _compress(kv_cmp, cmp_pos_k[layer], cmp_pos_v[layer],
                           cmp_k_w1[layer].astype(BF16), cmp_k_w2[layer].astype(BF16),
                           cmp_v_w1[layer].astype(BF16), cmp_v_w2[layer].astype(BF16), cos_c, sin_c)
        o_cmp, bias = _cmp_attn(q_t, kc, vc, overlap)
        o_sel = _flash_sel(q_t, kaug, vsl_t, bias)
        o_win = _flash_win(q_t, kwn_r, vwn_t)

        x1 = _merge(y_dn.reshape(T, WIDTH), o_cmp.reshape(T, WIDTH), o_sel.reshape(T, WIDTH),
                    o_win.reshape(T, WIDTH), small.reshape(T, LANE), h3.reshape(T, COL_KC), x2,
                    w_branch_dn[layer].astype(BF16), w_branch_nsa[layer].astype(BF16),
                    w_out[layer].astype(BF16), ln1_g[layer].reshape(1, D), ln1_b[layer].reshape(1, D))
        a = _ffn_up(x1, ffn_w_up[layer].astype(BF16), ffn_conv_w[layer],
                    ffn_conv_b[layer].reshape(1, 2 * D_FF), S)
        x2 = _ffn_down(a, ffn_w_down[layer].astype(BF16), x1, ln2_g[layer].reshape(1, D), ln2_b[layer].reshape(1, D))
    return x2.reshape(B, S, D)
```

```python
import functools
import math

import numpy as np
import jax
import jax.numpy as jnp
from jax import lax
from jax.experimental import pallas as pl
from jax.experimental.pallas import tpu as pltpu

F32 = jnp.float32
BF16 = jnp.bfloat16
HI = lax.Precision.HIGHEST

HEADS = 8
HEAD_DIM = 128
KV_GROUPS = 2
HPG = HEADS // KV_GROUPS
DN_CONV = 4
DN_CHUNK = 64
CMP_LEN = 32
CMP_STRIDE = 16
CMP_HIDDEN = 256
SEL_LEN = 64
SEL_TOPK = 16
WINDOW = 512
ROPE_THETA = 500000.0
ROPE_DIM = HEAD_DIM // 4
ROPE_HALF = ROPE_DIM // 2
D_FF = 2816
FFN_CONV = 3
LN_EPS = 1e-5
RMS_EPS = 1e-6
DEEPNORM_ALPHA = 2.0 ** 0.25
ATT_SCALE = HEAD_DIM ** -0.5

LANE = 128
SUBLANE = 8
VMEM_LIMIT = 56 * 1024 * 1024

WIDTH = HEADS * HEAD_DIM
KVW = KV_GROUPS * HEAD_DIM
NA = 3 * WIDTH + LANE
SMALL_BETA = 0
SMALL_DECAY = HEADS
SMALL_GATE = 2 * HEADS
NB = WIDTH + 4 * KVW
NC_COLS = 3 * WIDTH + 2 * KVW
COL_DZ = 0
COL_MG_DN = WIDTH
COL_MG_NSA = 2 * WIDTH
COL_KC = 3 * WIDTH
COL_VC = COL_KC + KVW

NEG_BIAS = -(2.0 ** 100)


def _cparams(sem):
    return pltpu.CompilerParams(dimension_semantics=sem, vmem_limit_bytes=VMEM_LIMIT)


def _sigmoid(x):
    return 1.0 / (1.0 + jnp.exp(-x))


def _silu(x):
    return x * _sigmoid(x)


def _dot(a, b, precision=None):
    return jnp.dot(a, b, preferred_element_type=F32, precision=precision)


def _dot_nt(a, b):
    return lax.dot_general(a, b, (((1,), (1,)), ((), ())), preferred_element_type=F32)


def _dot_tn(a, b):
    return lax.dot_general(a, b, (((0,), (0,)), ((), ())), preferred_element_type=F32)


def _layer_norm(z, g, b):
    mu = jnp.mean(z, -1, keepdims=True)
    zc = z - mu
    var = jnp.mean(zc * zc, -1, keepdims=True)
    return zc * lax.rsqrt(var + LN_EPS) * g + b


PROJ_TM = 512


def _proj_rest_kernel(x_ref, w_ref, g_ref, c_ref):
    h = _dot(x_ref[...].astype(BF16), w_ref[...])
    g_ref[...] = h[:, 0:COL_KC].astype(BF16)
    c_ref[...] = h[:, COL_KC:NC_COLS]


def _proj_rest(x3, w):
    B, S, D = x3.shape
    tm = PROJ_TM
    return pl.pallas_call(
        _proj_rest_kernel,
        grid=(B, S // tm),
        in_specs=[pl.BlockSpec((None, tm, D), lambda b, i: (b, i, 0)),
                  pl.BlockSpec((D, NC_COLS), lambda b, i: (0, 0))],
        out_specs=[pl.BlockSpec((None, tm, COL_KC), lambda b, i: (b, i, 0)),
                   pl.BlockSpec((None, tm, NC_COLS - COL_KC), lambda b, i: (b, i, 0))],
        out_shape=[jax.ShapeDtypeStruct((B, S, COL_KC), BF16),
                   jax.ShapeDtypeStruct((B, S, NC_COLS - COL_KC), F32)],
        compiler_params=_cparams(("parallel", "parallel")),
        name="proj_rest",
    )(x3, w)


def _proj_gdn_kernel(x_ref, w_ref, cw_ref, alog_ref, dtb_ref, q_ref, k_ref, v_ref, bg_ref, small_ref,
                     xs_ref, ys_ref, carry_ref):
    i = pl.program_id(1)
    ts = x_ref.shape[0]
    nv = ts // SUBLANE
    wq = 3 * WIDTH
    n_slab = x_ref.shape[1] // LANE

    @pl.when(i == 0)
    def _():
        carry_ref[...] = jnp.zeros(carry_ref.shape, F32)

    for sp in range(SUBLANE):
        for c in range(n_slab):
            xs_ref[c, pl.ds(sp, nv, stride=SUBLANE), :] = x_ref[sp * nv:(sp + 1) * nv, c * LANE:(c + 1) * LANE]
    xb = jnp.concatenate([xs_ref[c] for c in range(n_slab)], axis=1).astype(BF16)
    h = _dot(xb, w_ref[...])
    sub = lax.broadcasted_iota(jnp.int32, (SUBLANE, 1), 0)

    def token_order(c, dst_ref, dst_sl, dtype):
        for sp in range(SUBLANE):
            dst_ref[sp * nv:(sp + 1) * nv, dst_sl] = ys_ref[c, pl.ds(sp, nv, stride=SUBLANE), :].astype(dtype)

    for c in range(3 * HEADS):
        sl = slice(c * LANE, (c + 1) * LANE)
        col = h[:, sl]
        wrap = [jnp.where(sub == 0, carry_ref[SUBLANE - k:SUBLANE - k + 1, sl],
                          pltpu.roll(col[(nv - k) * SUBLANE:(nv - k + 1) * SUBLANE, :], 1, 0))
                for k in range(1, DN_CONV)]
        acc = cw_ref[DN_CONV - 1:DN_CONV, sl] * col
        for s in range(1, DN_CONV):
            shifted = jnp.concatenate(wrap[0:s][::-1] + [col[0:(nv - s) * SUBLANE, :]], axis=0)
            acc = acc + cw_ref[DN_CONV - 1 - s:DN_CONV - s, sl] * shifted
        y = _silu(acc)
        if c < 2 * HEADS:
            y = y * lax.rsqrt(jnp.sum(y * y, -1, keepdims=True) + RMS_EPS)
        if c < HEADS:
            y = y * ATT_SCALE
        ys_ref[c] = y
        dst = (q_ref, k_ref, v_ref)[c // HEADS]
        token_order(c, dst, slice((c % HEADS) * LANE, (c % HEADS + 1) * LANE), BF16)
    for k in range(1, DN_CONV):
        carry_ref[SUBLANE - k:SUBLANE - k + 1, :] = h[(nv - k + 1) * SUBLANE - 1:(nv - k + 1) * SUBLANE, 0:wq]
    small = h[:, wq:wq + LANE]
    beta = _sigmoid(small)
    xs = small + dtb_ref[...]
    softplus = jnp.maximum(xs, 0.0) + jnp.log1p(jnp.exp(-jnp.abs(xs)))
    g = -jnp.exp(alog_ref[...]) * softplus
    lane = lax.broadcasted_iota(jnp.int32, small.shape, 1)
    ys_ref[3 * HEADS] = small
    token_order(3 * HEADS, small_ref, slice(0, LANE), F32)
    ys_ref[3 * HEADS + 1] = jnp.where(lane < SMALL_DECAY, beta, g)
    token_order(3 * HEADS + 1, bg_ref, slice(0, LANE), F32)


def _proj_gdn(x3, w_a, conv_w, alog_row, dtb_row):
    B, S, D = x3.shape
    ts = PROJ_TM
    wq = 3 * WIDTH
    out_sd = jax.ShapeDtypeStruct((B, S, WIDTH), BF16)
    small_sd = jax.ShapeDtypeStruct((B, S, LANE), F32)
    wide = pl.BlockSpec((None, ts, WIDTH), lambda b, i: (b, i, 0))
    narrow = pl.BlockSpec((None, ts, LANE), lambda b, i: (b, i, 0))
    return pl.pallas_call(
        _proj_gdn_kernel,
        grid=(B, S // ts),
        in_specs=[
            pl.BlockSpec((None, ts, D), lambda b, i: (b, i, 0)),
            pl.BlockSpec((D, NA), lambda b, i: (0, 0)),
            pl.BlockSpec((DN_CONV, wq), lambda b, i: (0, 0)),
            pl.BlockSpec((1, LANE), lambda b, i: (0, 0)),
            pl.BlockSpec((1, LANE), lambda b, i: (0, 0)),
        ],
        out_specs=[wide, wide, wide, narrow, narrow],
        out_shape=[out_sd, out_sd, out_sd, small_sd, small_sd],
        scratch_shapes=[pltpu.VMEM((D // LANE, ts, LANE), F32),
                        pltpu.VMEM((3 * HEADS + 2, ts, LANE), F32),
                        pltpu.VMEM((SUBLANE, wq), F32)],
        compiler_params=_cparams(("parallel", "arbitrary")),
        name="proj_gdn",
    )(x3, w_a, conv_w, alog_row, dtb_row)


GDN_CHUNKS_PER_STEP = 4
GDN_BATCH_PER_STEP = 2


GDN_QUAD = 4


def _gdn_kernel(q_ref, k_ref, v_ref, bg_ref, z_ref, nw_ref, y_ref, state_ref):
    n = pl.program_id(1)

    @pl.when(n == 0)
    def _():
        state_ref[...] = jnp.zeros(state_ref.shape, F32)

    C = DN_CHUNK
    QW = GDN_QUAD * C
    QL = GDN_QUAD * LANE
    shift_c = int(math.log2(C))
    r64 = lax.broadcasted_iota(jnp.int32, (C, C), 0)
    c64 = lax.broadcasted_iota(jnp.int32, (C, C), 1)
    tril = jnp.where(r64 >= c64, 1.0, 0.0).astype(F32)
    triu = jnp.where(r64 <= c64, 1.0, 0.0).astype(F32)
    rr = lax.broadcasted_iota(jnp.int32, (C, QW), 0)
    ll = lax.broadcasted_iota(jnp.int32, (C, QW), 1)
    jj = ll & (C - 1)
    hb = ll >> shift_c
    incl_t = rr >= jj
    strict_t = rr > jj
    eye_t = jnp.where(rr == jj, 1.0, 0.0).astype(F32)
    bd_c = jnp.where((lax.broadcasted_iota(jnp.int32, (QW, QW), 0) >> shift_c)
                     == (lax.broadcasted_iota(jnp.int32, (QW, QW), 1) >> shift_c), 1.0, 0.0).astype(BF16)
    bd_l = jnp.where((lax.broadcasted_iota(jnp.int32, (QW, QL), 0) >> shift_c)
                     == (lax.broadcasted_iota(jnp.int32, (QW, QL), 1) >> int(math.log2(LANE))), 1.0, 0.0).astype(BF16)
    zeros_rhs = jnp.zeros((C, 2 * LANE), BF16)
    nw = nw_ref[...]
    nb = q_ref.shape[0]
    quads = [(bb, c, a) for bb in range(nb) for c in range(GDN_CHUNKS_PER_STEP) for a in range(HEADS // GDN_QUAD)]
    gc_cols, gc_rows, bgs = {}, {}, {}
    for bb in range(nb):
        bg = bg_ref[bb]
        bg_t = bg.T
        bgs[bb] = bg
        for c in range(GDN_CHUNKS_PER_STEP):
            rs = slice(c * C, (c + 1) * C)
            gc_cols[bb, c] = _dot(tril, bg[rs, :], HI)
            gc_rows[bb, c] = _dot(bg_t[SMALL_DECAY:SMALL_DECAY + HEADS, rs], triu, HI)
    decay, lower, qk, rhs_bd, qd, kd, eglast = {}, {}, {}, {}, {}, {}, {}
    for u_ in quads:
        bb, c, a = u_
        rs = slice(c * C, (c + 1) * C)
        h0 = a * GDN_QUAD
        gcols = [gc_cols[bb, c][:, SMALL_DECAY + h0 + hh:SMALL_DECAY + h0 + hh + 1] for hh in range(GDN_QUAD)]
        gcol_cat = jnp.broadcast_to(gcols[0], (C, QW))
        for hh in range(1, GDN_QUAD):
            gcol_cat = jnp.where(hb == hh, gcols[hh], gcol_cat)
        grow_cat = jnp.concatenate([gc_rows[bb, c][h0 + hh:h0 + hh + 1, :] for hh in range(GDN_QUAD)], axis=1)
        decay[u_] = jnp.exp(jnp.where(incl_t, gcol_cat - grow_cat, -jnp.inf))
        kq = k_ref[bb, rs, a * QL:(a + 1) * QL].astype(F32)
        qq = q_ref[bb, rs, a * QL:(a + 1) * QL].astype(F32)
        kbs, rows_bd = [], []
        for hh in range(GDN_QUAD):
            h = h0 + hh
            ls = slice(hh * LANE, (hh + 1) * LANE)
            beta = bgs[bb][rs, SMALL_BETA + h:SMALL_BETA + h + 1]
            eg = jnp.exp(gcols[hh])
            glast = gcols[hh][C - 1:C, :]
            kb = kq[:, ls] * beta
            vb = v_ref[bb, rs, h * LANE:(h + 1) * LANE].astype(F32) * beta
            kbs.append(kb)
            rhs_h = jnp.concatenate([vb, kb * eg], axis=1).astype(BF16)
            rows_bd.append(jnp.concatenate([zeros_rhs] * hh + [rhs_h] + [zeros_rhs] * (GDN_QUAD - 1 - hh), axis=1))
            qd[bb, c, h] = (qq[:, ls] * eg).astype(BF16)
            kd[bb, c, h] = (kq[:, ls] * jnp.exp(glast - gcols[hh])).astype(BF16)
            eglast[bb, c, h] = jnp.exp(glast)
        rhs_bd[u_] = jnp.concatenate(rows_bd, axis=0)
        k_bd = jnp.concatenate([kq.astype(BF16)] * GDN_QUAD, axis=0) * bd_l
        lhs = jnp.concatenate([jnp.concatenate(kbs, axis=1), qq], axis=0).astype(BF16)
        aq = _dot_nt(lhs, k_bd)
        lower[u_] = jnp.where(strict_t, aq[0:C] * decay[u_], 0.0)
        qk[u_] = (aq[C:2 * C] * decay[u_]).astype(BF16)

    def block_diag(y_b):
        return jnp.concatenate([y_b] * GDN_QUAD, axis=0) * bd_c

    tinv = {u_: eye_t - lower[u_] for u_ in quads}
    y_b = {u_: lower[u_].astype(BF16) for u_ in quads}
    ypow = {u_: _dot(y_b[u_], block_diag(y_b[u_])) for u_ in quads}
    for it in range(5):
        for u_ in quads:
            yb = ypow[u_].astype(BF16)
            if it < 4:
                r = _dot(jnp.concatenate([tinv[u_].astype(BF16), yb], axis=0), block_diag(yb))
                tinv[u_] = tinv[u_] + r[0:C]
                ypow[u_] = r[C:2 * C]
            else:
                tinv[u_] = tinv[u_] + _dot(tinv[u_].astype(BF16), block_diag(yb))
    uw = {u_: _dot(tinv[u_].astype(BF16), rhs_bd[u_]) for u_ in quads}
    zeros_v = jnp.zeros((C, LANE), BF16)
    chains = [(bb, h) for bb in range(nb) for h in range(HEADS)]
    for c in range(GDN_CHUNKS_PER_STEP):
        rs = slice(c * C, (c + 1) * C)
        states = {s_: state_ref[s_[0] * HEADS + s_[1]] for s_ in chains}
        ws_qs = {}
        for s_ in chains:
            bb, h = s_
            a, hh = divmod(h, GDN_QUAD)
            w_h = uw[bb, c, a][:, hh * 2 * LANE + LANE:(hh + 1) * 2 * LANE]
            ws_qs[s_] = _dot(jnp.concatenate([w_h.astype(BF16), qd[bb, c, h]], axis=0), states[s_].astype(BF16))
        v_new = {}
        for s_ in chains:
            bb, h = s_
            a, hh = divmod(h, GDN_QUAD)
            v_new[s_] = (uw[bb, c, a][:, hh * 2 * LANE:hh * 2 * LANE + LANE] - ws_qs[s_][0:C]).astype(BF16)
        outs = {}
        for s_ in chains:
            bb, h = s_
            a, hh = divmod(h, GDN_QUAD)
            v_pad = jnp.concatenate([zeros_v] * hh + [v_new[s_]] + [zeros_v] * (GDN_QUAD - 1 - hh), axis=0)
            outs[s_] = ws_qs[s_][C:2 * C] + _dot(qk[bb, c, a], v_pad)
            state_ref[bb * HEADS + h] = states[s_] * eglast[bb, c, h] + _dot_tn(kd[bb, c, h], v_new[s_])
        for s_ in chains:
            bb, h = s_
            ls = slice(h * LANE, (h + 1) * LANE)
            o = outs[s_]
            o = o * lax.rsqrt(jnp.mean(o * o, -1, keepdims=True) + RMS_EPS)
            y_ref[bb, rs, ls] = (o * nw * _silu(z_ref[bb, rs, ls].astype(F32))).astype(BF16)


def _gdn(q, k, v, bg, h3, norm_w):
    B, S, _ = q.shape
    rows = GDN_CHUNKS_PER_STEP * DN_CHUNK
    nb = GDN_BATCH_PER_STEP if B % GDN_BATCH_PER_STEP == 0 else 1
    tile = pl.BlockSpec((nb, rows, WIDTH), lambda b, n: (b, n, 0))
    return pl.pallas_call(
        _gdn_kernel,
        grid=(B // nb, S // rows),
        in_specs=[tile, tile, tile,
                  pl.BlockSpec((nb, rows, LANE), lambda b, n: (b, n, 0)),
                  pl.BlockSpec((nb, rows, WIDTH), lambda b, n: (b, n, COL_DZ // WIDTH)),
                  pl.BlockSpec((1, LANE), lambda b, n: (0, 0))],
        out_specs=tile,
        out_shape=jax.ShapeDtypeStruct((B, S, WIDTH), BF16),
        scratch_shapes=[pltpu.VMEM((nb * HEADS, HEAD_DIM, HEAD_DIM), F32)],
        compiler_params=_cparams(("parallel", "arbitrary")),
        name="gdn",
    )(q, k, v, bg, h3, norm_w)


def _rope(t, cosf, sinf, lo):
    partner = jnp.where(lo, pltpu.roll(t, LANE - ROPE_HALF, 1), pltpu.roll(t, ROPE_HALF, 1))
    return t * cosf + partner * sinf


SEL_TK = 256
WIN_TK = 256
SEL_VROWS = LANE + 16
LOG2E = math.log2(math.e)


def _proj_nsa_kernel(x_ref, w_ref, cos_ref, sin_ref, qt_o, kaug_o, vslt_o, kwn_o, vwnt_o):
    i = pl.program_id(1)
    ts = x_ref.shape[0]
    h = _dot(x_ref[...].astype(BF16), w_ref[...])
    c_ksl, c_vsl, c_kwn, c_vwn = WIDTH, WIDTH + KVW, WIDTH + 2 * KVW, WIDTH + 3 * KVW
    cosf = cos_ref[...]
    sinf = sin_ref[...]
    lane = lax.broadcasted_iota(jnp.int32, cosf.shape, 1)
    lo = lane < ROPE_HALF
    for hd in range(HEADS):
        sl = slice(hd * LANE, (hd + 1) * LANE)
        qt_o[sl, :] = (_rope(h[:, sl], cosf, sinf, lo) * (ATT_SCALE * LOG2E)).T.astype(BF16)
    tpos = i * ts + lax.broadcasted_iota(jnp.int32, cosf.shape, 0)
    onehot = jnp.where((tpos >> int(math.log2(SEL_LEN))) == lane, 1.0, 0.0).astype(BF16)
    for g in range(KV_GROUPS):
        sl = slice(g * LANE, (g + 1) * LANE)
        off = g * LANE
        kaug_o[:, 2 * g * LANE:(2 * g + 1) * LANE] = _rope(h[:, c_ksl + off:c_ksl + off + LANE], cosf, sinf, lo).astype(BF16)
        kaug_o[:, (2 * g + 1) * LANE:(2 * g + 2) * LANE] = onehot
        kwn_o[:, sl] = _rope(h[:, c_kwn + off:c_kwn + off + LANE], cosf, sinf, lo).astype(BF16)
        vs_t = h[:, c_vsl + off:c_vsl + off + LANE].T.astype(BF16)
        for c in range(ts // SEL_TK):
            ones_rows = jnp.where(lax.broadcasted_iota(jnp.int32, (SEL_VROWS - LANE, SEL_TK), 0) == 0, 1.0, 0.0)
            vslt_o[g, c] = jnp.concatenate([vs_t[:, c * SEL_TK:(c + 1) * SEL_TK], ones_rows.astype(BF16)], axis=0)
        vw_t = h[:, c_vwn + off:c_vwn + off + LANE].T.astype(BF16)
        for c in range(ts // WIN_TK):
            vwnt_o[g, c] = vw_t[:, c * WIN_TK:(c + 1) * WIN_TK]


def _proj_nsa(x3, w_b, cos_tab, sin_tab):
    B, S, D = x3.shape
    ts = PROJ_TM
    tab = pl.BlockSpec((ts, LANE), lambda b, i: (i, 0))
    return pl.pallas_call(
        _proj_nsa_kernel,
        grid=(B, S // ts),
        in_specs=[pl.BlockSpec((None, ts, D), lambda b, i: (b, i, 0)),
                  pl.BlockSpec((D, NB), lambda b, i: (0, 0)), tab, tab],
        out_specs=[pl.BlockSpec((None, WIDTH, ts), lambda b, i: (b, 0, i)),
                   pl.BlockSpec((None, ts, 2 * KVW), lambda b, i: (b, i, 0)),
                   pl.BlockSpec((None, KV_GROUPS, ts // SEL_TK, SEL_VROWS, SEL_TK), lambda b, i: (b, 0, i, 0, 0)),
                   pl.BlockSpec((None, ts, KVW), lambda b, i: (b, i, 0)),
                   pl.BlockSpec((None, KV_GROUPS, ts // WIN_TK, LANE, WIN_TK), lambda b, i: (b, 0, i, 0, 0))],
        out_shape=[jax.ShapeDtypeStruct((B, WIDTH, S), BF16),
                   jax.ShapeDtypeStruct((B, S, 2 * KVW), BF16),
                   jax.ShapeDtypeStruct((B, KV_GROUPS, S // SEL_TK, SEL_VROWS, SEL_TK), BF16),
                   jax.ShapeDtypeStruct((B, S, KVW), BF16),
                   jax.ShapeDtypeStruct((B, KV_GROUPS, S // WIN_TK, LANE, WIN_TK), BF16)],
        compiler_params=_cparams(("parallel", "parallel")),
        name="proj_nsa",
    )(x3, w_b, cos_tab, sin_tab)


def _compress_one(t_ref, pe_ref, w1_ref, w2_ref, nc):
    half = CMP_LEN // 2
    y1 = jnp.zeros((nc, CMP_HIDDEN), F32)
    y2 = jnp.zeros((nc, CMP_HIDDEN), F32)
    for l in range(half):
        x = t_ref[pl.ds(l, nc, stride=CMP_STRIDE), :]
        y1 = y1 + _dot((x + pe_ref[l:l + 1, :]).astype(BF16), w1_ref[l * LANE:(l + 1) * LANE, :])
        y2 = y2 + _dot((x + pe_ref[half + l:half + l + 1, :]).astype(BF16),
                       w1_ref[(half + l) * LANE:(half + l + 1) * LANE, :])
    y = y1 + pltpu.roll(y2, nc - 1, 0)
    return _dot(_silu(y).astype(BF16), w2_ref[...])


def _compress_kernel(kc_ref, vc_ref, pek_ref, pev_ref, w1k_ref, w2k_ref, w1v_ref, w2v_ref,
                     cos_ref, sin_ref, kc_o, vc_o):
    nc = kc_o.shape[0]
    kc = _compress_one(kc_ref, pek_ref, w1k_ref, w2k_ref, nc)
    lo = lax.broadcasted_iota(jnp.int32, kc.shape, 1) < ROPE_HALF
    kc_o[...] = _rope(kc, cos_ref[...], sin_ref[...], lo).astype(BF16)
    vc_o[...] = _compress_one(vc_ref, pev_ref, w1v_ref, w2v_ref, nc).astype(BF16)


def _compress(h3, pek, pev, w1k, w2k, w1v, w2v, cos_c, sin_c):
    B, S, _ = h3.shape
    nc = S // CMP_STRIDE
    full = lambda shape: pl.BlockSpec(shape, lambda b, g: tuple(0 for _ in shape))
    out_spec = pl.BlockSpec((None, None, nc, LANE), lambda b, g: (b, g, 0, 0))
    out_sd = jax.ShapeDtypeStruct((B, KV_GROUPS, nc, LANE), BF16)
    cin = CMP_LEN * HEAD_DIM
    return pl.pallas_call(
        _compress_kernel,
        grid=(B, KV_GROUPS),
        in_specs=[pl.BlockSpec((None, S, LANE), lambda b, g: (b, 0, g)),
                  pl.BlockSpec((None, S, LANE), lambda b, g: (b, 0, KV_GROUPS + g)),
                  full((CMP_LEN, LANE)), full((CMP_LEN, LANE)),
                  full((cin, CMP_HIDDEN)), full((CMP_HIDDEN, LANE)),
                  full((cin, CMP_HIDDEN)), full((CMP_HIDDEN, LANE)),
                  full((nc, LANE)), full((nc, LANE))],
        out_specs=[out_spec, out_spec],
        out_shape=[out_sd, out_sd],
        compiler_params=_cparams(("parallel", "parallel")),
        name="compress",
    )(h3, h3, pek, pev, w1k, w2k, w1v, w2v, cos_c, sin_c)


def _cmp_attn_kernel(q_ref, kc_ref, vc_ref, ovt_ref, o_ref, bias_ref, st_ref, cnt_ref, *, n_sel):
    i = pl.program_id(2)
    tq = q_ref.shape[1]
    cols = HPG * tq
    nc = kc_ref.shape[0]
    t0 = i * tq
    q_g = jnp.concatenate([q_ref[hh * LANE:(hh + 1) * LANE, :] for hh in range(HPG)], axis=1)
    tpos = t0 + (lax.broadcasted_iota(jnp.int32, (1, cols), 1) & (tq - 1))
    cmp_end = lax.broadcasted_iota(jnp.int32, (nc, 1), 0) * CMP_STRIDE + (CMP_LEN - 1)
    s = jnp.where(cmp_end <= tpos, _dot(kc_ref[...], q_g), -jnp.inf)
    m = jnp.max(s, axis=0, keepdims=True)
    m = jnp.where(m > -jnp.inf, m, 0.0)
    p = jnp.exp2(s - m)
    p = p / jnp.maximum(jnp.sum(p, axis=0, keepdims=True), jnp.finfo(F32).tiny)
    vct = vc_ref[...].astype(F32).T.astype(BF16)
    _store_heads(o_ref, _dot(vct, p.astype(BF16)), 0, tq)
    psum = p[:, 0:tq]
    for hh in range(1, HPG):
        psum = psum + p[:, hh * tq:(hh + 1) * tq]
    ovt = ovt_ref[...]
    p_hi = psum.astype(BF16)
    r1 = psum - p_hi.astype(F32)
    p_mid = r1.astype(BF16)
    p_lo = (r1 - p_mid.astype(F32)).astype(BF16)
    p_sel = _dot(ovt, p_hi) + (_dot(ovt, p_mid) + _dot(ovt, p_lo))
    jb = lax.broadcasted_iota(jnp.int32, (LANE, tq), 0)
    cur = (t0 + lax.broadcasted_iota(jnp.int32, (1, tq), 1)) >> int(math.log2(SEL_LEN))
    valid = jb <= cur
    forced = valid & ((jb == 0) | (jb == cur) | (jb == cur - 1))
    st = jnp.where(forced, jnp.inf, jnp.where(valid, p_sel, -jnp.inf))
    n_grp = n_sel // SUBLANE
    st_ref[...] = st
    cnt_ref[...] = jnp.zeros(cnt_ref.shape, F32)
    jrow = lax.broadcasted_iota(jnp.int32, (SUBLANE, tq), 0)
    cur_max = (t0 + tq - 1) >> int(math.log2(SEL_LEN))
    for gb in range(n_grp):
        @pl.when(cur_max >= gb * SUBLANE)
        def _():
            grp = [st_ref[g * SUBLANE:(g + 1) * SUBLANE, :] for g in range(n_grp)]
            cnt = [cnt_ref[g * SUBLANE:(g + 1) * SUBLANE, :] for g in range(n_grp)]
            for rb_ in range(SUBLANE):
                rb = grp[gb][rb_:rb_ + 1, :]
                for g in range(n_grp):
                    if g < gb:
                        inc = jnp.where(rb > grp[g], 1.0, 0.0)
                    elif g > gb:
                        inc = jnp.where(rb >= grp[g], 1.0, 0.0)
                    else:
                        inc = jnp.where(jrow > rb_, jnp.where(rb >= grp[g], 1.0, 0.0), jnp.where(rb > grp[g], 1.0, 0.0))
                    cnt[g] = cnt[g] + inc
            for g in range(n_grp):
                cnt_ref[g * SUBLANE:(g + 1) * SUBLANE, :] = cnt[g]

    keep = (cnt_ref[...] < float(SEL_TOPK)) & (st_ref[0:n_sel, :] > -jnp.inf)
    bias = jnp.where(keep, 0.0, NEG_BIAS)
    if n_sel < LANE:
        bias = jnp.concatenate([bias, jnp.zeros((LANE - n_sel, tq), F32)], axis=0)
    bias_ref[...] = bias.astype(BF16)


def _cmp_attn(q_t, kc, vc, overlap_t):
    B, _, S = q_t.shape
    nc = kc.shape[2]
    n_sel = S // SEL_LEN
    tq = 256
    gw = HPG * LANE
    return pl.pallas_call(
        functools.partial(_cmp_attn_kernel, n_sel=n_sel),
        grid=(B, KV_GROUPS, S // tq),
        in_specs=[pl.BlockSpec((None, gw, tq), lambda b, g, i: (b, g, i)),
                  pl.BlockSpec((None, None, nc, LANE), lambda b, g, i: (b, g, 0, 0)),
                  pl.BlockSpec((None, None, nc, LANE), lambda b, g, i: (b, g, 0, 0)),
                  pl.BlockSpec((LANE, nc), lambda b, g, i: (0, 0))],
        out_specs=[pl.BlockSpec((None, tq, gw), lambda b, g, i: (b, i, g)),
                   pl.BlockSpec((None, None, LANE, tq), lambda b, g, i: (b, g, 0, i))],
        out_shape=[jax.ShapeDtypeStruct((B, S, WIDTH), BF16),
                   jax.ShapeDtypeStruct((B, KV_GROUPS, LANE, S), BF16)],
        scratch_shapes=[pltpu.VMEM((LANE, tq), F32), pltpu.VMEM((n_sel, tq), F32)],
        compiler_params=_cparams(("parallel", "parallel", "parallel")),
        name="cmp_attn",
    )(q_t, kc, vc, overlap_t)


FLASH_TQ = 256


def _store_heads(o_ref, out_t, g, tq):
    for hh in range(HPG):
        h = g * HPG + hh
        o_ref[:, h * LANE:(h + 1) * LANE] = out_t[:, hh * tq:(hh + 1) * tq].T.astype(BF16)


def _flash_sel_kernel(q_ref, b_ref, k_ref, vt_ref, o_ref, qs_ref, s_ref, m_ref, acc_ref, *, tk):
    i = pl.program_id(1)
    tq = q_ref.shape[1]
    cols = HPG * tq
    t0 = i * tq
    kw = qs_ref.shape[1]
    for g in range(KV_GROUPS):
        for hh in range(HPG):
            h = g * HPG + hh
            qs_ref[g, 0:LANE, hh * tq:(hh + 1) * tq] = q_ref[h * LANE:(h + 1) * LANE, :]
            qs_ref[g, LANE:2 * LANE, hh * tq:(hh + 1) * tq] = b_ref[g]
    m_ref[...] = jnp.full(m_ref.shape, -jnp.inf, F32)
    acc_ref[...] = jnp.zeros(acc_ref.shape, F32)
    qpos = t0 + (lax.broadcasted_iota(jnp.int32, (1, cols), 1) & (tq - 1))

    def scores(j, slot):
        k0 = pl.multiple_of(j * tk, tk)
        for g in range(KV_GROUPS):
            s_ref[slot, g] = _dot(k_ref[pl.ds(k0, tk), g * kw:(g + 1) * kw], qs_ref[g])

    def update(j, slot, masked):
        if masked:
            mask = j * tk + lax.broadcasted_iota(jnp.int32, (tk, 1), 0) <= qpos
        for g in range(KV_GROUPS):
            s = s_ref[slot, g]
            if masked:
                s = jnp.where(mask, s, -jnp.inf)
            m_old = m_ref[g]
            m_new = jnp.maximum(m_old, jnp.max(s, axis=0, keepdims=True))
            alpha = jnp.exp2(m_old - m_new)
            p = jnp.exp2(s - m_new)
            acc_ref[g] = alpha * acc_ref[g] + _dot(vt_ref[g, j], p.astype(BF16))
            m_ref[g] = m_new

    jd = t0 // tk
    last = jnp.maximum(jd - 1, 0)
    scores(jd, 0)
    scores(0, 1)
    update(jd, 0, True)

    def body(p, carry):
        scores(jnp.minimum(2 * p + 1, last), 0)
        update(2 * p, 1, False)
        scores(jnp.minimum(2 * p + 2, last), 1)
        update(2 * p + 1, 0, False)
        return carry

    lax.fori_loop(0, jd // 2, body, 0)

    @pl.when(jd % 2 == 1)
    def _():
        update(jd - 1, 1, False)

    for g in range(KV_GROUPS):
        _store_heads(o_ref, acc_ref[g, 0:LANE] / jnp.maximum(acc_ref[g, LANE:LANE + 1], jnp.finfo(F32).tiny), g, tq)


def _flash_sel(q_t, kaug, v_t, bias_t):
    B, _, S = q_t.shape
    tq = tk = FLASH_TQ
    assert tk == SEL_TK
    kw = kaug.shape[2] // KV_GROUPS
    cols = HPG * tq
    return pl.pallas_call(
        functools.partial(_flash_sel_kernel, tk=tk),
        grid=(B, S // tq),
        in_specs=[pl.BlockSpec((None, WIDTH, tq), lambda b, i: (b, 0, i)),
                  pl.BlockSpec((None, KV_GROUPS, LANE, tq), lambda b, i: (b, 0, 0, i)),
                  pl.BlockSpec((None, S, KV_GROUPS * kw), lambda b, i: (b, 0, 0)),
                  pl.BlockSpec((None, KV_GROUPS, S // tk, SEL_VROWS, tk), lambda b, i: (b, 0, 0, 0, 0))],
        out_specs=pl.BlockSpec((None, tq, WIDTH), lambda b, i: (b, i, 0)),
        out_shape=jax.ShapeDtypeStruct((B, S, WIDTH), BF16),
        scratch_shapes=[pltpu.VMEM((KV_GROUPS, kw, cols), BF16),
                        pltpu.VMEM((2, KV_GROUPS, tk, cols), F32),
                        pltpu.VMEM((KV_GROUPS, 1, cols), F32),
                        pltpu.VMEM((KV_GROUPS, SEL_VROWS, cols), F32)],
        compiler_params=_cparams(("parallel", "parallel")),
        name="flash_sel",
    )(q_t, bias_t, kaug, v_t)


def _flash_win_kernel(q_ref, k_ref, vt_ref, wb_ref, o_ref, s_ref, *, tk):
    i = pl.program_id(1)
    tq = q_ref.shape[1]
    cols = HPG * tq
    n_t = WINDOW // tk + 1
    jt = [jnp.maximum(i - d, 0) for d in range(n_t)]
    for g in range(KV_GROUPS):
        q_g = jnp.concatenate([q_ref[(g * HPG + hh) * LANE:(g * HPG + hh + 1) * LANE, :] for hh in range(HPG)], axis=1)
        for d in range(n_t):
            k0 = pl.multiple_of(jt[d] * tk, tk)
            s_ref[g, d] = _dot(k_ref[pl.ds(k0, tk), g * LANE:(g + 1) * LANE], q_g)
    static_bias = {0: 0, n_t - 1: 1}

    def tile(g, d):
        s = s_ref[g, d]
        return s + wb_ref[static_bias[d]] if d in static_bias else s

    for g in range(KV_GROUPS):
        m = jnp.max(tile(g, 0), axis=0, keepdims=True)
        for d in range(1, n_t):
            m = jnp.maximum(m, jnp.where(i >= d, jnp.max(tile(g, d), axis=0, keepdims=True), -jnp.inf))
        l = jnp.zeros((1, cols), F32)
        acc = jnp.zeros((LANE, cols), F32)
        for d in range(n_t):
            p = jnp.exp2(tile(g, d) - (m if d == 0 else jnp.where(i >= d, m, jnp.inf)))
            l = l + jnp.sum(p, axis=0, keepdims=True)
            acc = acc + _dot(vt_ref[g, jt[d]], p.astype(BF16))
        _store_heads(o_ref, acc / jnp.maximum(l, jnp.finfo(F32).tiny), g, tq)


def _window_bias(tq):
    r = np.arange(tq)[:, None]
    c = (np.arange(HPG * tq) % tq)[None, :]
    neg = np.float32(-np.inf)
    return jnp.asarray(np.stack([np.where(r <= c, np.float32(0), neg), np.where(c < r, np.float32(0), neg)]))


def _flash_win(q_t, k_r, v_t):
    B, _, S = q_t.shape
    tq = tk = FLASH_TQ
    assert tk == WIN_TK and WINDOW == 2 * tk
    cols = HPG * tq
    return pl.pallas_call(
        functools.partial(_flash_win_kernel, tk=tk),
        grid=(B, S // tq),
        in_specs=[pl.BlockSpec((None, WIDTH, tq), lambda b, i: (b, 0, i)),
                  pl.BlockSpec((None, S, KVW), lambda b, i: (b, 0, 0)),
                  pl.BlockSpec((None, KV_GROUPS, S // tk, LANE, tk), lambda b, i: (b, 0, 0, 0, 0)),
                  pl.BlockSpec((2, tk, cols), lambda b, i: (0, 0, 0))],
        out_specs=pl.BlockSpec((None, tq, WIDTH), lambda b, i: (b, i, 0)),
        out_shape=jax.ShapeDtypeStruct((B, S, WIDTH), BF16),
        scratch_shapes=[pltpu.VMEM((KV_GROUPS, WINDOW // tk + 1, tk, cols), F32)],
        compiler_params=_cparams(("parallel", "parallel")),
        name="flash_win",
    )(q_t, k_r, v_t, _window_bias(tq))


def _merge_kernel(ydn_ref, oc_ref, os_ref, ow_ref, small_ref, mgd_ref, mgn_ref, x_ref,
                  wdn_ref, wnsa_ref, wout_ref, g_ref, b_ref, o_ref):
    gates = _sigmoid(small_ref[...])
    cols = []
    for h in range(HEADS):
        sl = slice(h * LANE, (h + 1) * LANE)
        gc = gates[:, SMALL_GATE + h:SMALL_GATE + h + 1]
        gs = gates[:, SMALL_GATE + HEADS + h:SMALL_GATE + HEADS + h + 1]
        gw = gates[:, SMALL_GATE + 2 * HEADS + h:SMALL_GATE + 2 * HEADS + h + 1]
        cols.append((gc * oc_ref[:, sl] + gs * os_ref[:, sl] + gw * ow_ref[:, sl]).astype(BF16))
    y_nsa = jnp.concatenate(cols, axis=1)
    y = (_sigmoid(mgd_ref[...].astype(F32)) * _dot(ydn_ref[...], wdn_ref[...])
         + _sigmoid(mgn_ref[...].astype(F32)) * _dot(y_nsa, wnsa_ref[...]))
    z = DEEPNORM_ALPHA * x_ref[...] + _dot(y.astype(BF16), wout_ref[...])
    o_ref[...] = _layer_norm(z, g_ref[...], b_ref[...])


def _merge(y_dn2, oc2, os2, ow2, small2, h, x2, wdn, wnsa, wout, g, b):
    T, D = x2.shape
    tm = 512
    row = pl.BlockSpec((tm, WIDTH), lambda i: (i, 0))
    full = lambda shape: pl.BlockSpec(shape, lambda i: (0, 0))
    return pl.pallas_call(
        _merge_kernel,
        grid=(T // tm,),
        in_specs=[row, row, row, row,
                  pl.BlockSpec((tm, LANE), lambda i: (i, 0)),
                  pl.BlockSpec((tm, WIDTH), lambda i: (i, COL_MG_DN // WIDTH)),
                  pl.BlockSpec((tm, WIDTH), lambda i: (i, COL_MG_NSA // WIDTH)),
                  pl.BlockSpec((tm, D), lambda i: (i, 0)),
                  full((WIDTH, D)), full((WIDTH, D)), full((D, D)), full((1, D)), full((1, D))],
        out_specs=pl.BlockSpec((tm, D), lambda i: (i, 0)),
        out_shape=jax.ShapeDtypeStruct((T, D), F32),
        compiler_params=_cparams(("parallel",)),
        name="merge",
    )(y_dn2, oc2, os2, ow2, small2, h, h, x2, wdn, wnsa, wout, g, b)


FFN_TM = 1024


def _ffn_up_kernel(x_ref, xp_ref, wg_ref, wv_ref, cwg_ref, cwv_ref, bg_ref, bv_ref, o_ref, xs_ref, *, tiles_per_seq):
    i = pl.program_id(1)
    tm = x_ref.shape[0]
    nv = tm // SUBLANE
    n_slab = x_ref.shape[1] // LANE
    for sp in range(SUBLANE):
        for c in range(n_slab):
            xs_ref[c, pl.ds(sp, nv, stride=SUBLANE), :] = x_ref[sp * nv:(sp + 1) * nv, c * LANE:(c + 1) * LANE]
    xb = jnp.concatenate([xs_ref[c] for c in range(n_slab)], axis=1).astype(BF16)
    notfirst = (i % tiles_per_seq != 0).astype(F32)
    xpb = (xp_ref[...] * notfirst).astype(BF16)
    sub = lax.broadcasted_iota(jnp.int32, (SUBLANE, 1), 0)

    def conv(w_ref, cw_ref, b_ref):
        u = _dot(xb, w_ref[...])
        p = _dot(xpb, w_ref[...])

        def wrap(k):
            blk = u[(nv - k) * SUBLANE:(nv - k + 1) * SUBLANE, :]
            return jnp.where(sub == 0, p[SUBLANE - k:SUBLANE - k + 1, :], pltpu.roll(blk, 1, 0))

        w1, w2 = wrap(1), wrap(2)
        u1 = jnp.concatenate([w1, u[0:(nv - 1) * SUBLANE, :]], axis=0)
        u2 = jnp.concatenate([w2, w1, u[0:(nv - 2) * SUBLANE, :]], axis=0)
        return cw_ref[0:1, :] * u2 + cw_ref[1:2, :] * u1 + cw_ref[2:3, :] * u + b_ref[...]

    o_ref[...] = (_silu(conv(wg_ref, cwg_ref, bg_ref)) * conv(wv_ref, cwv_ref, bv_ref)).astype(BF16)


def _ffn_up(x1, w_up, conv_w, conv_b, seq):
    T, D = x1.shape
    tm = FFN_TM
    nj = 2
    tn = D_FF // nj
    return pl.pallas_call(
        functools.partial(_ffn_up_kernel, tiles_per_seq=seq // tm),
        grid=(nj, T // tm),
        in_specs=[pl.BlockSpec((tm, D), lambda j, i: (i, 0)),
                  pl.BlockSpec((SUBLANE, D), lambda j, i: (jnp.maximum(i * (tm // SUBLANE) - 1, 0), 0)),
                  pl.BlockSpec((D, tn), lambda j, i: (0, j)),
                  pl.BlockSpec((D, tn), lambda j, i: (0, j + nj)),
                  pl.BlockSpec((FFN_CONV, tn), lambda j, i: (0, j)),
                  pl.BlockSpec((FFN_CONV, tn), lambda j, i: (0, j + nj)),
                  pl.BlockSpec((1, tn), lambda j, i: (0, j)),
                  pl.BlockSpec((1, tn), lambda j, i: (0, j + nj))],
        out_specs=pl.BlockSpec((tm, tn), lambda j, i: (i, j)),
        out_shape=jax.ShapeDtypeStruct((T, D_FF), BF16),
        scratch_shapes=[pltpu.VMEM((D // LANE, tm, LANE), F32)],
        compiler_params=_cparams(("parallel", "parallel")),
        name="ffn_up",
    )(x1, x1, w_up, w_up, conv_w, conv_w, conv_b, conv_b)


def _ffn_down_kernel(a_ref, w_ref, x_ref, g_ref, b_ref, o_ref, fs_ref):
    tm, d = x_ref.shape
    nv = tm // SUBLANE
    n_slab = d // LANE
    f = _dot(a_ref[...], w_ref[...])
    for c in range(n_slab):
        fs_ref[c] = f[:, c * LANE:(c + 1) * LANE]
    for sp in range(SUBLANE):
        rows = slice(sp * nv, (sp + 1) * nv)
        f_nat = jnp.concatenate([fs_ref[c, pl.ds(sp, nv, stride=SUBLANE), :] for c in range(n_slab)], axis=1)
        z = DEEPNORM_ALPHA * x_ref[rows, :] + f_nat
        o_ref[rows, :] = _layer_norm(z, g_ref[...], b_ref[...])


def _ffn_down(a, w_down, x1, g, b):
    T, D = x1.shape
    tm = FFN_TM
    full = lambda shape: pl.BlockSpec(shape, lambda i: (0, 0))
    return pl.pallas_call(
        _ffn_down_kernel,
        grid=(T // tm,),
        in_specs=[pl.BlockSpec((tm, D_FF), lambda i: (i, 0)), full((D_FF, D)),
                  pl.BlockSpec((tm, D), lambda i: (i, 0)), full((1, D)), full((1, D))],
        out_specs=pl.BlockSpec((tm, D), lambda i: (i, 0)),
        out_shape=jax.ShapeDtypeStruct((T, D), F32),
        scratch_shapes=[pltpu.VMEM((D // LANE, tm, LANE), F32)],
        compiler_params=_cparams(("parallel",)),
        name="ffn_down",
    )(a, w_down, x1, g, b)


def _split_w_in_kernel(w_ref, a_ref, b_ref, c_ref):
    o_dz = 3 * WIDTH
    o_db = 4 * WIDTH
    o_nq = o_db + 2 * HEADS
    o_kc = o_nq + WIDTH
    o_ksl = o_kc + 2 * KVW
    o_gate = o_ksl + 4 * KVW
    o_mg = o_gate + 3 * HEADS
    rows = w_ref.shape[0]
    pad = jnp.zeros((rows, LANE - 5 * HEADS), F32)
    a_ref[:, 0:o_dz] = w_ref[:, 0:o_dz].astype(BF16)
    a_ref[:, o_dz:NA] = jnp.concatenate([w_ref[:, o_db:o_nq], w_ref[:, o_gate:o_mg], pad], axis=1).astype(BF16)
    b_ref[:, 0:WIDTH] = w_ref[:, o_nq:o_kc].astype(BF16)
    b_ref[:, WIDTH:NB] = w_ref[:, o_ksl:o_gate].astype(BF16)
    c_ref[:, 0:WIDTH] = w_ref[:, o_dz:o_db].astype(BF16)
    c_ref[:, WIDTH:3 * WIDTH] = w_ref[:, o_mg:o_mg + 2 * WIDTH].astype(BF16)
    c_ref[:, 3 * WIDTH:NC_COLS] = w_ref[:, o_kc:o_ksl].astype(BF16)


def _split_w_in(w, layer):
    _, d, n_in = w.shape
    tr = 128
    return pl.pallas_call(
        _split_w_in_kernel,
        grid=(d // tr,),
        in_specs=[pl.BlockSpec((None, tr, n_in), lambda i: (layer, i, 0))],
        out_specs=[pl.BlockSpec((tr, NA), lambda i: (i, 0)), pl.BlockSpec((tr, NB), lambda i: (i, 0)),
                   pl.BlockSpec((tr, NC_COLS), lambda i: (i, 0))],
        out_shape=[jax.ShapeDtypeStruct((d, NA), BF16), jax.ShapeDtypeStruct((d, NB), BF16),
                   jax.ShapeDtypeStruct((d, NC_COLS), BF16)],
        compiler_params=_cparams(("parallel",)),
        name="split_w_in",
    )(w)


def _rope_tables(pos):
    pos = np.asarray(pos)
    inv_freq = np.float32(ROPE_THETA) ** (-(np.arange(ROPE_HALF, dtype=np.float32) / np.float32(ROPE_HALF)))
    ang = pos.astype(np.float32)[:, None] * inv_freq.astype(np.float32)
    cos = np.cos(ang.astype(np.float64)).astype(np.float32)
    sin = np.sin(ang.astype(np.float64)).astype(np.float32)
    p = pos.shape[0]
    cosf = np.concatenate([cos, cos, np.ones((p, LANE - ROPE_DIM), np.float32)], axis=1)
    sinf = np.concatenate([-sin, sin, np.zeros((p, LANE - ROPE_DIM), np.float32)], axis=1)
    return jnp.asarray(cosf), jnp.asarray(sinf)


def _overlap_table(nc, n_sel):
    n = np.arange(nc)[None, :]
    j = np.arange(LANE)[:, None]
    ov = (n * CMP_STRIDE <= j * SEL_LEN + SEL_LEN - 1) & (n * CMP_STRIDE + CMP_LEN - 1 >= j * SEL_LEN)
    ov = ov & (j < n_sel) & (n < nc - 1)
    return jnp.asarray(ov.astype(np.float32)).astype(BF16)


def _lane_row(vals, offset):
    return jnp.zeros((1, LANE), F32).at[0, offset:offset + vals.shape[0]].set(vals.astype(F32))


def kernel(x, w_in, dn_conv_w, dn_a_log, dn_dt_bias, dn_norm_w, cmp_pos_k, cmp_pos_v, cmp_k_w1, cmp_k_w2,
           cmp_v_w1, cmp_v_w2, w_branch_dn, w_branch_nsa, w_out, ln1_g, ln1_b, ffn_w_up, ffn_conv_w,
           ffn_conv_b, ffn_w_down, ln2_g, ln2_b):
    B, S, D = x.shape
    T = B * S
    nc = S // CMP_STRIDE
    n_sel = S // SEL_LEN
    assert D == WIDTH and S % 512 == 0 and n_sel <= LANE and n_sel % SUBLANE == 0
    cos_t, sin_t = _rope_tables(np.arange(S))
    cos_c, sin_c = _rope_tables(np.arange(nc) * CMP_STRIDE + (CMP_LEN - 1))
    overlap = _overlap_table(nc, n_sel)
    x2 = x.reshape(T, D)
    for layer in range(w_in.shape[0]):
        w_a, w_b, w_c = _split_w_in(w_in, layer)
        x3 = x2.reshape(B, S, D)
        h3, kv_cmp = _proj_rest(x3, w_c)

        q, k, v, bg, small = _proj_gdn(x3, w_a, dn_conv_w[layer], _lane_row(dn_a_log[layer], SMALL_DECAY),
                                       _lane_row(dn_dt_bias[layer], SMALL_DECAY))
        y_dn = _gdn(q, k, v, bg, h3, dn_norm_w[layer].reshape(1, LANE))

        q_t, kaug, vsl_t, kwn_r, vwn_t = _proj_nsa(x3, w_b, cos_t, sin_t)
        kc, vc = _compress(kv_cmp, cmp_pos_k[layer], cmp_pos_v[layer],
                           cmp_k_w1[layer].astype(BF16), cmp_k_w2[layer].astype(BF16),
                           cmp_v_w1[layer].astype(BF16), cmp_v_w2[layer].astype(BF16), cos_c, sin_c)
        o_cmp, bias = _cmp_attn(q_t, kc, vc, overlap)
        o_sel = _flash_sel(q_t, kaug, vsl_t, bias)
        o_win = _flash_win(q_t, kwn_r, vwn_t)

        x1 = _merge(y_dn.reshape(T, WIDTH), o_cmp.reshape(T, WIDTH), o_sel.reshape(T, WIDTH),
                    o_win.reshape(T, WIDTH), small.reshape(T, LANE), h3.reshape(T, COL_KC), x2,
                    w_branch_dn[layer].astype(BF16), w_branch_nsa[layer].astype(BF16),
                    w_out[layer].astype(BF16), ln1_g[layer].reshape(1, D), ln1_b[layer].reshape(1, D))
        a = _ffn_up(x1, ffn_w_up[layer].astype(BF16), ffn_conv_w[layer],
                    ffn_conv_b[layer].reshape(1, 2 * D_FF), S)
        x2 = _ffn_down(a, ffn_w_down[layer].astype(BF16), x1, ln2_g[layer].reshape(1, D), ln2_b[layer].reshape(1, D))
    return x2.reshape(B, S, D)
```

```python
import functools
import math

import numpy as np
import jax
import jax.numpy as jnp
from jax import lax
from jax.experimental import pallas as pl
from jax.experimental.pallas import tpu as pltpu

F32 = jnp.float32
BF16 = jnp.bfloat16
HI = lax.Precision.HIGHEST

HEADS = 8
HEAD_DIM = 128
KV_GROUPS = 2
HPG = HEADS // KV_GROUPS
DN_CONV = 4
DN_CHUNK = 64
CMP_LEN = 32
CMP_STRIDE = 16
CMP_HIDDEN = 256
SEL_LEN = 64
SEL_TOPK = 16
WINDOW = 512
ROPE_THETA = 500000.0
ROPE_DIM = HEAD_DIM // 4
ROPE_HALF = ROPE_DIM // 2
D_FF = 2816
FFN_CONV = 3
LN_EPS = 1e-5
RMS_EPS = 1e-6
DEEPNORM_ALPHA = 2.0 ** 0.25
ATT_SCALE = HEAD_DIM ** -0.5

LANE = 128
SUBLANE = 8
VMEM_LIMIT = 56 * 1024 * 1024

WIDTH = HEADS * HEAD_DIM
KVW = KV_GROUPS * HEAD_DIM
NA = 3 * WIDTH + LANE
SMALL_BETA = 0
SMALL_DECAY = HEADS
SMALL_GATE = 2 * HEADS
NB = WIDTH + 4 * KVW
NC_COLS = 3 * WIDTH + 2 * KVW
COL_DZ = 0
COL_MG_DN = WIDTH
COL_MG_NSA = 2 * WIDTH
COL_KC = 3 * WIDTH
COL_VC = COL_KC + KVW

NEG_BIAS = -(2.0 ** 100)


def _cparams(sem):
    return pltpu.CompilerParams(dimension_semantics=sem, vmem_limit_bytes=VMEM_LIMIT)


def _sigmoid(x):
    return 1.0 / (1.0 + jnp.exp(-x))


def _silu(x):
    return x * _sigmoid(x)


def _dot(a, b, precision=None):
    return jnp.dot(a, b, preferred_element_type=F32, precision=precision)


def _dot_nt(a, b):
    return lax.dot_general(a, b, (((1,), (1,)), ((), ())), preferred_element_type=F32)


def _dot_tn(a, b):
    return lax.dot_general(a, b, (((0,), (0,)), ((), ())), preferred_element_type=F32)


def _layer_norm(z, g, b):
    mu = jnp.mean(z, -1, keepdims=True)
    zc = z - mu
    var = jnp.mean(zc * zc, -1, keepdims=True)
    return zc * lax.rsqrt(var + LN_EPS) * g + b


PROJ_TM = 512


def _proj_rest_kernel(x_ref, w_ref, g_ref, c_ref):
    h = _dot(x_ref[...].astype(BF16), w_ref[...])
    g_ref[...] = h[:, 0:COL_KC].astype(BF16)
    c_ref[...] = h[:, COL_KC:NC_COLS]


def _proj_rest(x3, w):
    B, S, D = x3.shape
    tm = PROJ_TM
    return pl.pallas_call(
        _proj_rest_kernel,
        grid=(B, S // tm),
        in_specs=[pl.BlockSpec((None, tm, D), lambda b, i: (b, i, 0)),
                  pl.BlockSpec((D, NC_COLS), lambda b, i: (0, 0))],
        out_specs=[pl.BlockSpec((None, tm, COL_KC), lambda b, i: (b, i, 0)),
                   pl.BlockSpec((None, tm, NC_COLS - COL_KC), lambda b, i: (b, i, 0))],
        out_shape=[jax.ShapeDtypeStruct((B, S, COL_KC), BF16),
                   jax.ShapeDtypeStruct((B, S, NC_COLS - COL_KC), F32)],
        compiler_params=_cparams(("parallel", "parallel")),
        name="proj_rest",
    )(x3, w)


def _proj_gdn_kernel(x_ref, w_ref, cw_ref, alog_ref, dtb_ref, q_ref, k_ref, v_ref, bg_ref, small_ref,
                     xs_ref, ys_ref, carry_ref):
    i = pl.program_id(1)
    ts = x_ref.shape[0]
    nv = ts // SUBLANE
    wq = 3 * WIDTH
    n_slab = x_ref.shape[1] // LANE

    @pl.when(i == 0)
    def _():
        carry_ref[...] = jnp.zeros(carry_ref.shape, F32)

    for sp in range(SUBLANE):
        for c in range(n_slab):
            xs_ref[c, pl.ds(sp, nv, stride=SUBLANE), :] = x_ref[sp * nv:(sp + 1) * nv, c * LANE:(c + 1) * LANE]
    xb = jnp.concatenate([xs_ref[c] for c in range(n_slab)], axis=1).astype(BF16)
    h = _dot(xb, w_ref[...])
    sub = lax.broadcasted_iota(jnp.int32, (SUBLANE, 1), 0)

    def token_order(c, dst_ref, dst_sl, dtype):
        for sp in range(SUBLANE):
            dst_ref[sp * nv:(sp + 1) * nv, dst_sl] = ys_ref[c, pl.ds(sp, nv, stride=SUBLANE), :].astype(dtype)

    for c in range(3 * HEADS):
        sl = slice(c * LANE, (c + 1) * LANE)
        col = h[:, sl]
        wrap = [jnp.where(sub == 0, carry_ref[SUBLANE - k:SUBLANE - k + 1, sl],
                          pltpu.roll(col[(nv - k) * SUBLANE:(nv - k + 1) * SUBLANE, :], 1, 0))
                for k in range(1, DN_CONV)]
        acc = cw_ref[DN_CONV - 1:DN_CONV, sl] * col
        for s in range(1, DN_CONV):
            shifted = jnp.concatenate(wrap[0:s][::-1] + [col[0:(nv - s) * SUBLANE, :]], axis=0)
            acc = acc + cw_ref[DN_CONV - 1 - s:DN_CONV - s, sl] * shifted
        y = _silu(acc)
        if c < 2 * HEADS:
            y = y * lax.rsqrt(jnp.sum(y * y, -1, keepdims=True) + RMS_EPS)
        if c < HEADS:
            y = y * ATT_SCALE
        ys_ref[c] = y
        dst = (q_ref, k_ref, v_ref)[c // HEADS]
        token_order(c, dst, slice((c % HEADS) * LANE, (c % HEADS + 1) * LANE), BF16)
    for k in range(1, DN_CONV):
        carry_ref[SUBLANE - k:SUBLANE - k + 1, :] = h[(nv - k + 1) * SUBLANE - 1:(nv - k + 1) * SUBLANE, 0:wq]
    small = h[:, wq:wq + LANE]
    beta = _sigmoid(small)
    xs = small + dtb_ref[...]
    softplus = jnp.maximum(xs, 0.0) + jnp.log1p(jnp.exp(-jnp.abs(xs)))
    g = -jnp.exp(alog_ref[...]) * softplus
    lane = lax.broadcasted_iota(jnp.int32, small.shape, 1)
    ys_ref[3 * HEADS] = small
    token_order(3 * HEADS, small_ref, slice(0, LANE), F32)
    ys_ref[3 * HEADS + 1] = jnp.where(lane < SMALL_DECAY, beta, g)
    token_order(3 * HEADS + 1, bg_ref, slice(0, LANE), F32)


def _proj_gdn(x3, w_a, conv_w, alog_row, dtb_row):
    B, S, D = x3.shape
    ts = PROJ_TM
    wq = 3 * WIDTH
    out_sd = jax.ShapeDtypeStruct((B, S, WIDTH), BF16)
    small_sd = jax.ShapeDtypeStruct((B, S, LANE), F32)
    wide = pl.BlockSpec((None, ts, WIDTH), lambda b, i: (b, i, 0))
    narrow = pl.BlockSpec((None, ts, LANE), lambda b, i: (b, i, 0))
    return pl.pallas_call(
        _proj_gdn_kernel,
        grid=(B, S // ts),
        in_specs=[
            pl.BlockSpec((None, ts, D), lambda b, i: (b, i, 0)),
            pl.BlockSpec((D, NA), lambda b, i: (0, 0)),
            pl.BlockSpec((DN_CONV, wq), lambda b, i: (0, 0)),
            pl.BlockSpec((1, LANE), lambda b, i: (0, 0)),
            pl.BlockSpec((1, LANE), lambda b, i: (0, 0)),
        ],
        out_specs=[wide, wide, wide, narrow, narrow],
        out_shape=[out_sd, out_sd, out_sd, small_sd, small_sd],
        scratch_shapes=[pltpu.VMEM((D // LANE, ts, LANE), F32),
                        pltpu.VMEM((3 * HEADS + 2, ts, LANE), F32),
                        pltpu.VMEM((SUBLANE, wq), F32)],
        compiler_params=_cparams(("parallel", "arbitrary")),
        name="proj_gdn",
    )(x3, w_a, conv_w, alog_row, dtb_row)


GDN_CHUNKS_PER_STEP = 4
GDN_BATCH_PER_STEP = 2


GDN_QUAD = 4


def _gdn_kernel(q_ref, k_ref, v_ref, bg_ref, z_ref, nw_ref, y_ref, state_ref):
    n = pl.program_id(1)

    @pl.when(n == 0)
    def _():
        state_ref[...] = jnp.zeros(state_ref.shape, F32)

    C = DN_CHUNK
    QW = GDN_QUAD * C
    QL = GDN_QUAD * LANE
    shift_c = int(math.log2(C))
    r64 = lax.broadcasted_iota(jnp.int32, (C, C), 0)
    c64 = lax.broadcasted_iota(jnp.int32, (C, C), 1)
    tril = jnp.where(r64 >= c64, 1.0, 0.0).astype(F32)
    triu = jnp.where(r64 <= c64, 1.0, 0.0).astype(F32)
    rr = lax.broadcasted_iota(jnp.int32, (C, QW), 0)
    ll = lax.broadcasted_iota(jnp.int32, (C, QW), 1)
    jj = ll & (C - 1)
    hb = ll >> shift_c
    incl_t = rr >= jj
    strict_t = rr > jj
    eye_t = jnp.where(rr == jj, 1.0, 0.0).astype(F32)
    bd_c = jnp.where((lax.broadcasted_iota(jnp.int32, (QW, QW), 0) >> shift_c)
                     == (lax.broadcasted_iota(jnp.int32, (QW, QW), 1) >> shift_c), 1.0, 0.0).astype(BF16)
    bd_l = jnp.where((lax.broadcasted_iota(jnp.int32, (QW, QL), 0) >> shift_c)
                     == (lax.broadcasted_iota(jnp.int32, (QW, QL), 1) >> int(math.log2(LANE))), 1.0, 0.0).astype(BF16)
    zeros_rhs = jnp.zeros((C, 2 * LANE), BF16)
    nw = nw_ref[...]
    nb = q_ref.shape[0]
    quads = [(bb, c, a) for bb in range(nb) for c in range(GDN_CHUNKS_PER_STEP) for a in range(HEADS // GDN_QUAD)]
    gc_cols, gc_rows, bgs = {}, {}, {}
    for bb in range(nb):
        bg = bg_ref[bb]
        bg_t = bg.T
        bgs[bb] = bg
        for c in range(GDN_CHUNKS_PER_STEP):
            rs = slice(c * C, (c + 1) * C)
            gc_cols[bb, c] = _dot(tril, bg[rs, :], HI)
            gc_rows[bb, c] = _dot(bg_t[SMALL_DECAY:SMALL_DECAY + HEADS, rs], triu, HI)
    decay, lower, qk, rhs_bd, qd, kd, eglast = {}, {}, {}, {}, {}, {}, {}
    for u_ in quads:
        bb, c, a = u_
        rs = slice(c * C, (c + 1) * C)
        h0 = a * GDN_QUAD
        gcols = [gc_cols[bb, c][:, SMALL_DECAY + h0 + hh:SMALL_DECAY + h0 + hh + 1] for hh in range(GDN_QUAD)]
        gcol_cat = jnp.broadcast_to(gcols[0], (C, QW))
        for hh in range(1, GDN_QUAD):
            gcol_cat = jnp.where(hb == hh, gcols[hh], gcol_cat)
        grow_cat = jnp.concatenate([gc_rows[bb, c][h0 + hh:h0 + hh + 1, :] for hh in range(GDN_QUAD)], axis=1)
        decay[u_] = jnp.exp(jnp.where(incl_t, gcol_cat - grow_cat, -jnp.inf))
        kq = k_ref[bb, rs, a * QL:(a + 1) * QL].astype(F32)
        qq = q_ref[bb, rs, a * QL:(a + 1) * QL].astype(F32)
        kbs, rows_bd = [], []
        for hh in range(GDN_QUAD):
            h = h0 + hh
            ls = slice(hh * LANE, (hh + 1) * LANE)
            beta = bgs[bb][rs, SMALL_BETA + h:SMALL_BETA + h + 1]
            eg = jnp.exp(gcols[hh])
            glast = gcols[hh][C - 1:C, :]
            kb = kq[:, ls] * beta
            vb = v_ref[bb, rs, h * LANE:(h + 1) * LANE].astype(F32) * beta
            kbs.append(kb)
            rhs_h = jnp.concatenate([vb, kb * eg], axis=1).astype(BF16)
            rows_bd.append(jnp.concatenate([zeros_rhs] * hh + [rhs_h] + [zeros_rhs] * (GDN_QUAD - 1 - hh), axis=1))
            qd[bb, c, h] = (qq[:, ls] * eg).astype(BF16)
            kd[bb, c, h] = (kq[:, ls] * jnp.exp(glast - gcols[hh])).astype(BF16)
            eglast[bb, c, h] = jnp.exp(glast)
        rhs_bd[u_] = jnp.concatenate(rows_bd, axis=0)
        k_bd = jnp.concatenate([kq.astype(BF16)] * GDN_QUAD, axis=0) * bd_l
        lhs = jnp.concatenate([jnp.concatenate(kbs, axis=1), qq], axis=0).astype(BF16)
        aq = _dot_nt(lhs, k_bd)
        lower[u_] = jnp.where(strict_t, aq[0:C] * decay[u_], 0.0)
        qk[u_] = (aq[C:2 * C] * decay[u_]).astype(BF16)

    def block_diag(y_b):
        return jnp.concatenate([y_b] * GDN_QUAD, axis=0) * bd_c

    tinv = {u_: eye_t - lower[u_] for u_ in quads}
    y_b = {u_: lower[u_].astype(BF16) for u_ in quads}
    ypow = {u_: _dot(y_b[u_], block_diag(y_b[u_])) for u_ in quads}
    for it in range(5):
        for u_ in quads:
            yb = ypow[u_].astype(BF16)
            if it < 4:
                r = _dot(jnp.concatenate([tinv[u_].astype(BF16), yb], axis=0), block_diag(yb))
                tinv[u_] = tinv[u_] + r[0:C]
                ypow[u_] = r[C:2 * C]
            else:
                tinv[u_] = tinv[u_] + _dot(tinv[u_].astype(BF16), block_diag(yb))
    uw = {u_: _dot(tinv[u_].astype(BF16), rhs_bd[u_]) for u_ in quads}
    zeros_v = jnp.zeros((C, LANE), BF16)
    chains = [(bb, h) for bb in range(nb) for h in range(HEADS)]
    for c in range(GDN_CHUNKS_PER_STEP):
        rs = slice(c * C, (c + 1) * C)
        states = {s_: state_ref[s_[0] * HEADS + s_[1]] for s_ in chains}
        ws_qs = {}
        for s_ in chains:
            bb, h = s_
            a, hh = divmod(h, GDN_QUAD)
            w_h = uw[bb, c, a][:, hh * 2 * LANE + LANE:(hh + 1) * 2 * LANE]
            ws_qs[s_] = _dot(jnp.concatenate([w_h.astype(BF16), qd[bb, c, h]], axis=0), states[s_].astype(BF16))
        v_new = {}
        for s_ in chains:
            bb, h = s_
            a, hh = divmod(h, GDN_QUAD)
            v_new[s_] = (uw[bb, c, a][:, hh * 2 * LANE:hh * 2 * LANE + LANE] - ws_qs[s_][0:C]).astype(BF16)
        outs = {}
        for s_ in chains:
            bb, h = s_
            a, hh = divmod(h, GDN_QUAD)
            v_pad = jnp.concatenate([zeros_v] * hh + [v_new[s_]] + [zeros_v] * (GDN_QUAD - 1 - hh), axis=0)
            outs[s_] = ws_qs[s_][C:2 * C] + _dot(qk[bb, c, a], v_pad)
            state_ref[bb * HEADS + h] = states[s_] * eglast[bb, c, h] + _dot_tn(kd[bb, c, h], v_new[s_])
        for s_ in chains:
            bb, h = s_
            ls = slice(h * LANE, (h + 1) * LANE)
            o = outs[s_]
            o = o * lax.rsqrt(jnp.mean(o * o, -1, keepdims=True) + RMS_EPS)
            y_ref[bb, rs, ls] = (o * nw * _silu(z_ref[bb, rs, ls].astype(F32))).astype(BF16)


def _gdn(q, k, v, bg, h3, norm_w):
    B, S, _ = q.shape
    rows = GDN_CHUNKS_PER_STEP * DN_CHUNK
    nb = GDN_BATCH_PER_STEP if B % GDN_BATCH_PER_STEP == 0 else 1
    tile = pl.BlockSpec((nb, rows, WIDTH), lambda b, n: (b, n, 0))
    return pl.pallas_call(
        _gdn_kernel,
        grid=(B // nb, S // rows),
        in_specs=[tile, tile, tile,
                  pl.BlockSpec((nb, rows, LANE), lambda b, n: (b, n, 0)),
                  pl.BlockSpec((nb, rows, WIDTH), lambda b, n: (b, n, COL_DZ // WIDTH)),
                  pl.BlockSpec((1, LANE), lambda b, n: (0, 0))],
        out_specs=tile,
        out_shape=jax.ShapeDtypeStruct((B, S, WIDTH), BF16),
        scratch_shapes=[pltpu.VMEM((nb * HEADS, HEAD_DIM, HEAD_DIM), F32)],
        compiler_params=_cparams(("parallel", "arbitrary")),
        name="gdn",
    )(q, k, v, bg, h3, norm_w)


def _rope(t, cosf, sinf, lo):
    partner = jnp.where(lo, pltpu.roll(t, LANE - ROPE_HALF, 1), pltpu.roll(t, ROPE_HALF, 1))
    return t * cosf + partner * sinf


SEL_TK = 256
WIN_TK = 256
SEL_VROWS = LANE + 16
LOG2E = math.log2(math.e)


def _proj_nsa_kernel(x_ref, w_ref, cos_ref, sin_ref, qt_o, kaug_o, vslt_o, kwn_o, vwnt_o):
    i = pl.program_id(1)
    ts = x_ref.shape[0]
    h = _dot(x_ref[...].astype(BF16), w_ref[...])
    c_ksl, c_vsl, c_kwn, c_vwn = WIDTH, WIDTH + KVW, WIDTH + 2 * KVW, WIDTH + 3 * KVW
    cosf = cos_ref[...]
    sinf = sin_ref[...]
    lane = lax.broadcasted_iota(jnp.int32, cosf.shape, 1)
    lo = lane < ROPE_HALF
    for hd in range(HEADS):
        sl = slice(hd * LANE, (hd + 1) * LANE)
        qt_o[sl, :] = (_rope(h[:, sl], cosf, sinf, lo) * (ATT_SCALE * LOG2E)).T.astype(BF16)
    tpos = i * ts + lax.broadcasted_iota(jnp.int32, cosf.shape, 0)
    onehot = jnp.where((tpos >> int(math.log2(SEL_LEN))) == lane, 1.0, 0.0).astype(BF16)
    for g in range(KV_GROUPS):
        sl = slice(g * LANE, (g + 1) * LANE)
        off = g * LANE
        kaug_o[:, 2 * g * LANE:(2 * g + 1) * LANE] = _rope(h[:, c_ksl + off:c_ksl + off + LANE], cosf, sinf, lo).astype(BF16)
        kaug_o[:, (2 * g + 1) * LANE:(2 * g + 2) * LANE] = onehot
        kwn_o[:, sl] = _rope(h[:, c_kwn + off:c_kwn + off + LANE], cosf, sinf, lo).astype(BF16)
        vs_t = h[:, c_vsl + off:c_vsl + off + LANE].T.astype(BF16)
        for c in range(ts // SEL_TK):
            ones_rows = jnp.where(lax.broadcasted_iota(jnp.int32, (SEL_VROWS - LANE, SEL_TK), 0) == 0, 1.0, 0.0)
            vslt_o[g, c] = jnp.concatenate([vs_t[:, c * SEL_TK:(c + 1) * SEL_TK], ones_rows.astype(BF16)], axis=0)
        vw_t = h[:, c_vwn + off:c_vwn + off + LANE].T.astype(BF16)
        for c in range(ts // WIN_TK):
            vwnt_o[g, c] = vw_t[:, c * WIN_TK:(c + 1) * WIN_TK]


def _proj_nsa(x3, w_b, cos_tab, sin_tab):
    B, S, D = x3.shape
    ts = PROJ_TM
    tab = pl.BlockSpec((ts, LANE), lambda b, i: (i, 0))
    return pl.pallas_call(
        _proj_nsa_kernel,
        grid=(B, S // ts),
        in_specs=[pl.BlockSpec((None, ts, D), lambda b, i: (b, i, 0)),
                  pl.BlockSpec((D, NB), lambda b, i: (0, 0)), tab, tab],
        out_specs=[pl.BlockSpec((None, WIDTH, ts), lambda b, i: (b, 0, i)),
                   pl.BlockSpec((None, ts, 2 * KVW), lambda b, i: (b, i, 0)),
                   pl.BlockSpec((None, KV_GROUPS, ts // SEL_TK, SEL_VROWS, SEL_TK), lambda b, i: (b, 0, i, 0, 0)),
                   pl.BlockSpec((None, ts, KVW), lambda b, i: (b, i, 0)),
                   pl.BlockSpec((None, KV_GROUPS, ts // WIN_TK, LANE, WIN_TK), lambda b, i: (b, 0, i, 0, 0))],
        out_shape=[jax.ShapeDtypeStruct((B, WIDTH, S), BF16),
                   jax.ShapeDtypeStruct((B, S, 2 * KVW), BF16),
                   jax.ShapeDtypeStruct((B, KV_GROUPS, S // SEL_TK, SEL_VROWS, SEL_TK), BF16),
                   jax.ShapeDtypeStruct((B, S, KVW), BF16),
                   jax.ShapeDtypeStruct((B, KV_GROUPS, S // WIN_TK, LANE, WIN_TK), BF16)],
        compiler_params=_cparams(("parallel", "parallel")),
        name="proj_nsa",
    )(x3, w_b, cos_tab, sin_tab)


def _compress_one(t_ref, pe_ref, w1_ref, w2_ref, nc):
    half = CMP_LEN // 2
    y1 = jnp.zeros((nc, CMP_HIDDEN), F32)
    y2 = jnp.zeros((nc, CMP_HIDDEN), F32)
    for l in range(half):
        x = t_ref[pl.ds(l, nc, stride=CMP_STRIDE), :]
        y1 = y1 + _dot((x + pe_ref[l:l + 1, :]).astype(BF16), w1_ref[l * LANE:(l + 1) * LANE, :])
        y2 = y2 + _dot((x + pe_ref[half + l:half + l + 1, :]).astype(BF16),
                       w1_ref[(half + l) * LANE:(half + l + 1) * LANE, :])
    y = y1 + pltpu.roll(y2, nc - 1, 0)
    return _dot(_silu(y).astype(BF16), w2_ref[...])


def _compress_kernel(kc_ref, vc_ref, pek_ref, pev_ref, w1k_ref, w2k_ref, w1v_ref, w2v_ref,
                     cos_ref, sin_ref, kc_o, vc_o):
    nc = kc_o.shape[0]
    kc = _compress_one(kc_ref, pek_ref, w1k_ref, w2k_ref, nc)
    lo = lax.broadcasted_iota(jnp.int32, kc.shape, 1) < ROPE_HALF
    kc_o[...] = _rope(kc, cos_ref[...], sin_ref[...], lo).astype(BF16)
    vc_o[...] = _compress_one(vc_ref, pev_ref, w1v_ref, w2v_ref, nc).astype(BF16)


def _compress(h3, pek, pev, w1k, w2k, w1v, w2v, cos_c, sin_c):
    B, S, _ = h3.shape
    nc = S // CMP_STRIDE
    full = lambda shape: pl.BlockSpec(shape, lambda b, g: tuple(0 for _ in shape))
    out_spec = pl.BlockSpec((None, None, nc, LANE), lambda b, g: (b, g, 0, 0))
    out_sd = jax.ShapeDtypeStruct((B, KV_GROUPS, nc, LANE), BF16)
    cin = CMP_LEN * HEAD_DIM
    return pl.pallas_call(
        _compress_kernel,
        grid=(B, KV_GROUPS),
        in_specs=[pl.BlockSpec((None, S, LANE), lambda b, g: (b, 0, g)),
                  pl.BlockSpec((None, S, LANE), lambda b, g: (b, 0, KV_GROUPS + g)),
                  full((CMP_LEN, LANE)), full((CMP_LEN, LANE)),
                  full((cin, CMP_HIDDEN)), full((CMP_HIDDEN, LANE)),
                  full((cin, CMP_HIDDEN)), full((CMP_HIDDEN, LANE)),
                  full((nc, LANE)), full((nc, LANE))],
        out_specs=[out_spec, out_spec],
        out_shape=[out_sd, out_sd],
        compiler_params=_cparams(("parallel", "parallel")),
        name="compress",
    )(h3, h3, pek, pev, w1k, w2k, w1v, w2v, cos_c, sin_c)


def _cmp_attn_kernel(q_ref, kc_ref, vc_ref, ovt_ref, o_ref, bias_ref, st_ref, cnt_ref, *, n_sel):
    i = pl.program_id(2)
    tq = q_ref.shape[1]
    cols = HPG * tq
    nc = kc_ref.shape[0]
    t0 = i * tq
    q_g = jnp.concatenate([q_ref[hh * LANE:(hh + 1) * LANE, :] for hh in range(HPG)], axis=1)
    tpos = t0 + (lax.broadcasted_iota(jnp.int32, (1, cols), 1) & (tq - 1))
    cmp_end = lax.broadcasted_iota(jnp.int32, (nc, 1), 0) * CMP_STRIDE + (CMP_LEN - 1)
    s = jnp.where(cmp_end <= tpos, _dot(kc_ref[...], q_g), -jnp.inf)
    m = jnp.max(s, axis=0, keepdims=True)
    m = jnp.where(m > -jnp.inf, m, 0.0)
    p = jnp.exp2(s - m)
    p = p / jnp.maximum(jnp.sum(p, axis=0, keepdims=True), jnp.finfo(F32).tiny)
    vct = vc_ref[...].astype(F32).T.astype(BF16)
    _store_heads(o_ref, _dot(vct, p.astype(BF16)), 0, tq)
    psum = p[:, 0:tq]
    for hh in range(1, HPG):
        psum = psum + p[:, hh * tq:(hh + 1) * tq]
    ovt = ovt_ref[...]
    p_hi = psum.astype(BF16)
    r1 = psum - p_hi.astype(F32)
    p_mid = r1.astype(BF16)
    p_lo = (r1 - p_mid.astype(F32)).astype(BF16)
    p_sel = _dot(ovt, p_hi) + (_dot(ovt, p_mid) + _dot(ovt, p_lo))
    jb = lax.broadcasted_iota(jnp.int32, (LANE, tq), 0)
    cur = (t0 + lax.broadcasted_iota(jnp.int32, (1, tq), 1)) >> int(math.log2(SEL_LEN))
    valid = jb <= cur
    forced = valid & ((jb == 0) | (jb == cur) | (jb == cur - 1))
    st = jnp.where(forced, jnp.inf, jnp.where(valid, p_sel, -jnp.inf))
    n_grp = n_sel // SUBLANE
    st_ref[...] = st
    cnt_ref[...] = jnp.zeros(cnt_ref.shape, F32)
    jrow = lax.broadcasted_iota(jnp.int32, (SUBLANE, tq), 0)
    cur_max = (t0 + tq - 1) >> int(math.log2(SEL_LEN))
    for gb in range(n_grp):
        @pl.when(cur_max >= gb * SUBLANE)
        def _():
            grp = [st_ref[g * SUBLANE:(g + 1) * SUBLANE, :] for g in range(n_grp)]
            cnt = [cnt_ref[g * SUBLANE:(g + 1) * SUBLANE, :] for g in range(n_grp)]
            for rb_ in range(SUBLANE):
                rb = grp[gb][rb_:rb_ + 1, :]
                for g in range(n_grp):
                    if g < gb:
                        inc = jnp.where(rb > grp[g], 1.0, 0.0)
                    elif g > gb:
                        inc = jnp.where(rb >= grp[g], 1.0, 0.0)
                    else:
                        inc = jnp.where(jrow > rb_, jnp.where(rb >= grp[g], 1.0, 0.0), jnp.where(rb > grp[g], 1.0, 0.0))
                    cnt[g] = cnt[g] + inc
            for g in range(n_grp):
                cnt_ref[g * SUBLANE:(g + 1) * SUBLANE, :] = cnt[g]

    keep = (cnt_ref[...] < float(SEL_TOPK)) & (st_ref[0:n_sel, :] > -jnp.inf)
    bias = jnp.where(keep, 0.0, NEG_BIAS)
    if n_sel < LANE:
        bias = jnp.concatenate([bias, jnp.zeros((LANE - n_sel, tq), F32)], axis=0)
    bias_ref[...] = bias.astype(BF16)


def _cmp_attn(q_t, kc, vc, overlap_t):
    B, _, S = q_t.shape
    nc = kc.shape[2]
    n_sel = S // SEL_LEN
    tq = 256
    gw = HPG * LANE
    return pl.pallas_call(
        functools.partial(_cmp_attn_kernel, n_sel=n_sel),
        grid=(B, KV_GROUPS, S // tq),
        in_specs=[pl.BlockSpec((None, gw, tq), lambda b, g, i: (b, g, i)),
                  pl.BlockSpec((None, None, nc, LANE), lambda b, g, i: (b, g, 0, 0)),
                  pl.BlockSpec((None, None, nc, LANE), lambda b, g, i: (b, g, 0, 0)),
                  pl.BlockSpec((LANE, nc), lambda b, g, i: (0, 0))],
        out_specs=[pl.BlockSpec((None, tq, gw), lambda b, g, i: (b, i, g)),
                   pl.BlockSpec((None, None, LANE, tq), lambda b, g, i: (b, g, 0, i))],
        out_shape=[jax.ShapeDtypeStruct((B, S, WIDTH), BF16),
                   jax.ShapeDtypeStruct((B, KV_GROUPS, LANE, S), BF16)],
        scratch_shapes=[pltpu.VMEM((LANE, tq), F32), pltpu.VMEM((n_sel, tq), F32)],
        compiler_params=_cparams(("parallel", "parallel", "parallel")),
        name="cmp_attn",
    )(q_t, kc, vc, overlap_t)


FLASH_TQ = 256


def _store_heads(o_ref, out_t, g, tq):
    for hh in range(HPG):
        h = g * HPG + hh
        o_ref[:, h * LANE:(h + 1) * LANE] = out_t[:, hh * tq:(hh + 1) * tq].T.astype(BF16)


def _flash_kernel(q_ref, b_ref, k_ref, vt_ref, kw_ref, vwt_ref, wb_ref, o_ref, ow_ref,
                  qs_ref, s_ref, m_ref, acc_ref, sw_ref, *, tk):
    i = pl.program_id(1)
    tq = q_ref.shape[1]
    cols = HPG * tq
    t0 = i * tq
    kw = qs_ref.shape[1]
    for g in range(KV_GROUPS):
        for hh in range(HPG):
            h = g * HPG + hh
            qs_ref[g, 0:LANE, hh * tq:(hh + 1) * tq] = q_ref[h * LANE:(h + 1) * LANE, :]
            qs_ref[g, LANE:2 * LANE, hh * tq:(hh + 1) * tq] = b_ref[g]
    m_ref[...] = jnp.full(m_ref.shape, -jnp.inf, F32)
    acc_ref[...] = jnp.zeros(acc_ref.shape, F32)
    qpos = t0 + (lax.broadcasted_iota(jnp.int32, (1, cols), 1) & (tq - 1))

    def scores(j, slot):
        k0 = pl.multiple_of(j * tk, tk)
        for g in range(KV_GROUPS):
            s_ref[slot, g] = _dot(k_ref[pl.ds(k0, tk), g * kw:(g + 1) * kw], qs_ref[g])

    def update(j, slot, masked):
        if masked:
            mask = j * tk + lax.broadcasted_iota(jnp.int32, (tk, 1), 0) <= qpos
        for g in range(KV_GROUPS):
            s = s_ref[slot, g]
            if masked:
                s = jnp.where(mask, s, -jnp.inf)
            m_old = m_ref[g]
            m_new = jnp.maximum(m_old, jnp.max(s, axis=0, keepdims=True))
            alpha = jnp.exp2(m_old - m_new)
            p = jnp.exp2(s - m_new)
            acc_ref[g] = alpha * acc_ref[g] + _dot(vt_ref[g, j], p.astype(BF16))
            m_ref[g] = m_new

    jd = t0 // tk
    last = jnp.maximum(jd - 1, 0)
    scores(jd, 0)
    scores(0, 1)
    _window_pass(i, [qs_ref[g, 0:LANE, :] for g in range(KV_GROUPS)], kw_ref, vwt_ref, wb_ref, ow_ref, sw_ref, tq, tk)
    update(jd, 0, True)

    def body(p, carry):
        scores(jnp.minimum(2 * p + 1, last), 0)
        update(2 * p, 1, False)
        scores(jnp.minimum(2 * p + 2, last), 1)
        update(2 * p + 1, 0, False)
        return carry

    lax.fori_loop(0, jd // 2, body, 0)

    @pl.when(jd % 2 == 1)
    def _():
        update(jd - 1, 1, False)

    for g in range(KV_GROUPS):
        _store_heads(o_ref, acc_ref[g, 0:LANE] / jnp.maximum(acc_ref[g, LANE:LANE + 1], jnp.finfo(F32).tiny), g, tq)


def _window_pass(i, q_groups, k_ref, vt_ref, wb_ref, o_ref, s_ref, tq, tk):
    cols = HPG * tq
    n_t = WINDOW // tk + 1
    jt = [jnp.maximum(i - d, 0) for d in range(n_t)]
    for g in range(KV_GROUPS):
        for d in range(n_t):
            k0 = pl.multiple_of(jt[d] * tk, tk)
            s_ref[g, d] = _dot(k_ref[pl.ds(k0, tk), g * LANE:(g + 1) * LANE], q_groups[g])
    static_bias = {0: 0, n_t - 1: 1}

    def tile(g, d):
        s = s_ref[g, d]
        return s + wb_ref[static_bias[d]] if d in static_bias else s

    for g in range(KV_GROUPS):
        m = jnp.max(tile(g, 0), axis=0, keepdims=True)
        for d in range(1, n_t):
            m = jnp.maximum(m, jnp.where(i >= d, jnp.max(tile(g, d), axis=0, keepdims=True), -jnp.inf))
        l = jnp.zeros((1, cols), F32)
        acc = jnp.zeros((LANE, cols), F32)
        for d in range(n_t):
            p = jnp.exp2(tile(g, d) - (m if d == 0 else jnp.where(i >= d, m, jnp.inf)))
            l = l + jnp.sum(p, axis=0, keepdims=True)
            acc = acc + _dot(vt_ref[g, jt[d]], p.astype(BF16))
        _store_heads(o_ref, acc / jnp.maximum(l, jnp.finfo(F32).tiny), g, tq)


def _window_bias(tq):
    r = np.arange(tq)[:, None]
    c = (np.arange(HPG * tq) % tq)[None, :]
    neg = np.float32(-np.inf)
    return jnp.asarray(np.stack([np.where(r <= c, np.float32(0), neg), np.where(c < r, np.float32(0), neg)]))


def _flash(q_t, kaug, vsl_t, bias_t, kwn, vwn_t):
    B, _, S = q_t.shape
    tq = tk = FLASH_TQ
    assert tk == SEL_TK and tk == WIN_TK and WINDOW == 2 * tk
    kw = kaug.shape[2] // KV_GROUPS
    cols = HPG * tq
    out_spec = pl.BlockSpec((None, tq, WIDTH), lambda b, i: (b, i, 0))
    out_sd = jax.ShapeDtypeStruct((B, S, WIDTH), BF16)
    return pl.pallas_call(
        functools.partial(_flash_kernel, tk=tk),
        grid=(B, S // tq),
        in_specs=[pl.BlockSpec((None, WIDTH, tq), lambda b, i: (b, 0, i)),
                  pl.BlockSpec((None, KV_GROUPS, LANE, tq), lambda b, i: (b, 0, 0, i)),
                  pl.BlockSpec((None, S, KV_GROUPS * kw), lambda b, i: (b, 0, 0)),
                  pl.BlockSpec((None, KV_GROUPS, S // tk, SEL_VROWS, tk), lambda b, i: (b, 0, 0, 0, 0)),
                  pl.BlockSpec((None, S, KVW), lambda b, i: (b, 0, 0)),
                  pl.BlockSpec((None, KV_GROUPS, S // tk, LANE, tk), lambda b, i: (b, 0, 0, 0, 0)),
                  pl.BlockSpec((2, tk, cols), lambda b, i: (0, 0, 0))],
        out_specs=[out_spec, out_spec],
        out_shape=[out_sd, out_sd],
        scratch_shapes=[pltpu.VMEM((KV_GROUPS, kw, cols), BF16),
                        pltpu.VMEM((2, KV_GROUPS, tk, cols), F32),
                        pltpu.VMEM((KV_GROUPS, 1, cols), F32),
                        pltpu.VMEM((KV_GROUPS, SEL_VROWS, cols), F32),
                        pltpu.VMEM((KV_GROUPS, WINDOW // tk + 1, tk, cols), F32)],
        compiler_params=_cparams(("parallel", "parallel")),
        name="flash",
    )(q_t, bias_t, kaug, vsl_t, kwn, vwn_t, _window_bias(tq))


def _merge_kernel(ydn_ref, oc_ref, os_ref, ow_ref, small_ref, mgd_ref, mgn_ref, x_ref,
                  wdn_ref, wnsa_ref, wout_ref, g_ref, b_ref, o_ref):
    gates = _sigmoid(small_ref[...])
    cols = []
    for h in range(HEADS):
        sl = slice(h * LANE, (h + 1) * LANE)
        gc = gates[:, SMALL_GATE + h:SMALL_GATE + h + 1]
        gs = gates[:, SMALL_GATE + HEADS + h:SMALL_GATE + HEADS + h + 1]
        gw = gates[:, SMALL_GATE + 2 * HEADS + h:SMALL_GATE + 2 * HEADS + h + 1]
        cols.append((gc * oc_ref[:, sl] + gs * os_ref[:, sl] + gw * ow_ref[:, sl]).astype(BF16))
    y_nsa = jnp.concatenate(cols, axis=1)
    y = (_sigmoid(mgd_ref[...].astype(F32)) * _dot(ydn_ref[...], wdn_ref[...])
         + _sigmoid(mgn_ref[...].astype(F32)) * _dot(y_nsa, wnsa_ref[...]))
    z = DEEPNORM_ALPHA * x_ref[...] + _dot(y.astype(BF16), wout_ref[...])
    o_ref[...] = _layer_norm(z, g_ref[...], b_ref[...])


def _merge(y_dn2, oc2, os2, ow2, small2, h, x2, wdn, wnsa, wout, g, b):
    T, D = x2.shape
    tm = 512
    row = pl.BlockSpec((tm, WIDTH), lambda i: (i, 0))
    full = lambda shape: pl.BlockSpec(shape, lambda i: (0, 0))
    return pl.pallas_call(
        _merge_kernel,
        grid=(T // tm,),
        in_specs=[row, row, row, row,
                  pl.BlockSpec((tm, LANE), lambda i: (i, 0)),
                  pl.BlockSpec((tm, WIDTH), lambda i: (i, COL_MG_DN // WIDTH)),
                  pl.BlockSpec((tm, WIDTH), lambda i: (i, COL_MG_NSA // WIDTH)),
                  pl.BlockSpec((tm, D), lambda i: (i, 0)),
                  full((WIDTH, D)), full((WIDTH, D)), full((D, D)), full((1, D)), full((1, D))],
        out_specs=pl.BlockSpec((tm, D), lambda i: (i, 0)),
        out_shape=jax.ShapeDtypeStruct((T, D), F32),
        compiler_params=_cparams(("parallel",)),
        name="merge",
    )(y_dn2, oc2, os2, ow2, small2, h, h, x2, wdn, wnsa, wout, g, b)


FFN_TM = 1024


def _ffn_up_kernel(x_ref, xp_ref, wg_ref, wv_ref, cwg_ref, cwv_ref, bg_ref, bv_ref, o_ref, xs_ref, *, tiles_per_seq):
    i = pl.program_id(1)
    tm = x_ref.shape[0]
    nv = tm // SUBLANE
    n_slab = x_ref.shape[1] // LANE
    for sp in range(SUBLANE):
        for c in range(n_slab):
            xs_ref[c, pl.ds(sp, nv, stride=SUBLANE), :] = x_ref[sp * nv:(sp + 1) * nv, c * LANE:(c + 1) * LANE]
    xb = jnp.concatenate([xs_ref[c] for c in range(n_slab)], axis=1).astype(BF16)
    notfirst = (i % tiles_per_seq != 0).astype(F32)
    xpb = (xp_ref[...] * notfirst).astype(BF16)
    sub = lax.broadcasted_iota(jnp.int32, (SUBLANE, 1), 0)

    def conv(w_ref, cw_ref, b_ref):
        u = _dot(xb, w_ref[...])
        p = _dot(xpb, w_ref[...])

        def wrap(k):
            blk = u[(nv - k) * SUBLANE:(nv - k + 1) * SUBLANE, :]
            return jnp.where(sub == 0, p[SUBLANE - k:SUBLANE - k + 1, :], pltpu.roll(blk, 1, 0))

        w1, w2 = wrap(1), wrap(2)
        u1 = jnp.concatenate([w1, u[0:(nv - 1) * SUBLANE, :]], axis=0)
        u2 = jnp.concatenate([w2, w1, u[0:(nv - 2) * SUBLANE, :]], axis=0)
        return cw_ref[0:1, :] * u2 + cw_ref[1:2, :] * u1 + cw_ref[2:3, :] * u + b_ref[...]

    o_ref[...] = (_silu(conv(wg_ref, cwg_ref, bg_ref)) * conv(wv_ref, cwv_ref, bv_ref)).astype(BF16)


def _ffn_up(x1, w_up, conv_w, conv_b, seq):
    T, D = x1.shape
    tm = FFN_TM
    nj = 2
    tn = D_FF // nj
    return pl.pallas_call(
        functools.partial(_ffn_up_kernel, tiles_per_seq=seq // tm),
        grid=(nj, T // tm),
        in_specs=[pl.BlockSpec((tm, D), lambda j, i: (i, 0)),
                  pl.BlockSpec((SUBLANE, D), lambda j, i: (jnp.maximum(i * (tm // SUBLANE) - 1, 0), 0)),
                  pl.BlockSpec((D, tn), lambda j, i: (0, j)),
                  pl.BlockSpec((D, tn), lambda j, i: (0, j + nj)),
                  pl.BlockSpec((FFN_CONV, tn), lambda j, i: (0, j)),
                  pl.BlockSpec((FFN_CONV, tn), lambda j, i: (0, j + nj)),
                  pl.BlockSpec((1, tn), lambda j, i: (0, j)),
                  pl.BlockSpec((1, tn), lambda j, i: (0, j + nj))],
        out_specs=pl.BlockSpec((tm, tn), lambda j, i: (i, j)),
        out_shape=jax.ShapeDtypeStruct((T, D_FF), BF16),
        scratch_shapes=[pltpu.VMEM((D // LANE, tm, LANE), F32)],
        compiler_params=_cparams(("parallel", "parallel")),
        name="ffn_up",
    )(x1, x1, w_up, w_up, conv_w, conv_w, conv_b, conv_b)


def _ffn_down_kernel(a_ref, w_ref, x_ref, g_ref, b_ref, o_ref, fs_ref):
    tm, d = x_ref.shape
    nv = tm // SUBLANE
    n_slab = d // LANE
    f = _dot(a_ref[...], w_ref[...])
    for c in range(n_slab):
        fs_ref[c] = f[:, c * LANE:(c + 1) * LANE]
    for sp in range(SUBLANE):
        rows = slice(sp * nv, (sp + 1) * nv)
        f_nat = jnp.concatenate([fs_ref[c, pl.ds(sp, nv, stride=SUBLANE), :] for c in range(n_slab)], axis=1)
        z = DEEPNORM_ALPHA * x_ref[rows, :] + f_nat
        o_ref[rows, :] = _layer_norm(z, g_ref[...], b_ref[...])


def _ffn_down(a, w_down, x1, g, b):
    T, D = x1.shape
    tm = FFN_TM
    full = lambda shape: pl.BlockSpec(shape, lambda i: (0, 0))
    return pl.pallas_call(
        _ffn_down_kernel,
        grid=(T // tm,),
        in_specs=[pl.BlockSpec((tm, D_FF), lambda i: (i, 0)), full((D_FF, D)),
                  pl.BlockSpec((tm, D), lambda i: (i, 0)), full((1, D)), full((1, D))],
        out_specs=pl.BlockSpec((tm, D), lambda i: (i, 0)),
        out_shape=jax.ShapeDtypeStruct((T, D), F32),
        scratch_shapes=[pltpu.VMEM((D // LANE, tm, LANE), F32)],
        compiler_params=_cparams(("parallel",)),
        name="ffn_down",
    )(a, w_down, x1, g, b)


def _split_w_in_kernel(wt_ref, a_ref, b_ref, c_ref):
    o_dz = 3 * WIDTH
    o_db = 4 * WIDTH
    o_nq = o_db + 2 * HEADS
    o_kc = o_nq + WIDTH
    o_ksl = o_kc + 2 * KVW
    o_gate = o_ksl + 4 * KVW
    o_mg = o_gate + 3 * HEADS
    slab = wt_ref.shape[1]

    def put(dst, o0, c0, c1):
        dst[:, o0:o0 + (c1 - c0)] = wt_ref[c0:c1, :].T.astype(BF16)

    put(a_ref, 0, 0, o_dz)
    small = jnp.concatenate([wt_ref[o_db:o_nq, :], wt_ref[o_gate:o_mg, :],
                             jnp.zeros((LANE - 5 * HEADS, slab), F32)], axis=0)
    a_ref[:, o_dz:NA] = small.T.astype(BF16)
    put(b_ref, 0, o_nq, o_kc)
    put(b_ref, WIDTH, o_ksl, o_gate)
    put(c_ref, 0, o_dz, o_db)
    put(c_ref, WIDTH, o_mg, o_mg + 2 * WIDTH)
    put(c_ref, 3 * WIDTH, o_kc, o_ksl)


def _split_w_in(w, layer):
    _, d, n_in = w.shape
    slab = 256
    return pl.pallas_call(
        _split_w_in_kernel,
        grid=(d // slab,),
        in_specs=[pl.BlockSpec((None, n_in, slab), lambda i: (layer, 0, i))],
        out_specs=[pl.BlockSpec((slab, NA), lambda i: (i, 0)), pl.BlockSpec((slab, NB), lambda i: (i, 0)),
                   pl.BlockSpec((slab, NC_COLS), lambda i: (i, 0))],
        out_shape=[jax.ShapeDtypeStruct((d, NA), BF16), jax.ShapeDtypeStruct((d, NB), BF16),
                   jax.ShapeDtypeStruct((d, NC_COLS), BF16)],
        compiler_params=_cparams(("parallel",)),
        name="split_w_in",
    )(jnp.swapaxes(w, 1, 2))


def _rope_tables(pos):
    pos = np.asarray(pos)
    inv_freq = np.float32(ROPE_THETA) ** (-(np.arange(ROPE_HALF, dtype=np.float32) / np.float32(ROPE_HALF)))
    ang = pos.astype(np.float32)[:, None] * inv_freq.astype(np.float32)
    cos = np.cos(ang.astype(np.float64)).astype(np.float32)
    sin = np.sin(ang.astype(np.float64)).astype(np.float32)
    p = pos.shape[0]
    cosf = np.concatenate([cos, cos, np.ones((p, LANE - ROPE_DIM), np.float32)], axis=1)
    sinf = np.concatenate([-sin, sin, np.zeros((p, LANE - ROPE_DIM), np.float32)], axis=1)
    return jnp.asarray(cosf), jnp.asarray(sinf)


def _overlap_table(nc, n_sel):
    n = np.arange(nc)[None, :]
    j = np.arange(LANE)[:, None]
    ov = (n * CMP_STRIDE <= j * SEL_LEN + SEL_LEN - 1) & (n * CMP_STRIDE + CMP_LEN - 1 >= j * SEL_LEN)
    ov = ov & (j < n_sel) & (n < nc - 1)
    return jnp.asarray(ov.astype(np.float32)).astype(BF16)


def _lane_row(vals, offset):
    return jnp.zeros((1, LANE), F32).at[0, offset:offset + vals.shape[0]].set(vals.astype(F32))


def kernel(x, w_in, dn_conv_w, dn_a_log, dn_dt_bias, dn_norm_w, cmp_pos_k, cmp_pos_v, cmp_k_w1, cmp_k_w2,
           cmp_v_w1, cmp_v_w2, w_branch_dn, w_branch_nsa, w_out, ln1_g, ln1_b, ffn_w_up, ffn_conv_w,
           ffn_conv_b, ffn_w_down, ln2_g, ln2_b):
    B, S, D = x.shape
    T = B * S
    nc = S // CMP_STRIDE
    n_sel = S // SEL_LEN
    assert D == WIDTH and S % 512 == 0 and n_sel <= LANE and n_sel % SUBLANE == 0
    cos_t, sin_t = _rope_tables(np.arange(S))
    cos_c, sin_c = _rope_tables(np.arange(nc) * CMP_STRIDE + (CMP_LEN - 1))
    overlap = _overlap_table(nc, n_sel)
    x2 = x.reshape(T, D)
    for layer in range(w_in.shape[0]):
        w_a, w_b, w_c = _split_w_in(w_in, layer)
        x3 = x2.reshape(B, S, D)
        h3, kv_cmp = _proj_rest(x3, w_c)

        q, k, v, bg, small = _proj_gdn(x3, w_a, dn_conv_w[layer], _lane_row(dn_a_log[layer], SMALL_DECAY),
                                       _lane_row(dn_dt_bias[layer], SMALL_DECAY))
        y_dn = _gdn(q, k, v, bg, h3, dn_norm_w[layer].reshape(1, LANE))

        q_t, kaug, vsl_t, kwn_r, vwn_t = _proj_nsa(x3, w_b, cos_t, sin_t)
        kc, vc = _compress(kv_cmp, cmp_pos_k[layer], cmp_pos_v[layer],
                           cmp_k_w1[layer].astype(BF16), cmp_k_w2[layer].astype(BF16),
                           cmp_v_w1[layer].astype(BF16), cmp_v_w2[layer].astype(BF16), cos_c, sin_c)
        o_cmp, bias = _cmp_attn(q_t, kc, vc, overlap)
        o_sel, o_win = _flash(q_t, kaug, vsl_t, bias, kwn_r, vwn_t)

        x1 = _merge(y_dn.reshape(T, WIDTH), o_cmp.reshape(T, WIDTH), o_sel.reshape(T, WIDTH),
                    o_win.reshape(T, WIDTH), small.reshape(T, LANE), h3.reshape(T, COL_KC), x2,
                    w_branch_dn[layer].astype(BF16), w_branch_nsa[layer].astype(BF16),
                    w_out[layer].astype(BF16), ln1_g[layer].reshape(1, D), ln1_b[layer].reshape(1, D))
        a = _ffn_up(x1, ffn_w_up[layer].astype(BF16), ffn_conv_w[layer],
                    ffn_conv_b[layer].reshape(1, 2 * D_FF), S)
        x2 = _ffn_down(a, ffn_w_down[layer].astype(BF16), x1, ln2_g[layer].reshape(1, D), ln2_b[layer].reshape(1, D))
    return x2.reshape(B, S, D)
```

```python
import functools
import math

import numpy as np
import jax
import jax.numpy as jnp
from jax import lax
from jax.experimental import pallas as pl
from jax.experimental.pallas import tpu as pltpu

F32 = jnp.float32
BF16 = jnp.bfloat16
HI = lax.Precision.HIGHEST

HEADS = 8
HEAD_DIM = 128
KV_GROUPS = 2
HPG = HEADS // KV_GROUPS
DN_CONV = 4
DN_CHUNK = 64
CMP_LEN = 32
CMP_STRIDE = 16
CMP_HIDDEN = 256
SEL_LEN = 64
SEL_TOPK = 16
WINDOW = 512
ROPE_THETA = 500000.0
ROPE_DIM = HEAD_DIM // 4
ROPE_HALF = ROPE_DIM // 2
D_FF = 2816
FFN_CONV = 3
LN_EPS = 1e-5
RMS_EPS = 1e-6
DEEPNORM_ALPHA = 2.0 ** 0.25
ATT_SCALE = HEAD_DIM ** -0.5

LANE = 128
SUBLANE = 8
VMEM_LIMIT = 56 * 1024 * 1024

WIDTH = HEADS * HEAD_DIM
KVW = KV_GROUPS * HEAD_DIM
NA = 3 * WIDTH + LANE
SMALL_BETA = 0
SMALL_DECAY = HEADS
SMALL_GATE = 2 * HEADS
NB = WIDTH + 4 * KVW
NC_COLS = 3 * WIDTH + 2 * KVW
COL_DZ = 0
COL_MG_DN = WIDTH
COL_MG_NSA = 2 * WIDTH
COL_KC = 3 * WIDTH
COL_VC = COL_KC + KVW

NEG_BIAS = -(2.0 ** 100)


def _cparams(sem):
    return pltpu.CompilerParams(dimension_semantics=sem, vmem_limit_bytes=VMEM_LIMIT)


def _sigmoid(x):
    return 1.0 / (1.0 + jnp.exp(-x))


def _silu(x):
    return x * _sigmoid(x)


def _dot(a, b, precision=None):
    return jnp.dot(a, b, preferred_element_type=F32, precision=precision)


def _dot_nt(a, b):
    return lax.dot_general(a, b, (((1,), (1,)), ((), ())), preferred_element_type=F32)


def _dot_tn(a, b):
    return lax.dot_general(a, b, (((0,), (0,)), ((), ())), preferred_element_type=F32)


def _layer_norm(z, g, b):
    mu = jnp.mean(z, -1, keepdims=True)
    zc = z - mu
    var = jnp.mean(zc * zc, -1, keepdims=True)
    return zc * lax.rsqrt(var + LN_EPS) * g + b


PROJ_TM = 512


def _proj_rest_kernel(x_ref, w_ref, g_ref, c_ref):
    h = _dot(x_ref[...].astype(BF16), w_ref[...])
    g_ref[...] = h[:, 0:COL_KC].astype(BF16)
    c_ref[...] = h[:, COL_KC:NC_COLS]


def _proj_rest(x3, w):
    B, S, D = x3.shape
    tm = PROJ_TM
    return pl.pallas_call(
        _proj_rest_kernel,
        grid=(B, S // tm),
        in_specs=[pl.BlockSpec((None, tm, D), lambda b, i: (b, i, 0)),
                  pl.BlockSpec((D, NC_COLS), lambda b, i: (0, 0))],
        out_specs=[pl.BlockSpec((None, tm, COL_KC), lambda b, i: (b, i, 0)),
                   pl.BlockSpec((None, tm, NC_COLS - COL_KC), lambda b, i: (b, i, 0))],
        out_shape=[jax.ShapeDtypeStruct((B, S, COL_KC), BF16),
                   jax.ShapeDtypeStruct((B, S, NC_COLS - COL_KC), F32)],
        compiler_params=_cparams(("parallel", "parallel")),
        name="proj_rest",
    )(x3, w)


def _proj_gdn_kernel(x_ref, w_ref, cw_ref, alog_ref, dtb_ref, q_ref, k_ref, v_ref, bg_ref, small_ref,
                     xs_ref, ys_ref, carry_ref):
    i = pl.program_id(1)
    ts = x_ref.shape[0]
    nv = ts // SUBLANE
    wq = 3 * WIDTH
    n_slab = x_ref.shape[1] // LANE

    @pl.when(i == 0)
    def _():
        carry_ref[...] = jnp.zeros(carry_ref.shape, F32)

    for sp in range(SUBLANE):
        for c in range(n_slab):
            xs_ref[c, pl.ds(sp, nv, stride=SUBLANE), :] = x_ref[sp * nv:(sp + 1) * nv, c * LANE:(c + 1) * LANE]
    xb = jnp.concatenate([xs_ref[c] for c in range(n_slab)], axis=1).astype(BF16)
    h = _dot(xb, w_ref[...])
    sub = lax.broadcasted_iota(jnp.int32, (SUBLANE, 1), 0)

    def token_order(c, dst_ref, dst_sl, dtype):
        for sp in range(SUBLANE):
            dst_ref[sp * nv:(sp + 1) * nv, dst_sl] = ys_ref[c, pl.ds(sp, nv, stride=SUBLANE), :].astype(dtype)

    for c in range(3 * HEADS):
        sl = slice(c * LANE, (c + 1) * LANE)
        col = h[:, sl]
        wrap = [jnp.where(sub == 0, carry_ref[SUBLANE - k:SUBLANE - k + 1, sl],
                          pltpu.roll(col[(nv - k) * SUBLANE:(nv - k + 1) * SUBLANE, :], 1, 0))
                for k in range(1, DN_CONV)]
        acc = cw_ref[DN_CONV - 1:DN_CONV, sl] * col
        for s in range(1, DN_CONV):
            shifted = jnp.concatenate(wrap[0:s][::-1] + [col[0:(nv - s) * SUBLANE, :]], axis=0)
            acc = acc + cw_ref[DN_CONV - 1 - s:DN_CONV - s, sl] * shifted
        y = _silu(acc)
        if c < 2 * HEADS:
            y = y * lax.rsqrt(jnp.sum(y * y, -1, keepdims=True) + RMS_EPS)
        if c < HEADS:
            y = y * ATT_SCALE
        ys_ref[c] = y
        dst = (q_ref, k_ref, v_ref)[c // HEADS]
        token_order(c, dst, slice((c % HEADS) * LANE, (c % HEADS + 1) * LANE), BF16)
    for k in range(1, DN_CONV):
        carry_ref[SUBLANE - k:SUBLANE - k + 1, :] = h[(nv - k + 1) * SUBLANE - 1:(nv - k + 1) * SUBLANE, 0:wq]
    small = h[:, wq:wq + LANE]
    beta = _sigmoid(small)
    xs = small + dtb_ref[...]
    softplus = jnp.maximum(xs, 0.0) + jnp.log1p(jnp.exp(-jnp.abs(xs)))
    g = -jnp.exp(alog_ref[...]) * softplus
    lane = lax.broadcasted_iota(jnp.int32, small.shape, 1)
    ys_ref[3 * HEADS] = small
    token_order(3 * HEADS, small_ref, slice(0, LANE), F32)
    ys_ref[3 * HEADS + 1] = jnp.where(lane < SMALL_DECAY, beta, g)
    token_order(3 * HEADS + 1, bg_ref, slice(0, LANE), F32)


def _proj_gdn(x3, w_a, conv_w, alog_row, dtb_row):
    B, S, D = x3.shape
    ts = PROJ_TM
    wq = 3 * WIDTH
    out_sd = jax.ShapeDtypeStruct((B, S, WIDTH), BF16)
    small_sd = jax.ShapeDtypeStruct((B, S, LANE), F32)
    wide = pl.BlockSpec((None, ts, WIDTH), lambda b, i: (b, i, 0))
    narrow = pl.BlockSpec((None, ts, LANE), lambda b, i: (b, i, 0))
    return pl.pallas_call(
        _proj_gdn_kernel,
        grid=(B, S // ts),
        in_specs=[
            pl.BlockSpec((None, ts, D), lambda b, i: (b, i, 0)),
            pl.BlockSpec((D, NA), lambda b, i: (0, 0)),
            pl.BlockSpec((DN_CONV, wq), lambda b, i: (0, 0)),
            pl.BlockSpec((1, LANE), lambda b, i: (0, 0)),
            pl.BlockSpec((1, LANE), lambda b, i: (0, 0)),
        ],
        out_specs=[wide, wide, wide, narrow, narrow],
        out_shape=[out_sd, out_sd, out_sd, small_sd, small_sd],
        scratch_shapes=[pltpu.VMEM((D // LANE, ts, LANE), F32),
                        pltpu.VMEM((3 * HEADS + 2, ts, LANE), F32),
                        pltpu.VMEM((SUBLANE, wq), F32)],
        compiler_params=_cparams(("parallel", "arbitrary")),
        name="proj_gdn",
    )(x3, w_a, conv_w, alog_row, dtb_row)


GDN_CHUNKS_PER_STEP = 4
GDN_BATCH_PER_STEP = 2


GDN_QUAD = 4


def _gdn_kernel(q_ref, k_ref, v_ref, bg_ref, z_ref, nw_ref, y_ref, state_ref):
    n = pl.program_id(1)

    @pl.when(n == 0)
    def _():
        state_ref[...] = jnp.zeros(state_ref.shape, F32)

    C = DN_CHUNK
    QW = GDN_QUAD * C
    QL = GDN_QUAD * LANE
    shift_c = int(math.log2(C))
    r64 = lax.broadcasted_iota(jnp.int32, (C, C), 0)
    c64 = lax.broadcasted_iota(jnp.int32, (C, C), 1)
    tril = jnp.where(r64 >= c64, 1.0, 0.0).astype(F32)
    triu = jnp.where(r64 <= c64, 1.0, 0.0).astype(F32)
    rr = lax.broadcasted_iota(jnp.int32, (C, QW), 0)
    ll = lax.broadcasted_iota(jnp.int32, (C, QW), 1)
    jj = ll & (C - 1)
    hb = ll >> shift_c
    incl_t = rr >= jj
    strict_t = rr > jj
    eye_t = jnp.where(rr == jj, 1.0, 0.0).astype(F32)
    bd_c = jnp.where((lax.broadcasted_iota(jnp.int32, (QW, QW), 0) >> shift_c)
                     == (lax.broadcasted_iota(jnp.int32, (QW, QW), 1) >> shift_c), 1.0, 0.0).astype(BF16)
    bd_l = jnp.where((lax.broadcasted_iota(jnp.int32, (QW, QL), 0) >> shift_c)
                     == (lax.broadcasted_iota(jnp.int32, (QW, QL), 1) >> int(math.log2(LANE))), 1.0, 0.0).astype(BF16)
    zeros_rhs = jnp.zeros((C, 2 * LANE), BF16)
    nw = nw_ref[...]
    nb = q_ref.shape[0]
    quads = [(bb, c, a) for bb in range(nb) for c in range(GDN_CHUNKS_PER_STEP) for a in range(HEADS // GDN_QUAD)]
    gc_cols, gc_rows, bgs = {}, {}, {}
    for bb in range(nb):
        bg = bg_ref[bb]
        bg_t = bg.T
        bgs[bb] = bg
        for c in range(GDN_CHUNKS_PER_STEP):
            rs = slice(c * C, (c + 1) * C)
            gc_cols[bb, c] = _dot(tril, bg[rs, :], HI)
            gc_rows[bb, c] = _dot(bg_t[SMALL_DECAY:SMALL_DECAY + HEADS, rs], triu, HI)
    decay, lower, qk, rhs_bd, qd, kd, eglast = {}, {}, {}, {}, {}, {}, {}
    for u_ in quads:
        bb, c, a = u_
        rs = slice(c * C, (c + 1) * C)
        h0 = a * GDN_QUAD
        gcols = [gc_cols[bb, c][:, SMALL_DECAY + h0 + hh:SMALL_DECAY + h0 + hh + 1] for hh in range(GDN_QUAD)]
        gcol_cat = jnp.broadcast_to(gcols[0], (C, QW))
        for hh in range(1, GDN_QUAD):
            gcol_cat = jnp.where(hb == hh, gcols[hh], gcol_cat)
        grow_cat = jnp.concatenate([gc_rows[bb, c][h0 + hh:h0 + hh + 1, :] for hh in range(GDN_QUAD)], axis=1)
        decay[u_] = jnp.exp(jnp.where(incl_t, gcol_cat - grow_cat, -jnp.inf))
        kq = k_ref[bb, rs, a * QL:(a + 1) * QL].astype(F32)
        qq = q_ref[bb, rs, a * QL:(a + 1) * QL].astype(F32)
        kbs, rows_bd = [], []
        for hh in range(GDN_QUAD):
            h = h0 + hh
            ls = slice(hh * LANE, (hh + 1) * LANE)
            beta = bgs[bb][rs, SMALL_BETA + h:SMALL_BETA + h + 1]
            eg = jnp.exp(gcols[hh])
            glast = gcols[hh][C - 1:C, :]
            kb = kq[:, ls] * beta
            vb = v_ref[bb, rs, h * LANE:(h + 1) * LANE].astype(F32) * beta
            kbs.append(kb)
            rhs_h = jnp.concatenate([vb, kb * eg], axis=1).astype(BF16)
            rows_bd.append(jnp.concatenate([zeros_rhs] * hh + [rhs_h] + [zeros_rhs] * (GDN_QUAD - 1 - hh), axis=1))
            qd[bb, c, h] = (qq[:, ls] * eg).astype(BF16)
            kd[bb, c, h] = (kq[:, ls] * jnp.exp(glast - gcols[hh])).astype(BF16)
            eglast[bb, c, h] = jnp.exp(glast)
        rhs_bd[u_] = jnp.concatenate(rows_bd, axis=0)
        k_bd = jnp.concatenate([kq.astype(BF16)] * GDN_QUAD, axis=0) * bd_l
        lhs = jnp.concatenate([jnp.concatenate(kbs, axis=1), qq], axis=0).astype(BF16)
        aq = _dot_nt(lhs, k_bd)
        lower[u_] = jnp.where(strict_t, aq[0:C] * decay[u_], 0.0)
        qk[u_] = (aq[C:2 * C] * decay[u_]).astype(BF16)

    def block_diag(y_b):
        return jnp.concatenate([y_b] * GDN_QUAD, axis=0) * bd_c

    tinv = {u_: eye_t - lower[u_] for u_ in quads}
    y_b = {u_: lower[u_].astype(BF16) for u_ in quads}
    ypow = {u_: _dot(y_b[u_], block_diag(y_b[u_])) for u_ in quads}
    for it in range(5):
        for u_ in quads:
            yb = ypow[u_].astype(BF16)
            if it < 4:
                r = _dot(jnp.concatenate([tinv[u_].astype(BF16), yb], axis=0), block_diag(yb))
                tinv[u_] = tinv[u_] + r[0:C]
                ypow[u_] = r[C:2 * C]
            else:
                tinv[u_] = tinv[u_] + _dot(tinv[u_].astype(BF16), block_diag(yb))
    uw = {u_: _dot(tinv[u_].astype(BF16), rhs_bd[u_]) for u_ in quads}
    zeros_v = jnp.zeros((C, LANE), BF16)
    chains = [(bb, h) for bb in range(nb) for h in range(HEADS)]
    for c in range(GDN_CHUNKS_PER_STEP):
        rs = slice(c * C, (c + 1) * C)
        states = {s_: state_ref[s_[0] * HEADS + s_[1]] for s_ in chains}
        ws_qs = {}
        for s_ in chains:
            bb, h = s_
            a, hh = divmod(h, GDN_QUAD)
            w_h = uw[bb, c, a][:, hh * 2 * LANE + LANE:(hh + 1) * 2 * LANE]
            ws_qs[s_] = _dot(jnp.concatenate([w_h.astype(BF16), qd[bb, c, h]], axis=0), states[s_].astype(BF16))
        v_new = {}
        for s_ in chains:
            bb, h = s_
            a, hh = divmod(h, GDN_QUAD)
            v_new[s_] = (uw[bb, c, a][:, hh * 2 * LANE:hh * 2 * LANE + LANE] - ws_qs[s_][0:C]).astype(BF16)
        outs = {}
        for s_ in chains:
            bb, h = s_
            a, hh = divmod(h, GDN_QUAD)
            v_pad = jnp.concatenate([zeros_v] * hh + [v_new[s_]] + [zeros_v] * (GDN_QUAD - 1 - hh), axis=0)
            outs[s_] = ws_qs[s_][C:2 * C] + _dot(qk[bb, c, a], v_pad)
            state_ref[bb * HEADS + h] = states[s_] * eglast[bb, c, h] + _dot_tn(kd[bb, c, h], v_new[s_])
        for s_ in chains:
            bb, h = s_
            ls = slice(h * LANE, (h + 1) * LANE)
            o = outs[s_]
            o = o * lax.rsqrt(jnp.mean(o * o, -1, keepdims=True) + RMS_EPS)
            y_ref[bb, rs, ls] = (o * nw * _silu(z_ref[bb, rs, ls].astype(F32))).astype(BF16)


def _gdn(q, k, v, bg, h3, norm_w):
    B, S, _ = q.shape
    rows = GDN_CHUNKS_PER_STEP * DN_CHUNK
    nb = GDN_BATCH_PER_STEP if B % GDN_BATCH_PER_STEP == 0 else 1
    tile = pl.BlockSpec((nb, rows, WIDTH), lambda b, n: (b, n, 0))
    return pl.pallas_call(
        _gdn_kernel,
        grid=(B // nb, S // rows),
        in_specs=[tile, tile, tile,
                  pl.BlockSpec((nb, rows, LANE), lambda b, n: (b, n, 0)),
                  pl.BlockSpec((nb, rows, WIDTH), lambda b, n: (b, n, COL_DZ // WIDTH)),
                  pl.BlockSpec((1, LANE), lambda b, n: (0, 0))],
        out_specs=tile,
        out_shape=jax.ShapeDtypeStruct((B, S, WIDTH), BF16),
        scratch_shapes=[pltpu.VMEM((nb * HEADS, HEAD_DIM, HEAD_DIM), F32)],
        compiler_params=_cparams(("parallel", "arbitrary")),
        name="gdn",
    )(q, k, v, bg, h3, norm_w)


def _rope(t, cosf, sinf, lo):
    partner = jnp.where(lo, pltpu.roll(t, LANE - ROPE_HALF, 1), pltpu.roll(t, ROPE_HALF, 1))
    return t * cosf + partner * sinf


SEL_TK = 256
WIN_TK = 256
SEL_VROWS = LANE + 16
LOG2E = math.log2(math.e)


def _proj_nsa_kernel(x_ref, w_ref, cos_ref, sin_ref, qt_o, kaug_o, vslt_o, kwn_o, vwnt_o):
    i = pl.program_id(1)
    ts = x_ref.shape[0]
    h = _dot(x_ref[...].astype(BF16), w_ref[...])
    c_ksl, c_vsl, c_kwn, c_vwn = WIDTH, WIDTH + KVW, WIDTH + 2 * KVW, WIDTH + 3 * KVW
    cosf = cos_ref[...]
    sinf = sin_ref[...]
    lane = lax.broadcasted_iota(jnp.int32, cosf.shape, 1)
    lo = lane < ROPE_HALF
    for hd in range(HEADS):
        sl = slice(hd * LANE, (hd + 1) * LANE)
        qt_o[sl, :] = (_rope(h[:, sl], cosf, sinf, lo) * (ATT_SCALE * LOG2E)).T.astype(BF16)
    tpos = i * ts + lax.broadcasted_iota(jnp.int32, cosf.shape, 0)
    onehot = jnp.where((tpos >> int(math.log2(SEL_LEN))) == lane, 1.0, 0.0).astype(BF16)
    for g in range(KV_GROUPS):
        sl = slice(g * LANE, (g + 1) * LANE)
        off = g * LANE
        kaug_o[:, 2 * g * LANE:(2 * g + 1) * LANE] = _rope(h[:, c_ksl + off:c_ksl + off + LANE], cosf, sinf, lo).astype(BF16)
        kaug_o[:, (2 * g + 1) * LANE:(2 * g + 2) * LANE] = onehot
        kwn_o[:, sl] = _rope(h[:, c_kwn + off:c_kwn + off + LANE], cosf, sinf, lo).astype(BF16)
        vs_t = h[:, c_vsl + off:c_vsl + off + LANE].T.astype(BF16)
        for c in range(ts // SEL_TK):
            ones_rows = jnp.where(lax.broadcasted_iota(jnp.int32, (SEL_VROWS - LANE, SEL_TK), 0) == 0, 1.0, 0.0)
            vslt_o[g, c] = jnp.concatenate([vs_t[:, c * SEL_TK:(c + 1) * SEL_TK], ones_rows.astype(BF16)], axis=0)
        vw_t = h[:, c_vwn + off:c_vwn + off + LANE].T.astype(BF16)
        for c in range(ts // WIN_TK):
            ones_rows = jnp.where(lax.broadcasted_iota(jnp.int32, (SEL_VROWS - LANE, WIN_TK), 0) == 0, 1.0, 0.0)
            vwnt_o[g, c] = jnp.concatenate([vw_t[:, c * WIN_TK:(c + 1) * WIN_TK], ones_rows.astype(BF16)], axis=0)


def _proj_nsa(x3, w_b, cos_tab, sin_tab):
    B, S, D = x3.shape
    ts = PROJ_TM
    tab = pl.BlockSpec((ts, LANE), lambda b, i: (i, 0))
    return pl.pallas_call(
        _proj_nsa_kernel,
        grid=(B, S // ts),
        in_specs=[pl.BlockSpec((None, ts, D), lambda b, i: (b, i, 0)),
                  pl.BlockSpec((D, NB), lambda b, i: (0, 0)), tab, tab],
        out_specs=[pl.BlockSpec((None, WIDTH, ts), lambda b, i: (b, 0, i)),
                   pl.BlockSpec((None, ts, 2 * KVW), lambda b, i: (b, i, 0)),
                   pl.BlockSpec((None, KV_GROUPS, ts // SEL_TK, SEL_VROWS, SEL_TK), lambda b, i: (b, 0, i, 0, 0)),
                   pl.BlockSpec((None, ts, KVW), lambda b, i: (b, i, 0)),
                   pl.BlockSpec((None, KV_GROUPS, ts // WIN_TK, SEL_VROWS, WIN_TK), lambda b, i: (b, 0, i, 0, 0))],
        out_shape=[jax.ShapeDtypeStruct((B, WIDTH, S), BF16),
                   jax.ShapeDtypeStruct((B, S, 2 * KVW), BF16),
                   jax.ShapeDtypeStruct((B, KV_GROUPS, S // SEL_TK, SEL_VROWS, SEL_TK), BF16),
                   jax.ShapeDtypeStruct((B, S, KVW), BF16),
                   jax.ShapeDtypeStruct((B, KV_GROUPS, S // WIN_TK, SEL_VROWS, WIN_TK), BF16)],
        compiler_params=_cparams(("parallel", "parallel")),
        name="proj_nsa",
    )(x3, w_b, cos_tab, sin_tab)


def _compress_one(t_ref, pe_ref, w1_ref, w2_ref, nc):
    half = CMP_LEN // 2
    y1 = jnp.zeros((nc, CMP_HIDDEN), F32)
    y2 = jnp.zeros((nc, CMP_HIDDEN), F32)
    for l in range(half):
        x = t_ref[pl.ds(l, nc, stride=CMP_STRIDE), :]
        y1 = y1 + _dot((x + pe_ref[l:l + 1, :]).astype(BF16), w1_ref[l * LANE:(l + 1) * LANE, :])
        y2 = y2 + _dot((x + pe_ref[half + l:half + l + 1, :]).astype(BF16),
                       w1_ref[(half + l) * LANE:(half + l + 1) * LANE, :])
    y = y1 + pltpu.roll(y2, nc - 1, 0)
    return _dot(_silu(y).astype(BF16), w2_ref[...])


def _compress_kernel(kc_ref, vc_ref, pek_ref, pev_ref, w1k_ref, w2k_ref, w1v_ref, w2v_ref,
                     cos_ref, sin_ref, kc_o, vc_o):
    nc = kc_o.shape[0]
    kc = _compress_one(kc_ref, pek_ref, w1k_ref, w2k_ref, nc)
    lo = lax.broadcasted_iota(jnp.int32, kc.shape, 1) < ROPE_HALF
    kc_o[...] = _rope(kc, cos_ref[...], sin_ref[...], lo).astype(BF16)
    vc_o[...] = _compress_one(vc_ref, pev_ref, w1v_ref, w2v_ref, nc).astype(BF16)


def _compress(h3, pek, pev, w1k, w2k, w1v, w2v, cos_c, sin_c):
    B, S, _ = h3.shape
    nc = S // CMP_STRIDE
    full = lambda shape: pl.BlockSpec(shape, lambda b, g: tuple(0 for _ in shape))
    out_spec = pl.BlockSpec((None, None, nc, LANE), lambda b, g: (b, g, 0, 0))
    out_sd = jax.ShapeDtypeStruct((B, KV_GROUPS, nc, LANE), BF16)
    cin = CMP_LEN * HEAD_DIM
    return pl.pallas_call(
        _compress_kernel,
        grid=(B, KV_GROUPS),
        in_specs=[pl.BlockSpec((None, S, LANE), lambda b, g: (b, 0, g)),
                  pl.BlockSpec((None, S, LANE), lambda b, g: (b, 0, KV_GROUPS + g)),
                  full((CMP_LEN, LANE)), full((CMP_LEN, LANE)),
                  full((cin, CMP_HIDDEN)), full((CMP_HIDDEN, LANE)),
                  full((cin, CMP_HIDDEN)), full((CMP_HIDDEN, LANE)),
                  full((nc, LANE)), full((nc, LANE))],
        out_specs=[out_spec, out_spec],
        out_shape=[out_sd, out_sd],
        compiler_params=_cparams(("parallel", "parallel")),
        name="compress",
    )(h3, h3, pek, pev, w1k, w2k, w1v, w2v, cos_c, sin_c)


def _cmp_attn_kernel(q_ref, kc_ref, vc_ref, ovt_ref, o_ref, bias_ref, st_ref, cnt_ref, psel_ref, *, n_sel):
    i = pl.program_id(2)
    tq = q_ref.shape[1]
    cols = HPG * tq
    nc = kc_ref.shape[0]
    t0 = i * tq
    q_g = jnp.concatenate([q_ref[hh * LANE:(hh + 1) * LANE, :] for hh in range(HPG)], axis=1)
    tpos = t0 + (lax.broadcasted_iota(jnp.int32, (1, cols), 1) & (tq - 1))

    def attend(rows):
        cmp_end = lax.broadcasted_iota(jnp.int32, (rows, 1), 0) * CMP_STRIDE + (CMP_LEN - 1)
        s = jnp.where(cmp_end <= tpos, _dot(kc_ref[0:rows, :], q_g), -jnp.inf)
        m = jnp.max(s, axis=0, keepdims=True)
        m = jnp.where(m > -jnp.inf, m, 0.0)
        p = jnp.exp2(s - m)
        p = p / jnp.maximum(jnp.sum(p, axis=0, keepdims=True), jnp.finfo(F32).tiny)
        vct = vc_ref[0:rows, :].astype(F32).T.astype(BF16)
        _store_heads(o_ref, _dot(vct, p.astype(BF16)), 0, tq)
        psum = p[:, 0:tq]
        for hh in range(1, HPG):
            psum = psum + p[:, hh * tq:(hh + 1) * tq]
        ovt = ovt_ref[:, 0:rows]
        p_hi = psum.astype(BF16)
        r1 = psum - p_hi.astype(F32)
        p_mid = r1.astype(BF16)
        p_lo = (r1 - p_mid.astype(F32)).astype(BF16)
        psel_ref[...] = _dot(ovt, p_hi) + (_dot(ovt, p_mid) + _dot(ovt, p_lo))

    n_part = 4
    part = nc // n_part
    n_live = (t0 + tq - CMP_LEN) // CMP_STRIDE + 1
    branch = jnp.clip((n_live - 1) // part, 0, n_part - 1)
    for r in range(n_part):
        pl.when(branch == r)(functools.partial(attend, (r + 1) * part))
    p_sel = psel_ref[...]
    jb = lax.broadcasted_iota(jnp.int32, (LANE, tq), 0)
    cur = (t0 + lax.broadcasted_iota(jnp.int32, (1, tq), 1)) >> int(math.log2(SEL_LEN))
    valid = jb <= cur
    forced = valid & ((jb == 0) | (jb == cur) | (jb == cur - 1))
    st = jnp.where(forced, jnp.inf, jnp.where(valid, p_sel, -jnp.inf))
    n_grp = n_sel // SUBLANE
    st_ref[...] = st
    cnt_ref[...] = jnp.zeros(cnt_ref.shape, F32)
    jrow = lax.broadcasted_iota(jnp.int32, (SUBLANE, tq), 0)
    cur_max = (t0 + tq - 1) >> int(math.log2(SEL_LEN))
    for gb in range(n_grp):
        @pl.when(cur_max >= gb * SUBLANE)
        def _():
            grp = [st_ref[g * SUBLANE:(g + 1) * SUBLANE, :] for g in range(n_grp)]
            cnt = [cnt_ref[g * SUBLANE:(g + 1) * SUBLANE, :] for g in range(n_grp)]
            for rb_ in range(SUBLANE):
                rb = grp[gb][rb_:rb_ + 1, :]
                for g in range(n_grp):
                    if g < gb:
                        inc = jnp.where(rb > grp[g], 1.0, 0.0)
                    elif g > gb:
                        inc = jnp.where(rb >= grp[g], 1.0, 0.0)
                    else:
                        inc = jnp.where(jrow > rb_, jnp.where(rb >= grp[g], 1.0, 0.0), jnp.where(rb > grp[g], 1.0, 0.0))
                    cnt[g] = cnt[g] + inc
            for g in range(n_grp):
                cnt_ref[g * SUBLANE:(g + 1) * SUBLANE, :] = cnt[g]

    keep = (cnt_ref[...] < float(SEL_TOPK)) & (st_ref[0:n_sel, :] > -jnp.inf)
    bias = jnp.where(keep, 0.0, NEG_BIAS)
    if n_sel < LANE:
        bias = jnp.concatenate([bias, jnp.zeros((LANE - n_sel, tq), F32)], axis=0)
    bias_ref[...] = bias.astype(BF16)


def _cmp_attn(q_t, kc, vc, overlap_t):
    B, _, S = q_t.shape
    nc = kc.shape[2]
    n_sel = S // SEL_LEN
    tq = 256
    gw = HPG * LANE
    return pl.pallas_call(
        functools.partial(_cmp_attn_kernel, n_sel=n_sel),
        grid=(B, KV_GROUPS, S // tq),
        in_specs=[pl.BlockSpec((None, gw, tq), lambda b, g, i: (b, g, i)),
                  pl.BlockSpec((None, None, nc, LANE), lambda b, g, i: (b, g, 0, 0)),
                  pl.BlockSpec((None, None, nc, LANE), lambda b, g, i: (b, g, 0, 0)),
                  pl.BlockSpec((LANE, nc), lambda b, g, i: (0, 0))],
        out_specs=[pl.BlockSpec((None, tq, gw), lambda b, g, i: (b, i, g)),
                   pl.BlockSpec((None, None, LANE, tq), lambda b, g, i: (b, g, 0, i))],
        out_shape=[jax.ShapeDtypeStruct((B, S, WIDTH), BF16),
                   jax.ShapeDtypeStruct((B, KV_GROUPS, LANE, S), BF16)],
        scratch_shapes=[pltpu.VMEM((LANE, tq), F32), pltpu.VMEM((n_sel, tq), F32), pltpu.VMEM((LANE, tq), F32)],
        compiler_params=_cparams(("parallel", "parallel", "parallel")),
        name="cmp_attn",
    )(q_t, kc, vc, overlap_t)


FLASH_TQ = 256


def _store_heads(o_ref, out_t, g, tq):
    for hh in range(HPG):
        h = g * HPG + hh
        o_ref[:, h * LANE:(h + 1) * LANE] = out_t[:, hh * tq:(hh + 1) * tq].T.astype(BF16)


def _flash_kernel(q_ref, b_ref, k_ref, vt_ref, kw_ref, vwt_ref, wb_ref, o_ref, ow_ref,
                  qs_ref, s_ref, m_ref, acc_ref, sw_ref, pw_ref, *, tk):
    i = pl.program_id(1)
    tq = q_ref.shape[1]
    cols = HPG * tq
    t0 = i * tq
    kw = qs_ref.shape[1]
    for g in range(KV_GROUPS):
        for hh in range(HPG):
            h = g * HPG + hh
            qs_ref[g, 0:LANE, hh * tq:(hh + 1) * tq] = q_ref[h * LANE:(h + 1) * LANE, :]
            qs_ref[g, LANE:2 * LANE, hh * tq:(hh + 1) * tq] = b_ref[g]
    m_ref[...] = jnp.full(m_ref.shape, -jnp.inf, F32)
    acc_ref[...] = jnp.zeros(acc_ref.shape, F32)
    qpos = t0 + (lax.broadcasted_iota(jnp.int32, (1, cols), 1) & (tq - 1))

    def scores(j, slot):
        k0 = pl.multiple_of(j * tk, tk)
        for g in range(KV_GROUPS):
            s_ref[slot, g] = _dot(k_ref[pl.ds(k0, tk), g * kw:(g + 1) * kw], qs_ref[g])

    def update(j, slot, masked):
        if masked:
            mask = j * tk + lax.broadcasted_iota(jnp.int32, (tk, 1), 0) <= qpos
        for g in range(KV_GROUPS):
            s = s_ref[slot, g]
            if masked:
                s = jnp.where(mask, s, -jnp.inf)
            m_old = m_ref[g]
            m_new = jnp.maximum(m_old, jnp.max(s, axis=0, keepdims=True))
            alpha = jnp.exp2(m_old - m_new)
            p = jnp.exp2(s - m_new)
            acc_ref[g] = alpha * acc_ref[g] + _dot(vt_ref[g, j], p.astype(BF16))
            m_ref[g] = m_new

    jd = t0 // tk
    last = jnp.maximum(jd - 1, 0)
    win_scores, win_softmax = _window_pass(i, kw_ref, vwt_ref, wb_ref, ow_ref, sw_ref, pw_ref, tq, tk)
    scores(jd, 0)
    win_scores(0, qs_ref[0, 0:LANE, :])
    update(jd, 0, True)
    win_scores(1, qs_ref[1, 0:LANE, :])
    win_softmax(0)
    scores(0, 1)
    win_softmax(1)

    def body(p, carry):
        scores(jnp.minimum(2 * p + 1, last), 0)
        update(2 * p, 1, False)
        scores(jnp.minimum(2 * p + 2, last), 1)
        update(2 * p + 1, 0, False)
        return carry

    lax.fori_loop(0, jd // 2, body, 0)

    @pl.when(jd % 2 == 1)
    def _():
        update(jd - 1, 1, False)

    for g in range(KV_GROUPS):
        _store_heads(o_ref, acc_ref[g, 0:LANE] / jnp.maximum(acc_ref[g, LANE:LANE + 1], jnp.finfo(F32).tiny), g, tq)


def _window_pass(i, k_ref, vt_ref, wb_ref, o_ref, s_ref, p_ref, tq, tk):
    n_t = WINDOW // tk + 1
    jt = [jnp.maximum(i - d, 0) for d in range(n_t)]
    static_bias = {0: 0, n_t - 1: 1}

    def scores(g, q_g):
        for d in range(n_t):
            k0 = pl.multiple_of(jt[d] * tk, tk)
            s_ref[g, d] = _dot(k_ref[pl.ds(k0, tk), g * LANE:(g + 1) * LANE], q_g)

    def tile(g, d):
        s = s_ref[g, d]
        return s + wb_ref[static_bias[d]] if d in static_bias else s

    def softmax(g):
        m = jnp.max(tile(g, 0), axis=0, keepdims=True)
        for d in range(1, n_t):
            m = jnp.maximum(m, jnp.where(i >= d, jnp.max(tile(g, d), axis=0, keepdims=True), -jnp.inf))
        for d in range(n_t):
            p = jnp.exp2(tile(g, d) - (m if d == 0 else jnp.where(i >= d, m, jnp.inf)))
            p_ref[g, d * tk:(d + 1) * tk, :] = p.astype(BF16)
        v_cat = jnp.concatenate([vt_ref[g, jt[d]] for d in range(n_t)], axis=1)
        acc = _dot(v_cat, p_ref[g])
        _store_heads(o_ref, acc[0:LANE] / jnp.maximum(acc[LANE:LANE + 1], jnp.finfo(F32).tiny), g, tq)

    return scores, softmax


def _window_bias(tq):
    r = np.arange(tq)[:, None]
    c = (np.arange(HPG * tq) % tq)[None, :]
    neg = np.float32(-np.inf)
    return jnp.asarray(np.stack([np.where(r <= c, np.float32(0), neg), np.where(c < r, np.float32(0), neg)]))


def _flash(q_t, kaug, vsl_t, bias_t, kwn, vwn_t):
    B, _, S = q_t.shape
    tq = tk = FLASH_TQ
    assert tk == SEL_TK and tk == WIN_TK and WINDOW == 2 * tk
    kw = kaug.shape[2] // KV_GROUPS
    cols = HPG * tq
    out_spec = pl.BlockSpec((None, tq, WIDTH), lambda b, i: (b, i, 0))
    out_sd = jax.ShapeDtypeStruct((B, S, WIDTH), BF16)
    return pl.pallas_call(
        functools.partial(_flash_kernel, tk=tk),
        grid=(B, S // tq),
        in_specs=[pl.BlockSpec((None, WIDTH, tq), lambda b, i: (b, 0, i)),
                  pl.BlockSpec((None, KV_GROUPS, LANE, tq), lambda b, i: (b, 0, 0, i)),
                  pl.BlockSpec((None, S, KV_GROUPS * kw), lambda b, i: (b, 0, 0)),
                  pl.BlockSpec((None, KV_GROUPS, S // tk, SEL_VROWS, tk), lambda b, i: (b, 0, 0, 0, 0)),
                  pl.BlockSpec((None, S, KVW), lambda b, i: (b, 0, 0)),
                  pl.BlockSpec((None, KV_GROUPS, S // tk, SEL_VROWS, tk), lambda b, i: (b, 0, 0, 0, 0)),
                  pl.BlockSpec((2, tk, cols), lambda b, i: (0, 0, 0))],
        out_specs=[out_spec, out_spec],
        out_shape=[out_sd, out_sd],
        scratch_shapes=[pltpu.VMEM((KV_GROUPS, kw, cols), BF16),
                        pltpu.VMEM((2, KV_GROUPS, tk, cols), F32),
                        pltpu.VMEM((KV_GROUPS, 1, cols), F32),
                        pltpu.VMEM((KV_GROUPS, SEL_VROWS, cols), F32),
                        pltpu.VMEM((KV_GROUPS, WINDOW // tk + 1, tk, cols), F32),
                        pltpu.VMEM((KV_GROUPS, (WINDOW // tk + 1) * tk, cols), BF16)],
        compiler_params=_cparams(("parallel", "parallel")),
        name="flash",
    )(q_t, bias_t, kaug, vsl_t, kwn, vwn_t, _window_bias(tq))


def _merge_kernel(ydn_ref, oc_ref, os_ref, ow_ref, small_ref, mgd_ref, mgn_ref, x_ref,
                  wdn_ref, wnsa_ref, wout_ref, g_ref, b_ref, o_ref):
    gates = _sigmoid(small_ref[...])
    cols = []
    for h in range(HEADS):
        sl = slice(h * LANE, (h + 1) * LANE)
        gc = gates[:, SMALL_GATE + h:SMALL_GATE + h + 1]
        gs = gates[:, SMALL_GATE + HEADS + h:SMALL_GATE + HEADS + h + 1]
        gw = gates[:, SMALL_GATE + 2 * HEADS + h:SMALL_GATE + 2 * HEADS + h + 1]
        cols.append((gc * oc_ref[:, sl] + gs * os_ref[:, sl] + gw * ow_ref[:, sl]).astype(BF16))
    y_nsa = jnp.concatenate(cols, axis=1)
    y = (_sigmoid(mgd_ref[...].astype(F32)) * _dot(ydn_ref[...], wdn_ref[...])
         + _sigmoid(mgn_ref[...].astype(F32)) * _dot(y_nsa, wnsa_ref[...]))
    z = DEEPNORM_ALPHA * x_ref[...] + _dot(y.astype(BF16), wout_ref[...])
    o_ref[...] = _layer_norm(z, g_ref[...], b_ref[...])


def _merge(y_dn2, oc2, os2, ow2, small2, h, x2, wdn, wnsa, wout, g, b):
    T, D = x2.shape
    tm = 512
    row = pl.BlockSpec((tm, WIDTH), lambda i: (i, 0))
    full = lambda shape: pl.BlockSpec(shape, lambda i: (0, 0))
    return pl.pallas_call(
        _merge_kernel,
        grid=(T // tm,),
        in_specs=[row, row, row, row,
                  pl.BlockSpec((tm, LANE), lambda i: (i, 0)),
                  pl.BlockSpec((tm, WIDTH), lambda i: (i, COL_MG_DN // WIDTH)),
                  pl.BlockSpec((tm, WIDTH), lambda i: (i, COL_MG_NSA // WIDTH)),
                  pl.BlockSpec((tm, D), lambda i: (i, 0)),
                  full((WIDTH, D)), full((WIDTH, D)), full((D, D)), full((1, D)), full((1, D))],
        out_specs=pl.BlockSpec((tm, D), lambda i: (i, 0)),
        out_shape=jax.ShapeDtypeStruct((T, D), F32),
        compiler_params=_cparams(("parallel",)),
        name="merge",
    )(y_dn2, oc2, os2, ow2, small2, h, h, x2, wdn, wnsa, wout, g, b)


FFN_TM = 1024


def _ffn_up_kernel(x_ref, xp_ref, wg_ref, wv_ref, cwg_ref, cwv_ref, bg_ref, bv_ref, o_ref, xs_ref, *, tiles_per_seq):
    i = pl.program_id(1)
    tm = x_ref.shape[0]
    nv = tm // SUBLANE
    n_slab = x_ref.shape[1] // LANE
    for sp in range(SUBLANE):
        for c in range(n_slab):
            xs_ref[c, pl.ds(sp, nv, stride=SUBLANE), :] = x_ref[sp * nv:(sp + 1) * nv, c * LANE:(c + 1) * LANE]
    xb = jnp.concatenate([xs_ref[c] for c in range(n_slab)], axis=1).astype(BF16)
    notfirst = (i % tiles_per_seq != 0).astype(F32)
    xpb = (xp_ref[...] * notfirst).astype(BF16)
    sub = lax.broadcasted_iota(jnp.int32, (SUBLANE, 1), 0)

    def conv(w_ref, cw_ref, b_ref):
        u = _dot(xb, w_ref[...])
        p = _dot(xpb, w_ref[...])

        def wrap(k):
            blk = u[(nv - k) * SUBLANE:(nv - k + 1) * SUBLANE, :]
            return jnp.where(sub == 0, p[SUBLANE - k:SUBLANE - k + 1, :], pltpu.roll(blk, 1, 0))

        w1, w2 = wrap(1), wrap(2)
        u1 = jnp.concatenate([w1, u[0:(nv - 1) * SUBLANE, :]], axis=0)
        u2 = jnp.concatenate([w2, w1, u[0:(nv - 2) * SUBLANE, :]], axis=0)
        return cw_ref[0:1, :] * u2 + cw_ref[1:2, :] * u1 + cw_ref[2:3, :] * u + b_ref[...]

    o_ref[...] = (_silu(conv(wg_ref, cwg_ref, bg_ref)) * conv(wv_ref, cwv_ref, bv_ref)).astype(BF16)


def _ffn_up(x1, w_up, conv_w, conv_b, seq):
    T, D = x1.shape
    tm = FFN_TM
    nj = 2
    tn = D_FF // nj
    return pl.pallas_call(
        functools.partial(_ffn_up_kernel, tiles_per_seq=seq // tm),
        grid=(nj, T // tm),
        in_specs=[pl.BlockSpec((tm, D), lambda j, i: (i, 0)),
                  pl.BlockSpec((SUBLANE, D), lambda j, i: (jnp.maximum(i * (tm // SUBLANE) - 1, 0), 0)),
                  pl.BlockSpec((D, tn), lambda j, i: (0, j)),
                  pl.BlockSpec((D, tn), lambda j, i: (0, j + nj)),
                  pl.BlockSpec((FFN_CONV, tn), lambda j, i: (0, j)),
                  pl.BlockSpec((FFN_CONV, tn), lambda j, i: (0, j + nj)),
                  pl.BlockSpec((1, tn), lambda j, i: (0, j)),
                  pl.BlockSpec((1, tn), lambda j, i: (0, j + nj))],
        out_specs=pl.BlockSpec((tm, tn), lambda j, i: (i, j)),
        out_shape=jax.ShapeDtypeStruct((T, D_FF), BF16),
        scratch_shapes=[pltpu.VMEM((D // LANE, tm, LANE), F32)],
        compiler_params=_cparams(("parallel", "parallel")),
        name="ffn_up",
    )(x1, x1, w_up, w_up, conv_w, conv_w, conv_b, conv_b)


def _ffn_down_kernel(a_ref, w_ref, x_ref, g_ref, b_ref, o_ref, fs_ref):
    tm, d = x_ref.shape
    nv = tm // SUBLANE
    n_slab = d // LANE
    f = _dot(a_ref[...], w_ref[...])
    for c in range(n_slab):
        fs_ref[c] = f[:, c * LANE:(c + 1) * LANE]
    for sp in range(SUBLANE):
        rows = slice(sp * nv, (sp + 1) * nv)
        f_nat = jnp.concatenate([fs_ref[c, pl.ds(sp, nv, stride=SUBLANE), :] for c in range(n_slab)], axis=1)
        z = DEEPNORM_ALPHA * x_ref[rows, :] + f_nat
        o_ref[rows, :] = _layer_norm(z, g_ref[...], b_ref[...])


def _ffn_down(a, w_down, x1, g, b):
    T, D = x1.shape
    tm = FFN_TM
    full = lambda shape: pl.BlockSpec(shape, lambda i: (0, 0))
    return pl.pallas_call(
        _ffn_down_kernel,
        grid=(T // tm,),
        in_specs=[pl.BlockSpec((tm, D_FF), lambda i: (i, 0)), full((D_FF, D)),
                  pl.BlockSpec((tm, D), lambda i: (i, 0)), full((1, D)), full((1, D))],
        out_specs=pl.BlockSpec((tm, D), lambda i: (i, 0)),
        out_shape=jax.ShapeDtypeStruct((T, D), F32),
        scratch_shapes=[pltpu.VMEM((D // LANE, tm, LANE), F32)],
        compiler_params=_cparams(("parallel",)),
        name="ffn_down",
    )(a, w_down, x1, g, b)


def _split_w_in_kernel(wt_ref, a_ref, b_ref, c_ref):
    o_dz = 3 * WIDTH
    o_db = 4 * WIDTH
    o_nq = o_db + 2 * HEADS
    o_kc = o_nq + WIDTH
    o_ksl = o_kc + 2 * KVW
    o_gate = o_ksl + 4 * KVW
    o_mg = o_gate + 3 * HEADS
    slab = wt_ref.shape[1]

    def put(dst, o0, c0, c1):
        dst[:, o0:o0 + (c1 - c0)] = wt_ref[c0:c1, :].T.astype(BF16)

    put(a_ref, 0, 0, o_dz)
    small = jnp.concatenate([wt_ref[o_db:o_nq, :], wt_ref[o_gate:o_mg, :],
                             jnp.zeros((LANE - 5 * HEADS, slab), F32)], axis=0)
    a_ref[:, o_dz:NA] = small.T.astype(BF16)
    put(b_ref, 0, o_nq, o_kc)
    put(b_ref, WIDTH, o_ksl, o_gate)
    put(c_ref, 0, o_dz, o_db)
    put(c_ref, WIDTH, o_mg, o_mg + 2 * WIDTH)
    put(c_ref, 3 * WIDTH, o_kc, o_ksl)


def _split_w_in(w, layer):
    _, d, n_in = w.shape
    slab = 256
    return pl.pallas_call(
        _split_w_in_kernel,
        grid=(d // slab,),
        in_specs=[pl.BlockSpec((None, n_in, slab), lambda i: (layer, 0, i))],
        out_specs=[pl.BlockSpec((slab, NA), lambda i: (i, 0)), pl.BlockSpec((slab, NB), lambda i: (i, 0)),
                   pl.BlockSpec((slab, NC_COLS), lambda i: (i, 0))],
        out_shape=[jax.ShapeDtypeStruct((d, NA), BF16), jax.ShapeDtypeStruct((d, NB), BF16),
                   jax.ShapeDtypeStruct((d, NC_COLS), BF16)],
        compiler_params=_cparams(("parallel",)),
        name="split_w_in",
    )(jnp.swapaxes(w, 1, 2))


def _rope_tables(pos):
    pos = np.asarray(pos)
    inv_freq = np.float32(ROPE_THETA) ** (-(np.arange(ROPE_HALF, dtype=np.float32) / np.float32(ROPE_HALF)))
    ang = pos.astype(np.float32)[:, None] * inv_freq.astype(np.float32)
    cos = np.cos(ang.astype(np.float64)).astype(np.float32)
    sin = np.sin(ang.astype(np.float64)).astype(np.float32)
    p = pos.shape[0]
    cosf = np.concatenate([cos, cos, np.ones((p, LANE - ROPE_DIM), np.float32)], axis=1)
    sinf = np.concatenate([-sin, sin, np.zeros((p, LANE - ROPE_DIM), np.float32)], axis=1)
    return jnp.asarray(cosf), jnp.asarray(sinf)


def _overlap_table(nc, n_sel):
    n = np.arange(nc)[None, :]
    j = np.arange(LANE)[:, None]
    ov = (n * CMP_STRIDE <= j * SEL_LEN + SEL_LEN - 1) & (n * CMP_STRIDE + CMP_LEN - 1 >= j * SEL_LEN)
    ov = ov & (j < n_sel) & (n < nc - 1)
    return jnp.asarray(ov.astype(np.float32)).astype(BF16)


def _lane_row(vals, offset):
    return jnp.zeros((1, LANE), F32).at[0, offset:offset + vals.shape[0]].set(vals.astype(F32))


def kernel(x, w_in, dn_conv_w, dn_a_log, dn_dt_bias, dn_norm_w, cmp_pos_k, cmp_pos_v, cmp_k_w1, cmp_k_w2,
           cmp_v_w1, cmp_v_w2, w_branch_dn, w_branch_nsa, w_out, ln1_g, ln1_b, ffn_w_up, ffn_conv_w,
           ffn_conv_b, ffn_w_down, ln2_g, ln2_b):
    B, S, D = x.shape
    T = B * S
    nc = S // CMP_STRIDE
    n_sel = S // SEL_LEN
    assert D == WIDTH and S % 512 == 0 and n_sel <= LANE and n_sel % SUBLANE == 0
    cos_t, sin_t = _rope_tables(np.arange(S))
    cos_c, sin_c = _rope_tables(np.arange(nc) * CMP_STRIDE + (CMP_LEN - 1))
    overlap = _overlap_table(nc, n_sel)
    x2 = x.reshape(T, D)
    for layer in range(w_in.shape[0]):
        w_a, w_b, w_c = _split_w_in(w_in, layer)
        x3 = x2.reshape(B, S, D)
        h3, kv_cmp = _proj_rest(x3, w_c)

        q, k, v, bg, small = _proj_gdn(x3, w_a, dn_conv_w[layer], _lane_row(dn_a_log[layer], SMALL_DECAY),
                                       _lane_row(dn_dt_bias[layer], SMALL_DECAY))
        y_dn = _gdn(q, k, v, bg, h3, dn_norm_w[layer].reshape(1, LANE))

        q_t, kaug, vsl_t, kwn_r, vwn_t = _proj_nsa(x3, w_b, cos_t, sin_t)
        kc, vc = _compress(kv_cmp, cmp_pos_k[layer], cmp_pos_v[layer],
                           cmp_k_w1[layer].astype(BF16), cmp_k_w2[layer].astype(BF16),
                           cmp_v_w1[layer].astype(BF16), cmp_v_w2[layer].astype(BF16), cos_c, sin_c)
        o_cmp, bias = _cmp_attn(q_t, kc, vc, overlap)
        o_sel, o_win = _flash(q_t, kaug, vsl_t, bias, kwn_r, vwn_t)

        x1 = _merge(y_dn.reshape(T, WIDTH), o_cmp.reshape(T, WIDTH), o_sel.reshape(T, WIDTH),
                    o_win.reshape(T, WIDTH), small.reshape(T, LANE), h3.reshape(T, COL_KC), x2,
                    w_branch_dn[layer].astype(BF16), w_branch_nsa[layer].astype(BF16),
                    w_out[layer].astype(BF16), ln1_g[layer].reshape(1, D), ln1_b[layer].reshape(1, D))
        a = _ffn_up(x1, ffn_w_up[layer].astype(BF16), ffn_conv_w[layer],
                    ffn_conv_b[layer].reshape(1, 2 * D_FF), S)
        x2 = _ffn_down(a, ffn_w_down[layer].astype(BF16), x1, ln2_g[layer].reshape(1, D), ln2_b[layer].reshape(1, D))
    return x2.reshape(B, S, D)
```

```python
import functools
import math

import numpy as np
import jax
import jax.numpy as jnp
from jax import lax
from jax.experimental import pallas as pl
from jax.experimental.pallas import tpu as pltpu

F32 = jnp.float32
BF16 = jnp.bfloat16
HI = lax.Precision.HIGHEST

HEADS = 8
HEAD_DIM = 128
KV_GROUPS = 2
HPG = HEADS // KV_GROUPS
DN_CONV = 4
DN_CHUNK = 64
CMP_LEN = 32
CMP_STRIDE = 16
CMP_HIDDEN = 256
SEL_LEN = 64
SEL_TOPK = 16
WINDOW = 512
ROPE_THETA = 500000.0
ROPE_DIM = HEAD_DIM // 4
ROPE_HALF = ROPE_DIM // 2
D_FF = 2816
FFN_CONV = 3
LN_EPS = 1e-5
RMS_EPS = 1e-6
DEEPNORM_ALPHA = 2.0 ** 0.25
ATT_SCALE = HEAD_DIM ** -0.5

LANE = 128
SUBLANE = 8
VMEM_LIMIT = 56 * 1024 * 1024

WIDTH = HEADS * HEAD_DIM
KVW = KV_GROUPS * HEAD_DIM
NA = 3 * WIDTH + LANE
SMALL_BETA = 0
SMALL_DECAY = HEADS
SMALL_GATE = 2 * HEADS
NB = WIDTH + 4 * KVW
NC_COLS = 3 * WIDTH + 2 * KVW
COL_DZ = 0
COL_MG_DN = WIDTH
COL_MG_NSA = 2 * WIDTH
COL_KC = 3 * WIDTH

NEG_BIAS = -(2.0 ** 100)


def _cparams(sem):
    return pltpu.CompilerParams(dimension_semantics=sem, vmem_limit_bytes=VMEM_LIMIT)


def _sigmoid(x):
    return 1.0 / (1.0 + jnp.exp(-x))


def _silu(x):
    return x * _sigmoid(x)


def _dot(a, b, precision=None):
    return jnp.dot(a, b, preferred_element_type=F32, precision=precision)


def _dot_nt(a, b):
    return lax.dot_general(a, b, (((1,), (1,)), ((), ())), preferred_element_type=F32)


def _dot_tn(a, b):
    return lax.dot_general(a, b, (((0,), (0,)), ((), ())), preferred_element_type=F32)


def _layer_norm(z, g, b):
    mu = jnp.mean(z, -1, keepdims=True)
    zc = z - mu
    var = jnp.mean(zc * zc, -1, keepdims=True)
    return zc * lax.rsqrt(var + LN_EPS) * g + b


PROJ_TM = 512


def _proj_rest_kernel(x_ref, w_ref, g_ref, c_ref):
    h = _dot(x_ref[...].astype(BF16), w_ref[...])
    g_ref[...] = h[:, 0:COL_KC].astype(BF16)
    c_ref[...] = h[:, COL_KC:NC_COLS]


def _proj_rest(x3, w):
    B, S, D = x3.shape
    tm = PROJ_TM
    return pl.pallas_call(
        _proj_rest_kernel,
        grid=(B, S // tm),
        in_specs=[pl.BlockSpec((None, tm, D), lambda b, i: (b, i, 0)),
                  pl.BlockSpec((D, NC_COLS), lambda b, i: (0, 0))],
        out_specs=[pl.BlockSpec((None, tm, COL_KC), lambda b, i: (b, i, 0)),
                   pl.BlockSpec((None, tm, NC_COLS - COL_KC), lambda b, i: (b, i, 0))],
        out_shape=[jax.ShapeDtypeStruct((B, S, COL_KC), BF16),
                   jax.ShapeDtypeStruct((B, S, NC_COLS - COL_KC), F32)],
        compiler_params=_cparams(("parallel", "parallel")),
        name="proj_rest",
    )(x3, w)


def _proj_gdn_kernel(x_ref, w_ref, cw_ref, alog_ref, dtb_ref, q_ref, k_ref, v_ref, bg_ref, small_ref,
                     xs_ref, ys_ref, carry_ref):
    i = pl.program_id(1)
    ts = x_ref.shape[0]
    nv = ts // SUBLANE
    wq = 3 * WIDTH
    n_slab = x_ref.shape[1] // LANE

    @pl.when(i == 0)
    def _():
        carry_ref[...] = jnp.zeros(carry_ref.shape, F32)

    for sp in range(SUBLANE):
        for c in range(n_slab):
            xs_ref[c, pl.ds(sp, nv, stride=SUBLANE), :] = x_ref[sp * nv:(sp + 1) * nv, c * LANE:(c + 1) * LANE]
    xb = jnp.concatenate([xs_ref[c] for c in range(n_slab)], axis=1).astype(BF16)
    h = _dot(xb, w_ref[...])
    sub = lax.broadcasted_iota(jnp.int32, (SUBLANE, 1), 0)

    def token_order(c, dst_ref, dst_sl, dtype):
        for sp in range(SUBLANE):
            dst_ref[sp * nv:(sp + 1) * nv, dst_sl] = ys_ref[c, pl.ds(sp, nv, stride=SUBLANE), :].astype(dtype)

    for c in range(3 * HEADS):
        sl = slice(c * LANE, (c + 1) * LANE)
        col = h[:, sl]
        wrap = [jnp.where(sub == 0, carry_ref[SUBLANE - k:SUBLANE - k + 1, sl],
                          pltpu.roll(col[(nv - k) * SUBLANE:(nv - k + 1) * SUBLANE, :], 1, 0))
                for k in range(1, DN_CONV)]
        acc = cw_ref[DN_CONV - 1:DN_CONV, sl] * col
        for s in range(1, DN_CONV):
            shifted = jnp.concatenate(wrap[0:s][::-1] + [col[0:(nv - s) * SUBLANE, :]], axis=0)
            acc = acc + cw_ref[DN_CONV - 1 - s:DN_CONV - s, sl] * shifted
        y = _silu(acc)
        if c < 2 * HEADS:
            y = y * lax.rsqrt(jnp.sum(y * y, -1, keepdims=True) + RMS_EPS)
        if c < HEADS:
            y = y * ATT_SCALE
        ys_ref[c] = y
        dst = (q_ref, k_ref, v_ref)[c // HEADS]
        token_order(c, dst, slice((c % HEADS) * LANE, (c % HEADS + 1) * LANE), BF16)
    for k in range(1, DN_CONV):
        carry_ref[SUBLANE - k:SUBLANE - k + 1, :] = h[(nv - k + 1) * SUBLANE - 1:(nv - k + 1) * SUBLANE, 0:wq]
    small = h[:, wq:wq + LANE]
    beta = _sigmoid(small)
    xs = small + dtb_ref[...]
    softplus = jnp.maximum(xs, 0.0) + jnp.log1p(jnp.exp(-jnp.abs(xs)))
    g = -jnp.exp(alog_ref[...]) * softplus
    lane = lax.broadcasted_iota(jnp.int32, small.shape, 1)
    ys_ref[3 * HEADS] = small
    token_order(3 * HEADS, small_ref, slice(0, LANE), F32)
    ys_ref[3 * HEADS + 1] = jnp.where(lane < SMALL_DECAY, beta, g)
    token_order(3 * HEADS + 1, bg_ref, slice(0, LANE), F32)


def _proj_gdn(x3, w_a, conv_w, alog_row, dtb_row):
    B, S, D = x3.shape
    ts = PROJ_TM
    wq = 3 * WIDTH
    out_sd = jax.ShapeDtypeStruct((B, S, WIDTH), BF16)
    small_sd = jax.ShapeDtypeStruct((B, S, LANE), F32)
    wide = pl.BlockSpec((None, ts, WIDTH), lambda b, i: (b, i, 0))
    narrow = pl.BlockSpec((None, ts, LANE), lambda b, i: (b, i, 0))
    return pl.pallas_call(
        _proj_gdn_kernel,
        grid=(B, S // ts),
        in_specs=[
            pl.BlockSpec((None, ts, D), lambda b, i: (b, i, 0)),
            pl.BlockSpec((D, NA), lambda b, i: (0, 0)),
            pl.BlockSpec((DN_CONV, wq), lambda b, i: (0, 0)),
            pl.BlockSpec((1, LANE), lambda b, i: (0, 0)),
            pl.BlockSpec((1, LANE), lambda b, i: (0, 0)),
        ],
        out_specs=[wide, wide, wide, narrow, narrow],
        out_shape=[out_sd, out_sd, out_sd, small_sd, small_sd],
        scratch_shapes=[pltpu.VMEM((D // LANE, ts, LANE), F32),
                        pltpu.VMEM((3 * HEADS + 2, ts, LANE), F32),
                        pltpu.VMEM((SUBLANE, wq), F32)],
        compiler_params=_cparams(("parallel", "arbitrary")),
        name="proj_gdn",
    )(x3, w_a, conv_w, alog_row, dtb_row)


GDN_CHUNKS_PER_STEP = 4
GDN_BATCH_PER_STEP = 2


GDN_QUAD = 4


def _gdn_kernel(q_ref, k_ref, v_ref, bg_ref, z_ref, nw_ref, y_ref, state_ref):
    n = pl.program_id(1)

    @pl.when(n == 0)
    def _():
        state_ref[...] = jnp.zeros(state_ref.shape, F32)

    C = DN_CHUNK
    QW = GDN_QUAD * C
    QL = GDN_QUAD * LANE
    shift_c = int(math.log2(C))
    r64 = lax.broadcasted_iota(jnp.int32, (C, C), 0)
    c64 = lax.broadcasted_iota(jnp.int32, (C, C), 1)
    tril = jnp.where(r64 >= c64, 1.0, 0.0).astype(F32)
    triu = jnp.where(r64 <= c64, 1.0, 0.0).astype(F32)
    rr = lax.broadcasted_iota(jnp.int32, (C, QW), 0)
    ll = lax.broadcasted_iota(jnp.int32, (C, QW), 1)
    jj = ll & (C - 1)
    hb = ll >> shift_c
    incl_t = rr >= jj
    strict_t = rr > jj
    eye_t = jnp.where(rr == jj, 1.0, 0.0).astype(F32)
    bd_c = jnp.where((lax.broadcasted_iota(jnp.int32, (QW, QW), 0) >> shift_c)
                     == (lax.broadcasted_iota(jnp.int32, (QW, QW), 1) >> shift_c), 1.0, 0.0).astype(BF16)
    bd_l = jnp.where((lax.broadcasted_iota(jnp.int32, (QW, QL), 0) >> shift_c)
                     == (lax.broadcasted_iota(jnp.int32, (QW, QL), 1) >> int(math.log2(LANE))), 1.0, 0.0).astype(BF16)
    zeros_rhs = jnp.zeros((C, 2 * LANE), BF16)
    nw = nw_ref[...]
    nb = q_ref.shape[0]
    quads = [(bb, c, a) for bb in range(nb) for c in range(GDN_CHUNKS_PER_STEP) for a in range(HEADS // GDN_QUAD)]
    gc_cols, gc_rows, bgs = {}, {}, {}
    for bb in range(nb):
        bg = bg_ref[bb]
        bg_t = bg.T
        bgs[bb] = bg
        for c in range(GDN_CHUNKS_PER_STEP):
            rs = slice(c * C, (c + 1) * C)
            gc_cols[bb, c] = _dot(tril, bg[rs, :], HI)
            gc_rows[bb, c] = _dot(bg_t[SMALL_DECAY:SMALL_DECAY + HEADS, rs], triu, HI)
    decay, lower, qk, rhs_bd, qd, kd, eglast = {}, {}, {}, {}, {}, {}, {}
    for u_ in quads:
        bb, c, a = u_
        rs = slice(c * C, (c + 1) * C)
        h0 = a * GDN_QUAD
        gcols = [gc_cols[bb, c][:, SMALL_DECAY + h0 + hh:SMALL_DECAY + h0 + hh + 1] for hh in range(GDN_QUAD)]
        gcol_cat = jnp.broadcast_to(gcols[0], (C, QW))
        for hh in range(1, GDN_QUAD):
            gcol_cat = jnp.where(hb == hh, gcols[hh], gcol_cat)
        grow_cat = jnp.concatenate([gc_rows[bb, c][h0 + hh:h0 + hh + 1, :] for hh in range(GDN_QUAD)], axis=1)
        decay[u_] = jnp.exp(jnp.where(incl_t, gcol_cat - grow_cat, -jnp.inf))
        kq = k_ref[bb, rs, a * QL:(a + 1) * QL].astype(F32)
        qq = q_ref[bb, rs, a * QL:(a + 1) * QL].astype(F32)
        kbs, rows_bd = [], []
        for hh in range(GDN_QUAD):
            h = h0 + hh
            ls = slice(hh * LANE, (hh + 1) * LANE)
            beta = bgs[bb][rs, SMALL_BETA + h:SMALL_BETA + h + 1]
            eg = jnp.exp(gcols[hh])
            glast = gcols[hh][C - 1:C, :]
            kb = kq[:, ls] * beta
            vb = v_ref[bb, rs, h * LANE:(h + 1) * LANE].astype(F32) * beta
            kbs.append(kb)
            rhs_h = jnp.concatenate([vb, kb * eg], axis=1).astype(BF16)
            rows_bd.append(jnp.concatenate([zeros_rhs] * hh + [rhs_h] + [zeros_rhs] * (GDN_QUAD - 1 - hh), axis=1))
            qd[bb, c, h] = (qq[:, ls] * eg).astype(BF16)
            kd[bb, c, h] = (kq[:, ls] * jnp.exp(glast - gcols[hh])).astype(BF16)
            eglast[bb, c, h] = jnp.exp(glast)
        rhs_bd[u_] = jnp.concatenate(rows_bd, axis=0)
        k_bd = jnp.concatenate([kq.astype(BF16)] * GDN_QUAD, axis=0) * bd_l
        lhs = jnp.concatenate([jnp.concatenate(kbs, axis=1), qq], axis=0).astype(BF16)
        aq = _dot_nt(lhs, k_bd)
        lower[u_] = jnp.where(strict_t, aq[0:C] * decay[u_], 0.0)
        qk[u_] = (aq[C:2 * C] * decay[u_]).astype(BF16)

    def block_diag(y_b):
        return jnp.concatenate([y_b] * GDN_QUAD, axis=0) * bd_c

    tinv = {u_: eye_t - lower[u_] for u_ in quads}
    y_b = {u_: lower[u_].astype(BF16) for u_ in quads}
    ypow = {u_: _dot(y_b[u_], block_diag(y_b[u_])) for u_ in quads}
    for it in range(5):
        for u_ in quads:
            yb = ypow[u_].astype(BF16)
            if it < 4:
                r = _dot(jnp.concatenate([tinv[u_].astype(BF16), yb], axis=0), block_diag(yb))
                tinv[u_] = tinv[u_] + r[0:C]
                ypow[u_] = r[C:2 * C]
            else:
                tinv[u_] = tinv[u_] + _dot(tinv[u_].astype(BF16), block_diag(yb))
    uw = {u_: _dot(tinv[u_].astype(BF16), rhs_bd[u_]) for u_ in quads}
    zeros_v = jnp.zeros((C, LANE), BF16)
    chains = [(bb, h) for bb in range(nb) for h in range(HEADS)]
    for c in range(GDN_CHUNKS_PER_STEP):
        rs = slice(c * C, (c + 1) * C)
        states = {s_: state_ref[s_[0] * HEADS + s_[1]] for s_ in chains}
        ws_qs = {}
        for s_ in chains:
            bb, h = s_
            a, hh = divmod(h, GDN_QUAD)
            w_h = uw[bb, c, a][:, hh * 2 * LANE + LANE:(hh + 1) * 2 * LANE]
            ws_qs[s_] = _dot(jnp.concatenate([w_h.astype(BF16), qd[bb, c, h]], axis=0), states[s_].astype(BF16))
        v_new = {}
        for s_ in chains:
            bb, h = s_
            a, hh = divmod(h, GDN_QUAD)
            v_new[s_] = (uw[bb, c, a][:, hh * 2 * LANE:hh * 2 * LANE + LANE] - ws_qs[s_][0:C]).astype(BF16)
        outs = {}
        for s_ in chains:
            bb, h = s_
            a, hh = divmod(h, GDN_QUAD)
            v_pad = jnp.concatenate([zeros_v] * hh + [v_new[s_]] + [zeros_v] * (GDN_QUAD - 1 - hh), axis=0)
            outs[s_] = ws_qs[s_][C:2 * C] + _dot(qk[bb, c, a], v_pad)
            state_ref[bb * HEADS + h] = states[s_] * eglast[bb, c, h] + _dot_tn(kd[bb, c, h], v_new[s_])
        for s_ in chains:
            bb, h = s_
            ls = slice(h * LANE, (h + 1) * LANE)
            o = outs[s_]
            o = o * lax.rsqrt(jnp.mean(o * o, -1, keepdims=True) + RMS_EPS)
            y_ref[bb, rs, ls] = (o * nw * _silu(z_ref[bb, rs, ls].astype(F32))).astype(BF16)


def _gdn(q, k, v, bg, h3, norm_w):
    B, S, _ = q.shape
    rows = GDN_CHUNKS_PER_STEP * DN_CHUNK
    nb = GDN_BATCH_PER_STEP if B % GDN_BATCH_PER_STEP == 0 else 1
    tile = pl.BlockSpec((nb, rows, WIDTH), lambda b, n: (b, n, 0))
    return pl.pallas_call(
        _gdn_kernel,
        grid=(B // nb, S // rows),
        in_specs=[tile, tile, tile,
                  pl.BlockSpec((nb, rows, LANE), lambda b, n: (b, n, 0)),
                  pl.BlockSpec((nb, rows, WIDTH), lambda b, n: (b, n, COL_DZ // WIDTH)),
                  pl.BlockSpec((1, LANE), lambda b, n: (0, 0))],
        out_specs=tile,
        out_shape=jax.ShapeDtypeStruct((B, S, WIDTH), BF16),
        scratch_shapes=[pltpu.VMEM((nb * HEADS, HEAD_DIM, HEAD_DIM), F32)],
        compiler_params=_cparams(("parallel", "arbitrary")),
        name="gdn",
    )(q, k, v, bg, h3, norm_w)


def _rope(t, cosf, sinf, lo):
    partner = jnp.where(lo, pltpu.roll(t, LANE - ROPE_HALF, 1), pltpu.roll(t, ROPE_HALF, 1))
    return t * cosf + partner * sinf


SEL_TK = 256
WIN_TK = 256
SEL_VROWS = LANE + 16
LOG2E = math.log2(math.e)


def _proj_nsa_kernel(x_ref, w_ref, cos_ref, sin_ref, qt_o, kaug_o, vslt_o, kwn_o, vwnt_o):
    i = pl.program_id(1)
    ts = x_ref.shape[0]
    h = _dot(x_ref[...].astype(BF16), w_ref[...])
    c_ksl, c_vsl, c_kwn, c_vwn = WIDTH, WIDTH + KVW, WIDTH + 2 * KVW, WIDTH + 3 * KVW
    cosf = cos_ref[...]
    sinf = sin_ref[...]
    lane = lax.broadcasted_iota(jnp.int32, cosf.shape, 1)
    lo = lane < ROPE_HALF
    for hd in range(HEADS):
        sl = slice(hd * LANE, (hd + 1) * LANE)
        qt_o[sl, :] = (_rope(h[:, sl], cosf, sinf, lo) * (ATT_SCALE * LOG2E)).T.astype(BF16)
    tpos = i * ts + lax.broadcasted_iota(jnp.int32, cosf.shape, 0)
    onehot = jnp.where((tpos >> int(math.log2(SEL_LEN))) == lane, 1.0, 0.0).astype(BF16)
    for g in range(KV_GROUPS):
        sl = slice(g * LANE, (g + 1) * LANE)
        off = g * LANE
        kaug_o[:, 2 * g * LANE:(2 * g + 1) * LANE] = _rope(h[:, c_ksl + off:c_ksl + off + LANE], cosf, sinf, lo).astype(BF16)
        kaug_o[:, (2 * g + 1) * LANE:(2 * g + 2) * LANE] = onehot
        kwn_o[:, sl] = _rope(h[:, c_kwn + off:c_kwn + off + LANE], cosf, sinf, lo).astype(BF16)
        vs_t = h[:, c_vsl + off:c_vsl + off + LANE].T.astype(BF16)
        for c in range(ts // SEL_TK):
            ones_rows = jnp.where(lax.broadcasted_iota(jnp.int32, (SEL_VROWS - LANE, SEL_TK), 0) == 0, 1.0, 0.0)
            vslt_o[g, c] = jnp.concatenate([vs_t[:, c * SEL_TK:(c + 1) * SEL_TK], ones_rows.astype(BF16)], axis=0)
        vw_t = h[:, c_vwn + off:c_vwn + off + LANE].T.astype(BF16)
        for c in range(ts // WIN_TK):
            ones_rows = jnp.where(lax.broadcasted_iota(jnp.int32, (SEL_VROWS - LANE, WIN_TK), 0) == 0, 1.0, 0.0)
            vwnt_o[g, c] = jnp.concatenate([vw_t[:, c * WIN_TK:(c + 1) * WIN_TK], ones_rows.astype(BF16)], axis=0)


def _proj_nsa(x3, w_b, cos_tab, sin_tab):
    B, S, D = x3.shape
    ts = PROJ_TM
    tab = pl.BlockSpec((ts, LANE), lambda b, i: (i, 0))
    return pl.pallas_call(
        _proj_nsa_kernel,
        grid=(B, S // ts),
        in_specs=[pl.BlockSpec((None, ts, D), lambda b, i: (b, i, 0)),
                  pl.BlockSpec((D, NB), lambda b, i: (0, 0)), tab, tab],
        out_specs=[pl.BlockSpec((None, WIDTH, ts), lambda b, i: (b, 0, i)),
                   pl.BlockSpec((None, ts, 2 * KVW), lambda b, i: (b, i, 0)),
                   pl.BlockSpec((None, KV_GROUPS, ts // SEL_TK, SEL_VROWS, SEL_TK), lambda b, i: (b, 0, i, 0, 0)),
                   pl.BlockSpec((None, ts, KVW), lambda b, i: (b, i, 0)),
                   pl.BlockSpec((None, KV_GROUPS, ts // WIN_TK, SEL_VROWS, WIN_TK), lambda b, i: (b, 0, i, 0, 0))],
        out_shape=[jax.ShapeDtypeStruct((B, WIDTH, S), BF16),
                   jax.ShapeDtypeStruct((B, S, 2 * KVW), BF16),
                   jax.ShapeDtypeStruct((B, KV_GROUPS, S // SEL_TK, SEL_VROWS, SEL_TK), BF16),
                   jax.ShapeDtypeStruct((B, S, KVW), BF16),
                   jax.ShapeDtypeStruct((B, KV_GROUPS, S // WIN_TK, SEL_VROWS, WIN_TK), BF16)],
        compiler_params=_cparams(("parallel", "parallel")),
        name="proj_nsa",
    )(x3, w_b, cos_tab, sin_tab)


def _compress_one(t_ref, pe_ref, w1_ref, w2_ref, nc):
    half = CMP_LEN // 2
    y1 = jnp.zeros((nc, CMP_HIDDEN), F32)
    y2 = jnp.zeros((nc, CMP_HIDDEN), F32)
    for l in range(half):
        x = t_ref[pl.ds(l, nc, stride=CMP_STRIDE), :]
        y1 = y1 + _dot((x + pe_ref[l:l + 1, :]).astype(BF16), w1_ref[l * LANE:(l + 1) * LANE, :])
        y2 = y2 + _dot((x + pe_ref[half + l:half + l + 1, :]).astype(BF16),
                       w1_ref[(half + l) * LANE:(half + l + 1) * LANE, :])
    y = y1 + pltpu.roll(y2, nc - 1, 0)
    return _dot(_silu(y).astype(BF16), w2_ref[...])


def _compress_kernel(kc_ref, vc_ref, pek_ref, pev_ref, w1k_ref, w2k_ref, w1v_ref, w2v_ref,
                     cos_ref, sin_ref, kc_o, vc_o):
    nc = kc_o.shape[0]
    kc = _compress_one(kc_ref, pek_ref, w1k_ref, w2k_ref, nc)
    lo = lax.broadcasted_iota(jnp.int32, kc.shape, 1) < ROPE_HALF
    kc_o[...] = _rope(kc, cos_ref[...], sin_ref[...], lo).astype(BF16)
    vc_o[...] = _compress_one(vc_ref, pev_ref, w1v_ref, w2v_ref, nc).astype(BF16)


def _compress(h3, pek, pev, w1k, w2k, w1v, w2v, cos_c, sin_c):
    B, S, _ = h3.shape
    nc = S // CMP_STRIDE
    full = lambda shape: pl.BlockSpec(shape, lambda b, g: tuple(0 for _ in shape))
    out_spec = pl.BlockSpec((None, None, nc, LANE), lambda b, g: (b, g, 0, 0))
    out_sd = jax.ShapeDtypeStruct((B, KV_GROUPS, nc, LANE), BF16)
    cin = CMP_LEN * HEAD_DIM
    return pl.pallas_call(
        _compress_kernel,
        grid=(B, KV_GROUPS),
        in_specs=[pl.BlockSpec((None, S, LANE), lambda b, g: (b, 0, g)),
                  pl.BlockSpec((None, S, LANE), lambda b, g: (b, 0, KV_GROUPS + g)),
                  full((CMP_LEN, LANE)), full((CMP_LEN, LANE)),
                  full((cin, CMP_HIDDEN)), full((CMP_HIDDEN, LANE)),
                  full((cin, CMP_HIDDEN)), full((CMP_HIDDEN, LANE)),
                  full((nc, LANE)), full((nc, LANE))],
        out_specs=[out_spec, out_spec],
        out_shape=[out_sd, out_sd],
        compiler_params=_cparams(("parallel", "parallel")),
        name="compress",
    )(h3, h3, pek, pev, w1k, w2k, w1v, w2v, cos_c, sin_c)


def _cmp_attn_kernel(q_ref, kc_ref, vc_ref, ovt_ref, o_ref, bias_ref, st_ref, cnt_ref, psel_ref, *, n_sel):
    i = pl.program_id(2)
    tq = q_ref.shape[1]
    cols = HPG * tq
    nc = kc_ref.shape[0]
    t0 = i * tq
    q_g = jnp.concatenate([q_ref[hh * LANE:(hh + 1) * LANE, :] for hh in range(HPG)], axis=1)
    tpos = t0 + (lax.broadcasted_iota(jnp.int32, (1, cols), 1) & (tq - 1))

    def attend(rows):
        cmp_end = lax.broadcasted_iota(jnp.int32, (rows, 1), 0) * CMP_STRIDE + (CMP_LEN - 1)
        s = jnp.where(cmp_end <= tpos, _dot(kc_ref[0:rows, :], q_g), -jnp.inf)
        m = jnp.max(s, axis=0, keepdims=True)
        m = jnp.where(m > -jnp.inf, m, 0.0)
        p = jnp.exp2(s - m)
        p = p / jnp.maximum(jnp.sum(p, axis=0, keepdims=True), jnp.finfo(F32).tiny)
        vct = vc_ref[0:rows, :].astype(F32).T.astype(BF16)
        _store_heads(o_ref, _dot(vct, p.astype(BF16)), 0, tq)
        psum = p[:, 0:tq]
        for hh in range(1, HPG):
            psum = psum + p[:, hh * tq:(hh + 1) * tq]
        ovt = ovt_ref[:, 0:rows]
        p_hi = psum.astype(BF16)
        r1 = psum - p_hi.astype(F32)
        p_mid = r1.astype(BF16)
        p_lo = (r1 - p_mid.astype(F32)).astype(BF16)
        psel_ref[...] = _dot(ovt, p_hi) + (_dot(ovt, p_mid) + _dot(ovt, p_lo))

    n_part = 4
    part = nc // n_part
    n_live = (t0 + tq - CMP_LEN) // CMP_STRIDE + 1
    branch = jnp.clip((n_live - 1) // part, 0, n_part - 1)
    for r in range(n_part):
        pl.when(branch == r)(functools.partial(attend, (r + 1) * part))
    p_sel = psel_ref[...]
    jb = lax.broadcasted_iota(jnp.int32, (LANE, tq), 0)
    cur = (t0 + lax.broadcasted_iota(jnp.int32, (1, tq), 1)) >> int(math.log2(SEL_LEN))
    valid = jb <= cur
    forced = valid & ((jb == 0) | (jb == cur) | (jb == cur - 1))
    st = jnp.where(forced, jnp.inf, jnp.where(valid, p_sel, -jnp.inf))
    n_grp = n_sel // SUBLANE
    st_ref[...] = st
    cnt_ref[...] = jnp.zeros(cnt_ref.shape, F32)
    jrow = lax.broadcasted_iota(jnp.int32, (SUBLANE, tq), 0)
    cur_max = (t0 + tq - 1) >> int(math.log2(SEL_LEN))
    for gb in range(n_grp):
        @pl.when(cur_max >= gb * SUBLANE)
        def _():
            grp = [st_ref[g * SUBLANE:(g + 1) * SUBLANE, :] for g in range(n_grp)]
            cnt = [cnt_ref[g * SUBLANE:(g + 1) * SUBLANE, :] for g in range(n_grp)]
            for rb_ in range(SUBLANE):
                rb = grp[gb][rb_:rb_ + 1, :]
                for g in range(n_grp):
                    if g < gb:
                        inc = jnp.where(rb > grp[g], 1.0, 0.0)
                    elif g > gb:
                        inc = jnp.where(rb >= grp[g], 1.0, 0.0)
                    else:
                        inc = jnp.where(jrow > rb_, jnp.where(rb >= grp[g], 1.0, 0.0), jnp.where(rb > grp[g], 1.0, 0.0))
                    cnt[g] = cnt[g] + inc
            for g in range(n_grp):
                cnt_ref[g * SUBLANE:(g + 1) * SUBLANE, :] = cnt[g]

    keep = (cnt_ref[...] < float(SEL_TOPK)) & (st_ref[0:n_sel, :] > -jnp.inf)
    bias = jnp.where(keep, 0.0, NEG_BIAS)
    if n_sel < LANE:
        bias = jnp.concatenate([bias, jnp.zeros((LANE - n_sel, tq), F32)], axis=0)
    bias_ref[...] = bias.astype(BF16)


def _cmp_attn(q_t, kc, vc, overlap_t):
    B, _, S = q_t.shape
    nc = kc.shape[2]
    n_sel = S // SEL_LEN
    tq = FLASH_TQ
    gw = HPG * LANE
    return pl.pallas_call(
        functools.partial(_cmp_attn_kernel, n_sel=n_sel),
        grid=(B, KV_GROUPS, S // tq),
        in_specs=[pl.BlockSpec((None, gw, tq), lambda b, g, i: (b, g, i)),
                  pl.BlockSpec((None, None, nc, LANE), lambda b, g, i: (b, g, 0, 0)),
                  pl.BlockSpec((None, None, nc, LANE), lambda b, g, i: (b, g, 0, 0)),
                  pl.BlockSpec((LANE, nc), lambda b, g, i: (0, 0))],
        out_specs=[pl.BlockSpec((None, tq, gw), lambda b, g, i: (b, i, g)),
                   pl.BlockSpec((None, None, LANE, tq), lambda b, g, i: (b, g, 0, i))],
        out_shape=[jax.ShapeDtypeStruct((B, S, WIDTH), BF16),
                   jax.ShapeDtypeStruct((B, KV_GROUPS, LANE, S), BF16)],
        scratch_shapes=[pltpu.VMEM((LANE, tq), F32), pltpu.VMEM((n_sel, tq), F32), pltpu.VMEM((LANE, tq), F32)],
        compiler_params=_cparams(("parallel", "parallel", "parallel")),
        name="cmp_attn",
    )(q_t, kc, vc, overlap_t)


FLASH_TQ = 256


def _store_heads(o_ref, out_t, g, tq):
    for hh in range(HPG):
        h = g * HPG + hh
        o_ref[:, h * LANE:(h + 1) * LANE] = out_t[:, hh * tq:(hh + 1) * tq].T.astype(BF16)


def _flash_kernel(q_ref, b_ref, k_ref, vt_ref, kw_ref, vwt_ref, wb_ref, o_ref, ow_ref,
                  qs_ref, s_ref, m_ref, acc_ref, sw_ref, pw_ref, *, tk):
    i = pl.program_id(1)
    tq = q_ref.shape[1]
    cols = HPG * tq
    t0 = i * tq
    kw = qs_ref.shape[1]
    for g in range(KV_GROUPS):
        for hh in range(HPG):
            h = g * HPG + hh
            qs_ref[g, 0:LANE, hh * tq:(hh + 1) * tq] = q_ref[h * LANE:(h + 1) * LANE, :]
            qs_ref[g, LANE:2 * LANE, hh * tq:(hh + 1) * tq] = b_ref[g]
    m_ref[...] = jnp.full(m_ref.shape, -jnp.inf, F32)
    acc_ref[...] = jnp.zeros(acc_ref.shape, F32)
    qpos = t0 + (lax.broadcasted_iota(jnp.int32, (1, cols), 1) & (tq - 1))

    def scores(j, slot):
        k0 = pl.multiple_of(j * tk, tk)
        for g in range(KV_GROUPS):
            s_ref[slot, g] = _dot(k_ref[pl.ds(k0, tk), g * kw:(g + 1) * kw], qs_ref[g])

    def update(j, slot, masked):
        if masked:
            mask = j * tk + lax.broadcasted_iota(jnp.int32, (tk, 1), 0) <= qpos
        for g in range(KV_GROUPS):
            s = s_ref[slot, g]
            if masked:
                s = jnp.where(mask, s, -jnp.inf)
            m_old = m_ref[g]
            m_new = jnp.maximum(m_old, jnp.max(s, axis=0, keepdims=True))
            alpha = jnp.exp2(m_old - m_new)
            p = jnp.exp2(s - m_new)
            acc_ref[g] = alpha * acc_ref[g] + _dot(vt_ref[g, j], p.astype(BF16))
            m_ref[g] = m_new

    jd = t0 // tk
    last = jnp.maximum(jd - 1, 0)
    win_scores, win_softmax = _window_pass(i, kw_ref, vwt_ref, wb_ref, ow_ref, sw_ref, pw_ref, tq, tk)
    scores(jd, 0)
    win_scores(0, qs_ref[0, 0:LANE, :])
    update(jd, 0, True)
    win_scores(1, qs_ref[1, 0:LANE, :])
    win_softmax(0)
    scores(0, 1)
    win_softmax(1)

    def body(p, carry):
        scores(jnp.minimum(2 * p + 1, last), 0)
        update(2 * p, 1, False)
        scores(jnp.minimum(2 * p + 2, last), 1)
        update(2 * p + 1, 0, False)
        return carry

    lax.fori_loop(0, jd // 2, body, 0)

    @pl.when(jd % 2 == 1)
    def _():
        update(jd - 1, 1, False)

    for g in range(KV_GROUPS):
        _store_heads(o_ref, acc_ref[g, 0:LANE] / jnp.maximum(acc_ref[g, LANE:LANE + 1], jnp.finfo(F32).tiny), g, tq)


def _window_pass(i, k_ref, vt_ref, wb_ref, o_ref, s_ref, p_ref, tq, tk):
    n_t = WINDOW // tk + 1
    jt = [jnp.maximum(i - d, 0) for d in range(n_t)]
    static_bias = {0: 0, n_t - 1: 1}

    def scores(g, q_g):
        for d in range(n_t):
            k0 = pl.multiple_of(jt[d] * tk, tk)
            s_ref[g, d] = _dot(k_ref[pl.ds(k0, tk), g * LANE:(g + 1) * LANE], q_g)

    def tile(g, d):
        s = s_ref[g, d]
        return s + wb_ref[static_bias[d]] if d in static_bias else s

    def softmax(g):
        m = jnp.max(tile(g, 0), axis=0, keepdims=True)
        for d in range(1, n_t):
            m = jnp.maximum(m, jnp.where(i >= d, jnp.max(tile(g, d), axis=0, keepdims=True), -jnp.inf))
        for d in range(n_t):
            p = jnp.exp2(tile(g, d) - (m if d == 0 else jnp.where(i >= d, m, jnp.inf)))
            p_ref[g, d * tk:(d + 1) * tk, :] = p.astype(BF16)
        v_cat = jnp.concatenate([vt_ref[g, jt[d]] for d in range(n_t)], axis=1)
        acc = _dot(v_cat, p_ref[g])
        _store_heads(o_ref, acc[0:LANE] / jnp.maximum(acc[LANE:LANE + 1], jnp.finfo(F32).tiny), g, tq)

    return scores, softmax


def _window_bias(tq):
    r = np.arange(tq)[:, None]
    c = (np.arange(HPG * tq) % tq)[None, :]
    neg = np.float32(-np.inf)
    return jnp.asarray(np.stack([np.where(r <= c, np.float32(0), neg), np.where(c < r, np.float32(0), neg)]))


def _flash(q_t, kaug, vsl_t, bias_t, kwn, vwn_t):
    B, _, S = q_t.shape
    tq = tk = FLASH_TQ
    assert tk == SEL_TK and tk == WIN_TK and WINDOW == 2 * tk
    kw = kaug.shape[2] // KV_GROUPS
    cols = HPG * tq
    out_spec = pl.BlockSpec((None, tq, WIDTH), lambda b, i: (b, i, 0))
    out_sd = jax.ShapeDtypeStruct((B, S, WIDTH), BF16)
    return pl.pallas_call(
        functools.partial(_flash_kernel, tk=tk),
        grid=(B, S // tq),
        in_specs=[pl.BlockSpec((None, WIDTH, tq), lambda b, i: (b, 0, i)),
                  pl.BlockSpec((None, KV_GROUPS, LANE, tq), lambda b, i: (b, 0, 0, i)),
                  pl.BlockSpec((None, S, KV_GROUPS * kw), lambda b, i: (b, 0, 0)),
                  pl.BlockSpec((None, KV_GROUPS, S // tk, SEL_VROWS, tk), lambda b, i: (b, 0, 0, 0, 0)),
                  pl.BlockSpec((None, S, KVW), lambda b, i: (b, 0, 0)),
                  pl.BlockSpec((None, KV_GROUPS, S // tk, SEL_VROWS, tk), lambda b, i: (b, 0, 0, 0, 0)),
                  pl.BlockSpec((2, tk, cols), lambda b, i: (0, 0, 0))],
        out_specs=[out_spec, out_spec],
        out_shape=[out_sd, out_sd],
        scratch_shapes=[pltpu.VMEM((KV_GROUPS, kw, cols), BF16),
                        pltpu.VMEM((2, KV_GROUPS, tk, cols), F32),
                        pltpu.VMEM((KV_GROUPS, 1, cols), F32),
                        pltpu.VMEM((KV_GROUPS, SEL_VROWS, cols), F32),
                        pltpu.VMEM((KV_GROUPS, WINDOW // tk + 1, tk, cols), F32),
                        pltpu.VMEM((KV_GROUPS, (WINDOW // tk + 1) * tk, cols), BF16)],
        compiler_params=_cparams(("parallel", "parallel")),
        name="flash",
    )(q_t, bias_t, kaug, vsl_t, kwn, vwn_t, _window_bias(tq))


def _merge_kernel(ydn_ref, oc_ref, os_ref, ow_ref, small_ref, mgd_ref, mgn_ref, x_ref,
                  wdn_ref, wnsa_ref, wout_ref, g_ref, b_ref, o_ref):
    gates = _sigmoid(small_ref[...])
    cols = []
    for h in range(HEADS):
        sl = slice(h * LANE, (h + 1) * LANE)
        gc = gates[:, SMALL_GATE + h:SMALL_GATE + h + 1]
        gs = gates[:, SMALL_GATE + HEADS + h:SMALL_GATE + HEADS + h + 1]
        gw = gates[:, SMALL_GATE + 2 * HEADS + h:SMALL_GATE + 2 * HEADS + h + 1]
        cols.append((gc * oc_ref[:, sl] + gs * os_ref[:, sl] + gw * ow_ref[:, sl]).astype(BF16))
    y_nsa = jnp.concatenate(cols, axis=1)
    y = (_sigmoid(mgd_ref[...].astype(F32)) * _dot(ydn_ref[...], wdn_ref[...])
         + _sigmoid(mgn_ref[...].astype(F32)) * _dot(y_nsa, wnsa_ref[...]))
    z = DEEPNORM_ALPHA * x_ref[...] + _dot(y.astype(BF16), wout_ref[...])
    o_ref[...] = _layer_norm(z, g_ref[...], b_ref[...])


def _merge(y_dn2, oc2, os2, ow2, small2, h, x2, wdn, wnsa, wout, g, b):
    T, D = x2.shape
    tm = PROJ_TM
    row = pl.BlockSpec((tm, WIDTH), lambda i: (i, 0))
    full = lambda shape: pl.BlockSpec(shape, lambda i: (0, 0))
    return pl.pallas_call(
        _merge_kernel,
        grid=(T // tm,),
        in_specs=[row, row, row, row,
                  pl.BlockSpec((tm, LANE), lambda i: (i, 0)),
                  pl.BlockSpec((tm, WIDTH), lambda i: (i, COL_MG_DN // WIDTH)),
                  pl.BlockSpec((tm, WIDTH), lambda i: (i, COL_MG_NSA // WIDTH)),
                  pl.BlockSpec((tm, D), lambda i: (i, 0)),
                  full((WIDTH, D)), full((WIDTH, D)), full((D, D)), full((1, D)), full((1, D))],
        out_specs=pl.BlockSpec((tm, D), lambda i: (i, 0)),
        out_shape=jax.ShapeDtypeStruct((T, D), F32),
        compiler_params=_cparams(("parallel",)),
        name="merge",
    )(y_dn2, oc2, os2, ow2, small2, h, h, x2, wdn, wnsa, wout, g, b)


FFN_TM = 1024


def _ffn_up_kernel(x_ref, xp_ref, wg_ref, wv_ref, cwg_ref, cwv_ref, bg_ref, bv_ref, o_ref, xs_ref, *, tiles_per_seq):
    i = pl.program_id(1)
    tm = x_ref.shape[0]
    nv = tm // SUBLANE
    n_slab = x_ref.shape[1] // LANE
    for sp in range(SUBLANE):
        for c in range(n_slab):
            xs_ref[c, pl.ds(sp, nv, stride=SUBLANE), :] = x_ref[sp * nv:(sp + 1) * nv, c * LANE:(c + 1) * LANE]
    xb = jnp.concatenate([xs_ref[c] for c in range(n_slab)], axis=1).astype(BF16)
    notfirst = (i % tiles_per_seq != 0).astype(F32)
    xpb = (xp_ref[...] * notfirst).astype(BF16)
    sub = lax.broadcasted_iota(jnp.int32, (SUBLANE, 1), 0)

    def conv(w_ref, cw_ref, b_ref):
        u = _dot(xb, w_ref[...])
        p = _dot(xpb, w_ref[...])

        def wrap(k):
            blk = u[(nv - k) * SUBLANE:(nv - k + 1) * SUBLANE, :]
            return jnp.where(sub == 0, p[SUBLANE - k:SUBLANE - k + 1, :], pltpu.roll(blk, 1, 0))

        w1, w2 = wrap(1), wrap(2)
        u1 = jnp.concatenate([w1, u[0:(nv - 1) * SUBLANE, :]], axis=0)
        u2 = jnp.concatenate([w2, w1, u[0:(nv - 2) * SUBLANE, :]], axis=0)
        return cw_ref[0:1, :] * u2 + cw_ref[1:2, :] * u1 + cw_ref[2:3, :] * u + b_ref[...]

    o_ref[...] = (_silu(conv(wg_ref, cwg_ref, bg_ref)) * conv(wv_ref, cwv_ref, bv_ref)).astype(BF16)


def _ffn_up(x1, w_up, conv_w, conv_b, seq):
    T, D = x1.shape
    tm = FFN_TM
    nj = 2
    tn = D_FF // nj
    return pl.pallas_call(
        functools.partial(_ffn_up_kernel, tiles_per_seq=seq // tm),
        grid=(nj, T // tm),
        in_specs=[pl.BlockSpec((tm, D), lambda j, i: (i, 0)),
                  pl.BlockSpec((SUBLANE, D), lambda j, i: (jnp.maximum(i * (tm // SUBLANE) - 1, 0), 0)),
                  pl.BlockSpec((D, tn), lambda j, i: (0, j)),
                  pl.BlockSpec((D, tn), lambda j, i: (0, j + nj)),
                  pl.BlockSpec((FFN_CONV, tn), lambda j, i: (0, j)),
                  pl.BlockSpec((FFN_CONV, tn), lambda j, i: (0, j + nj)),
                  pl.BlockSpec((1, tn), lambda j, i: (0, j)),
                  pl.BlockSpec((1, tn), lambda j, i: (0, j + nj))],
        out_specs=pl.BlockSpec((tm, tn), lambda j, i: (i, j)),
        out_shape=jax.ShapeDtypeStruct((T, D_FF), BF16),
        scratch_shapes=[pltpu.VMEM((D // LANE, tm, LANE), F32)],
        compiler_params=_cparams(("parallel", "parallel")),
        name="ffn_up",
    )(x1, x1, w_up, w_up, conv_w, conv_w, conv_b, conv_b)


def _ffn_down_kernel(a_ref, w_ref, x_ref, g_ref, b_ref, o_ref, fs_ref):
    tm, d = x_ref.shape
    nv = tm // SUBLANE
    n_slab = d // LANE
    f = _dot(a_ref[...], w_ref[...])
    for c in range(n_slab):
        fs_ref[c] = f[:, c * LANE:(c + 1) * LANE]
    for sp in range(SUBLANE):
        rows = slice(sp * nv, (sp + 1) * nv)
        f_nat = jnp.concatenate([fs_ref[c, pl.ds(sp, nv, stride=SUBLANE), :] for c in range(n_slab)], axis=1)
        z = DEEPNORM_ALPHA * x_ref[rows, :] + f_nat
        o_ref[rows, :] = _layer_norm(z, g_ref[...], b_ref[...])


def _ffn_down(a, w_down, x1, g, b):
    T, D = x1.shape
    tm = FFN_TM
    full = lambda shape: pl.BlockSpec(shape, lambda i: (0, 0))
    return pl.pallas_call(
        _ffn_down_kernel,
        grid=(T // tm,),
        in_specs=[pl.BlockSpec((tm, D_FF), lambda i: (i, 0)), full((D_FF, D)),
                  pl.BlockSpec((tm, D), lambda i: (i, 0)), full((1, D)), full((1, D))],
        out_specs=pl.BlockSpec((tm, D), lambda i: (i, 0)),
        out_shape=jax.ShapeDtypeStruct((T, D), F32),
        scratch_shapes=[pltpu.VMEM((D // LANE, tm, LANE), F32)],
        compiler_params=_cparams(("parallel",)),
        name="ffn_down",
    )(a, w_down, x1, g, b)


def _split_w_in_kernel(wt_ref, a_ref, b_ref, c_ref):
    o_dz = 3 * WIDTH
    o_db = 4 * WIDTH
    o_nq = o_db + 2 * HEADS
    o_kc = o_nq + WIDTH
    o_ksl = o_kc + 2 * KVW
    o_gate = o_ksl + 4 * KVW
    o_mg = o_gate + 3 * HEADS
    slab = wt_ref.shape[1]

    def put(dst, o0, c0, c1):
        dst[:, o0:o0 + (c1 - c0)] = wt_ref[c0:c1, :].T.astype(BF16)

    put(a_ref, 0, 0, o_dz)
    small = jnp.concatenate([wt_ref[o_db:o_nq, :], wt_ref[o_gate:o_mg, :],
                             jnp.zeros((LANE - 5 * HEADS, slab), F32)], axis=0)
    a_ref[:, o_dz:NA] = small.T.astype(BF16)
    put(b_ref, 0, o_nq, o_kc)
    put(b_ref, WIDTH, o_ksl, o_gate)
    put(c_ref, 0, o_dz, o_db)
    put(c_ref, WIDTH, o_mg, o_mg + 2 * WIDTH)
    put(c_ref, 3 * WIDTH, o_kc, o_ksl)


def _split_w_in(w, layer):
    _, d, n_in = w.shape
    slab = 2 * LANE
    return pl.pallas_call(
        _split_w_in_kernel,
        grid=(d // slab,),
        in_specs=[pl.BlockSpec((None, n_in, slab), lambda i: (layer, 0, i))],
        out_specs=[pl.BlockSpec((slab, NA), lambda i: (i, 0)), pl.BlockSpec((slab, NB), lambda i: (i, 0)),
                   pl.BlockSpec((slab, NC_COLS), lambda i: (i, 0))],
        out_shape=[jax.ShapeDtypeStruct((d, NA), BF16), jax.ShapeDtypeStruct((d, NB), BF16),
                   jax.ShapeDtypeStruct((d, NC_COLS), BF16)],
        compiler_params=_cparams(("parallel",)),
        name="split_w_in",
    )(jnp.swapaxes(w, 1, 2))


def _rope_tables(pos):
    pos = np.asarray(pos)
    inv_freq = np.float32(ROPE_THETA) ** (-(np.arange(ROPE_HALF, dtype=np.float32) / np.float32(ROPE_HALF)))
    ang = pos.astype(np.float32)[:, None] * inv_freq.astype(np.float32)
    cos = np.cos(ang.astype(np.float64)).astype(np.float32)
    sin = np.sin(ang.astype(np.float64)).astype(np.float32)
    p = pos.shape[0]
    cosf = np.concatenate([cos, cos, np.ones((p, LANE - ROPE_DIM), np.float32)], axis=1)
    sinf = np.concatenate([-sin, sin, np.zeros((p, LANE - ROPE_DIM), np.float32)], axis=1)
    return jnp.asarray(cosf), jnp.asarray(sinf)


def _overlap_table(nc, n_sel):
    n = np.arange(nc)[None, :]
    j = np.arange(LANE)[:, None]
    ov = (n * CMP_STRIDE <= j * SEL_LEN + SEL_LEN - 1) & (n * CMP_STRIDE + CMP_LEN - 1 >= j * SEL_LEN)
    ov = ov & (j < n_sel) & (n < nc - 1)
    return jnp.asarray(ov.astype(np.float32)).astype(BF16)


def _lane_row(vals, offset):
    return jnp.zeros((1, LANE), F32).at[0, offset:offset + vals.shape[0]].set(vals.astype(F32))


def kernel(x, w_in, dn_conv_w, dn_a_log, dn_dt_bias, dn_norm_w, cmp_pos_k, cmp_pos_v, cmp_k_w1, cmp_k_w2,
           cmp_v_w1, cmp_v_w2, w_branch_dn, w_branch_nsa, w_out, ln1_g, ln1_b, ffn_w_up, ffn_conv_w,
           ffn_conv_b, ffn_w_down, ln2_g, ln2_b):
    B, S, D = x.shape
    T = B * S
    nc = S // CMP_STRIDE
    n_sel = S // SEL_LEN
    assert D == WIDTH and S % FFN_TM == 0 and S % PROJ_TM == 0 and n_sel <= LANE and n_sel % SUBLANE == 0
    cos_t, sin_t = _rope_tables(np.arange(S))
    cos_c, sin_c = _rope_tables(np.arange(nc) * CMP_STRIDE + (CMP_LEN - 1))
    overlap = _overlap_table(nc, n_sel)
    x2 = x.reshape(T, D)
    for layer in range(w_in.shape[0]):
        w_a, w_b, w_c = _split_w_in(w_in, layer)
        x3 = x2.reshape(B, S, D)
        h3, kv_cmp = _proj_rest(x3, w_c)

        q, k, v, bg, small = _proj_gdn(x3, w_a, dn_conv_w[layer], _lane_row(dn_a_log[layer], SMALL_DECAY),
                                       _lane_row(dn_dt_bias[layer], SMALL_DECAY))
        y_dn = _gdn(q, k, v, bg, h3, dn_norm_w[layer].reshape(1, LANE))

        q_t, kaug, vsl_t, kwn_r, vwn_t = _proj_nsa(x3, w_b, cos_t, sin_t)
        kc, vc = _compress(kv_cmp, cmp_pos_k[layer], cmp_pos_v[layer],
                           cmp_k_w1[layer].astype(BF16), cmp_k_w2[layer].astype(BF16),
                           cmp_v_w1[layer].astype(BF16), cmp_v_w2[layer].astype(BF16), cos_c, sin_c)
        o_cmp, bias = _cmp_attn(q_t, kc, vc, overlap)
        o_sel, o_win = _flash(q_t, kaug, vsl_t, bias, kwn_r, vwn_t)

        x1 = _merge(y_dn.reshape(T, WIDTH), o_cmp.reshape(T, WIDTH), o_sel.reshape(T, WIDTH),
                    o_win.reshape(T, WIDTH), small.reshape(T, LANE), h3.reshape(T, COL_KC), x2,
                    w_branch_dn[layer].astype(BF16), w_branch_nsa[layer].astype(BF16),
                    w_out[layer].astype(BF16), ln1_g[layer].reshape(1, D), ln1_b[layer].reshape(1, D))
        a = _ffn_up(x1, ffn_w_up[layer].astype(BF16), ffn_conv_w[layer],
                    ffn_conv_b[layer].reshape(1, 2 * D_FF), S)
        x2 = _ffn_down(a, ffn_w_down[layer].astype(BF16), x1, ln2_g[layer].reshape(1, D), ln2_b[layer].reshape(1, D))
    return x2.reshape(B, S, D)
```

```python
import functools
import math

import numpy as np
import jax
import jax.numpy as jnp
from jax import lax
from jax.experimental import pallas as pl
from jax.experimental.pallas import tpu as pltpu

F32 = jnp.float32
BF16 = jnp.bfloat16
HI = lax.Precision.HIGHEST

HEADS = 8
HEAD_DIM = 128
KV_GROUPS = 2
HPG = HEADS // KV_GROUPS
DN_CONV = 4
DN_CHUNK = 64
CMP_LEN = 32
CMP_STRIDE = 16
CMP_HIDDEN = 256
SEL_LEN = 64
SEL_TOPK = 16
WINDOW = 512
ROPE_THETA = 500000.0
ROPE_DIM = HEAD_DIM // 4
ROPE_HALF = ROPE_DIM // 2
D_FF = 2816
FFN_CONV = 3
LN_EPS = 1e-5
RMS_EPS = 1e-6
DEEPNORM_ALPHA = 2.0 ** 0.25
ATT_SCALE = HEAD_DIM ** -0.5

LANE = 128
SUBLANE = 8
VMEM_LIMIT = 56 * 1024 * 1024

WIDTH = HEADS * HEAD_DIM
KVW = KV_GROUPS * HEAD_DIM
NA = 3 * WIDTH + LANE
SMALL_BETA = 0
SMALL_DECAY = HEADS
SMALL_GATE = 2 * HEADS
NB = WIDTH + 4 * KVW
NC_COLS = 3 * WIDTH + 2 * KVW
COL_DZ = 0
COL_MG_DN = WIDTH
COL_MG_NSA = 2 * WIDTH
COL_KC = 3 * WIDTH

NEG_BIAS = -(2.0 ** 100)


def _cparams(sem):
    return pltpu.CompilerParams(dimension_semantics=sem, vmem_limit_bytes=VMEM_LIMIT)


def _sigmoid(x):
    return 1.0 / (1.0 + jnp.exp(-x))


def _silu(x):
    return x * _sigmoid(x)


def _dot(a, b, precision=None):
    return jnp.dot(a, b, preferred_element_type=F32, precision=precision)


def _dot_nt(a, b):
    return lax.dot_general(a, b, (((1,), (1,)), ((), ())), preferred_element_type=F32)


def _dot_tn(a, b):
    return lax.dot_general(a, b, (((0,), (0,)), ((), ())), preferred_element_type=F32)


def _layer_norm(z, g, b):
    mu = jnp.mean(z, -1, keepdims=True)
    zc = z - mu
    var = jnp.mean(zc * zc, -1, keepdims=True)
    return zc * lax.rsqrt(var + LN_EPS) * g + b


PROJ_TM = 512


def _proj_rest_kernel(x_ref, w_ref, g_ref, c_ref):
    h = _dot(x_ref[...].astype(BF16), w_ref[...])
    g_ref[...] = h[:, 0:COL_KC].astype(BF16)
    c_ref[...] = h[:, COL_KC:NC_COLS]


def _proj_rest(x3, w):
    B, S, D = x3.shape
    tm = PROJ_TM
    return pl.pallas_call(
        _proj_rest_kernel,
        grid=(B, S // tm),
        in_specs=[pl.BlockSpec((None, tm, D), lambda b, i: (b, i, 0)),
                  pl.BlockSpec((D, NC_COLS), lambda b, i: (0, 0))],
        out_specs=[pl.BlockSpec((None, tm, COL_KC), lambda b, i: (b, i, 0)),
                   pl.BlockSpec((None, tm, NC_COLS - COL_KC), lambda b, i: (b, i, 0))],
        out_shape=[jax.ShapeDtypeStruct((B, S, COL_KC), BF16),
                   jax.ShapeDtypeStruct((B, S, NC_COLS - COL_KC), F32)],
        compiler_params=_cparams(("parallel", "parallel")),
        name="proj_rest",
    )(x3, w)


def _proj_gdn_kernel(x_ref, w_ref, cw_ref, alog_ref, dtb_ref, q_ref, k_ref, v_ref, bg_ref, small_ref,
                     xs_ref, ys_ref, carry_ref):
    i = pl.program_id(1)
    ts = x_ref.shape[0]
    nv = ts // SUBLANE
    wq = 3 * WIDTH
    n_slab = x_ref.shape[1] // LANE

    @pl.when(i == 0)
    def _():
        carry_ref[...] = jnp.zeros(carry_ref.shape, F32)

    for sp in range(SUBLANE):
        for c in range(n_slab):
            xs_ref[c, pl.ds(sp, nv, stride=SUBLANE), :] = x_ref[sp * nv:(sp + 1) * nv, c * LANE:(c + 1) * LANE]
    xb = jnp.concatenate([xs_ref[c] for c in range(n_slab)], axis=1).astype(BF16)
    h = _dot(xb, w_ref[...])
    sub = lax.broadcasted_iota(jnp.int32, (SUBLANE, 1), 0)

    def token_order(c, dst_ref, dst_sl, dtype):
        for sp in range(SUBLANE):
            dst_ref[sp * nv:(sp + 1) * nv, dst_sl] = ys_ref[c, pl.ds(sp, nv, stride=SUBLANE), :].astype(dtype)

    for c in range(3 * HEADS):
        sl = slice(c * LANE, (c + 1) * LANE)
        col = h[:, sl]
        wrap = [jnp.where(sub == 0, carry_ref[SUBLANE - k:SUBLANE - k + 1, sl],
                          pltpu.roll(col[(nv - k) * SUBLANE:(nv - k + 1) * SUBLANE, :], 1, 0))
                for k in range(1, DN_CONV)]
        acc = cw_ref[DN_CONV - 1:DN_CONV, sl] * col
        for s in range(1, DN_CONV):
            shifted = jnp.concatenate(wrap[0:s][::-1] + [col[0:(nv - s) * SUBLANE, :]], axis=0)
            acc = acc + cw_ref[DN_CONV - 1 - s:DN_CONV - s, sl] * shifted
        y = _silu(acc)
        if c < 2 * HEADS:
            y = y * lax.rsqrt(jnp.sum(y * y, -1, keepdims=True) + RMS_EPS)
        if c < HEADS:
            y = y * ATT_SCALE
        ys_ref[c] = y
        dst = (q_ref, k_ref, v_ref)[c // HEADS]
        token_order(c, dst, slice((c % HEADS) * LANE, (c % HEADS + 1) * LANE), BF16)
    for k in range(1, DN_CONV):
        carry_ref[SUBLANE - k:SUBLANE - k + 1, :] = h[(nv - k + 1) * SUBLANE - 1:(nv - k + 1) * SUBLANE, 0:wq]
    small = h[:, wq:wq + LANE]
    beta = _sigmoid(small)
    xs = small + dtb_ref[...]
    softplus = jnp.maximum(xs, 0.0) + jnp.log1p(jnp.exp(-jnp.abs(xs)))
    g = -jnp.exp(alog_ref[...]) * softplus
    lane = lax.broadcasted_iota(jnp.int32, small.shape, 1)
    ys_ref[3 * HEADS] = small
    token_order(3 * HEADS, small_ref, slice(0, LANE), F32)
    ys_ref[3 * HEADS + 1] = jnp.where(lane < SMALL_DECAY, beta, g)
    token_order(3 * HEADS + 1, bg_ref, slice(0, LANE), F32)


def _proj_gdn(x3, w_a, conv_w, alog_row, dtb_row):
    B, S, D = x3.shape
    ts = PROJ_TM
    wq = 3 * WIDTH
    out_sd = jax.ShapeDtypeStruct((B, S, WIDTH), BF16)
    small_sd = jax.ShapeDtypeStruct((B, S, LANE), F32)
    wide = pl.BlockSpec((None, ts, WIDTH), lambda b, i: (b, i, 0))
    narrow = pl.BlockSpec((None, ts, LANE), lambda b, i: (b, i, 0))
    return pl.pallas_call(
        _proj_gdn_kernel,
        grid=(B, S // ts),
        in_specs=[
            pl.BlockSpec((None, ts, D), lambda b, i: (b, i, 0)),
            pl.BlockSpec((D, NA), lambda b, i: (0, 0)),
            pl.BlockSpec((DN_CONV, wq), lambda b, i: (0, 0)),
            pl.BlockSpec((1, LANE), lambda b, i: (0, 0)),
            pl.BlockSpec((1, LANE), lambda b, i: (0, 0)),
        ],
        out_specs=[wide, wide, wide, narrow, narrow],
        out_shape=[out_sd, out_sd, out_sd, small_sd, small_sd],
        scratch_shapes=[pltpu.VMEM((D // LANE, ts, LANE), F32),
                        pltpu.VMEM((3 * HEADS + 2, ts, LANE), F32),
                        pltpu.VMEM((SUBLANE, wq), F32)],
        compiler_params=_cparams(("parallel", "arbitrary")),
        name="proj_gdn",
    )(x3, w_a, conv_w, alog_row, dtb_row)


GDN_CHUNKS_PER_STEP = 4
GDN_BATCH_PER_STEP = 2


GDN_QUAD = 4


def _gdn_kernel(q_ref, k_ref, v_ref, bg_ref, z_ref, nw_ref, y_ref, state_ref):
    n = pl.program_id(1)

    @pl.when(n == 0)
    def _():
        state_ref[...] = jnp.zeros(state_ref.shape, F32)

    C = DN_CHUNK
    QW = GDN_QUAD * C
    QL = GDN_QUAD * LANE
    shift_c = int(math.log2(C))
    r64 = lax.broadcasted_iota(jnp.int32, (C, C), 0)
    c64 = lax.broadcasted_iota(jnp.int32, (C, C), 1)
    tril = jnp.where(r64 >= c64, 1.0, 0.0).astype(F32)
    triu = jnp.where(r64 <= c64, 1.0, 0.0).astype(F32)
    rr = lax.broadcasted_iota(jnp.int32, (C, QW), 0)
    ll = lax.broadcasted_iota(jnp.int32, (C, QW), 1)
    jj = ll & (C - 1)
    hb = ll >> shift_c
    incl_t = rr >= jj
    strict_t = rr > jj
    eye_t = jnp.where(rr == jj, 1.0, 0.0).astype(F32)
    bd_c = jnp.where((lax.broadcasted_iota(jnp.int32, (QW, QW), 0) >> shift_c)
                     == (lax.broadcasted_iota(jnp.int32, (QW, QW), 1) >> shift_c), 1.0, 0.0).astype(BF16)
    bd_l = jnp.where((lax.broadcasted_iota(jnp.int32, (QW, QL), 0) >> shift_c)
                     == (lax.broadcasted_iota(jnp.int32, (QW, QL), 1) >> int(math.log2(LANE))), 1.0, 0.0).astype(BF16)
    zeros_rhs = jnp.zeros((C, 2 * LANE), BF16)
    nw = nw_ref[...]
    nb = q_ref.shape[0]
    quads = [(bb, c, a) for bb in range(nb) for c in range(GDN_CHUNKS_PER_STEP) for a in range(HEADS // GDN_QUAD)]
    gc_cols, gc_rows, bgs = {}, {}, {}
    for bb in range(nb):
        bg = bg_ref[bb]
        bg_t = bg.T
        bgs[bb] = bg
        for c in range(GDN_CHUNKS_PER_STEP):
            rs = slice(c * C, (c + 1) * C)
            gc_cols[bb, c] = _dot(tril, bg[rs, :], HI)
            gc_rows[bb, c] = _dot(bg_t[SMALL_DECAY:SMALL_DECAY + HEADS, rs], triu, HI)
    decay, lower, qk, rhs_bd, qd, kd, eglast = {}, {}, {}, {}, {}, {}, {}
    for u_ in quads:
        bb, c, a = u_
        rs = slice(c * C, (c + 1) * C)
        h0 = a * GDN_QUAD
        gcols = [gc_cols[bb, c][:, SMALL_DECAY + h0 + hh:SMALL_DECAY + h0 + hh + 1] for hh in range(GDN_QUAD)]
        gcol_cat = jnp.broadcast_to(gcols[0], (C, QW))
        for hh in range(1, GDN_QUAD):
            gcol_cat = jnp.where(hb == hh, gcols[hh], gcol_cat)
        grow_cat = jnp.concatenate([gc_rows[bb, c][h0 + hh:h0 + hh + 1, :] for hh in range(GDN_QUAD)], axis=1)
        decay[u_] = jnp.exp(jnp.where(incl_t, gcol_cat - grow_cat, -jnp.inf))
        kq = k_ref[bb, rs, a * QL:(a + 1) * QL].astype(F32)
        qq = q_ref[bb, rs, a * QL:(a + 1) * QL].astype(F32)
        kbs, rows_bd = [], []
        for hh in range(GDN_QUAD):
            h = h0 + hh
            ls = slice(hh * LANE, (hh + 1) * LANE)
            beta = bgs[bb][rs, SMALL_BETA + h:SMALL_BETA + h + 1]
            eg = jnp.exp(gcols[hh])
            glast = gcols[hh][C - 1:C, :]
            kb = kq[:, ls] * beta
            vb = v_ref[bb, rs, h * LANE:(h + 1) * LANE].astype(F32) * beta
            kbs.append(kb)
            rhs_h = jnp.concatenate([vb, kb * eg], axis=1).astype(BF16)
            rows_bd.append(jnp.concatenate([zeros_rhs] * hh + [rhs_h] + [zeros_rhs] * (GDN_QUAD - 1 - hh), axis=1))
            qd[bb, c, h] = (qq[:, ls] * eg).astype(BF16)
            kd[bb, c, h] = (kq[:, ls] * jnp.exp(glast - gcols[hh])).astype(BF16)
            eglast[bb, c, h] = jnp.exp(glast)
        rhs_bd[u_] = jnp.concatenate(rows_bd, axis=0)
        k_bd = jnp.concatenate([kq.astype(BF16)] * GDN_QUAD, axis=0) * bd_l
        lhs = jnp.concatenate([jnp.concatenate(kbs, axis=1), qq], axis=0).astype(BF16)
        aq = _dot_nt(lhs, k_bd)
        lower[u_] = jnp.where(strict_t, aq[0:C] * decay[u_], 0.0)
        qk[u_] = (aq[C:2 * C] * decay[u_]).astype(BF16)

    def block_diag(y_b):
        return jnp.concatenate([y_b] * GDN_QUAD, axis=0) * bd_c

    tinv = {u_: eye_t - lower[u_] for u_ in quads}
    y_b = {u_: lower[u_].astype(BF16) for u_ in quads}
    ypow = {u_: _dot(y_b[u_], block_diag(y_b[u_])) for u_ in quads}
    for it in range(5):
        for u_ in quads:
            yb = ypow[u_].astype(BF16)
            if it < 4:
                r = _dot(jnp.concatenate([tinv[u_].astype(BF16), yb], axis=0), block_diag(yb))
                tinv[u_] = tinv[u_] + r[0:C]
                ypow[u_] = r[C:2 * C]
            else:
                tinv[u_] = tinv[u_] + _dot(tinv[u_].astype(BF16), block_diag(yb))
    uw = {u_: _dot(tinv[u_].astype(BF16), rhs_bd[u_]) for u_ in quads}
    zeros_v = jnp.zeros((C, LANE), BF16)
    chains = [(bb, h) for bb in range(nb) for h in range(HEADS)]
    for c in range(GDN_CHUNKS_PER_STEP):
        rs = slice(c * C, (c + 1) * C)
        states = {s_: state_ref[s_[0] * HEADS + s_[1]] for s_ in chains}
        ws_qs = {}
        for s_ in chains:
            bb, h = s_
            a, hh = divmod(h, GDN_QUAD)
            w_h = uw[bb, c, a][:, hh * 2 * LANE + LANE:(hh + 1) * 2 * LANE]
            ws_qs[s_] = _dot(jnp.concatenate([w_h.astype(BF16), qd[bb, c, h]], axis=0), states[s_].astype(BF16))
        v_new = {}
        for s_ in chains:
            bb, h = s_
            a, hh = divmod(h, GDN_QUAD)
            v_new[s_] = (uw[bb, c, a][:, hh * 2 * LANE:hh * 2 * LANE + LANE] - ws_qs[s_][0:C]).astype(BF16)
        outs = {}
        for s_ in chains:
            bb, h = s_
            a, hh = divmod(h, GDN_QUAD)
            v_pad = jnp.concatenate([zeros_v] * hh + [v_new[s_]] + [zeros_v] * (GDN_QUAD - 1 - hh), axis=0)
            outs[s_] = ws_qs[s_][C:2 * C] + _dot(qk[bb, c, a], v_pad)
            state_ref[bb * HEADS + h] = states[s_] * eglast[bb, c, h] + _dot_tn(kd[bb, c, h], v_new[s_])
        for s_ in chains:
            bb, h = s_
            ls = slice(h * LANE, (h + 1) * LANE)
            o = outs[s_]
            o = o * lax.rsqrt(jnp.mean(o * o, -1, keepdims=True) + RMS_EPS)
            y_ref[bb, rs, ls] = (o * nw * _silu(z_ref[bb, rs, ls].astype(F32))).astype(BF16)


def _gdn(q, k, v, bg, h3, norm_w):
    B, S, _ = q.shape
    rows = GDN_CHUNKS_PER_STEP * DN_CHUNK
    nb = GDN_BATCH_PER_STEP if B % GDN_BATCH_PER_STEP == 0 else 1
    tile = pl.BlockSpec((nb, rows, WIDTH), lambda b, n: (b, n, 0))
    return pl.pallas_call(
        _gdn_kernel,
        grid=(B // nb, S // rows),
        in_specs=[tile, tile, tile,
                  pl.BlockSpec((nb, rows, LANE), lambda b, n: (b, n, 0)),
                  pl.BlockSpec((nb, rows, WIDTH), lambda b, n: (b, n, COL_DZ // WIDTH)),
                  pl.BlockSpec((1, LANE), lambda b, n: (0, 0))],
        out_specs=tile,
        out_shape=jax.ShapeDtypeStruct((B, S, WIDTH), BF16),
        scratch_shapes=[pltpu.VMEM((nb * HEADS, HEAD_DIM, HEAD_DIM), F32)],
        compiler_params=_cparams(("parallel", "arbitrary")),
        name="gdn",
    )(q, k, v, bg, h3, norm_w)


def _rope(t, cosf, sinf, lo):
    partner = jnp.where(lo, pltpu.roll(t, LANE - ROPE_HALF, 1), pltpu.roll(t, ROPE_HALF, 1))
    return t * cosf + partner * sinf


SEL_TK = 512
WIN_TK = 256
SEL_VROWS = LANE + 16
LOG2E = math.log2(math.e)


def _proj_nsa_kernel(x_ref, w_ref, cos_ref, sin_ref, qt_o, kaug_o, vslt_o, kwn_o, vwnt_o):
    i = pl.program_id(1)
    ts = x_ref.shape[0]
    h = _dot(x_ref[...].astype(BF16), w_ref[...])
    c_ksl, c_vsl, c_kwn, c_vwn = WIDTH, WIDTH + KVW, WIDTH + 2 * KVW, WIDTH + 3 * KVW
    cosf = cos_ref[...]
    sinf = sin_ref[...]
    lane = lax.broadcasted_iota(jnp.int32, cosf.shape, 1)
    lo = lane < ROPE_HALF
    for hd in range(HEADS):
        sl = slice(hd * LANE, (hd + 1) * LANE)
        qt_o[sl, :] = (_rope(h[:, sl], cosf, sinf, lo) * (ATT_SCALE * LOG2E)).T.astype(BF16)
    tpos = i * ts + lax.broadcasted_iota(jnp.int32, cosf.shape, 0)
    onehot = jnp.where((tpos >> int(math.log2(SEL_LEN))) == lane, 1.0, 0.0).astype(BF16)
    for g in range(KV_GROUPS):
        sl = slice(g * LANE, (g + 1) * LANE)
        off = g * LANE
        kaug_o[:, 2 * g * LANE:(2 * g + 1) * LANE] = _rope(h[:, c_ksl + off:c_ksl + off + LANE], cosf, sinf, lo).astype(BF16)
        kaug_o[:, (2 * g + 1) * LANE:(2 * g + 2) * LANE] = onehot
        kwn_o[:, sl] = _rope(h[:, c_kwn + off:c_kwn + off + LANE], cosf, sinf, lo).astype(BF16)
        vs_t = h[:, c_vsl + off:c_vsl + off + LANE].T.astype(BF16)
        for c in range(ts // SEL_TK):
            ones_rows = jnp.where(lax.broadcasted_iota(jnp.int32, (SEL_VROWS - LANE, SEL_TK), 0) == 0, 1.0, 0.0)
            vslt_o[g, c] = jnp.concatenate([vs_t[:, c * SEL_TK:(c + 1) * SEL_TK], ones_rows.astype(BF16)], axis=0)
        vw_t = h[:, c_vwn + off:c_vwn + off + LANE].T.astype(BF16)
        for c in range(ts // WIN_TK):
            ones_rows = jnp.where(lax.broadcasted_iota(jnp.int32, (SEL_VROWS - LANE, WIN_TK), 0) == 0, 1.0, 0.0)
            vwnt_o[g, c] = jnp.concatenate([vw_t[:, c * WIN_TK:(c + 1) * WIN_TK], ones_rows.astype(BF16)], axis=0)


def _proj_nsa(x3, w_b, cos_tab, sin_tab):
    B, S, D = x3.shape
    ts = PROJ_TM
    tab = pl.BlockSpec((ts, LANE), lambda b, i: (i, 0))
    return pl.pallas_call(
        _proj_nsa_kernel,
        grid=(B, S // ts),
        in_specs=[pl.BlockSpec((None, ts, D), lambda b, i: (b, i, 0)),
                  pl.BlockSpec((D, NB), lambda b, i: (0, 0)), tab, tab],
        out_specs=[pl.BlockSpec((None, WIDTH, ts), lambda b, i: (b, 0, i)),
                   pl.BlockSpec((None, ts, 2 * KVW), lambda b, i: (b, i, 0)),
                   pl.BlockSpec((None, KV_GROUPS, ts // SEL_TK, SEL_VROWS, SEL_TK), lambda b, i: (b, 0, i, 0, 0)),
                   pl.BlockSpec((None, ts, KVW), lambda b, i: (b, i, 0)),
                   pl.BlockSpec((None, KV_GROUPS, ts // WIN_TK, SEL_VROWS, WIN_TK), lambda b, i: (b, 0, i, 0, 0))],
        out_shape=[jax.ShapeDtypeStruct((B, WIDTH, S), BF16),
                   jax.ShapeDtypeStruct((B, S, 2 * KVW), BF16),
                   jax.ShapeDtypeStruct((B, KV_GROUPS, S // SEL_TK, SEL_VROWS, SEL_TK), BF16),
                   jax.ShapeDtypeStruct((B, S, KVW), BF16),
                   jax.ShapeDtypeStruct((B, KV_GROUPS, S // WIN_TK, SEL_VROWS, WIN_TK), BF16)],
        compiler_params=_cparams(("parallel", "parallel")),
        name="proj_nsa",
    )(x3, w_b, cos_tab, sin_tab)


def _compress_one(t_ref, pe_ref, w1_ref, w2_ref, nc):
    half = CMP_LEN // 2
    y1 = jnp.zeros((nc, CMP_HIDDEN), F32)
    y2 = jnp.zeros((nc, CMP_HIDDEN), F32)
    for l in range(half):
        x = t_ref[pl.ds(l, nc, stride=CMP_STRIDE), :]
        y1 = y1 + _dot((x + pe_ref[l:l + 1, :]).astype(BF16), w1_ref[l * LANE:(l + 1) * LANE, :])
        y2 = y2 + _dot((x + pe_ref[half + l:half + l + 1, :]).astype(BF16),
                       w1_ref[(half + l) * LANE:(half + l + 1) * LANE, :])
    y = y1 + pltpu.roll(y2, nc - 1, 0)
    return _dot(_silu(y).astype(BF16), w2_ref[...])


def _compress_kernel(kc_ref, vc_ref, pek_ref, pev_ref, w1k_ref, w2k_ref, w1v_ref, w2v_ref,
                     cos_ref, sin_ref, kc_o, vc_o):
    nc = kc_o.shape[0]
    kc = _compress_one(kc_ref, pek_ref, w1k_ref, w2k_ref, nc)
    lo = lax.broadcasted_iota(jnp.int32, kc.shape, 1) < ROPE_HALF
    kc_o[...] = _rope(kc, cos_ref[...], sin_ref[...], lo).astype(BF16)
    vc_o[...] = _compress_one(vc_ref, pev_ref, w1v_ref, w2v_ref, nc).astype(BF16)


def _compress(h3, pek, pev, w1k, w2k, w1v, w2v, cos_c, sin_c):
    B, S, _ = h3.shape
    nc = S // CMP_STRIDE
    full = lambda shape: pl.BlockSpec(shape, lambda b, g: tuple(0 for _ in shape))
    out_spec = pl.BlockSpec((None, None, nc, LANE), lambda b, g: (b, g, 0, 0))
    out_sd = jax.ShapeDtypeStruct((B, KV_GROUPS, nc, LANE), BF16)
    cin = CMP_LEN * HEAD_DIM
    return pl.pallas_call(
        _compress_kernel,
        grid=(B, KV_GROUPS),
        in_specs=[pl.BlockSpec((None, S, LANE), lambda b, g: (b, 0, g)),
                  pl.BlockSpec((None, S, LANE), lambda b, g: (b, 0, KV_GROUPS + g)),
                  full((CMP_LEN, LANE)), full((CMP_LEN, LANE)),
                  full((cin, CMP_HIDDEN)), full((CMP_HIDDEN, LANE)),
                  full((cin, CMP_HIDDEN)), full((CMP_HIDDEN, LANE)),
                  full((nc, LANE)), full((nc, LANE))],
        out_specs=[out_spec, out_spec],
        out_shape=[out_sd, out_sd],
        compiler_params=_cparams(("parallel", "parallel")),
        name="compress",
    )(h3, h3, pek, pev, w1k, w2k, w1v, w2v, cos_c, sin_c)


def _cmp_attn_kernel(q_ref, kc_ref, vc_ref, ovt_ref, o_ref, bias_ref, st_ref, cnt_ref, psel_ref, *, n_sel):
    i = pl.program_id(2)
    tq = q_ref.shape[1]
    cols = HPG * tq
    nc = kc_ref.shape[0]
    t0 = i * tq
    q_g = jnp.concatenate([q_ref[hh * LANE:(hh + 1) * LANE, :] for hh in range(HPG)], axis=1)
    tpos = t0 + (lax.broadcasted_iota(jnp.int32, (1, cols), 1) & (tq - 1))

    def attend(rows):
        cmp_end = lax.broadcasted_iota(jnp.int32, (rows, 1), 0) * CMP_STRIDE + (CMP_LEN - 1)
        s = jnp.where(cmp_end <= tpos, _dot(kc_ref[0:rows, :], q_g), -jnp.inf)
        m = jnp.max(s, axis=0, keepdims=True)
        m = jnp.where(m > -jnp.inf, m, 0.0)
        p = jnp.exp2(s - m)
        p = p / jnp.maximum(jnp.sum(p, axis=0, keepdims=True), jnp.finfo(F32).tiny)
        vct = vc_ref[0:rows, :].astype(F32).T.astype(BF16)
        _store_heads(o_ref, _dot(vct, p.astype(BF16)), 0, tq)
        psum = p[:, 0:tq]
        for hh in range(1, HPG):
            psum = psum + p[:, hh * tq:(hh + 1) * tq]
        ovt = ovt_ref[:, 0:rows]
        p_hi = psum.astype(BF16)
        r1 = psum - p_hi.astype(F32)
        p_mid = r1.astype(BF16)
        p_lo = (r1 - p_mid.astype(F32)).astype(BF16)
        psel_ref[...] = _dot(ovt, p_hi) + (_dot(ovt, p_mid) + _dot(ovt, p_lo))

    n_part = 4
    part = nc // n_part
    n_live = (t0 + tq - CMP_LEN) // CMP_STRIDE + 1
    branch = jnp.clip((n_live - 1) // part, 0, n_part - 1)
    for r in range(n_part):
        pl.when(branch == r)(functools.partial(attend, (r + 1) * part))
    p_sel = psel_ref[...]
    jb = lax.broadcasted_iota(jnp.int32, (LANE, tq), 0)
    cur = (t0 + lax.broadcasted_iota(jnp.int32, (1, tq), 1)) >> int(math.log2(SEL_LEN))
    valid = jb <= cur
    forced = valid & ((jb == 0) | (jb == cur) | (jb == cur - 1))
    st = jnp.where(forced, jnp.inf, jnp.where(valid, p_sel, -jnp.inf))
    n_grp = n_sel // SUBLANE
    st_ref[...] = st
    cnt_ref[...] = jnp.zeros(cnt_ref.shape, F32)
    jrow = lax.broadcasted_iota(jnp.int32, (SUBLANE, tq), 0)
    cur_max = (t0 + tq - 1) >> int(math.log2(SEL_LEN))
    for gb in range(n_grp):
        @pl.when(cur_max >= gb * SUBLANE)
        def _():
            grp = [st_ref[g * SUBLANE:(g + 1) * SUBLANE, :] for g in range(n_grp)]
            cnt = [cnt_ref[g * SUBLANE:(g + 1) * SUBLANE, :] for g in range(n_grp)]
            for rb_ in range(SUBLANE):
                rb = grp[gb][rb_:rb_ + 1, :]
                for g in range(n_grp):
                    if g < gb:
                        inc = jnp.where(rb > grp[g], 1.0, 0.0)
                    elif g > gb:
                        inc = jnp.where(rb >= grp[g], 1.0, 0.0)
                    else:
                        inc = jnp.where(jrow > rb_, jnp.where(rb >= grp[g], 1.0, 0.0), jnp.where(rb > grp[g], 1.0, 0.0))
                    cnt[g] = cnt[g] + inc
            for g in range(n_grp):
                cnt_ref[g * SUBLANE:(g + 1) * SUBLANE, :] = cnt[g]

    keep = (cnt_ref[...] < float(SEL_TOPK)) & (st_ref[0:n_sel, :] > -jnp.inf)
    bias = jnp.where(keep, 0.0, NEG_BIAS)
    if n_sel < LANE:
        bias = jnp.concatenate([bias, jnp.zeros((LANE - n_sel, tq), F32)], axis=0)
    bias_ref[...] = bias.astype(BF16)


def _cmp_attn(q_t, kc, vc, overlap_t):
    B, _, S = q_t.shape
    nc = kc.shape[2]
    n_sel = S // SEL_LEN
    tq = CMP_TQ
    gw = HPG * LANE
    return pl.pallas_call(
        functools.partial(_cmp_attn_kernel, n_sel=n_sel),
        grid=(B, KV_GROUPS, S // tq),
        in_specs=[pl.BlockSpec((None, gw, tq), lambda b, g, i: (b, g, i)),
                  pl.BlockSpec((None, None, nc, LANE), lambda b, g, i: (b, g, 0, 0)),
                  pl.BlockSpec((None, None, nc, LANE), lambda b, g, i: (b, g, 0, 0)),
                  pl.BlockSpec((LANE, nc), lambda b, g, i: (0, 0))],
        out_specs=[pl.BlockSpec((None, tq, gw), lambda b, g, i: (b, i, g)),
                   pl.BlockSpec((None, None, LANE, tq), lambda b, g, i: (b, g, 0, i))],
        out_shape=[jax.ShapeDtypeStruct((B, S, WIDTH), BF16),
                   jax.ShapeDtypeStruct((B, KV_GROUPS, LANE, S), BF16)],
        scratch_shapes=[pltpu.VMEM((LANE, tq), F32), pltpu.VMEM((n_sel, tq), F32), pltpu.VMEM((LANE, tq), F32)],
        compiler_params=_cparams(("parallel", "parallel", "parallel")),
        name="cmp_attn",
    )(q_t, kc, vc, overlap_t)


FLASH_TQ = 256
CMP_TQ = 1024


def _store_heads(o_ref, out_t, g, tq):
    for hh in range(HPG):
        h = g * HPG + hh
        o_ref[:, h * LANE:(h + 1) * LANE] = out_t[:, hh * tq:(hh + 1) * tq].T.astype(BF16)


def _flash_kernel(q_ref, b_ref, k_ref, vt_ref, kw_ref, vwt_ref, wb_ref, o_ref, ow_ref,
                  qs_ref, s_ref, m_ref, acc_ref, sw_ref, pw_ref, *, tk):
    i = pl.program_id(1)
    tq = q_ref.shape[1]
    cols = HPG * tq
    t0 = i * tq
    kw = qs_ref.shape[1]
    for g in range(KV_GROUPS):
        for hh in range(HPG):
            h = g * HPG + hh
            qs_ref[g, 0:LANE, hh * tq:(hh + 1) * tq] = q_ref[h * LANE:(h + 1) * LANE, :]
            qs_ref[g, LANE:2 * LANE, hh * tq:(hh + 1) * tq] = b_ref[g]
    m_ref[...] = jnp.full(m_ref.shape, -jnp.inf, F32)
    acc_ref[...] = jnp.zeros(acc_ref.shape, F32)
    qpos = t0 + (lax.broadcasted_iota(jnp.int32, (1, cols), 1) & (tq - 1))

    def scores(j, slot):
        k0 = pl.multiple_of(j * tk, tk)
        for g in range(KV_GROUPS):
            s_ref[slot, g] = _dot(k_ref[pl.ds(k0, tk), g * kw:(g + 1) * kw], qs_ref[g])

    def update(j, slot, masked):
        if masked:
            mask = j * tk + lax.broadcasted_iota(jnp.int32, (tk, 1), 0) <= qpos
        for g in range(KV_GROUPS):
            s = s_ref[slot, g]
            if masked:
                s = jnp.where(mask, s, -jnp.inf)
            m_old = m_ref[g]
            m_new = jnp.maximum(m_old, jnp.max(s, axis=0, keepdims=True))
            alpha = jnp.exp2(m_old - m_new)
            p = jnp.exp2(s - m_new)
            acc_ref[g] = alpha * acc_ref[g] + _dot(vt_ref[g, j], p.astype(BF16))
            m_ref[g] = m_new

    jd = t0 // tk
    last = jnp.maximum(jd - 1, 0)
    win_scores, win_softmax = _window_pass(i, kw_ref, vwt_ref, wb_ref, ow_ref, sw_ref, pw_ref, tq, WIN_TK)
    scores(jd, 0)
    win_scores(0, qs_ref[0, 0:LANE, :])
    update(jd, 0, True)
    win_scores(1, qs_ref[1, 0:LANE, :])
    win_softmax(0)
    scores(0, 1)
    win_softmax(1)

    def body(p, carry):
        scores(jnp.minimum(2 * p + 1, last), 0)
        update(2 * p, 1, False)
        scores(jnp.minimum(2 * p + 2, last), 1)
        update(2 * p + 1, 0, False)
        return carry

    lax.fori_loop(0, jd // 2, body, 0)

    @pl.when(jd % 2 == 1)
    def _():
        update(jd - 1, 1, False)

    for g in range(KV_GROUPS):
        _store_heads(o_ref, acc_ref[g, 0:LANE] / jnp.maximum(acc_ref[g, LANE:LANE + 1], jnp.finfo(F32).tiny), g, tq)


def _window_pass(i, k_ref, vt_ref, wb_ref, o_ref, s_ref, p_ref, tq, tk):
    n_t = WINDOW // tk + 1
    jt = [jnp.maximum(i - d, 0) for d in range(n_t)]
    static_bias = {0: 0, n_t - 1: 1}

    def scores(g, q_g):
        for d in range(n_t):
            k0 = pl.multiple_of(jt[d] * tk, tk)
            s_ref[g, d] = _dot(k_ref[pl.ds(k0, tk), g * LANE:(g + 1) * LANE], q_g)

    def tile(g, d):
        s = s_ref[g, d]
        return s + wb_ref[static_bias[d]] if d in static_bias else s

    def softmax(g):
        m = jnp.max(tile(g, 0), axis=0, keepdims=True)
        for d in range(1, n_t):
            m = jnp.maximum(m, jnp.where(i >= d, jnp.max(tile(g, d), axis=0, keepdims=True), -jnp.inf))
        for d in range(n_t):
            p = jnp.exp2(tile(g, d) - (m if d == 0 else jnp.where(i >= d, m, jnp.inf)))
            p_ref[g, d * tk:(d + 1) * tk, :] = p.astype(BF16)
        v_cat = jnp.concatenate([vt_ref[g, jt[d]] for d in range(n_t)], axis=1)
        acc = _dot(v_cat, p_ref[g])
        _store_heads(o_ref, acc[0:LANE] / jnp.maximum(acc[LANE:LANE + 1], jnp.finfo(F32).tiny), g, tq)

    return scores, softmax


def _window_bias(tq):
    r = np.arange(tq)[:, None]
    c = (np.arange(HPG * tq) % tq)[None, :]
    neg = np.float32(-np.inf)
    return jnp.asarray(np.stack([np.where(r <= c, np.float32(0), neg), np.where(c < r, np.float32(0), neg)]))


def _flash(q_t, kaug, vsl_t, bias_t, kwn, vwn_t):
    B, _, S = q_t.shape
    tq = FLASH_TQ
    tk = SEL_TK
    tw = WIN_TK
    assert tk % tq == 0 and tw == tq and WINDOW == 2 * tw
    kw = kaug.shape[2] // KV_GROUPS
    cols = HPG * tq
    out_spec = pl.BlockSpec((None, tq, WIDTH), lambda b, i: (b, i, 0))
    out_sd = jax.ShapeDtypeStruct((B, S, WIDTH), BF16)
    return pl.pallas_call(
        functools.partial(_flash_kernel, tk=tk),
        grid=(B, S // tq),
        in_specs=[pl.BlockSpec((None, WIDTH, tq), lambda b, i: (b, 0, i)),
                  pl.BlockSpec((None, KV_GROUPS, LANE, tq), lambda b, i: (b, 0, 0, i)),
                  pl.BlockSpec((None, S, KV_GROUPS * kw), lambda b, i: (b, 0, 0)),
                  pl.BlockSpec((None, KV_GROUPS, S // tk, SEL_VROWS, tk), lambda b, i: (b, 0, 0, 0, 0)),
                  pl.BlockSpec((None, S, KVW), lambda b, i: (b, 0, 0)),
                  pl.BlockSpec((None, KV_GROUPS, S // tw, SEL_VROWS, tw), lambda b, i: (b, 0, 0, 0, 0)),
                  pl.BlockSpec((2, tw, cols), lambda b, i: (0, 0, 0))],
        out_specs=[out_spec, out_spec],
        out_shape=[out_sd, out_sd],
        scratch_shapes=[pltpu.VMEM((KV_GROUPS, kw, cols), BF16),
                        pltpu.VMEM((2, KV_GROUPS, tk, cols), F32),
                        pltpu.VMEM((KV_GROUPS, 1, cols), F32),
                        pltpu.VMEM((KV_GROUPS, SEL_VROWS, cols), F32),
                        pltpu.VMEM((KV_GROUPS, WINDOW // tw + 1, tw, cols), F32),
                        pltpu.VMEM((KV_GROUPS, (WINDOW // tw + 1) * tw, cols), BF16)],
        compiler_params=_cparams(("parallel", "parallel")),
        name="flash",
    )(q_t, bias_t, kaug, vsl_t, kwn, vwn_t, _window_bias(tq))


def _merge_kernel(ydn_ref, oc_ref, os_ref, ow_ref, small_ref, mgd_ref, mgn_ref, x_ref,
                  wdn_ref, wnsa_ref, wout_ref, g_ref, b_ref, o_ref):
    gates = _sigmoid(small_ref[...])
    cols = []
    for h in range(HEADS):
        sl = slice(h * LANE, (h + 1) * LANE)
        gc = gates[:, SMALL_GATE + h:SMALL_GATE + h + 1]
        gs = gates[:, SMALL_GATE + HEADS + h:SMALL_GATE + HEADS + h + 1]
        gw = gates[:, SMALL_GATE + 2 * HEADS + h:SMALL_GATE + 2 * HEADS + h + 1]
        cols.append((gc * oc_ref[:, sl] + gs * os_ref[:, sl] + gw * ow_ref[:, sl]).astype(BF16))
    y_nsa = jnp.concatenate(cols, axis=1)
    y = (_sigmoid(mgd_ref[...].astype(F32)) * _dot(ydn_ref[...], wdn_ref[...])
         + _sigmoid(mgn_ref[...].astype(F32)) * _dot(y_nsa, wnsa_ref[...]))
    z = DEEPNORM_ALPHA * x_ref[...] + _dot(y.astype(BF16), wout_ref[...])
    o_ref[...] = _layer_norm(z, g_ref[...], b_ref[...])


def _merge(y_dn2, oc2, os2, ow2, small2, h, x2, wdn, wnsa, wout, g, b):
    T, D = x2.shape
    tm = PROJ_TM
    row = pl.BlockSpec((tm, WIDTH), lambda i: (i, 0))
    full = lambda shape: pl.BlockSpec(shape, lambda i: (0, 0))
    return pl.pallas_call(
        _merge_kernel,
        grid=(T // tm,),
        in_specs=[row, row, row, row,
                  pl.BlockSpec((tm, LANE), lambda i: (i, 0)),
                  pl.BlockSpec((tm, WIDTH), lambda i: (i, COL_MG_DN // WIDTH)),
                  pl.BlockSpec((tm, WIDTH), lambda i: (i, COL_MG_NSA // WIDTH)),
                  pl.BlockSpec((tm, D), lambda i: (i, 0)),
                  full((WIDTH, D)), full((WIDTH, D)), full((D, D)), full((1, D)), full((1, D))],
        out_specs=pl.BlockSpec((tm, D), lambda i: (i, 0)),
        out_shape=jax.ShapeDtypeStruct((T, D), F32),
        compiler_params=_cparams(("parallel",)),
        name="merge",
    )(y_dn2, oc2, os2, ow2, small2, h, h, x2, wdn, wnsa, wout, g, b)


FFN_TM = 1024


def _ffn_up_kernel(x_ref, xp_ref, wg_ref, wv_ref, cwg_ref, cwv_ref, bg_ref, bv_ref, o_ref, xs_ref, *, tiles_per_seq):
    i = pl.program_id(1)
    tm = x_ref.shape[0]
    nv = tm // SUBLANE
    n_slab = x_ref.shape[1] // LANE
    for sp in range(SUBLANE):
        for c in range(n_slab):
            xs_ref[c, pl.ds(sp, nv, stride=SUBLANE), :] = x_ref[sp * nv:(sp + 1) * nv, c * LANE:(c + 1) * LANE]
    xb = jnp.concatenate([xs_ref[c] for c in range(n_slab)], axis=1).astype(BF16)
    notfirst = (i % tiles_per_seq != 0).astype(F32)
    xpb = (xp_ref[...] * notfirst).astype(BF16)
    sub = lax.broadcasted_iota(jnp.int32, (SUBLANE, 1), 0)

    def conv(w_ref, cw_ref, b_ref):
        u = _dot(xb, w_ref[...])
        p = _dot(xpb, w_ref[...])

        def wrap(k):
            blk = u[(nv - k) * SUBLANE:(nv - k + 1) * SUBLANE, :]
            return jnp.where(sub == 0, p[SUBLANE - k:SUBLANE - k + 1, :], pltpu.roll(blk, 1, 0))

        w1, w2 = wrap(1), wrap(2)
        u1 = jnp.concatenate([w1, u[0:(nv - 1) * SUBLANE, :]], axis=0)
        u2 = jnp.concatenate([w2, w1, u[0:(nv - 2) * SUBLANE, :]], axis=0)
        return cw_ref[0:1, :] * u2 + cw_ref[1:2, :] * u1 + cw_ref[2:3, :] * u + b_ref[...]

    o_ref[...] = (_silu(conv(wg_ref, cwg_ref, bg_ref)) * conv(wv_ref, cwv_ref, bv_ref)).astype(BF16)


def _ffn_up(x1, w_up, conv_w, conv_b, seq):
    T, D = x1.shape
    tm = FFN_TM
    nj = 2
    tn = D_FF // nj
    return pl.pallas_call(
        functools.partial(_ffn_up_kernel, tiles_per_seq=seq // tm),
        grid=(nj, T // tm),
        in_specs=[pl.BlockSpec((tm, D), lambda j, i: (i, 0)),
                  pl.BlockSpec((SUBLANE, D), lambda j, i: (jnp.maximum(i * (tm // SUBLANE) - 1, 0), 0)),
                  pl.BlockSpec((D, tn), lambda j, i: (0, j)),
                  pl.BlockSpec((D, tn), lambda j, i: (0, j + nj)),
                  pl.BlockSpec((FFN_CONV, tn), lambda j, i: (0, j)),
                  pl.BlockSpec((FFN_CONV, tn), lambda j, i: (0, j + nj)),
                  pl.BlockSpec((1, tn), lambda j, i: (0, j)),
                  pl.BlockSpec((1, tn), lambda j, i: (0, j + nj))],
        out_specs=pl.BlockSpec((tm, tn), lambda j, i: (i, j)),
        out_shape=jax.ShapeDtypeStruct((T, D_FF), BF16),
        scratch_shapes=[pltpu.VMEM((D // LANE, tm, LANE), F32)],
        compiler_params=_cparams(("parallel", "parallel")),
        name="ffn_up",
    )(x1, x1, w_up, w_up, conv_w, conv_w, conv_b, conv_b)


def _ffn_down_kernel(a_ref, w_ref, x_ref, g_ref, b_ref, o_ref, fs_ref):
    tm, d = x_ref.shape
    nv = tm // SUBLANE
    n_slab = d // LANE
    f = _dot(a_ref[...], w_ref[...])
    for c in range(n_slab):
        fs_ref[c] = f[:, c * LANE:(c + 1) * LANE]
    for sp in range(SUBLANE):
        rows = slice(sp * nv, (sp + 1) * nv)
        f_nat = jnp.concatenate([fs_ref[c, pl.ds(sp, nv, stride=SUBLANE), :] for c in range(n_slab)], axis=1)
        z = DEEPNORM_ALPHA * x_ref[rows, :] + f_nat
        o_ref[rows, :] = _layer_norm(z, g_ref[...], b_ref[...])


def _ffn_down(a, w_down, x1, g, b):
    T, D = x1.shape
    tm = FFN_TM
    full = lambda shape: pl.BlockSpec(shape, lambda i: (0, 0))
    return pl.pallas_call(
        _ffn_down_kernel,
        grid=(T // tm,),
        in_specs=[pl.BlockSpec((tm, D_FF), lambda i: (i, 0)), full((D_FF, D)),
                  pl.BlockSpec((tm, D), lambda i: (i, 0)), full((1, D)), full((1, D))],
        out_specs=pl.BlockSpec((tm, D), lambda i: (i, 0)),
        out_shape=jax.ShapeDtypeStruct((T, D), F32),
        scratch_shapes=[pltpu.VMEM((D // LANE, tm, LANE), F32)],
        compiler_params=_cparams(("parallel",)),
        name="ffn_down",
    )(a, w_down, x1, g, b)


def _split_w_in_kernel(wt_ref, a_ref, b_ref, c_ref):
    o_dz = 3 * WIDTH
    o_db = 4 * WIDTH
    o_nq = o_db + 2 * HEADS
    o_kc = o_nq + WIDTH
    o_ksl = o_kc + 2 * KVW
    o_gate = o_ksl + 4 * KVW
    o_mg = o_gate + 3 * HEADS
    slab = wt_ref.shape[1]

    def put(dst, o0, c0, c1):
        dst[:, o0:o0 + (c1 - c0)] = wt_ref[c0:c1, :].T.astype(BF16)

    put(a_ref, 0, 0, o_dz)
    small = jnp.concatenate([wt_ref[o_db:o_nq, :], wt_ref[o_gate:o_mg, :],
                             jnp.zeros((LANE - 5 * HEADS, slab), F32)], axis=0)
    a_ref[:, o_dz:NA] = small.T.astype(BF16)
    put(b_ref, 0, o_nq, o_kc)
    put(b_ref, WIDTH, o_ksl, o_gate)
    put(c_ref, 0, o_dz, o_db)
    put(c_ref, WIDTH, o_mg, o_mg + 2 * WIDTH)
    put(c_ref, 3 * WIDTH, o_kc, o_ksl)


def _split_w_in(w, layer):
    _, d, n_in = w.shape
    slab = 2 * LANE
    return pl.pallas_call(
        _split_w_in_kernel,
        grid=(d // slab,),
        in_specs=[pl.BlockSpec((None, n_in, slab), lambda i: (layer, 0, i))],
        out_specs=[pl.BlockSpec((slab, NA), lambda i: (i, 0)), pl.BlockSpec((slab, NB), lambda i: (i, 0)),
                   pl.BlockSpec((slab, NC_COLS), lambda i: (i, 0))],
        out_shape=[jax.ShapeDtypeStruct((d, NA), BF16), jax.ShapeDtypeStruct((d, NB), BF16),
                   jax.ShapeDtypeStruct((d, NC_COLS), BF16)],
        compiler_params=_cparams(("parallel",)),
        name="split_w_in",
    )(jnp.swapaxes(w, 1, 2))


def _rope_tables(pos):
    pos = np.asarray(pos)
    inv_freq = np.float32(ROPE_THETA) ** (-(np.arange(ROPE_HALF, dtype=np.float32) / np.float32(ROPE_HALF)))
    ang = pos.astype(np.float32)[:, None] * inv_freq.astype(np.float32)
    cos = np.cos(ang.astype(np.float64)).astype(np.float32)
    sin = np.sin(ang.astype(np.float64)).astype(np.float32)
    p = pos.shape[0]
    cosf = np.concatenate([cos, cos, np.ones((p, LANE - ROPE_DIM), np.float32)], axis=1)
    sinf = np.concatenate([-sin, sin, np.zeros((p, LANE - ROPE_DIM), np.float32)], axis=1)
    return jnp.asarray(cosf), jnp.asarray(sinf)


def _overlap_table(nc, n_sel):
    n = np.arange(nc)[None, :]
    j = np.arange(LANE)[:, None]
    ov = (n * CMP_STRIDE <= j * SEL_LEN + SEL_LEN - 1) & (n * CMP_STRIDE + CMP_LEN - 1 >= j * SEL_LEN)
    ov = ov & (j < n_sel) & (n < nc - 1)
    return jnp.asarray(ov.astype(np.float32)).astype(BF16)


def _lane_row(vals, offset):
    return jnp.zeros((1, LANE), F32).at[0, offset:offset + vals.shape[0]].set(vals.astype(F32))


def kernel(x, w_in, dn_conv_w, dn_a_log, dn_dt_bias, dn_norm_w, cmp_pos_k, cmp_pos_v, cmp_k_w1, cmp_k_w2,
           cmp_v_w1, cmp_v_w2, w_branch_dn, w_branch_nsa, w_out, ln1_g, ln1_b, ffn_w_up, ffn_conv_w,
           ffn_conv_b, ffn_w_down, ln2_g, ln2_b):
    B, S, D = x.shape
    T = B * S
    nc = S // CMP_STRIDE
    n_sel = S // SEL_LEN
    assert D == WIDTH and S % FFN_TM == 0 and S % PROJ_TM == 0 and n_sel <= LANE and n_sel % SUBLANE == 0
    cos_t, sin_t = _rope_tables(np.arange(S))
    cos_c, sin_c = _rope_tables(np.arange(nc) * CMP_STRIDE + (CMP_LEN - 1))
    overlap = _overlap_table(nc, n_sel)
    x2 = x.reshape(T, D)
    for layer in range(w_in.shape[0]):
        w_a, w_b, w_c = _split_w_in(w_in, layer)
        x3 = x2.reshape(B, S, D)
        h3, kv_cmp = _proj_rest(x3, w_c)

        q, k, v, bg, small = _proj_gdn(x3, w_a, dn_conv_w[layer], _lane_row(dn_a_log[layer], SMALL_DECAY),
                                       _lane_row(dn_dt_bias[layer], SMALL_DECAY))
        y_dn = _gdn(q, k, v, bg, h3, dn_norm_w[layer].reshape(1, LANE))

        q_t, kaug, vsl_t, kwn_r, vwn_t = _proj_nsa(x3, w_b, cos_t, sin_t)
        kc, vc = _compress(kv_cmp, cmp_pos_k[layer], cmp_pos_v[layer],
                           cmp_k_w1[layer].astype(BF16), cmp_k_w2[layer].astype(BF16),
                           cmp_v_w1[layer].astype(BF16), cmp_v_w2[layer].astype(BF16), cos_c, sin_c)
        o_cmp, bias = _cmp_attn(q_t, kc, vc, overlap)
        o_sel, o_win = _flash(q_t, kaug, vsl_t, bias, kwn_r, vwn_t)

        x1 = _merge(y_dn.reshape(T, WIDTH), o_cmp.reshape(T, WIDTH), o_sel.reshape(T, WIDTH),
                    o_win.reshape(T, WIDTH), small.reshape(T, LANE), h3.reshape(T, COL_KC), x2,
                    w_branch_dn[layer].astype(BF16), w_branch_nsa[layer].astype(BF16),
                    w_out[layer].astype(BF16), ln1_g[layer].reshape(1, D), ln1_b[layer].reshape(1, D))
        a = _ffn_up(x1, ffn_w_up[layer].astype(BF16), ffn_conv_w[layer],
                    ffn_conv_b[layer].reshape(1, 2 * D_FF), S)
        x2 = _ffn_down(a, ffn_w_down[layer].astype(BF16), x1, ln2_g[layer].reshape(1, D), ln2_b[layer].reshape(1, D))
    return x2.reshape(B, S, D)
```

```python
import functools
import math

import numpy as np
import jax
import jax.numpy as jnp
from jax import lax
from jax.experimental import pallas as pl
from jax.experimental.pallas import tpu as pltpu

F32 = jnp.float32
BF16 = jnp.bfloat16
HI = lax.Precision.HIGHEST

HEADS = 8
HEAD_DIM = 128
KV_GROUPS = 2
HPG = HEADS // KV_GROUPS
DN_CONV = 4
DN_CHUNK = 64
CMP_LEN = 32
CMP_STRIDE = 16
CMP_HIDDEN = 256
SEL_LEN = 64
SEL_TOPK = 16
WINDOW = 512
ROPE_THETA = 500000.0
ROPE_DIM = HEAD_DIM // 4
ROPE_HALF = ROPE_DIM // 2
D_FF = 2816
FFN_CONV = 3
LN_EPS = 1e-5
RMS_EPS = 1e-6
DEEPNORM_ALPHA = 2.0 ** 0.25
ATT_SCALE = HEAD_DIM ** -0.5

LANE = 128
SUBLANE = 8
VMEM_LIMIT = 56 * 1024 * 1024

WIDTH = HEADS * HEAD_DIM
KVW = KV_GROUPS * HEAD_DIM
NA = 3 * WIDTH + LANE
SMALL_BETA = 0
SMALL_DECAY = HEADS
SMALL_GATE = 2 * HEADS
NB = WIDTH + 4 * KVW
NC_COLS = 3 * WIDTH + 2 * KVW
COL_DZ = 0
COL_MG_DN = WIDTH
COL_MG_NSA = 2 * WIDTH
COL_KC = 3 * WIDTH

NEG_BIAS = -(2.0 ** 100)


def _cparams(sem):
    return pltpu.CompilerParams(dimension_semantics=sem, vmem_limit_bytes=VMEM_LIMIT)


def _sigmoid(x):
    return 1.0 / (1.0 + jnp.exp(-x))


def _silu(x):
    return x * _sigmoid(x)


def _dot(a, b, precision=None):
    return jnp.dot(a, b, preferred_element_type=F32, precision=precision)


def _dot_nt(a, b):
    return lax.dot_general(a, b, (((1,), (1,)), ((), ())), preferred_element_type=F32)


def _dot_tn(a, b):
    return lax.dot_general(a, b, (((0,), (0,)), ((), ())), preferred_element_type=F32)


def _layer_norm(z, g, b):
    mu = jnp.mean(z, -1, keepdims=True)
    zc = z - mu
    var = jnp.mean(zc * zc, -1, keepdims=True)
    return zc * lax.rsqrt(var + LN_EPS) * g + b


PROJ_TM = 512


def _proj_rest_kernel(x_ref, w_ref, g_ref, c_ref):
    h = _dot(x_ref[...].astype(BF16), w_ref[...])
    g_ref[...] = h[:, 0:COL_KC].astype(BF16)
    c_ref[...] = h[:, COL_KC:NC_COLS]


def _proj_rest(x3, w):
    B, S, D = x3.shape
    tm = PROJ_TM
    return pl.pallas_call(
        _proj_rest_kernel,
        grid=(B, S // tm),
        in_specs=[pl.BlockSpec((None, tm, D), lambda b, i: (b, i, 0)),
                  pl.BlockSpec((D, NC_COLS), lambda b, i: (0, 0))],
        out_specs=[pl.BlockSpec((None, tm, COL_KC), lambda b, i: (b, i, 0)),
                   pl.BlockSpec((None, tm, NC_COLS - COL_KC), lambda b, i: (b, i, 0))],
        out_shape=[jax.ShapeDtypeStruct((B, S, COL_KC), BF16),
                   jax.ShapeDtypeStruct((B, S, NC_COLS - COL_KC), F32)],
        compiler_params=_cparams(("parallel", "parallel")),
        name="proj_rest",
    )(x3, w)


def _proj_gdn_kernel(x_ref, w_ref, cw_ref, alog_ref, dtb_ref, q_ref, k_ref, v_ref, bg_ref, small_ref,
                     xs_ref, ys_ref, carry_ref):
    i = pl.program_id(1)
    ts = x_ref.shape[0]
    nv = ts // SUBLANE
    wq = 3 * WIDTH
    n_slab = x_ref.shape[1] // LANE

    @pl.when(i == 0)
    def _():
        carry_ref[...] = jnp.zeros(carry_ref.shape, F32)

    for sp in range(SUBLANE):
        for c in range(n_slab):
            xs_ref[c, pl.ds(sp, nv, stride=SUBLANE), :] = x_ref[sp * nv:(sp + 1) * nv, c * LANE:(c + 1) * LANE]
    xb = jnp.concatenate([xs_ref[c] for c in range(n_slab)], axis=1).astype(BF16)
    h = _dot(xb, w_ref[...])
    sub = lax.broadcasted_iota(jnp.int32, (SUBLANE, 1), 0)

    def token_order(c, dst_ref, dst_sl, dtype):
        for sp in range(SUBLANE):
            dst_ref[sp * nv:(sp + 1) * nv, dst_sl] = ys_ref[c, pl.ds(sp, nv, stride=SUBLANE), :].astype(dtype)

    for c in range(3 * HEADS):
        sl = slice(c * LANE, (c + 1) * LANE)
        col = h[:, sl]
        wrap = [jnp.where(sub == 0, carry_ref[SUBLANE - k:SUBLANE - k + 1, sl],
                          pltpu.roll(col[(nv - k) * SUBLANE:(nv - k + 1) * SUBLANE, :], 1, 0))
                for k in range(1, DN_CONV)]
        acc = cw_ref[DN_CONV - 1:DN_CONV, sl] * col
        for s in range(1, DN_CONV):
            shifted = jnp.concatenate(wrap[0:s][::-1] + [col[0:(nv - s) * SUBLANE, :]], axis=0)
            acc = acc + cw_ref[DN_CONV - 1 - s:DN_CONV - s, sl] * shifted
        y = _silu(acc)
        if c < 2 * HEADS:
            y = y * lax.rsqrt(jnp.sum(y * y, -1, keepdims=True) + RMS_EPS)
        if c < HEADS:
            y = y * ATT_SCALE
        ys_ref[c] = y
        dst = (q_ref, k_ref, v_ref)[c // HEADS]
        token_order(c, dst, slice((c % HEADS) * LANE, (c % HEADS + 1) * LANE), BF16)
    for k in range(1, DN_CONV):
        carry_ref[SUBLANE - k:SUBLANE - k + 1, :] = h[(nv - k + 1) * SUBLANE - 1:(nv - k + 1) * SUBLANE, 0:wq]
    small = h[:, wq:wq + LANE]
    beta = _sigmoid(small)
    xs = small + dtb_ref[...]
    softplus = jnp.maximum(xs, 0.0) + jnp.log1p(jnp.exp(-jnp.abs(xs)))
    g = -jnp.exp(alog_ref[...]) * softplus
    lane = lax.broadcasted_iota(jnp.int32, small.shape, 1)
    ys_ref[3 * HEADS] = small
    token_order(3 * HEADS, small_ref, slice(0, LANE), F32)
    ys_ref[3 * HEADS + 1] = jnp.where(lane < SMALL_DECAY, beta, g)
    token_order(3 * HEADS + 1, bg_ref, slice(0, LANE), F32)


def _proj_gdn(x3, w_a, conv_w, alog_row, dtb_row):
    B, S, D = x3.shape
    ts = PROJ_TM
    wq = 3 * WIDTH
    out_sd = jax.ShapeDtypeStruct((B, S, WIDTH), BF16)
    small_sd = jax.ShapeDtypeStruct((B, S, LANE), F32)
    wide = pl.BlockSpec((None, ts, WIDTH), lambda b, i: (b, i, 0))
    narrow = pl.BlockSpec((None, ts, LANE), lambda b, i: (b, i, 0))
    return pl.pallas_call(
        _proj_gdn_kernel,
        grid=(B, S // ts),
        in_specs=[
            pl.BlockSpec((None, ts, D), lambda b, i: (b, i, 0)),
            pl.BlockSpec((D, NA), lambda b, i: (0, 0)),
            pl.BlockSpec((DN_CONV, wq), lambda b, i: (0, 0)),
            pl.BlockSpec((1, LANE), lambda b, i: (0, 0)),
            pl.BlockSpec((1, LANE), lambda b, i: (0, 0)),
        ],
        out_specs=[wide, wide, wide, narrow, narrow],
        out_shape=[out_sd, out_sd, out_sd, small_sd, small_sd],
        scratch_shapes=[pltpu.VMEM((D // LANE, ts, LANE), F32),
                        pltpu.VMEM((3 * HEADS + 2, ts, LANE), F32),
                        pltpu.VMEM((SUBLANE, wq), F32)],
        compiler_params=_cparams(("parallel", "arbitrary")),
        name="proj_gdn",
    )(x3, w_a, conv_w, alog_row, dtb_row)


GDN_CHUNKS_PER_STEP = 4
GDN_BATCH_PER_STEP = 2


GDN_QUAD = 4


def _gdn_kernel(q_ref, k_ref, v_ref, bg_ref, z_ref, nw_ref, y_ref, state_ref):
    n = pl.program_id(1)

    @pl.when(n == 0)
    def _():
        state_ref[...] = jnp.zeros(state_ref.shape, F32)

    C = DN_CHUNK
    QW = GDN_QUAD * C
    QL = GDN_QUAD * LANE
    shift_c = int(math.log2(C))
    r64 = lax.broadcasted_iota(jnp.int32, (C, C), 0)
    c64 = lax.broadcasted_iota(jnp.int32, (C, C), 1)
    tril = jnp.where(r64 >= c64, 1.0, 0.0).astype(F32)
    triu = jnp.where(r64 <= c64, 1.0, 0.0).astype(F32)
    rr = lax.broadcasted_iota(jnp.int32, (C, QW), 0)
    ll = lax.broadcasted_iota(jnp.int32, (C, QW), 1)
    jj = ll & (C - 1)
    hb = ll >> shift_c
    incl_t = rr >= jj
    strict_t = rr > jj
    eye_t = jnp.where(rr == jj, 1.0, 0.0).astype(F32)
    bd_c = jnp.where((lax.broadcasted_iota(jnp.int32, (QW, QW), 0) >> shift_c)
                     == (lax.broadcasted_iota(jnp.int32, (QW, QW), 1) >> shift_c), 1.0, 0.0).astype(BF16)
    bd_l = jnp.where((lax.broadcasted_iota(jnp.int32, (QW, QL), 0) >> shift_c)
                     == (lax.broadcasted_iota(jnp.int32, (QW, QL), 1) >> int(math.log2(LANE))), 1.0, 0.0).astype(BF16)
    zeros_rhs = jnp.zeros((C, 2 * LANE), BF16)
    nw = nw_ref[...]
    nb = q_ref.shape[0]
    quads = [(bb, c, a) for bb in range(nb) for c in range(GDN_CHUNKS_PER_STEP) for a in range(HEADS // GDN_QUAD)]
    gc_cols, gc_rows, bgs = {}, {}, {}
    for bb in range(nb):
        bg = bg_ref[bb]
        bg_t = bg.T
        bgs[bb] = bg
        for c in range(GDN_CHUNKS_PER_STEP):
            rs = slice(c * C, (c + 1) * C)
            gc_cols[bb, c] = _dot(tril, bg[rs, :], HI)
            gc_rows[bb, c] = _dot(bg_t[SMALL_DECAY:SMALL_DECAY + HEADS, rs], triu, HI)
    decay, lower, qk, rhs_bd, qd, kd, eglast = {}, {}, {}, {}, {}, {}, {}
    for u_ in quads:
        bb, c, a = u_
        rs = slice(c * C, (c + 1) * C)
        h0 = a * GDN_QUAD
        gcols = [gc_cols[bb, c][:, SMALL_DECAY + h0 + hh:SMALL_DECAY + h0 + hh + 1] for hh in range(GDN_QUAD)]
        gcol_cat = jnp.broadcast_to(gcols[0], (C, QW))
        for hh in range(1, GDN_QUAD):
            gcol_cat = jnp.where(hb == hh, gcols[hh], gcol_cat)
        grow_cat = jnp.concatenate([gc_rows[bb, c][h0 + hh:h0 + hh + 1, :] for hh in range(GDN_QUAD)], axis=1)
        decay[u_] = jnp.exp(jnp.where(incl_t, gcol_cat - grow_cat, -jnp.inf))
        kq = k_ref[bb, rs, a * QL:(a + 1) * QL].astype(F32)
        qq = q_ref[bb, rs, a * QL:(a + 1) * QL].astype(F32)
        kbs, rows_bd = [], []
        for hh in range(GDN_QUAD):
            h = h0 + hh
            ls = slice(hh * LANE, (hh + 1) * LANE)
            beta = bgs[bb][rs, SMALL_BETA + h:SMALL_BETA + h + 1]
            eg = jnp.exp(gcols[hh])
            glast = gcols[hh][C - 1:C, :]
            kb = kq[:, ls] * beta
            vb = v_ref[bb, rs, h * LANE:(h + 1) * LANE].astype(F32) * beta
            kbs.append(kb)
            rhs_h = jnp.concatenate([vb, kb * eg], axis=1).astype(BF16)
            rows_bd.append(jnp.concatenate([zeros_rhs] * hh + [rhs_h] + [zeros_rhs] * (GDN_QUAD - 1 - hh), axis=1))
            qd[bb, c, h] = (qq[:, ls] * eg).astype(BF16)
            kd[bb, c, h] = (kq[:, ls] * jnp.exp(glast - gcols[hh])).astype(BF16)
            eglast[bb, c, h] = jnp.exp(glast)
        rhs_bd[u_] = jnp.concatenate(rows_bd, axis=0)
        k_bd = jnp.concatenate([kq.astype(BF16)] * GDN_QUAD, axis=0) * bd_l
        lhs = jnp.concatenate([jnp.concatenate(kbs, axis=1), qq], axis=0).astype(BF16)
        aq = _dot_nt(lhs, k_bd)
        lower[u_] = jnp.where(strict_t, aq[0:C] * decay[u_], 0.0)
        qk[u_] = (aq[C:2 * C] * decay[u_]).astype(BF16)

    def block_diag(y_b):
        return jnp.concatenate([y_b] * GDN_QUAD, axis=0) * bd_c

    tinv = {u_: eye_t - lower[u_] for u_ in quads}
    y_b = {u_: lower[u_].astype(BF16) for u_ in quads}
    ypow = {u_: _dot(y_b[u_], block_diag(y_b[u_])) for u_ in quads}
    for it in range(5):
        for u_ in quads:
            yb = ypow[u_].astype(BF16)
            if it < 4:
                r = _dot(jnp.concatenate([tinv[u_].astype(BF16), yb], axis=0), block_diag(yb))
                tinv[u_] = tinv[u_] + r[0:C]
                ypow[u_] = r[C:2 * C]
            else:
                tinv[u_] = tinv[u_] + _dot(tinv[u_].astype(BF16), block_diag(yb))
    uw = {u_: _dot(tinv[u_].astype(BF16), rhs_bd[u_]) for u_ in quads}
    zeros_v = jnp.zeros((C, LANE), BF16)
    chains = [(bb, h) for bb in range(nb) for h in range(HEADS)]
    for c in range(GDN_CHUNKS_PER_STEP):
        rs = slice(c * C, (c + 1) * C)
        states = {s_: state_ref[s_[0] * HEADS + s_[1]] for s_ in chains}
        ws_qs = {}
        for s_ in chains:
            bb, h = s_
            a, hh = divmod(h, GDN_QUAD)
            w_h = uw[bb, c, a][:, hh * 2 * LANE + LANE:(hh + 1) * 2 * LANE]
            ws_qs[s_] = _dot(jnp.concatenate([w_h.astype(BF16), qd[bb, c, h]], axis=0), states[s_].astype(BF16))
        v_new = {}
        for s_ in chains:
            bb, h = s_
            a, hh = divmod(h, GDN_QUAD)
            v_new[s_] = (uw[bb, c, a][:, hh * 2 * LANE:hh * 2 * LANE + LANE] - ws_qs[s_][0:C]).astype(BF16)
        outs = {}
        for s_ in chains:
            bb, h = s_
            a, hh = divmod(h, GDN_QUAD)
            v_pad = jnp.concatenate([zeros_v] * hh + [v_new[s_]] + [zeros_v] * (GDN_QUAD - 1 - hh), axis=0)
            outs[s_] = ws_qs[s_][C:2 * C] + _dot(qk[bb, c, a], v_pad)
            state_ref[bb * HEADS + h] = states[s_] * eglast[bb, c, h] + _dot_tn(kd[bb, c, h], v_new[s_])
        for s_ in chains:
            bb, h = s_
            ls = slice(h * LANE, (h + 1) * LANE)
            o = outs[s_]
            o = o * lax.rsqrt(jnp.mean(o * o, -1, keepdims=True) + RMS_EPS)
            y_ref[bb, rs, ls] = (o * nw * _silu(z_ref[bb, rs, ls].astype(F32))).astype(BF16)


def _gdn(q, k, v, bg, h3, norm_w):
    B, S, _ = q.shape
    rows = GDN_CHUNKS_PER_STEP * DN_CHUNK
    nb = GDN_BATCH_PER_STEP if B % GDN_BATCH_PER_STEP == 0 else 1
    tile = pl.BlockSpec((nb, rows, WIDTH), lambda b, n: (b, n, 0))
    return pl.pallas_call(
        _gdn_kernel,
        grid=(B // nb, S // rows),
        in_specs=[tile, tile, tile,
                  pl.BlockSpec((nb, rows, LANE), lambda b, n: (b, n, 0)),
                  pl.BlockSpec((nb, rows, WIDTH), lambda b, n: (b, n, COL_DZ // WIDTH)),
                  pl.BlockSpec((1, LANE), lambda b, n: (0, 0))],
        out_specs=tile,
        out_shape=jax.ShapeDtypeStruct((B, S, WIDTH), BF16),
        scratch_shapes=[pltpu.VMEM((nb * HEADS, HEAD_DIM, HEAD_DIM), F32)],
        compiler_params=_cparams(("parallel", "arbitrary")),
        name="gdn",
    )(q, k, v, bg, h3, norm_w)


def _rope(t, cosf, sinf, lo):
    partner = jnp.where(lo, pltpu.roll(t, LANE - ROPE_HALF, 1), pltpu.roll(t, ROPE_HALF, 1))
    return t * cosf + partner * sinf


SEL_TK = 512
WIN_TK = 256
SEL_VROWS = LANE + 16
LOG2E = math.log2(math.e)


def _proj_nsa_kernel(x_ref, w_ref, cos_ref, sin_ref, qt_o, kaug_o, vslt_o, kwn_o, vwnt_o):
    i = pl.program_id(1)
    ts = x_ref.shape[0]
    h = _dot(x_ref[...].astype(BF16), w_ref[...])
    c_ksl, c_vsl, c_kwn, c_vwn = WIDTH, WIDTH + KVW, WIDTH + 2 * KVW, WIDTH + 3 * KVW
    cosf = cos_ref[...]
    sinf = sin_ref[...]
    lane = lax.broadcasted_iota(jnp.int32, cosf.shape, 1)
    lo = lane < ROPE_HALF
    for hd in range(HEADS):
        sl = slice(hd * LANE, (hd + 1) * LANE)
        qt_o[sl, :] = (_rope(h[:, sl], cosf, sinf, lo) * (ATT_SCALE * LOG2E)).T.astype(BF16)
    tpos = i * ts + lax.broadcasted_iota(jnp.int32, cosf.shape, 0)
    onehot = jnp.where((tpos >> int(math.log2(SEL_LEN))) == lane, 1.0, 0.0).astype(BF16)
    for g in range(KV_GROUPS):
        sl = slice(g * LANE, (g + 1) * LANE)
        off = g * LANE
        kaug_o[:, 2 * g * LANE:(2 * g + 1) * LANE] = _rope(h[:, c_ksl + off:c_ksl + off + LANE], cosf, sinf, lo).astype(BF16)
        kaug_o[:, (2 * g + 1) * LANE:(2 * g + 2) * LANE] = onehot
        kwn_o[:, sl] = _rope(h[:, c_kwn + off:c_kwn + off + LANE], cosf, sinf, lo).astype(BF16)
        vs_t = h[:, c_vsl + off:c_vsl + off + LANE].T.astype(BF16)
        for c in range(ts // SEL_TK):
            ones_rows = jnp.where(lax.broadcasted_iota(jnp.int32, (SEL_VROWS - LANE, SEL_TK), 0) == 0, 1.0, 0.0)
            vslt_o[g, c] = jnp.concatenate([vs_t[:, c * SEL_TK:(c + 1) * SEL_TK], ones_rows.astype(BF16)], axis=0)
        vw_t = h[:, c_vwn + off:c_vwn + off + LANE].T.astype(BF16)
        for c in range(ts // WIN_TK):
            ones_rows = jnp.where(lax.broadcasted_iota(jnp.int32, (SEL_VROWS - LANE, WIN_TK), 0) == 0, 1.0, 0.0)
            vwnt_o[g, c] = jnp.concatenate([vw_t[:, c * WIN_TK:(c + 1) * WIN_TK], ones_rows.astype(BF16)], axis=0)


def _proj_nsa(x3, w_b, cos_tab, sin_tab):
    B, S, D = x3.shape
    ts = PROJ_TM
    tab = pl.BlockSpec((ts, LANE), lambda b, i: (i, 0))
    return pl.pallas_call(
        _proj_nsa_kernel,
        grid=(B, S // ts),
        in_specs=[pl.BlockSpec((None, ts, D), lambda b, i: (b, i, 0)),
                  pl.BlockSpec((D, NB), lambda b, i: (0, 0)), tab, tab],
        out_specs=[pl.BlockSpec((None, WIDTH, ts), lambda b, i: (b, 0, i)),
                   pl.BlockSpec((None, ts, 2 * KVW), lambda b, i: (b, i, 0)),
                   pl.BlockSpec((None, KV_GROUPS, ts // SEL_TK, SEL_VROWS, SEL_TK), lambda b, i: (b, 0, i, 0, 0)),
                   pl.BlockSpec((None, ts, KVW), lambda b, i: (b, i, 0)),
                   pl.BlockSpec((None, KV_GROUPS, ts // WIN_TK, SEL_VROWS, WIN_TK), lambda b, i: (b, 0, i, 0, 0))],
        out_shape=[jax.ShapeDtypeStruct((B, WIDTH, S), BF16),
                   jax.ShapeDtypeStruct((B, S, 2 * KVW), BF16),
                   jax.ShapeDtypeStruct((B, KV_GROUPS, S // SEL_TK, SEL_VROWS, SEL_TK), BF16),
                   jax.ShapeDtypeStruct((B, S, KVW), BF16),
                   jax.ShapeDtypeStruct((B, KV_GROUPS, S // WIN_TK, SEL_VROWS, WIN_TK), BF16)],
        compiler_params=_cparams(("parallel", "parallel")),
        name="proj_nsa",
    )(x3, w_b, cos_tab, sin_tab)


def _compress_one(t_ref, pe_ref, w1_ref, w2_ref, nc):
    half = CMP_LEN // 2
    y1 = jnp.zeros((nc, CMP_HIDDEN), F32)
    y2 = jnp.zeros((nc, CMP_HIDDEN), F32)
    for l in range(half):
        x = t_ref[pl.ds(l, nc, stride=CMP_STRIDE), :]
        y1 = y1 + _dot((x + pe_ref[l:l + 1, :]).astype(BF16), w1_ref[l * LANE:(l + 1) * LANE, :])
        y2 = y2 + _dot((x + pe_ref[half + l:half + l + 1, :]).astype(BF16),
                       w1_ref[(half + l) * LANE:(half + l + 1) * LANE, :])
    y = y1 + pltpu.roll(y2, nc - 1, 0)
    return _dot(_silu(y).astype(BF16), w2_ref[...])


def _compress_kernel(kc_ref, vc_ref, pek_ref, pev_ref, w1k_ref, w2k_ref, w1v_ref, w2v_ref,
                     cos_ref, sin_ref, kc_o, vc_o):
    nc = kc_o.shape[0]
    kc = _compress_one(kc_ref, pek_ref, w1k_ref, w2k_ref, nc)
    lo = lax.broadcasted_iota(jnp.int32, kc.shape, 1) < ROPE_HALF
    kc_o[...] = _rope(kc, cos_ref[...], sin_ref[...], lo).astype(BF16)
    vc_o[...] = _compress_one(vc_ref, pev_ref, w1v_ref, w2v_ref, nc).astype(BF16)


def _compress(h3, pek, pev, w1k, w2k, w1v, w2v, cos_c, sin_c):
    B, S, _ = h3.shape
    nc = S // CMP_STRIDE
    full = lambda shape: pl.BlockSpec(shape, lambda b, g: tuple(0 for _ in shape))
    out_spec = pl.BlockSpec((None, None, nc, LANE), lambda b, g: (b, g, 0, 0))
    out_sd = jax.ShapeDtypeStruct((B, KV_GROUPS, nc, LANE), BF16)
    cin = CMP_LEN * HEAD_DIM
    return pl.pallas_call(
        _compress_kernel,
        grid=(B, KV_GROUPS),
        in_specs=[pl.BlockSpec((None, S, LANE), lambda b, g: (b, 0, g)),
                  pl.BlockSpec((None, S, LANE), lambda b, g: (b, 0, KV_GROUPS + g)),
                  full((CMP_LEN, LANE)), full((CMP_LEN, LANE)),
                  full((cin, CMP_HIDDEN)), full((CMP_HIDDEN, LANE)),
                  full((cin, CMP_HIDDEN)), full((CMP_HIDDEN, LANE)),
                  full((nc, LANE)), full((nc, LANE))],
        out_specs=[out_spec, out_spec],
        out_shape=[out_sd, out_sd],
        compiler_params=_cparams(("parallel", "parallel")),
        name="compress",
    )(h3, h3, pek, pev, w1k, w2k, w1v, w2v, cos_c, sin_c)


def _cmp_attn_kernel(q_ref, kc_ref, vc_ref, ovt_ref, gate_ref, o_ref, bias_ref, st_ref, cnt_ref, psel_ref, *, n_sel):
    i = pl.program_id(2)
    tq = q_ref.shape[1]
    cols = HPG * tq
    nc = kc_ref.shape[0]
    t0 = i * tq
    q_g = jnp.concatenate([q_ref[hh * LANE:(hh + 1) * LANE, :] for hh in range(HPG)], axis=1)
    tpos = t0 + (lax.broadcasted_iota(jnp.int32, (1, cols), 1) & (tq - 1))
    gate = _gate_row(gate_ref[...])

    def attend(rows):
        cmp_end = lax.broadcasted_iota(jnp.int32, (rows, 1), 0) * CMP_STRIDE + (CMP_LEN - 1)
        s = jnp.where(cmp_end <= tpos, _dot(kc_ref[0:rows, :], q_g), -jnp.inf)
        m = jnp.max(s, axis=0, keepdims=True)
        m = jnp.where(m > -jnp.inf, m, 0.0)
        p = jnp.exp2(s - m)
        p = p / jnp.maximum(jnp.sum(p, axis=0, keepdims=True), jnp.finfo(F32).tiny)
        vct = vc_ref[0:rows, :].astype(F32).T.astype(BF16)
        _store_heads(o_ref, _dot(vct, p.astype(BF16)) * gate, 0, tq)
        psum = p[:, 0:tq]
        for hh in range(1, HPG):
            psum = psum + p[:, hh * tq:(hh + 1) * tq]
        ovt = ovt_ref[:, 0:rows]
        p_hi = psum.astype(BF16)
        r1 = psum - p_hi.astype(F32)
        p_mid = r1.astype(BF16)
        p_lo = (r1 - p_mid.astype(F32)).astype(BF16)
        psel_ref[...] = _dot(ovt, p_hi) + (_dot(ovt, p_mid) + _dot(ovt, p_lo))

    n_part = 4
    part = nc // n_part
    n_live = (t0 + tq - CMP_LEN) // CMP_STRIDE + 1
    branch = jnp.clip((n_live - 1) // part, 0, n_part - 1)
    for r in range(n_part):
        pl.when(branch == r)(functools.partial(attend, (r + 1) * part))
    p_sel = psel_ref[...]
    jb = lax.broadcasted_iota(jnp.int32, (LANE, tq), 0)
    cur = (t0 + lax.broadcasted_iota(jnp.int32, (1, tq), 1)) >> int(math.log2(SEL_LEN))
    valid = jb <= cur
    forced = valid & ((jb == 0) | (jb == cur) | (jb == cur - 1))
    st = jnp.where(forced, jnp.inf, jnp.where(valid, p_sel, -jnp.inf))
    n_grp = n_sel // SUBLANE
    st_ref[...] = st
    cnt_ref[...] = jnp.zeros(cnt_ref.shape, F32)
    jrow = lax.broadcasted_iota(jnp.int32, (SUBLANE, tq), 0)
    cur_max = (t0 + tq - 1) >> int(math.log2(SEL_LEN))
    for gb in range(n_grp):
        @pl.when(cur_max >= gb * SUBLANE)
        def _():
            grp = [st_ref[g * SUBLANE:(g + 1) * SUBLANE, :] for g in range(n_grp)]
            cnt = [cnt_ref[g * SUBLANE:(g + 1) * SUBLANE, :] for g in range(n_grp)]
            for rb_ in range(SUBLANE):
                rb = grp[gb][rb_:rb_ + 1, :]
                for g in range(n_grp):
                    if g < gb:
                        inc = jnp.where(rb > grp[g], 1.0, 0.0)
                    elif g > gb:
                        inc = jnp.where(rb >= grp[g], 1.0, 0.0)
                    else:
                        inc = jnp.where(jrow > rb_, jnp.where(rb >= grp[g], 1.0, 0.0), jnp.where(rb > grp[g], 1.0, 0.0))
                    cnt[g] = cnt[g] + inc
            for g in range(n_grp):
                cnt_ref[g * SUBLANE:(g + 1) * SUBLANE, :] = cnt[g]

    keep = (cnt_ref[...] < float(SEL_TOPK)) & (st_ref[0:n_sel, :] > -jnp.inf)
    bias = jnp.where(keep, 0.0, NEG_BIAS)
    if n_sel < LANE:
        bias = jnp.concatenate([bias, jnp.zeros((LANE - n_sel, tq), F32)], axis=0)
    bias_ref[...] = bias.astype(BF16)


def _cmp_attn(q_t, kc, vc, overlap_t, gates_t):
    B, _, S = q_t.shape
    nc = kc.shape[2]
    n_sel = S // SEL_LEN
    tq = CMP_TQ
    gw = HPG * LANE
    return pl.pallas_call(
        functools.partial(_cmp_attn_kernel, n_sel=n_sel),
        grid=(B, KV_GROUPS, S // tq),
        in_specs=[pl.BlockSpec((None, gw, tq), lambda b, g, i: (b, g, i)),
                  pl.BlockSpec((None, None, nc, LANE), lambda b, g, i: (b, g, 0, 0)),
                  pl.BlockSpec((None, None, nc, LANE), lambda b, g, i: (b, g, 0, 0)),
                  pl.BlockSpec((LANE, nc), lambda b, g, i: (0, 0)),
                  pl.BlockSpec((None, None, None, HPG, tq), lambda b, g, i: (b, 0, g, 0, i))],
        out_specs=[pl.BlockSpec((None, tq, gw), lambda b, g, i: (b, i, g)),
                   pl.BlockSpec((None, None, LANE, tq), lambda b, g, i: (b, g, 0, i))],
        out_shape=[jax.ShapeDtypeStruct((B, S, WIDTH), BF16),
                   jax.ShapeDtypeStruct((B, KV_GROUPS, LANE, S), BF16)],
        scratch_shapes=[pltpu.VMEM((LANE, tq), F32), pltpu.VMEM((n_sel, tq), F32), pltpu.VMEM((LANE, tq), F32)],
        compiler_params=_cparams(("parallel", "parallel", "parallel")),
        name="cmp_attn",
    )(q_t, kc, vc, overlap_t, gates_t)


FLASH_TQ = 256
CMP_TQ = 1024


def _gate_row(logits):
    gt = _sigmoid(logits)
    return jnp.concatenate([gt[hh:hh + 1, :] for hh in range(HPG)], axis=1)


def _store_heads(o_ref, out_t, g, tq):
    for hh in range(HPG):
        h = g * HPG + hh
        o_ref[:, h * LANE:(h + 1) * LANE] = out_t[:, hh * tq:(hh + 1) * tq].T.astype(BF16)


def _flash_kernel(q_ref, b_ref, k_ref, vt_ref, kw_ref, vwt_ref, wb_ref, gate_ref, o_ref, ow_ref,
                  qs_ref, s_ref, m_ref, acc_ref, sw_ref, pw_ref, *, tk):
    i = pl.program_id(1)
    tq = q_ref.shape[1]
    cols = HPG * tq
    t0 = i * tq
    kw = qs_ref.shape[1]
    for g in range(KV_GROUPS):
        for hh in range(HPG):
            h = g * HPG + hh
            qs_ref[g, 0:LANE, hh * tq:(hh + 1) * tq] = q_ref[h * LANE:(h + 1) * LANE, :]
            qs_ref[g, LANE:2 * LANE, hh * tq:(hh + 1) * tq] = b_ref[g]
    m_ref[...] = jnp.full(m_ref.shape, -jnp.inf, F32)
    acc_ref[...] = jnp.zeros(acc_ref.shape, F32)
    qpos = t0 + (lax.broadcasted_iota(jnp.int32, (1, cols), 1) & (tq - 1))

    def scores(j, slot):
        k0 = pl.multiple_of(j * tk, tk)
        for g in range(KV_GROUPS):
            s_ref[slot, g] = _dot(k_ref[pl.ds(k0, tk), g * kw:(g + 1) * kw], qs_ref[g])

    def update(j, slot, masked):
        if masked:
            mask = j * tk + lax.broadcasted_iota(jnp.int32, (tk, 1), 0) <= qpos
        for g in range(KV_GROUPS):
            s = s_ref[slot, g]
            if masked:
                s = jnp.where(mask, s, -jnp.inf)
            m_old = m_ref[g]
            m_new = jnp.maximum(m_old, jnp.max(s, axis=0, keepdims=True))
            alpha = jnp.exp2(m_old - m_new)
            p = jnp.exp2(s - m_new)
            acc_ref[g] = alpha * acc_ref[g] + _dot(vt_ref[g, j], p.astype(BF16))
            m_ref[g] = m_new

    jd = t0 // tk
    last = jnp.maximum(jd - 1, 0)
    win_scores, win_softmax = _window_pass(i, kw_ref, vwt_ref, wb_ref, gate_ref, ow_ref, sw_ref, pw_ref, tq, WIN_TK)
    scores(jd, 0)
    win_scores(0, qs_ref[0, 0:LANE, :])
    update(jd, 0, True)
    win_scores(1, qs_ref[1, 0:LANE, :])
    win_softmax(0)
    scores(0, 1)
    win_softmax(1)

    def body(p, carry):
        scores(jnp.minimum(2 * p + 1, last), 0)
        update(2 * p, 1, False)
        scores(jnp.minimum(2 * p + 2, last), 1)
        update(2 * p + 1, 0, False)
        return carry

    lax.fori_loop(0, jd // 2, body, 0)

    @pl.when(jd % 2 == 1)
    def _():
        update(jd - 1, 1, False)

    for g in range(KV_GROUPS):
        out = acc_ref[g, 0:LANE] / jnp.maximum(acc_ref[g, LANE:LANE + 1], jnp.finfo(F32).tiny)
        _store_heads(o_ref, out * _gate_row(gate_ref[1, g]), g, tq)


def _window_pass(i, k_ref, vt_ref, wb_ref, gate_ref, o_ref, s_ref, p_ref, tq, tk):
    n_t = WINDOW // tk + 1
    jt = [jnp.maximum(i - d, 0) for d in range(n_t)]
    static_bias = {0: 0, n_t - 1: 1}

    def scores(g, q_g):
        for d in range(n_t):
            k0 = pl.multiple_of(jt[d] * tk, tk)
            s_ref[g, d] = _dot(k_ref[pl.ds(k0, tk), g * LANE:(g + 1) * LANE], q_g)

    def tile(g, d):
        s = s_ref[g, d]
        return s + wb_ref[static_bias[d]] if d in static_bias else s

    def softmax(g):
        m = jnp.max(tile(g, 0), axis=0, keepdims=True)
        for d in range(1, n_t):
            m = jnp.maximum(m, jnp.where(i >= d, jnp.max(tile(g, d), axis=0, keepdims=True), -jnp.inf))
        for d in range(n_t):
            p = jnp.exp2(tile(g, d) - (m if d == 0 else jnp.where(i >= d, m, jnp.inf)))
            p_ref[g, d * tk:(d + 1) * tk, :] = p.astype(BF16)
        v_cat = jnp.concatenate([vt_ref[g, jt[d]] for d in range(n_t)], axis=1)
        acc = _dot(v_cat, p_ref[g])
        out = acc[0:LANE] / jnp.maximum(acc[LANE:LANE + 1], jnp.finfo(F32).tiny)
        _store_heads(o_ref, out * _gate_row(gate_ref[2, g]), g, tq)

    return scores, softmax


def _window_bias(tq):
    r = np.arange(tq)[:, None]
    c = (np.arange(HPG * tq) % tq)[None, :]
    neg = np.float32(-np.inf)
    return jnp.asarray(np.stack([np.where(r <= c, np.float32(0), neg), np.where(c < r, np.float32(0), neg)]))


def _flash(q_t, kaug, vsl_t, bias_t, kwn, vwn_t, gates_t):
    B, _, S = q_t.shape
    tq = FLASH_TQ
    tk = SEL_TK
    tw = WIN_TK
    assert tk % tq == 0 and tw == tq and WINDOW == 2 * tw
    kw = kaug.shape[2] // KV_GROUPS
    cols = HPG * tq
    out_spec = pl.BlockSpec((None, tq, WIDTH), lambda b, i: (b, i, 0))
    out_sd = jax.ShapeDtypeStruct((B, S, WIDTH), BF16)
    return pl.pallas_call(
        functools.partial(_flash_kernel, tk=tk),
        grid=(B, S // tq),
        in_specs=[pl.BlockSpec((None, WIDTH, tq), lambda b, i: (b, 0, i)),
                  pl.BlockSpec((None, KV_GROUPS, LANE, tq), lambda b, i: (b, 0, 0, i)),
                  pl.BlockSpec((None, S, KV_GROUPS * kw), lambda b, i: (b, 0, 0)),
                  pl.BlockSpec((None, KV_GROUPS, S // tk, SEL_VROWS, tk), lambda b, i: (b, 0, 0, 0, 0)),
                  pl.BlockSpec((None, S, KVW), lambda b, i: (b, 0, 0)),
                  pl.BlockSpec((None, KV_GROUPS, S // tw, SEL_VROWS, tw), lambda b, i: (b, 0, 0, 0, 0)),
                  pl.BlockSpec((2, tw, cols), lambda b, i: (0, 0, 0)),
                  pl.BlockSpec((None, 3, KV_GROUPS, HPG, tq), lambda b, i: (b, 0, 0, 0, i))],
        out_specs=[out_spec, out_spec],
        out_shape=[out_sd, out_sd],
        scratch_shapes=[pltpu.VMEM((KV_GROUPS, kw, cols), BF16),
                        pltpu.VMEM((2, KV_GROUPS, tk, cols), F32),
                        pltpu.VMEM((KV_GROUPS, 1, cols), F32),
                        pltpu.VMEM((KV_GROUPS, SEL_VROWS, cols), F32),
                        pltpu.VMEM((KV_GROUPS, WINDOW // tw + 1, tw, cols), F32),
                        pltpu.VMEM((KV_GROUPS, (WINDOW // tw + 1) * tw, cols), BF16)],
        compiler_params=_cparams(("parallel", "parallel")),
        name="flash",
    )(q_t, bias_t, kaug, vsl_t, kwn, vwn_t, _window_bias(tq), gates_t)


def _merge_kernel(ydn_ref, oc_ref, os_ref, ow_ref, mgd_ref, mgn_ref, x_ref,
                  wdn_ref, wnsa_ref, wout_ref, g_ref, b_ref, o_ref):
    y_nsa = (oc_ref[...].astype(F32) + os_ref[...].astype(F32) + ow_ref[...].astype(F32)).astype(BF16)
    y = (_sigmoid(mgd_ref[...].astype(F32)) * _dot(ydn_ref[...], wdn_ref[...])
         + _sigmoid(mgn_ref[...].astype(F32)) * _dot(y_nsa, wnsa_ref[...]))
    z = DEEPNORM_ALPHA * x_ref[...] + _dot(y.astype(BF16), wout_ref[...])
    o_ref[...] = _layer_norm(z, g_ref[...], b_ref[...])


def _merge(y_dn2, oc2, os2, ow2, h, x2, wdn, wnsa, wout, g, b):
    T, D = x2.shape
    tm = PROJ_TM
    row = pl.BlockSpec((tm, WIDTH), lambda i: (i, 0))
    full = lambda shape: pl.BlockSpec(shape, lambda i: (0, 0))
    return pl.pallas_call(
        _merge_kernel,
        grid=(T // tm,),
        in_specs=[row, row, row, row,
                  pl.BlockSpec((tm, WIDTH), lambda i: (i, COL_MG_DN // WIDTH)),
                  pl.BlockSpec((tm, WIDTH), lambda i: (i, COL_MG_NSA // WIDTH)),
                  pl.BlockSpec((tm, D), lambda i: (i, 0)),
                  full((WIDTH, D)), full((WIDTH, D)), full((D, D)), full((1, D)), full((1, D))],
        out_specs=pl.BlockSpec((tm, D), lambda i: (i, 0)),
        out_shape=jax.ShapeDtypeStruct((T, D), F32),
        compiler_params=_cparams(("parallel",)),
        name="merge",
    )(y_dn2, oc2, os2, ow2, h, h, x2, wdn, wnsa, wout, g, b)


FFN_TM = 1024


def _ffn_up_kernel(x_ref, xp_ref, wg_ref, wv_ref, cwg_ref, cwv_ref, bg_ref, bv_ref, o_ref, xs_ref, *, tiles_per_seq):
    i = pl.program_id(1)
    tm = x_ref.shape[0]
    nv = tm // SUBLANE
    n_slab = x_ref.shape[1] // LANE
    for sp in range(SUBLANE):
        for c in range(n_slab):
            xs_ref[c, pl.ds(sp, nv, stride=SUBLANE), :] = x_ref[sp * nv:(sp + 1) * nv, c * LANE:(c + 1) * LANE]
    xb = jnp.concatenate([xs_ref[c] for c in range(n_slab)], axis=1).astype(BF16)
    notfirst = (i % tiles_per_seq != 0).astype(F32)
    xpb = (xp_ref[...] * notfirst).astype(BF16)
    sub = lax.broadcasted_iota(jnp.int32, (SUBLANE, 1), 0)

    def conv(w_ref, cw_ref, b_ref):
        u = _dot(xb, w_ref[...])
        p = _dot(xpb, w_ref[...])

        def wrap(k):
            blk = u[(nv - k) * SUBLANE:(nv - k + 1) * SUBLANE, :]
            return jnp.where(sub == 0, p[SUBLANE - k:SUBLANE - k + 1, :], pltpu.roll(blk, 1, 0))

        w1, w2 = wrap(1), wrap(2)
        u1 = jnp.concatenate([w1, u[0:(nv - 1) * SUBLANE, :]], axis=0)
        u2 = jnp.concatenate([w2, w1, u[0:(nv - 2) * SUBLANE, :]], axis=0)
        return cw_ref[0:1, :] * u2 + cw_ref[1:2, :] * u1 + cw_ref[2:3, :] * u + b_ref[...]

    o_ref[...] = (_silu(conv(wg_ref, cwg_ref, bg_ref)) * conv(wv_ref, cwv_ref, bv_ref)).astype(BF16)


def _ffn_up(x1, w_up, conv_w, conv_b, seq):
    T, D = x1.shape
    tm = FFN_TM
    nj = 2
    tn = D_FF // nj
    return pl.pallas_call(
        functools.partial(_ffn_up_kernel, tiles_per_seq=seq // tm),
        grid=(nj, T // tm),
        in_specs=[pl.BlockSpec((tm, D), lambda j, i: (i, 0)),
                  pl.BlockSpec((SUBLANE, D), lambda j, i: (jnp.maximum(i * (tm // SUBLANE) - 1, 0), 0)),
                  pl.BlockSpec((D, tn), lambda j, i: (0, j)),
                  pl.BlockSpec((D, tn), lambda j, i: (0, j + nj)),
                  pl.BlockSpec((FFN_CONV, tn), lambda j, i: (0, j)),
                  pl.BlockSpec((FFN_CONV, tn), lambda j, i: (0, j + nj)),
                  pl.BlockSpec((1, tn), lambda j, i: (0, j)),
                  pl.BlockSpec((1, tn), lambda j, i: (0, j + nj))],
        out_specs=pl.BlockSpec((tm, tn), lambda j, i: (i, j)),
        out_shape=jax.ShapeDtypeStruct((T, D_FF), BF16),
        scratch_shapes=[pltpu.VMEM((D // LANE, tm, LANE), F32)],
        compiler_params=_cparams(("parallel", "parallel")),
        name="ffn_up",
    )(x1, x1, w_up, w_up, conv_w, conv_w, conv_b, conv_b)


def _ffn_down_kernel(a_ref, w_ref, x_ref, g_ref, b_ref, o_ref, fs_ref):
    tm, d = x_ref.shape
    nv = tm // SUBLANE
    n_slab = d // LANE
    f = _dot(a_ref[...], w_ref[...])
    for c in range(n_slab):
        fs_ref[c] = f[:, c * LANE:(c + 1) * LANE]
    for sp in range(SUBLANE):
        rows = slice(sp * nv, (sp + 1) * nv)
        f_nat = jnp.concatenate([fs_ref[c, pl.ds(sp, nv, stride=SUBLANE), :] for c in range(n_slab)], axis=1)
        z = DEEPNORM_ALPHA * x_ref[rows, :] + f_nat
        o_ref[rows, :] = _layer_norm(z, g_ref[...], b_ref[...])


def _ffn_down(a, w_down, x1, g, b):
    T, D = x1.shape
    tm = FFN_TM
    full = lambda shape: pl.BlockSpec(shape, lambda i: (0, 0))
    return pl.pallas_call(
        _ffn_down_kernel,
        grid=(T // tm,),
        in_specs=[pl.BlockSpec((tm, D_FF), lambda i: (i, 0)), full((D_FF, D)),
                  pl.BlockSpec((tm, D), lambda i: (i, 0)), full((1, D)), full((1, D))],
        out_specs=pl.BlockSpec((tm, D), lambda i: (i, 0)),
        out_shape=jax.ShapeDtypeStruct((T, D), F32),
        scratch_shapes=[pltpu.VMEM((D // LANE, tm, LANE), F32)],
        compiler_params=_cparams(("parallel",)),
        name="ffn_down",
    )(a, w_down, x1, g, b)


def _split_w_in_kernel(wt_ref, a_ref, b_ref, c_ref):
    o_dz = 3 * WIDTH
    o_db = 4 * WIDTH
    o_nq = o_db + 2 * HEADS
    o_kc = o_nq + WIDTH
    o_ksl = o_kc + 2 * KVW
    o_gate = o_ksl + 4 * KVW
    o_mg = o_gate + 3 * HEADS
    slab = wt_ref.shape[1]

    def put(dst, o0, c0, c1):
        dst[:, o0:o0 + (c1 - c0)] = wt_ref[c0:c1, :].T.astype(BF16)

    put(a_ref, 0, 0, o_dz)
    small = jnp.concatenate([wt_ref[o_db:o_nq, :], wt_ref[o_gate:o_mg, :],
                             jnp.zeros((LANE - 5 * HEADS, slab), F32)], axis=0)
    a_ref[:, o_dz:NA] = small.T.astype(BF16)
    put(b_ref, 0, o_nq, o_kc)
    put(b_ref, WIDTH, o_ksl, o_gate)
    put(c_ref, 0, o_dz, o_db)
    put(c_ref, WIDTH, o_mg, o_mg + 2 * WIDTH)
    put(c_ref, 3 * WIDTH, o_kc, o_ksl)


def _split_w_in(w, layer):
    _, d, n_in = w.shape
    slab = 2 * LANE
    return pl.pallas_call(
        _split_w_in_kernel,
        grid=(d // slab,),
        in_specs=[pl.BlockSpec((None, n_in, slab), lambda i: (layer, 0, i))],
        out_specs=[pl.BlockSpec((slab, NA), lambda i: (i, 0)), pl.BlockSpec((slab, NB), lambda i: (i, 0)),
                   pl.BlockSpec((slab, NC_COLS), lambda i: (i, 0))],
        out_shape=[jax.ShapeDtypeStruct((d, NA), BF16), jax.ShapeDtypeStruct((d, NB), BF16),
                   jax.ShapeDtypeStruct((d, NC_COLS), BF16)],
        compiler_params=_cparams(("parallel",)),
        name="split_w_in",
    )(jnp.swapaxes(w, 1, 2))


def _rope_tables(pos):
    pos = np.asarray(pos)
    inv_freq = np.float32(ROPE_THETA) ** (-(np.arange(ROPE_HALF, dtype=np.float32) / np.float32(ROPE_HALF)))
    ang = pos.astype(np.float32)[:, None] * inv_freq.astype(np.float32)
    cos = np.cos(ang.astype(np.float64)).astype(np.float32)
    sin = np.sin(ang.astype(np.float64)).astype(np.float32)
    p = pos.shape[0]
    cosf = np.concatenate([cos, cos, np.ones((p, LANE - ROPE_DIM), np.float32)], axis=1)
    sinf = np.concatenate([-sin, sin, np.zeros((p, LANE - ROPE_DIM), np.float32)], axis=1)
    return jnp.asarray(cosf), jnp.asarray(sinf)


def _overlap_table(nc, n_sel):
    n = np.arange(nc)[None, :]
    j = np.arange(LANE)[:, None]
    ov = (n * CMP_STRIDE <= j * SEL_LEN + SEL_LEN - 1) & (n * CMP_STRIDE + CMP_LEN - 1 >= j * SEL_LEN)
    ov = ov & (j < n_sel) & (n < nc - 1)
    return jnp.asarray(ov.astype(np.float32)).astype(BF16)


def _lane_row(vals, offset):
    return jnp.zeros((1, LANE), F32).at[0, offset:offset + vals.shape[0]].set(vals.astype(F32))


def kernel(x, w_in, dn_conv_w, dn_a_log, dn_dt_bias, dn_norm_w, cmp_pos_k, cmp_pos_v, cmp_k_w1, cmp_k_w2,
           cmp_v_w1, cmp_v_w2, w_branch_dn, w_branch_nsa, w_out, ln1_g, ln1_b, ffn_w_up, ffn_conv_w,
           ffn_conv_b, ffn_w_down, ln2_g, ln2_b):
    B, S, D = x.shape
    T = B * S
    nc = S // CMP_STRIDE
    n_sel = S // SEL_LEN
    assert D == WIDTH and S % FFN_TM == 0 and S % PROJ_TM == 0 and n_sel <= LANE and n_sel % SUBLANE == 0
    cos_t, sin_t = _rope_tables(np.arange(S))
    cos_c, sin_c = _rope_tables(np.arange(nc) * CMP_STRIDE + (CMP_LEN - 1))
    overlap = _overlap_table(nc, n_sel)
    x2 = x.reshape(T, D)
    for layer in range(w_in.shape[0]):
        w_a, w_b, w_c = _split_w_in(w_in, layer)
        x3 = x2.reshape(B, S, D)
        h3, kv_cmp = _proj_rest(x3, w_c)

        q, k, v, bg, small = _proj_gdn(x3, w_a, dn_conv_w[layer], _lane_row(dn_a_log[layer], SMALL_DECAY),
                                       _lane_row(dn_dt_bias[layer], SMALL_DECAY))
        y_dn = _gdn(q, k, v, bg, h3, dn_norm_w[layer].reshape(1, LANE))

        q_t, kaug, vsl_t, kwn_r, vwn_t = _proj_nsa(x3, w_b, cos_t, sin_t)
        kc, vc = _compress(kv_cmp, cmp_pos_k[layer], cmp_pos_v[layer],
                           cmp_k_w1[layer].astype(BF16), cmp_k_w2[layer].astype(BF16),
                           cmp_v_w1[layer].astype(BF16), cmp_v_w2[layer].astype(BF16), cos_c, sin_c)
        gates_t = jnp.transpose(small[..., SMALL_GATE:SMALL_GATE + 3 * HEADS].reshape(B, S, 3, KV_GROUPS, HPG),
                                (0, 2, 3, 4, 1))
        o_cmp, bias = _cmp_attn(q_t, kc, vc, overlap, gates_t)
        o_sel, o_win = _flash(q_t, kaug, vsl_t, bias, kwn_r, vwn_t, gates_t)

        x1 = _merge(y_dn.reshape(T, WIDTH), o_cmp.reshape(T, WIDTH), o_sel.reshape(T, WIDTH),
                    o_win.reshape(T, WIDTH), h3.reshape(T, COL_KC), x2,
                    w_branch_dn[layer].astype(BF16), w_branch_nsa[layer].astype(BF16),
                    w_out[layer].astype(BF16), ln1_g[layer].reshape(1, D), ln1_b[layer].reshape(1, D))
        a = _ffn_up(x1, ffn_w_up[layer].astype(BF16), ffn_conv_w[layer],
                    ffn_conv_b[layer].reshape(1, 2 * D_FF), S)
        x2 = _ffn_down(a, ffn_w_down[layer].astype(BF16), x1, ln2_g[layer].reshape(1, D), ln2_b[layer].reshape(1, D))
    return x2.reshape(B, S, D)
```

```python
import functools
import math

import numpy as np
import jax
import jax.numpy as jnp
from jax import lax
from jax.experimental import pallas as pl
from jax.experimental.pallas import tpu as pltpu

F32 = jnp.float32
BF16 = jnp.bfloat16
HI = lax.Precision.HIGHEST

HEADS = 8
HEAD_DIM = 128
KV_GROUPS = 2
HPG = HEADS // KV_GROUPS
DN_CONV = 4
DN_CHUNK = 64
CMP_LEN = 32
CMP_STRIDE = 16
CMP_HIDDEN = 256
SEL_LEN = 64
SEL_TOPK = 16
WINDOW = 512
ROPE_THETA = 500000.0
ROPE_DIM = HEAD_DIM // 4
ROPE_HALF = ROPE_DIM // 2
D_FF = 2816
FFN_CONV = 3
LN_EPS = 1e-5
RMS_EPS = 1e-6
DEEPNORM_ALPHA = 2.0 ** 0.25
ATT_SCALE = HEAD_DIM ** -0.5

LANE = 128
SUBLANE = 8
VMEM_LIMIT = 56 * 1024 * 1024

WIDTH = HEADS * HEAD_DIM
KVW = KV_GROUPS * HEAD_DIM
NA = 3 * WIDTH + LANE
SMALL_BETA = 0
SMALL_DECAY = HEADS
SMALL_GATE = 2 * HEADS
N_GATES = 3 * HEADS
NB = WIDTH + 4 * KVW
NC_COLS = 3 * WIDTH + 2 * KVW
COL_DZ = 0
COL_MG_DN = WIDTH
COL_MG_NSA = 2 * WIDTH
COL_KC = 3 * WIDTH

NEG_BIAS = -(2.0 ** 100)


def _cparams(sem):
    return pltpu.CompilerParams(dimension_semantics=sem, vmem_limit_bytes=VMEM_LIMIT)


def _sigmoid(x):
    return 1.0 / (1.0 + jnp.exp(-x))


def _silu(x):
    return x * _sigmoid(x)


def _dot(a, b, precision=None):
    return jnp.dot(a, b, preferred_element_type=F32, precision=precision)


def _dot_nt(a, b):
    return lax.dot_general(a, b, (((1,), (1,)), ((), ())), preferred_element_type=F32)


def _dot_tn(a, b):
    return lax.dot_general(a, b, (((0,), (0,)), ((), ())), preferred_element_type=F32)


def _layer_norm(z, g, b):
    mu = jnp.mean(z, -1, keepdims=True)
    zc = z - mu
    var = jnp.mean(zc * zc, -1, keepdims=True)
    return zc * lax.rsqrt(var + LN_EPS) * g + b


PROJ_TM = 512


def _proj_rest_kernel(x_ref, w_ref, g_ref, c_ref):
    h = _dot(x_ref[...].astype(BF16), w_ref[...])
    g_ref[...] = h[:, 0:COL_KC].astype(BF16)
    c_ref[...] = h[:, COL_KC:NC_COLS]


def _proj_rest(x3, w):
    B, S, D = x3.shape
    tm = PROJ_TM
    return pl.pallas_call(
        _proj_rest_kernel,
        grid=(B, S // tm),
        in_specs=[pl.BlockSpec((None, tm, D), lambda b, i: (b, i, 0)),
                  pl.BlockSpec((D, NC_COLS), lambda b, i: (0, 0))],
        out_specs=[pl.BlockSpec((None, tm, COL_KC), lambda b, i: (b, i, 0)),
                   pl.BlockSpec((None, tm, NC_COLS - COL_KC), lambda b, i: (b, i, 0))],
        out_shape=[jax.ShapeDtypeStruct((B, S, COL_KC), BF16),
                   jax.ShapeDtypeStruct((B, S, NC_COLS - COL_KC), F32)],
        compiler_params=_cparams(("parallel", "parallel")),
        name="proj_rest",
    )(x3, w)


def _proj_gdn_kernel(x_ref, w_ref, cw_ref, alog_ref, dtb_ref, q_ref, k_ref, v_ref, bg_ref, gate_ref,
                     xs_ref, ys_ref, carry_ref):
    i = pl.program_id(1)
    ts = x_ref.shape[0]
    nv = ts // SUBLANE
    wq = 3 * WIDTH
    n_slab = x_ref.shape[1] // LANE

    @pl.when(i == 0)
    def _():
        carry_ref[...] = jnp.zeros(carry_ref.shape, F32)

    for sp in range(SUBLANE):
        for c in range(n_slab):
            xs_ref[c, pl.ds(sp, nv, stride=SUBLANE), :] = x_ref[sp * nv:(sp + 1) * nv, c * LANE:(c + 1) * LANE]
    xb = jnp.concatenate([xs_ref[c] for c in range(n_slab)], axis=1).astype(BF16)
    h = _dot(xb, w_ref[...])
    sub = lax.broadcasted_iota(jnp.int32, (SUBLANE, 1), 0)

    def token_order(c, dst_ref, dst_sl, dtype):
        for sp in range(SUBLANE):
            dst_ref[sp * nv:(sp + 1) * nv, dst_sl] = ys_ref[c, pl.ds(sp, nv, stride=SUBLANE), :].astype(dtype)

    for c in range(3 * HEADS):
        sl = slice(c * LANE, (c + 1) * LANE)
        col = h[:, sl]
        wrap = [jnp.where(sub == 0, carry_ref[SUBLANE - k:SUBLANE - k + 1, sl],
                          pltpu.roll(col[(nv - k) * SUBLANE:(nv - k + 1) * SUBLANE, :], 1, 0))
                for k in range(1, DN_CONV)]
        acc = cw_ref[DN_CONV - 1:DN_CONV, sl] * col
        for s in range(1, DN_CONV):
            shifted = jnp.concatenate(wrap[0:s][::-1] + [col[0:(nv - s) * SUBLANE, :]], axis=0)
            acc = acc + cw_ref[DN_CONV - 1 - s:DN_CONV - s, sl] * shifted
        y = _silu(acc)
        if c < 2 * HEADS:
            y = y * lax.rsqrt(jnp.sum(y * y, -1, keepdims=True) + RMS_EPS)
        if c < HEADS:
            y = y * ATT_SCALE
        ys_ref[c] = y
        dst = (q_ref, k_ref, v_ref)[c // HEADS]
        token_order(c, dst, slice((c % HEADS) * LANE, (c % HEADS + 1) * LANE), BF16)
    for k in range(1, DN_CONV):
        carry_ref[SUBLANE - k:SUBLANE - k + 1, :] = h[(nv - k + 1) * SUBLANE - 1:(nv - k + 1) * SUBLANE, 0:wq]
    small = h[:, wq:wq + LANE]
    beta = _sigmoid(small)
    xs = small + dtb_ref[...]
    softplus = jnp.maximum(xs, 0.0) + jnp.log1p(jnp.exp(-jnp.abs(xs)))
    g = -jnp.exp(alog_ref[...]) * softplus
    lane = lax.broadcasted_iota(jnp.int32, small.shape, 1)
    ys_ref[3 * HEADS] = small
    tok = jnp.concatenate([ys_ref[3 * HEADS, pl.ds(sp, nv, stride=SUBLANE), :] for sp in range(SUBLANE)], axis=0)
    gate_ref[...] = tok.T[SMALL_GATE:SMALL_GATE + N_GATES, :]
    ys_ref[3 * HEADS + 1] = jnp.where(lane < SMALL_DECAY, beta, g)
    token_order(3 * HEADS + 1, bg_ref, slice(0, LANE), F32)


def _proj_gdn(x3, w_a, conv_w, alog_row, dtb_row):
    B, S, D = x3.shape
    ts = PROJ_TM
    wq = 3 * WIDTH
    out_sd = jax.ShapeDtypeStruct((B, S, WIDTH), BF16)
    small_sd = jax.ShapeDtypeStruct((B, S, LANE), F32)
    wide = pl.BlockSpec((None, ts, WIDTH), lambda b, i: (b, i, 0))
    narrow = pl.BlockSpec((None, ts, LANE), lambda b, i: (b, i, 0))
    return pl.pallas_call(
        _proj_gdn_kernel,
        grid=(B, S // ts),
        in_specs=[
            pl.BlockSpec((None, ts, D), lambda b, i: (b, i, 0)),
            pl.BlockSpec((D, NA), lambda b, i: (0, 0)),
            pl.BlockSpec((DN_CONV, wq), lambda b, i: (0, 0)),
            pl.BlockSpec((1, LANE), lambda b, i: (0, 0)),
            pl.BlockSpec((1, LANE), lambda b, i: (0, 0)),
        ],
        out_specs=[wide, wide, wide, narrow, pl.BlockSpec((None, N_GATES, ts), lambda b, i: (b, 0, i))],
        out_shape=[out_sd, out_sd, out_sd, small_sd, jax.ShapeDtypeStruct((B, N_GATES, S), F32)],
        scratch_shapes=[pltpu.VMEM((D // LANE, ts, LANE), F32),
                        pltpu.VMEM((3 * HEADS + 2, ts, LANE), F32),
                        pltpu.VMEM((SUBLANE, wq), F32)],
        compiler_params=_cparams(("parallel", "arbitrary")),
        name="proj_gdn",
    )(x3, w_a, conv_w, alog_row, dtb_row)


GDN_CHUNKS_PER_STEP = 4
GDN_BATCH_PER_STEP = 2


GDN_QUAD = 4


def _gdn_kernel(q_ref, k_ref, v_ref, bg_ref, z_ref, nw_ref, y_ref, state_ref):
    n = pl.program_id(1)

    @pl.when(n == 0)
    def _():
        state_ref[...] = jnp.zeros(state_ref.shape, F32)

    C = DN_CHUNK
    QW = GDN_QUAD * C
    QL = GDN_QUAD * LANE
    shift_c = int(math.log2(C))
    r64 = lax.broadcasted_iota(jnp.int32, (C, C), 0)
    c64 = lax.broadcasted_iota(jnp.int32, (C, C), 1)
    tril = jnp.where(r64 >= c64, 1.0, 0.0).astype(F32)
    triu = jnp.where(r64 <= c64, 1.0, 0.0).astype(F32)
    rr = lax.broadcasted_iota(jnp.int32, (C, QW), 0)
    ll = lax.broadcasted_iota(jnp.int32, (C, QW), 1)
    jj = ll & (C - 1)
    hb = ll >> shift_c
    incl_t = rr >= jj
    strict_t = rr > jj
    eye_t = jnp.where(rr == jj, 1.0, 0.0).astype(F32)
    bd_c = jnp.where((lax.broadcasted_iota(jnp.int32, (QW, QW), 0) >> shift_c)
                     == (lax.broadcasted_iota(jnp.int32, (QW, QW), 1) >> shift_c), 1.0, 0.0).astype(BF16)
    bd_l = jnp.where((lax.broadcasted_iota(jnp.int32, (QW, QL), 0) >> shift_c)
                     == (lax.broadcasted_iota(jnp.int32, (QW, QL), 1) >> int(math.log2(LANE))), 1.0, 0.0).astype(BF16)
    zeros_rhs = jnp.zeros((C, 2 * LANE), BF16)
    nw = nw_ref[...]
    nb = q_ref.shape[0]
    quads = [(bb, c, a) for bb in range(nb) for c in range(GDN_CHUNKS_PER_STEP) for a in range(HEADS // GDN_QUAD)]
    gc_cols, gc_rows, bgs = {}, {}, {}
    for bb in range(nb):
        bg = bg_ref[bb]
        bg_t = bg.T
        bgs[bb] = bg
        for c in range(GDN_CHUNKS_PER_STEP):
            rs = slice(c * C, (c + 1) * C)
            gc_cols[bb, c] = _dot(tril, bg[rs, :], HI)
            gc_rows[bb, c] = _dot(bg_t[SMALL_DECAY:SMALL_DECAY + HEADS, rs], triu, HI)
    decay, lower, qk, rhs_bd, qd, kd, eglast = {}, {}, {}, {}, {}, {}, {}
    for u_ in quads:
        bb, c, a = u_
        rs = slice(c * C, (c + 1) * C)
        h0 = a * GDN_QUAD
        gcols = [gc_cols[bb, c][:, SMALL_DECAY + h0 + hh:SMALL_DECAY + h0 + hh + 1] for hh in range(GDN_QUAD)]
        gcol_cat = jnp.broadcast_to(gcols[0], (C, QW))
        for hh in range(1, GDN_QUAD):
            gcol_cat = jnp.where(hb == hh, gcols[hh], gcol_cat)
        grow_cat = jnp.concatenate([gc_rows[bb, c][h0 + hh:h0 + hh + 1, :] for hh in range(GDN_QUAD)], axis=1)
        decay[u_] = jnp.exp(jnp.where(incl_t, gcol_cat - grow_cat, -jnp.inf))
        kq = k_ref[bb, rs, a * QL:(a + 1) * QL].astype(F32)
        qq = q_ref[bb, rs, a * QL:(a + 1) * QL].astype(F32)
        kbs, rows_bd = [], []
        for hh in range(GDN_QUAD):
            h = h0 + hh
            ls = slice(hh * LANE, (hh + 1) * LANE)
            beta = bgs[bb][rs, SMALL_BETA + h:SMALL_BETA + h + 1]
            eg = jnp.exp(gcols[hh])
            glast = gcols[hh][C - 1:C, :]
            kb = kq[:, ls] * beta
            vb = v_ref[bb, rs, h * LANE:(h + 1) * LANE].astype(F32) * beta
            kbs.append(kb)
            rhs_h = jnp.concatenate([vb, kb * eg], axis=1).astype(BF16)
            rows_bd.append(jnp.concatenate([zeros_rhs] * hh + [rhs_h] + [zeros_rhs] * (GDN_QUAD - 1 - hh), axis=1))
            qd[bb, c, h] = (qq[:, ls] * eg).astype(BF16)
            kd[bb, c, h] = (kq[:, ls] * jnp.exp(glast - gcols[hh])).astype(BF16)
            eglast[bb, c, h] = jnp.exp(glast)
        rhs_bd[u_] = jnp.concatenate(rows_bd, axis=0)
        k_bd = jnp.concatenate([kq.astype(BF16)] * GDN_QUAD, axis=0) * bd_l
        lhs = jnp.concatenate([jnp.concatenate(kbs, axis=1), qq], axis=0).astype(BF16)
        aq = _dot_nt(lhs, k_bd)
        lower[u_] = jnp.where(strict_t, aq[0:C] * decay[u_], 0.0)
        qk[u_] = (aq[C:2 * C] * decay[u_]).astype(BF16)

    def block_diag(y_b):
        return jnp.concatenate([y_b] * GDN_QUAD, axis=0) * bd_c

    tinv = {u_: eye_t - lower[u_] for u_ in quads}
    y_b = {u_: lower[u_].astype(BF16) for u_ in quads}
    ypow = {u_: _dot(y_b[u_], block_diag(y_b[u_])) for u_ in quads}
    for it in range(5):
        for u_ in quads:
            yb = ypow[u_].astype(BF16)
            if it < 4:
                r = _dot(jnp.concatenate([tinv[u_].astype(BF16), yb], axis=0), block_diag(yb))
                tinv[u_] = tinv[u_] + r[0:C]
                ypow[u_] = r[C:2 * C]
            else:
                tinv[u_] = tinv[u_] + _dot(tinv[u_].astype(BF16), block_diag(yb))
    uw = {u_: _dot(tinv[u_].astype(BF16), rhs_bd[u_]) for u_ in quads}
    zeros_v = jnp.zeros((C, LANE), BF16)
    chains = [(bb, h) for bb in range(nb) for h in range(HEADS)]
    for c in range(GDN_CHUNKS_PER_STEP):
        rs = slice(c * C, (c + 1) * C)
        states = {s_: state_ref[s_[0] * HEADS + s_[1]] for s_ in chains}
        ws_qs = {}
        for s_ in chains:
            bb, h = s_
            a, hh = divmod(h, GDN_QUAD)
            w_h = uw[bb, c, a][:, hh * 2 * LANE + LANE:(hh + 1) * 2 * LANE]
            ws_qs[s_] = _dot(jnp.concatenate([w_h.astype(BF16), qd[bb, c, h]], axis=0), states[s_].astype(BF16))
        v_new = {}
        for s_ in chains:
            bb, h = s_
            a, hh = divmod(h, GDN_QUAD)
            v_new[s_] = (uw[bb, c, a][:, hh * 2 * LANE:hh * 2 * LANE + LANE] - ws_qs[s_][0:C]).astype(BF16)
        outs = {}
        for s_ in chains:
            bb, h = s_
            a, hh = divmod(h, GDN_QUAD)
            v_pad = jnp.concatenate([zeros_v] * hh + [v_new[s_]] + [zeros_v] * (GDN_QUAD - 1 - hh), axis=0)
            outs[s_] = ws_qs[s_][C:2 * C] + _dot(qk[bb, c, a], v_pad)
            state_ref[bb * HEADS + h] = states[s_] * eglast[bb, c, h] + _dot_tn(kd[bb, c, h], v_new[s_])
        for s_ in chains:
            bb, h = s_
            ls = slice(h * LANE, (h + 1) * LANE)
            o = outs[s_]
            o = o * lax.rsqrt(jnp.mean(o * o, -1, keepdims=True) + RMS_EPS)
            y_ref[bb, rs, ls] = (o * nw * _silu(z_ref[bb, rs, ls].astype(F32))).astype(BF16)


def _gdn(q, k, v, bg, h3, norm_w):
    B, S, _ = q.shape
    rows = GDN_CHUNKS_PER_STEP * DN_CHUNK
    nb = GDN_BATCH_PER_STEP if B % GDN_BATCH_PER_STEP == 0 else 1
    tile = pl.BlockSpec((nb, rows, WIDTH), lambda b, n: (b, n, 0))
    return pl.pallas_call(
        _gdn_kernel,
        grid=(B // nb, S // rows),
        in_specs=[tile, tile, tile,
                  pl.BlockSpec((nb, rows, LANE), lambda b, n: (b, n, 0)),
                  pl.BlockSpec((nb, rows, WIDTH), lambda b, n: (b, n, COL_DZ // WIDTH)),
                  pl.BlockSpec((1, LANE), lambda b, n: (0, 0))],
        out_specs=tile,
        out_shape=jax.ShapeDtypeStruct((B, S, WIDTH), BF16),
        scratch_shapes=[pltpu.VMEM((nb * HEADS, HEAD_DIM, HEAD_DIM), F32)],
        compiler_params=_cparams(("parallel", "arbitrary")),
        name="gdn",
    )(q, k, v, bg, h3, norm_w)


def _rope(t, cosf, sinf, lo):
    partner = jnp.where(lo, pltpu.roll(t, LANE - ROPE_HALF, 1), pltpu.roll(t, ROPE_HALF, 1))
    return t * cosf + partner * sinf


SEL_TK = 512
WIN_TK = 256
SEL_VROWS = LANE + 16
LOG2E = math.log2(math.e)


def _proj_nsa_kernel(x_ref, w_ref, cos_ref, sin_ref, qt_o, kaug_o, vslt_o, kwn_o, vwnt_o):
    i = pl.program_id(1)
    ts = x_ref.shape[0]
    h = _dot(x_ref[...].astype(BF16), w_ref[...])
    c_ksl, c_vsl, c_kwn, c_vwn = WIDTH, WIDTH + KVW, WIDTH + 2 * KVW, WIDTH + 3 * KVW
    cosf = cos_ref[...]
    sinf = sin_ref[...]
    lane = lax.broadcasted_iota(jnp.int32, cosf.shape, 1)
    lo = lane < ROPE_HALF
    for hd in range(HEADS):
        sl = slice(hd * LANE, (hd + 1) * LANE)
        qt_o[sl, :] = (_rope(h[:, sl], cosf, sinf, lo) * (ATT_SCALE * LOG2E)).T.astype(BF16)
    tpos = i * ts + lax.broadcasted_iota(jnp.int32, cosf.shape, 0)
    onehot = jnp.where((tpos >> int(math.log2(SEL_LEN))) == lane, 1.0, 0.0).astype(BF16)
    for g in range(KV_GROUPS):
        sl = slice(g * LANE, (g + 1) * LANE)
        off = g * LANE
        kaug_o[:, 2 * g * LANE:(2 * g + 1) * LANE] = _rope(h[:, c_ksl + off:c_ksl + off + LANE], cosf, sinf, lo).astype(BF16)
        kaug_o[:, (2 * g + 1) * LANE:(2 * g + 2) * LANE] = onehot
        kwn_o[:, sl] = _rope(h[:, c_kwn + off:c_kwn + off + LANE], cosf, sinf, lo).astype(BF16)
        vs_t = h[:, c_vsl + off:c_vsl + off + LANE].T.astype(BF16)
        for c in range(ts // SEL_TK):
            ones_rows = jnp.where(lax.broadcasted_iota(jnp.int32, (SEL_VROWS - LANE, SEL_TK), 0) == 0, 1.0, 0.0)
            vslt_o[g, c] = jnp.concatenate([vs_t[:, c * SEL_TK:(c + 1) * SEL_TK], ones_rows.astype(BF16)], axis=0)
        vw_t = h[:, c_vwn + off:c_vwn + off + LANE].T.astype(BF16)
        for c in range(ts // WIN_TK):
            ones_rows = jnp.where(lax.broadcasted_iota(jnp.int32, (SEL_VROWS - LANE, WIN_TK), 0) == 0, 1.0, 0.0)
            vwnt_o[g, c] = jnp.concatenate([vw_t[:, c * WIN_TK:(c + 1) * WIN_TK], ones_rows.astype(BF16)], axis=0)


def _proj_nsa(x3, w_b, cos_tab, sin_tab):
    B, S, D = x3.shape
    ts = PROJ_TM
    tab = pl.BlockSpec((ts, LANE), lambda b, i: (i, 0))
    return pl.pallas_call(
        _proj_nsa_kernel,
        grid=(B, S // ts),
        in_specs=[pl.BlockSpec((None, ts, D), lambda b, i: (b, i, 0)),
                  pl.BlockSpec((D, NB), lambda b, i: (0, 0)), tab, tab],
        out_specs=[pl.BlockSpec((None, WIDTH, ts), lambda b, i: (b, 0, i)),
                   pl.BlockSpec((None, ts, 2 * KVW), lambda b, i: (b, i, 0)),
                   pl.BlockSpec((None, KV_GROUPS, ts // SEL_TK, SEL_VROWS, SEL_TK), lambda b, i: (b, 0, i, 0, 0)),
                   pl.BlockSpec((None, ts, KVW), lambda b, i: (b, i, 0)),
                   pl.BlockSpec((None, KV_GROUPS, ts // WIN_TK, SEL_VROWS, WIN_TK), lambda b, i: (b, 0, i, 0, 0))],
        out_shape=[jax.ShapeDtypeStruct((B, WIDTH, S), BF16),
                   jax.ShapeDtypeStruct((B, S, 2 * KVW), BF16),
                   jax.ShapeDtypeStruct((B, KV_GROUPS, S // SEL_TK, SEL_VROWS, SEL_TK), BF16),
                   jax.ShapeDtypeStruct((B, S, KVW), BF16),
                   jax.ShapeDtypeStruct((B, KV_GROUPS, S // WIN_TK, SEL_VROWS, WIN_TK), BF16)],
        compiler_params=_cparams(("parallel", "parallel")),
        name="proj_nsa",
    )(x3, w_b, cos_tab, sin_tab)


def _compress_one(t_ref, pe_ref, w1_ref, w2_ref, nc):
    half = CMP_LEN // 2
    y1 = jnp.zeros((nc, CMP_HIDDEN), F32)
    y2 = jnp.zeros((nc, CMP_HIDDEN), F32)
    for l in range(half):
        x = t_ref[pl.ds(l, nc, stride=CMP_STRIDE), :]
        y1 = y1 + _dot((x + pe_ref[l:l + 1, :]).astype(BF16), w1_ref[l * LANE:(l + 1) * LANE, :])
        y2 = y2 + _dot((x + pe_ref[half + l:half + l + 1, :]).astype(BF16),
                       w1_ref[(half + l) * LANE:(half + l + 1) * LANE, :])
    y = y1 + pltpu.roll(y2, nc - 1, 0)
    return _dot(_silu(y).astype(BF16), w2_ref[...])


def _compress_kernel(kc_ref, vc_ref, pek_ref, pev_ref, w1k_ref, w2k_ref, w1v_ref, w2v_ref,
                     cos_ref, sin_ref, kc_o, vc_o):
    nc = kc_o.shape[0]
    kc = _compress_one(kc_ref, pek_ref, w1k_ref, w2k_ref, nc)
    lo = lax.broadcasted_iota(jnp.int32, kc.shape, 1) < ROPE_HALF
    kc_o[...] = _rope(kc, cos_ref[...], sin_ref[...], lo).astype(BF16)
    vc_o[...] = _compress_one(vc_ref, pev_ref, w1v_ref, w2v_ref, nc).astype(BF16)


def _compress(h3, pek, pev, w1k, w2k, w1v, w2v, cos_c, sin_c):
    B, S, _ = h3.shape
    nc = S // CMP_STRIDE
    full = lambda shape: pl.BlockSpec(shape, lambda b, g: tuple(0 for _ in shape))
    out_spec = pl.BlockSpec((None, None, nc, LANE), lambda b, g: (b, g, 0, 0))
    out_sd = jax.ShapeDtypeStruct((B, KV_GROUPS, nc, LANE), BF16)
    cin = CMP_LEN * HEAD_DIM
    return pl.pallas_call(
        _compress_kernel,
        grid=(B, KV_GROUPS),
        in_specs=[pl.BlockSpec((None, S, LANE), lambda b, g: (b, 0, g)),
                  pl.BlockSpec((None, S, LANE), lambda b, g: (b, 0, KV_GROUPS + g)),
                  full((CMP_LEN, LANE)), full((CMP_LEN, LANE)),
                  full((cin, CMP_HIDDEN)), full((CMP_HIDDEN, LANE)),
                  full((cin, CMP_HIDDEN)), full((CMP_HIDDEN, LANE)),
                  full((nc, LANE)), full((nc, LANE))],
        out_specs=[out_spec, out_spec],
        out_shape=[out_sd, out_sd],
        compiler_params=_cparams(("parallel", "parallel")),
        name="compress",
    )(h3, h3, pek, pev, w1k, w2k, w1v, w2v, cos_c, sin_c)


def _cmp_attn_kernel(q_ref, kc_ref, vc_ref, ovt_ref, gate_ref, o_ref, bias_ref, st_ref, cnt_ref, psel_ref, *, n_sel):
    i = pl.program_id(2)
    tq = q_ref.shape[1]
    cols = HPG * tq
    nc = kc_ref.shape[0]
    t0 = i * tq
    q_g = jnp.concatenate([q_ref[hh * LANE:(hh + 1) * LANE, :] for hh in range(HPG)], axis=1)
    tpos = t0 + (lax.broadcasted_iota(jnp.int32, (1, cols), 1) & (tq - 1))
    gate = _gate_row(jnp.where(pl.program_id(1) == 0, gate_ref[0:HPG, :], gate_ref[HPG:HEADS, :]))

    def attend(rows):
        cmp_end = lax.broadcasted_iota(jnp.int32, (rows, 1), 0) * CMP_STRIDE + (CMP_LEN - 1)
        s = jnp.where(cmp_end <= tpos, _dot(kc_ref[0:rows, :], q_g), -jnp.inf)
        m = jnp.max(s, axis=0, keepdims=True)
        m = jnp.where(m > -jnp.inf, m, 0.0)
        p = jnp.exp2(s - m)
        p = p / jnp.maximum(jnp.sum(p, axis=0, keepdims=True), jnp.finfo(F32).tiny)
        vct = vc_ref[0:rows, :].astype(F32).T.astype(BF16)
        _store_heads(o_ref, _dot(vct, p.astype(BF16)) * gate, 0, tq)
        psum = p[:, 0:tq]
        for hh in range(1, HPG):
            psum = psum + p[:, hh * tq:(hh + 1) * tq]
        ovt = ovt_ref[:, 0:rows]
        p_hi = psum.astype(BF16)
        r1 = psum - p_hi.astype(F32)
        p_mid = r1.astype(BF16)
        p_lo = (r1 - p_mid.astype(F32)).astype(BF16)
        psel_ref[...] = _dot(ovt, p_hi) + (_dot(ovt, p_mid) + _dot(ovt, p_lo))

    n_part = 4
    part = nc // n_part
    n_live = (t0 + tq - CMP_LEN) // CMP_STRIDE + 1
    branch = jnp.clip((n_live - 1) // part, 0, n_part - 1)
    for r in range(n_part):
        pl.when(branch == r)(functools.partial(attend, (r + 1) * part))
    p_sel = psel_ref[...]
    jb = lax.broadcasted_iota(jnp.int32, (LANE, tq), 0)
    cur = (t0 + lax.broadcasted_iota(jnp.int32, (1, tq), 1)) >> int(math.log2(SEL_LEN))
    valid = jb <= cur
    forced = valid & ((jb == 0) | (jb == cur) | (jb == cur - 1))
    st = jnp.where(forced, jnp.inf, jnp.where(valid, p_sel, -jnp.inf))
    n_grp = n_sel // SUBLANE
    st_ref[...] = st
    cnt_ref[...] = jnp.zeros(cnt_ref.shape, F32)
    jrow = lax.broadcasted_iota(jnp.int32, (SUBLANE, tq), 0)
    cur_max = (t0 + tq - 1) >> int(math.log2(SEL_LEN))
    for gb in range(n_grp):
        @pl.when(cur_max >= gb * SUBLANE)
        def _():
            grp = [st_ref[g * SUBLANE:(g + 1) * SUBLANE, :] for g in range(n_grp)]
            cnt = [cnt_ref[g * SUBLANE:(g + 1) * SUBLANE, :] for g in range(n_grp)]
            for rb_ in range(SUBLANE):
                rb = grp[gb][rb_:rb_ + 1, :]
                for g in range(n_grp):
                    if g < gb:
                        inc = jnp.where(rb > grp[g], 1.0, 0.0)
                    elif g > gb:
                        inc = jnp.where(rb >= grp[g], 1.0, 0.0)
                    else:
                        inc = jnp.where(jrow > rb_, jnp.where(rb >= grp[g], 1.0, 0.0), jnp.where(rb > grp[g], 1.0, 0.0))
                    cnt[g] = cnt[g] + inc
            for g in range(n_grp):
                cnt_ref[g * SUBLANE:(g + 1) * SUBLANE, :] = cnt[g]

    keep = (cnt_ref[...] < float(SEL_TOPK)) & (st_ref[0:n_sel, :] > -jnp.inf)
    bias = jnp.where(keep, 0.0, NEG_BIAS)
    if n_sel < LANE:
        bias = jnp.concatenate([bias, jnp.zeros((LANE - n_sel, tq), F32)], axis=0)
    bias_ref[...] = bias.astype(BF16)


def _cmp_attn(q_t, kc, vc, overlap_t, gates_t):
    B, _, S = q_t.shape
    nc = kc.shape[2]
    n_sel = S // SEL_LEN
    tq = CMP_TQ
    gw = HPG * LANE
    return pl.pallas_call(
        functools.partial(_cmp_attn_kernel, n_sel=n_sel),
        grid=(B, KV_GROUPS, S // tq),
        in_specs=[pl.BlockSpec((None, gw, tq), lambda b, g, i: (b, g, i)),
                  pl.BlockSpec((None, None, nc, LANE), lambda b, g, i: (b, g, 0, 0)),
                  pl.BlockSpec((None, None, nc, LANE), lambda b, g, i: (b, g, 0, 0)),
                  pl.BlockSpec((LANE, nc), lambda b, g, i: (0, 0)),
                  pl.BlockSpec((None, HEADS, tq), lambda b, g, i: (b, 0, i))],
        out_specs=[pl.BlockSpec((None, tq, gw), lambda b, g, i: (b, i, g)),
                   pl.BlockSpec((None, None, LANE, tq), lambda b, g, i: (b, g, 0, i))],
        out_shape=[jax.ShapeDtypeStruct((B, S, WIDTH), BF16),
                   jax.ShapeDtypeStruct((B, KV_GROUPS, LANE, S), BF16)],
        scratch_shapes=[pltpu.VMEM((LANE, tq), F32), pltpu.VMEM((n_sel, tq), F32), pltpu.VMEM((LANE, tq), F32)],
        compiler_params=_cparams(("parallel", "parallel", "parallel")),
        name="cmp_attn",
    )(q_t, kc, vc, overlap_t, gates_t)


FLASH_TQ = 256
CMP_TQ = 1024


def _gate_row(logits):
    gt = _sigmoid(logits)
    return jnp.concatenate([gt[hh:hh + 1, :] for hh in range(HPG)], axis=1)


def _store_heads(o_ref, out_t, g, tq):
    for hh in range(HPG):
        h = g * HPG + hh
        o_ref[:, h * LANE:(h + 1) * LANE] = out_t[:, hh * tq:(hh + 1) * tq].T.astype(BF16)


def _flash_kernel(q_ref, b_ref, k_ref, vt_ref, kw_ref, vwt_ref, wb_ref, gate_ref, o_ref, ow_ref,
                  qs_ref, s_ref, m_ref, acc_ref, sw_ref, pw_ref, *, tk):
    i = pl.program_id(1)
    tq = q_ref.shape[1]
    cols = HPG * tq
    t0 = i * tq
    kw = qs_ref.shape[1]
    for g in range(KV_GROUPS):
        for hh in range(HPG):
            h = g * HPG + hh
            qs_ref[g, 0:LANE, hh * tq:(hh + 1) * tq] = q_ref[h * LANE:(h + 1) * LANE, :]
            qs_ref[g, LANE:2 * LANE, hh * tq:(hh + 1) * tq] = b_ref[g]
    m_ref[...] = jnp.full(m_ref.shape, -jnp.inf, F32)
    acc_ref[...] = jnp.zeros(acc_ref.shape, F32)
    qpos = t0 + (lax.broadcasted_iota(jnp.int32, (1, cols), 1) & (tq - 1))

    def scores(j, slot):
        k0 = pl.multiple_of(j * tk, tk)
        for g in range(KV_GROUPS):
            s_ref[slot, g] = _dot(k_ref[pl.ds(k0, tk), g * kw:(g + 1) * kw], qs_ref[g])

    def update(j, slot, masked):
        if masked:
            mask = j * tk + lax.broadcasted_iota(jnp.int32, (tk, 1), 0) <= qpos
        for g in range(KV_GROUPS):
            s = s_ref[slot, g]
            if masked:
                s = jnp.where(mask, s, -jnp.inf)
            m_old = m_ref[g]
            m_new = jnp.maximum(m_old, jnp.max(s, axis=0, keepdims=True))
            alpha = jnp.exp2(m_old - m_new)
            p = jnp.exp2(s - m_new)
            acc_ref[g] = alpha * acc_ref[g] + _dot(vt_ref[g, j], p.astype(BF16))
            m_ref[g] = m_new

    jd = t0 // tk
    last = jnp.maximum(jd - 1, 0)
    win_scores, win_softmax = _window_pass(i, kw_ref, vwt_ref, wb_ref, gate_ref, ow_ref, sw_ref, pw_ref, tq, WIN_TK)
    scores(jd, 0)
    win_scores(0, qs_ref[0, 0:LANE, :])
    update(jd, 0, True)
    win_scores(1, qs_ref[1, 0:LANE, :])
    win_softmax(0)
    scores(0, 1)
    win_softmax(1)

    def body(p, carry):
        scores(jnp.minimum(2 * p + 1, last), 0)
        update(2 * p, 1, False)
        scores(jnp.minimum(2 * p + 2, last), 1)
        update(2 * p + 1, 0, False)
        return carry

    lax.fori_loop(0, jd // 2, body, 0)

    @pl.when(jd % 2 == 1)
    def _():
        update(jd - 1, 1, False)

    for g in range(KV_GROUPS):
        out = acc_ref[g, 0:LANE] / jnp.maximum(acc_ref[g, LANE:LANE + 1], jnp.finfo(F32).tiny)
        _store_heads(o_ref, out * _gate_row(gate_ref[HEADS + g * HPG:HEADS + (g + 1) * HPG, :]), g, tq)


def _window_pass(i, k_ref, vt_ref, wb_ref, gate_ref, o_ref, s_ref, p_ref, tq, tk):
    n_t = WINDOW // tk + 1
    jt = [jnp.maximum(i - d, 0) for d in range(n_t)]
    static_bias = {0: 0, n_t - 1: 1}

    def scores(g, q_g):
        for d in range(n_t):
            k0 = pl.multiple_of(jt[d] * tk, tk)
            s_ref[g, d] = _dot(k_ref[pl.ds(k0, tk), g * LANE:(g + 1) * LANE], q_g)

    def tile(g, d):
        s = s_ref[g, d]
        return s + wb_ref[static_bias[d]] if d in static_bias else s

    def softmax(g):
        m = jnp.max(tile(g, 0), axis=0, keepdims=True)
        for d in range(1, n_t):
            m = jnp.maximum(m, jnp.where(i >= d, jnp.max(tile(g, d), axis=0, keepdims=True), -jnp.inf))
        for d in range(n_t):
            p = jnp.exp2(tile(g, d) - (m if d == 0 else jnp.where(i >= d, m, jnp.inf)))
            p_ref[g, d * tk:(d + 1) * tk, :] = p.astype(BF16)
        v_cat = jnp.concatenate([vt_ref[g, jt[d]] for d in range(n_t)], axis=1)
        acc = _dot(v_cat, p_ref[g])
        out = acc[0:LANE] / jnp.maximum(acc[LANE:LANE + 1], jnp.finfo(F32).tiny)
        _store_heads(o_ref, out * _gate_row(gate_ref[2 * HEADS + g * HPG:2 * HEADS + (g + 1) * HPG, :]), g, tq)

    return scores, softmax


def _window_bias(tq):
    r = np.arange(tq)[:, None]
    c = (np.arange(HPG * tq) % tq)[None, :]
    neg = np.float32(-np.inf)
    return jnp.asarray(np.stack([np.where(r <= c, np.float32(0), neg), np.where(c < r, np.float32(0), neg)]))


def _flash(q_t, kaug, vsl_t, bias_t, kwn, vwn_t, gates_t):
    B, _, S = q_t.shape
    tq = FLASH_TQ
    tk = SEL_TK
    tw = WIN_TK
    assert tk % tq == 0 and tw == tq and WINDOW == 2 * tw
    kw = kaug.shape[2] // KV_GROUPS
    cols = HPG * tq
    out_spec = pl.BlockSpec((None, tq, WIDTH), lambda b, i: (b, i, 0))
    out_sd = jax.ShapeDtypeStruct((B, S, WIDTH), BF16)
    return pl.pallas_call(
        functools.partial(_flash_kernel, tk=tk),
        grid=(B, S // tq),
        in_specs=[pl.BlockSpec((None, WIDTH, tq), lambda b, i: (b, 0, i)),
                  pl.BlockSpec((None, KV_GROUPS, LANE, tq), lambda b, i: (b, 0, 0, i)),
                  pl.BlockSpec((None, S, KV_GROUPS * kw), lambda b, i: (b, 0, 0)),
                  pl.BlockSpec((None, KV_GROUPS, S // tk, SEL_VROWS, tk), lambda b, i: (b, 0, 0, 0, 0)),
                  pl.BlockSpec((None, S, KVW), lambda b, i: (b, 0, 0)),
                  pl.BlockSpec((None, KV_GROUPS, S // tw, SEL_VROWS, tw), lambda b, i: (b, 0, 0, 0, 0)),
                  pl.BlockSpec((2, tw, cols), lambda b, i: (0, 0, 0)),
                  pl.BlockSpec((None, N_GATES, tq), lambda b, i: (b, 0, i))],
        out_specs=[out_spec, out_spec],
        out_shape=[out_sd, out_sd],
        scratch_shapes=[pltpu.VMEM((KV_GROUPS, kw, cols), BF16),
                        pltpu.VMEM((2, KV_GROUPS, tk, cols), F32),
                        pltpu.VMEM((KV_GROUPS, 1, cols), F32),
                        pltpu.VMEM((KV_GROUPS, SEL_VROWS, cols), F32),
                        pltpu.VMEM((KV_GROUPS, WINDOW // tw + 1, tw, cols), F32),
                        pltpu.VMEM((KV_GROUPS, (WINDOW // tw + 1) * tw, cols), BF16)],
        compiler_params=_cparams(("parallel", "parallel")),
        name="flash",
    )(q_t, bias_t, kaug, vsl_t, kwn, vwn_t, _window_bias(tq), gates_t)


def _merge_kernel(ydn_ref, oc_ref, os_ref, ow_ref, mgd_ref, mgn_ref, x_ref,
                  wdn_ref, wnsa_ref, wout_ref, g_ref, b_ref, o_ref):
    y_nsa = (oc_ref[...].astype(F32) + os_ref[...].astype(F32) + ow_ref[...].astype(F32)).astype(BF16)
    y = (_sigmoid(mgd_ref[...].astype(F32)) * _dot(ydn_ref[...], wdn_ref[...])
         + _sigmoid(mgn_ref[...].astype(F32)) * _dot(y_nsa, wnsa_ref[...]))
    z = DEEPNORM_ALPHA * x_ref[...] + _dot(y.astype(BF16), wout_ref[...])
    o_ref[...] = _layer_norm(z, g_ref[...], b_ref[...])


def _merge(y_dn2, oc2, os2, ow2, h, x2, wdn, wnsa, wout, g, b):
    T, D = x2.shape
    tm = PROJ_TM
    row = pl.BlockSpec((tm, WIDTH), lambda i: (i, 0))
    full = lambda shape: pl.BlockSpec(shape, lambda i: (0, 0))
    return pl.pallas_call(
        _merge_kernel,
        grid=(T // tm,),
        in_specs=[row, row, row, row,
                  pl.BlockSpec((tm, WIDTH), lambda i: (i, COL_MG_DN // WIDTH)),
                  pl.BlockSpec((tm, WIDTH), lambda i: (i, COL_MG_NSA // WIDTH)),
                  pl.BlockSpec((tm, D), lambda i: (i, 0)),
                  full((WIDTH, D)), full((WIDTH, D)), full((D, D)), full((1, D)), full((1, D))],
        out_specs=pl.BlockSpec((tm, D), lambda i: (i, 0)),
        out_shape=jax.ShapeDtypeStruct((T, D), F32),
        compiler_params=_cparams(("parallel",)),
        name="merge",
    )(y_dn2, oc2, os2, ow2, h, h, x2, wdn, wnsa, wout, g, b)


FFN_TM = 1024


def _ffn_up_kernel(x_ref, xp_ref, wg_ref, wv_ref, cwg_ref, cwv_ref, bg_ref, bv_ref, o_ref, xs_ref, *, tiles_per_seq):
    i = pl.program_id(1)
    tm = x_ref.shape[0]
    nv = tm // SUBLANE
    n_slab = x_ref.shape[1] // LANE
    for sp in range(SUBLANE):
        for c in range(n_slab):
            xs_ref[c, pl.ds(sp, nv, stride=SUBLANE), :] = x_ref[sp * nv:(sp + 1) * nv, c * LANE:(c + 1) * LANE]
    xb = jnp.concatenate([xs_ref[c] for c in range(n_slab)], axis=1).astype(BF16)
    notfirst = (i % tiles_per_seq != 0).astype(F32)
    xpb = (xp_ref[...] * notfirst).astype(BF16)
    sub = lax.broadcasted_iota(jnp.int32, (SUBLANE, 1), 0)

    def conv(w_ref, cw_ref, b_ref):
        u = _dot(xb, w_ref[...])
        p = _dot(xpb, w_ref[...])

        def wrap(k):
            blk = u[(nv - k) * SUBLANE:(nv - k + 1) * SUBLANE, :]
            return jnp.where(sub == 0, p[SUBLANE - k:SUBLANE - k + 1, :], pltpu.roll(blk, 1, 0))

        w1, w2 = wrap(1), wrap(2)
        u1 = jnp.concatenate([w1, u[0:(nv - 1) * SUBLANE, :]], axis=0)
        u2 = jnp.concatenate([w2, w1, u[0:(nv - 2) * SUBLANE, :]], axis=0)
        return cw_ref[0:1, :] * u2 + cw_ref[1:2, :] * u1 + cw_ref[2:3, :] * u + b_ref[...]

    o_ref[...] = (_silu(conv(wg_ref, cwg_ref, bg_ref)) * conv(wv_ref, cwv_ref, bv_ref)).astype(BF16)


def _ffn_up(x1, w_up, conv_w, conv_b, seq):
    T, D = x1.shape
    tm = FFN_TM
    nj = 2
    tn = D_FF // nj
    return pl.pallas_call(
        functools.partial(_ffn_up_kernel, tiles_per_seq=seq // tm),
        grid=(nj, T // tm),
        in_specs=[pl.BlockSpec((tm, D), lambda j, i: (i, 0)),
                  pl.BlockSpec((SUBLANE, D), lambda j, i: (jnp.maximum(i * (tm // SUBLANE) - 1, 0), 0)),
                  pl.BlockSpec((D, tn), lambda j, i: (0, j)),
                  pl.BlockSpec((D, tn), lambda j, i: (0, j + nj)),
                  pl.BlockSpec((FFN_CONV, tn), lambda j, i: (0, j)),
                  pl.BlockSpec((FFN_CONV, tn), lambda j, i: (0, j + nj)),
                  pl.BlockSpec((1, tn), lambda j, i: (0, j)),
                  pl.BlockSpec((1, tn), lambda j, i: (0, j + nj))],
        out_specs=pl.BlockSpec((tm, tn), lambda j, i: (i, j)),
        out_shape=jax.ShapeDtypeStruct((T, D_FF), BF16),
        scratch_shapes=[pltpu.VMEM((D // LANE, tm, LANE), F32)],
        compiler_params=_cparams(("parallel", "parallel")),
        name="ffn_up",
    )(x1, x1, w_up, w_up, conv_w, conv_w, conv_b, conv_b)


def _ffn_down_kernel(a_ref, w_ref, x_ref, g_ref, b_ref, o_ref, fs_ref):
    tm, d = x_ref.shape
    nv = tm // SUBLANE
    n_slab = d // LANE
    f = _dot(a_ref[...], w_ref[...])
    for c in range(n_slab):
        fs_ref[c] = f[:, c * LANE:(c + 1) * LANE]
    for sp in range(SUBLANE):
        rows = slice(sp * nv, (sp + 1) * nv)
        f_nat = jnp.concatenate([fs_ref[c, pl.ds(sp, nv, stride=SUBLANE), :] for c in range(n_slab)], axis=1)
        z = DEEPNORM_ALPHA * x_ref[rows, :] + f_nat
        o_ref[rows, :] = _layer_norm(z, g_ref[...], b_ref[...])


def _ffn_down(a, w_down, x1, g, b):
    T, D = x1.shape
    tm = FFN_TM
    full = lambda shape: pl.BlockSpec(shape, lambda i: (0, 0))
    return pl.pallas_call(
        _ffn_down_kernel,
        grid=(T // tm,),
        in_specs=[pl.BlockSpec((tm, D_FF), lambda i: (i, 0)), full((D_FF, D)),
                  pl.BlockSpec((tm, D), lambda i: (i, 0)), full((1, D)), full((1, D))],
        out_specs=pl.BlockSpec((tm, D), lambda i: (i, 0)),
        out_shape=jax.ShapeDtypeStruct((T, D), F32),
        scratch_shapes=[pltpu.VMEM((D // LANE, tm, LANE), F32)],
        compiler_params=_cparams(("parallel",)),
        name="ffn_down",
    )(a, w_down, x1, g, b)


def _split_w_in_kernel(wt_ref, a_ref, b_ref, c_ref):
    o_dz = 3 * WIDTH
    o_db = 4 * WIDTH
    o_nq = o_db + 2 * HEADS
    o_kc = o_nq + WIDTH
    o_ksl = o_kc + 2 * KVW
    o_gate = o_ksl + 4 * KVW
    o_mg = o_gate + 3 * HEADS
    slab = wt_ref.shape[1]

    def put(dst, o0, c0, c1):
        dst[:, o0:o0 + (c1 - c0)] = wt_ref[c0:c1, :].T.astype(BF16)

    put(a_ref, 0, 0, o_dz)
    small = jnp.concatenate([wt_ref[o_db:o_nq, :], wt_ref[o_gate:o_mg, :],
                             jnp.zeros((LANE - 5 * HEADS, slab), F32)], axis=0)
    a_ref[:, o_dz:NA] = small.T.astype(BF16)
    put(b_ref, 0, o_nq, o_kc)
    put(b_ref, WIDTH, o_ksl, o_gate)
    put(c_ref, 0, o_dz, o_db)
    put(c_ref, WIDTH, o_mg, o_mg + 2 * WIDTH)
    put(c_ref, 3 * WIDTH, o_kc, o_ksl)


def _split_w_in(w, layer):
    _, d, n_in = w.shape
    slab = 2 * LANE
    return pl.pallas_call(
        _split_w_in_kernel,
        grid=(d // slab,),
        in_specs=[pl.BlockSpec((None, n_in, slab), lambda i: (layer, 0, i))],
        out_specs=[pl.BlockSpec((slab, NA), lambda i: (i, 0)), pl.BlockSpec((slab, NB), lambda i: (i, 0)),
                   pl.BlockSpec((slab, NC_COLS), lambda i: (i, 0))],
        out_shape=[jax.ShapeDtypeStruct((d, NA), BF16), jax.ShapeDtypeStruct((d, NB), BF16),
                   jax.ShapeDtypeStruct((d, NC_COLS), BF16)],
        compiler_params=_cparams(("parallel",)),
        name="split_w_in",
    )(jnp.swapaxes(w, 1, 2))


def _rope_tables(pos):
    pos = np.asarray(pos)
    inv_freq = np.float32(ROPE_THETA) ** (-(np.arange(ROPE_HALF, dtype=np.float32) / np.float32(ROPE_HALF)))
    ang = pos.astype(np.float32)[:, None] * inv_freq.astype(np.float32)
    cos = np.cos(ang.astype(np.float64)).astype(np.float32)
    sin = np.sin(ang.astype(np.float64)).astype(np.float32)
    p = pos.shape[0]
    cosf = np.concatenate([cos, cos, np.ones((p, LANE - ROPE_DIM), np.float32)], axis=1)
    sinf = np.concatenate([-sin, sin, np.zeros((p, LANE - ROPE_DIM), np.float32)], axis=1)
    return jnp.asarray(cosf), jnp.asarray(sinf)


def _overlap_table(nc, n_sel):
    n = np.arange(nc)[None, :]
    j = np.arange(LANE)[:, None]
    ov = (n * CMP_STRIDE <= j * SEL_LEN + SEL_LEN - 1) & (n * CMP_STRIDE + CMP_LEN - 1 >= j * SEL_LEN)
    ov = ov & (j < n_sel) & (n < nc - 1)
    return jnp.asarray(ov.astype(np.float32)).astype(BF16)


def _lane_row(vals, offset):
    return jnp.zeros((1, LANE), F32).at[0, offset:offset + vals.shape[0]].set(vals.astype(F32))


def kernel(x, w_in, dn_conv_w, dn_a_log, dn_dt_bias, dn_norm_w, cmp_pos_k, cmp_pos_v, cmp_k_w1, cmp_k_w2,
           cmp_v_w1, cmp_v_w2, w_branch_dn, w_branch_nsa, w_out, ln1_g, ln1_b, ffn_w_up, ffn_conv_w,
           ffn_conv_b, ffn_w_down, ln2_g, ln2_b):
    B, S, D = x.shape
    T = B * S
    nc = S // CMP_STRIDE
    n_sel = S // SEL_LEN
    assert D == WIDTH and S % FFN_TM == 0 and S % PROJ_TM == 0 and n_sel <= LANE and n_sel % SUBLANE == 0
    cos_t, sin_t = _rope_tables(np.arange(S))
    cos_c, sin_c = _rope_tables(np.arange(nc) * CMP_STRIDE + (CMP_LEN - 1))
    overlap = _overlap_table(nc, n_sel)
    x2 = x.reshape(T, D)
    for layer in range(w_in.shape[0]):
        w_a, w_b, w_c = _split_w_in(w_in, layer)
        x3 = x2.reshape(B, S, D)
        h3, kv_cmp = _proj_rest(x3, w_c)

        q, k, v, bg, gates_t = _proj_gdn(x3, w_a, dn_conv_w[layer], _lane_row(dn_a_log[layer], SMALL_DECAY),
                                       _lane_row(dn_dt_bias[layer], SMALL_DECAY))
        y_dn = _gdn(q, k, v, bg, h3, dn_norm_w[layer].reshape(1, LANE))

        q_t, kaug, vsl_t, kwn_r, vwn_t = _proj_nsa(x3, w_b, cos_t, sin_t)
        kc, vc = _compress(kv_cmp, cmp_pos_k[layer], cmp_pos_v[layer],
                           cmp_k_w1[layer].astype(BF16), cmp_k_w2[layer].astype(BF16),
                           cmp_v_w1[layer].astype(BF16), cmp_v_w2[layer].astype(BF16), cos_c, sin_c)
        o_cmp, bias = _cmp_attn(q_t, kc, vc, overlap, gates_t)
        o_sel, o_win = _flash(q_t, kaug, vsl_t, bias, kwn_r, vwn_t, gates_t)

        x1 = _merge(y_dn.reshape(T, WIDTH), o_cmp.reshape(T, WIDTH), o_sel.reshape(T, WIDTH),
                    o_win.reshape(T, WIDTH), h3.reshape(T, COL_KC), x2,
                    w_branch_dn[layer].astype(BF16), w_branch_nsa[layer].astype(BF16),
                    w_out[layer].astype(BF16), ln1_g[layer].reshape(1, D), ln1_b[layer].reshape(1, D))
        a = _ffn_up(x1, ffn_w_up[layer].astype(BF16), ffn_conv_w[layer],
                    ffn_conv_b[layer].reshape(1, 2 * D_FF), S)
        x2 = _ffn_down(a, ffn_w_down[layer].astype(BF16), x1, ln2_g[layer].reshape(1, D), ln2_b[layer].reshape(1, D))
    return x2.reshape(B, S, D)
```

```python
import functools
import math

import numpy as np
import jax
import jax.numpy as jnp
from jax import lax
from jax.experimental import pallas as pl
from jax.experimental.pallas import tpu as pltpu

F32 = jnp.float32
BF16 = jnp.bfloat16
HI = lax.Precision.HIGHEST

HEADS = 8
HEAD_DIM = 128
KV_GROUPS = 2
HPG = HEADS // KV_GROUPS
DN_CONV = 4
DN_CHUNK = 64
CMP_LEN = 32
CMP_STRIDE = 16
CMP_HIDDEN = 256
SEL_LEN = 64
SEL_TOPK = 16
WINDOW = 512
ROPE_THETA = 500000.0
ROPE_DIM = HEAD_DIM // 4
ROPE_HALF = ROPE_DIM // 2
D_FF = 2816
FFN_CONV = 3
LN_EPS = 1e-5
RMS_EPS = 1e-6
DEEPNORM_ALPHA = 2.0 ** 0.25
ATT_SCALE = HEAD_DIM ** -0.5

LANE = 128
SUBLANE = 8
VMEM_LIMIT = 56 * 1024 * 1024

WIDTH = HEADS * HEAD_DIM
KVW = KV_GROUPS * HEAD_DIM
NA = 3 * WIDTH + LANE
SMALL_BETA = 0
SMALL_DECAY = HEADS
SMALL_GATE = 2 * HEADS
N_GATES = 3 * HEADS
NB = WIDTH + 4 * KVW
NC_COLS = 3 * WIDTH + 2 * KVW
COL_DZ = 0
COL_MG_DN = WIDTH
COL_MG_NSA = 2 * WIDTH
COL_KC = 3 * WIDTH

NEG_BIAS = -(2.0 ** 100)


def _cparams(sem):
    return pltpu.CompilerParams(dimension_semantics=sem, vmem_limit_bytes=VMEM_LIMIT)


def _sigmoid(x):
    return 0.5 * jnp.tanh(0.5 * x) + 0.5


def _silu(x):
    return x * _sigmoid(x)


def _dot(a, b, precision=None):
    return jnp.dot(a, b, preferred_element_type=F32, precision=precision)


def _dot_nt(a, b):
    return lax.dot_general(a, b, (((1,), (1,)), ((), ())), preferred_element_type=F32)


def _dot_tn(a, b):
    return lax.dot_general(a, b, (((0,), (0,)), ((), ())), preferred_element_type=F32)


def _layer_norm(z, g, b):
    mu = jnp.mean(z, -1, keepdims=True)
    zc = z - mu
    var = jnp.mean(zc * zc, -1, keepdims=True)
    return zc * lax.rsqrt(var + LN_EPS) * g + b


PROJ_TM = 512


def _proj_rest_kernel(x_ref, w_ref, g_ref, c_ref):
    h = _dot(x_ref[...].astype(BF16), w_ref[...])
    g_ref[...] = h[:, 0:COL_KC].astype(BF16)
    c_ref[...] = h[:, COL_KC:NC_COLS]


def _proj_rest(x3, w):
    B, S, D = x3.shape
    tm = PROJ_TM
    return pl.pallas_call(
        _proj_rest_kernel,
        grid=(B, S // tm),
        in_specs=[pl.BlockSpec((None, tm, D), lambda b, i: (b, i, 0)),
                  pl.BlockSpec((D, NC_COLS), lambda b, i: (0, 0))],
        out_specs=[pl.BlockSpec((None, tm, COL_KC), lambda b, i: (b, i, 0)),
                   pl.BlockSpec((None, tm, NC_COLS - COL_KC), lambda b, i: (b, i, 0))],
        out_shape=[jax.ShapeDtypeStruct((B, S, COL_KC), BF16),
                   jax.ShapeDtypeStruct((B, S, NC_COLS - COL_KC), F32)],
        compiler_params=_cparams(("parallel", "parallel")),
        name="proj_rest",
    )(x3, w)


def _proj_gdn_kernel(x_ref, w_ref, cw_ref, alog_ref, dtb_ref, q_ref, k_ref, v_ref, bg_ref, gate_ref,
                     xs_ref, ys_ref, carry_ref):
    i = pl.program_id(1)
    ts = x_ref.shape[0]
    nv = ts // SUBLANE
    wq = 3 * WIDTH
    n_slab = x_ref.shape[1] // LANE

    @pl.when(i == 0)
    def _():
        carry_ref[...] = jnp.zeros(carry_ref.shape, F32)

    for sp in range(SUBLANE):
        for c in range(n_slab):
            xs_ref[c, pl.ds(sp, nv, stride=SUBLANE), :] = x_ref[sp * nv:(sp + 1) * nv, c * LANE:(c + 1) * LANE]
    xb = jnp.concatenate([xs_ref[c] for c in range(n_slab)], axis=1).astype(BF16)
    h = _dot(xb, w_ref[...])
    sub = lax.broadcasted_iota(jnp.int32, (SUBLANE, 1), 0)

    def token_order(c, dst_ref, dst_sl, dtype):
        for sp in range(SUBLANE):
            dst_ref[sp * nv:(sp + 1) * nv, dst_sl] = ys_ref[c, pl.ds(sp, nv, stride=SUBLANE), :].astype(dtype)

    for c in range(3 * HEADS):
        sl = slice(c * LANE, (c + 1) * LANE)
        col = h[:, sl]
        wrap = [jnp.where(sub == 0, carry_ref[SUBLANE - k:SUBLANE - k + 1, sl],
                          pltpu.roll(col[(nv - k) * SUBLANE:(nv - k + 1) * SUBLANE, :], 1, 0))
                for k in range(1, DN_CONV)]
        acc = cw_ref[DN_CONV - 1:DN_CONV, sl] * col
        for s in range(1, DN_CONV):
            shifted = jnp.concatenate(wrap[0:s][::-1] + [col[0:(nv - s) * SUBLANE, :]], axis=0)
            acc = acc + cw_ref[DN_CONV - 1 - s:DN_CONV - s, sl] * shifted
        y = _silu(acc)
        if c < 2 * HEADS:
            y = y * lax.rsqrt(jnp.sum(y * y, -1, keepdims=True) + RMS_EPS)
        if c < HEADS:
            y = y * ATT_SCALE
        ys_ref[c] = y
        dst = (q_ref, k_ref, v_ref)[c // HEADS]
        token_order(c, dst, slice((c % HEADS) * LANE, (c % HEADS + 1) * LANE), BF16)
    for k in range(1, DN_CONV):
        carry_ref[SUBLANE - k:SUBLANE - k + 1, :] = h[(nv - k + 1) * SUBLANE - 1:(nv - k + 1) * SUBLANE, 0:wq]
    small = h[:, wq:wq + LANE]
    beta = _sigmoid(small)
    xs = small + dtb_ref[...]
    softplus = jnp.maximum(xs, 0.0) + jnp.log1p(jnp.exp(-jnp.abs(xs)))
    g = -jnp.exp(alog_ref[...]) * softplus
    lane = lax.broadcasted_iota(jnp.int32, small.shape, 1)
    ys_ref[3 * HEADS] = small
    tok = jnp.concatenate([ys_ref[3 * HEADS, pl.ds(sp, nv, stride=SUBLANE), :] for sp in range(SUBLANE)], axis=0)
    gate_ref[...] = tok.T[SMALL_GATE:SMALL_GATE + N_GATES, :]
    ys_ref[3 * HEADS + 1] = jnp.where(lane < SMALL_DECAY, beta, g)
    token_order(3 * HEADS + 1, bg_ref, slice(0, LANE), F32)


def _proj_gdn(x3, w_a, conv_w, alog_row, dtb_row):
    B, S, D = x3.shape
    ts = PROJ_TM
    wq = 3 * WIDTH
    out_sd = jax.ShapeDtypeStruct((B, S, WIDTH), BF16)
    small_sd = jax.ShapeDtypeStruct((B, S, LANE), F32)
    wide = pl.BlockSpec((None, ts, WIDTH), lambda b, i: (b, i, 0))
    narrow = pl.BlockSpec((None, ts, LANE), lambda b, i: (b, i, 0))
    return pl.pallas_call(
        _proj_gdn_kernel,
        grid=(B, S // ts),
        in_specs=[
            pl.BlockSpec((None, ts, D), lambda b, i: (b, i, 0)),
            pl.BlockSpec((D, NA), lambda b, i: (0, 0)),
            pl.BlockSpec((DN_CONV, wq), lambda b, i: (0, 0)),
            pl.BlockSpec((1, LANE), lambda b, i: (0, 0)),
            pl.BlockSpec((1, LANE), lambda b, i: (0, 0)),
        ],
        out_specs=[wide, wide, wide, narrow, pl.BlockSpec((None, N_GATES, ts), lambda b, i: (b, 0, i))],
        out_shape=[out_sd, out_sd, out_sd, small_sd, jax.ShapeDtypeStruct((B, N_GATES, S), F32)],
        scratch_shapes=[pltpu.VMEM((D // LANE, ts, LANE), F32),
                        pltpu.VMEM((3 * HEADS + 2, ts, LANE), F32),
                        pltpu.VMEM((SUBLANE, wq), F32)],
        compiler_params=_cparams(("parallel", "arbitrary")),
        name="proj_gdn",
    )(x3, w_a, conv_w, alog_row, dtb_row)


GDN_CHUNKS_PER_STEP = 4
GDN_BATCH_PER_STEP = 2


GDN_QUAD = 4


def _gdn_kernel(q_ref, k_ref, v_ref, bg_ref, z_ref, nw_ref, y_ref, state_ref):
    n = pl.program_id(1)

    @pl.when(n == 0)
    def _():
        state_ref[...] = jnp.zeros(state_ref.shape, F32)

    C = DN_CHUNK
    QW = GDN_QUAD * C
    QL = GDN_QUAD * LANE
    shift_c = int(math.log2(C))
    r64 = lax.broadcasted_iota(jnp.int32, (C, C), 0)
    c64 = lax.broadcasted_iota(jnp.int32, (C, C), 1)
    tril = jnp.where(r64 >= c64, 1.0, 0.0).astype(F32)
    triu = jnp.where(r64 <= c64, 1.0, 0.0).astype(F32)
    rr = lax.broadcasted_iota(jnp.int32, (C, QW), 0)
    ll = lax.broadcasted_iota(jnp.int32, (C, QW), 1)
    jj = ll & (C - 1)
    hb = ll >> shift_c
    incl_t = rr >= jj
    strict_t = rr > jj
    eye_t = jnp.where(rr == jj, 1.0, 0.0).astype(F32)
    bd_c = jnp.where((lax.broadcasted_iota(jnp.int32, (QW, QW), 0) >> shift_c)
                     == (lax.broadcasted_iota(jnp.int32, (QW, QW), 1) >> shift_c), 1.0, 0.0).astype(BF16)
    bd_l = jnp.where((lax.broadcasted_iota(jnp.int32, (QW, QL), 0) >> shift_c)
                     == (lax.broadcasted_iota(jnp.int32, (QW, QL), 1) >> int(math.log2(LANE))), 1.0, 0.0).astype(BF16)
    zeros_rhs = jnp.zeros((C, 2 * LANE), BF16)
    nw = nw_ref[...]
    nb = q_ref.shape[0]
    quads = [(bb, c, a) for bb in range(nb) for c in range(GDN_CHUNKS_PER_STEP) for a in range(HEADS // GDN_QUAD)]
    gc_cols, gc_rows, bgs = {}, {}, {}
    for bb in range(nb):
        bg = bg_ref[bb]
        bg_t = bg.T
        bgs[bb] = bg
        for c in range(GDN_CHUNKS_PER_STEP):
            rs = slice(c * C, (c + 1) * C)
            gc_cols[bb, c] = _dot(tril, bg[rs, :], HI)
            gc_rows[bb, c] = _dot(bg_t[SMALL_DECAY:SMALL_DECAY + HEADS, rs], triu, HI)
    decay, lower, qk, rhs_bd, qd, kd, eglast = {}, {}, {}, {}, {}, {}, {}
    for u_ in quads:
        bb, c, a = u_
        rs = slice(c * C, (c + 1) * C)
        h0 = a * GDN_QUAD
        gcols = [gc_cols[bb, c][:, SMALL_DECAY + h0 + hh:SMALL_DECAY + h0 + hh + 1] for hh in range(GDN_QUAD)]
        gcol_cat = jnp.broadcast_to(gcols[0], (C, QW))
        for hh in range(1, GDN_QUAD):
            gcol_cat = jnp.where(hb == hh, gcols[hh], gcol_cat)
        grow_cat = jnp.concatenate([gc_rows[bb, c][h0 + hh:h0 + hh + 1, :] for hh in range(GDN_QUAD)], axis=1)
        decay[u_] = jnp.exp(jnp.where(incl_t, gcol_cat - grow_cat, -jnp.inf))
        kq = k_ref[bb, rs, a * QL:(a + 1) * QL].astype(F32)
        qq = q_ref[bb, rs, a * QL:(a + 1) * QL].astype(F32)
        kbs, rows_bd = [], []
        for hh in range(GDN_QUAD):
            h = h0 + hh
            ls = slice(hh * LANE, (hh + 1) * LANE)
            beta = bgs[bb][rs, SMALL_BETA + h:SMALL_BETA + h + 1]
            eg = jnp.exp(gcols[hh])
            glast = gcols[hh][C - 1:C, :]
            kb = kq[:, ls] * beta
            vb = v_ref[bb, rs, h * LANE:(h + 1) * LANE].astype(F32) * beta
            kbs.append(kb)
            rhs_h = jnp.concatenate([vb, kb * eg], axis=1).astype(BF16)
            rows_bd.append(jnp.concatenate([zeros_rhs] * hh + [rhs_h] + [zeros_rhs] * (GDN_QUAD - 1 - hh), axis=1))
            qd[bb, c, h] = (qq[:, ls] * eg).astype(BF16)
            kd[bb, c, h] = (kq[:, ls] * jnp.exp(glast - gcols[hh])).astype(BF16)
            eglast[bb, c, h] = jnp.exp(glast)
        rhs_bd[u_] = jnp.concatenate(rows_bd, axis=0)
        k_bd = jnp.concatenate([kq.astype(BF16)] * GDN_QUAD, axis=0) * bd_l
        lhs = jnp.concatenate([jnp.concatenate(kbs, axis=1), qq], axis=0).astype(BF16)
        aq = _dot_nt(lhs, k_bd)
        lower[u_] = jnp.where(strict_t, aq[0:C] * decay[u_], 0.0)
        qk[u_] = (aq[C:2 * C] * decay[u_]).astype(BF16)

    def block_diag(y_b):
        return jnp.concatenate([y_b] * GDN_QUAD, axis=0) * bd_c

    tinv = {u_: eye_t - lower[u_] for u_ in quads}
    y_b = {u_: lower[u_].astype(BF16) for u_ in quads}
    ypow = {u_: _dot(y_b[u_], block_diag(y_b[u_])) for u_ in quads}
    for it in range(5):
        for u_ in quads:
            yb = ypow[u_].astype(BF16)
            if it < 4:
                r = _dot(jnp.concatenate([tinv[u_].astype(BF16), yb], axis=0), block_diag(yb))
                tinv[u_] = tinv[u_] + r[0:C]
                ypow[u_] = r[C:2 * C]
            else:
                tinv[u_] = tinv[u_] + _dot(tinv[u_].astype(BF16), block_diag(yb))
    uw = {u_: _dot(tinv[u_].astype(BF16), rhs_bd[u_]) for u_ in quads}
    zeros_v = jnp.zeros((C, LANE), BF16)
    chains = [(bb, h) for bb in range(nb) for h in range(HEADS)]
    for c in range(GDN_CHUNKS_PER_STEP):
        rs = slice(c * C, (c + 1) * C)
        states = {s_: state_ref[s_[0] * HEADS + s_[1]] for s_ in chains}
        ws_qs = {}
        for s_ in chains:
            bb, h = s_
            a, hh = divmod(h, GDN_QUAD)
            w_h = uw[bb, c, a][:, hh * 2 * LANE + LANE:(hh + 1) * 2 * LANE]
            ws_qs[s_] = _dot(jnp.concatenate([w_h.astype(BF16), qd[bb, c, h]], axis=0), states[s_].astype(BF16))
        v_new = {}
        for s_ in chains:
            bb, h = s_
            a, hh = divmod(h, GDN_QUAD)
            v_new[s_] = (uw[bb, c, a][:, hh * 2 * LANE:hh * 2 * LANE + LANE] - ws_qs[s_][0:C]).astype(BF16)
        outs = {}
        for s_ in chains:
            bb, h = s_
            a, hh = divmod(h, GDN_QUAD)
            v_pad = jnp.concatenate([zeros_v] * hh + [v_new[s_]] + [zeros_v] * (GDN_QUAD - 1 - hh), axis=0)
            outs[s_] = ws_qs[s_][C:2 * C] + _dot(qk[bb, c, a], v_pad)
            state_ref[bb * HEADS + h] = states[s_] * eglast[bb, c, h] + _dot_tn(kd[bb, c, h], v_new[s_])
        for s_ in chains:
            bb, h = s_
            ls = slice(h * LANE, (h + 1) * LANE)
            o = outs[s_]
            o = o * lax.rsqrt(jnp.mean(o * o, -1, keepdims=True) + RMS_EPS)
            y_ref[bb, rs, ls] = (o * nw * _silu(z_ref[bb, rs, ls].astype(F32))).astype(BF16)


def _gdn(q, k, v, bg, h3, norm_w):
    B, S, _ = q.shape
    rows = GDN_CHUNKS_PER_STEP * DN_CHUNK
    nb = GDN_BATCH_PER_STEP if B % GDN_BATCH_PER_STEP == 0 else 1
    tile = pl.BlockSpec((nb, rows, WIDTH), lambda b, n: (b, n, 0))
    return pl.pallas_call(
        _gdn_kernel,
        grid=(B // nb, S // rows),
        in_specs=[tile, tile, tile,
                  pl.BlockSpec((nb, rows, LANE), lambda b, n: (b, n, 0)),
                  pl.BlockSpec((nb, rows, WIDTH), lambda b, n: (b, n, COL_DZ // WIDTH)),
                  pl.BlockSpec((1, LANE), lambda b, n: (0, 0))],
        out_specs=tile,
        out_shape=jax.ShapeDtypeStruct((B, S, WIDTH), BF16),
        scratch_shapes=[pltpu.VMEM((nb * HEADS, HEAD_DIM, HEAD_DIM), F32)],
        compiler_params=_cparams(("parallel", "arbitrary")),
        name="gdn",
    )(q, k, v, bg, h3, norm_w)


def _rope(t, cosf, sinf, lo):
    partner = jnp.where(lo, pltpu.roll(t, LANE - ROPE_HALF, 1), pltpu.roll(t, ROPE_HALF, 1))
    return t * cosf + partner * sinf


SEL_TK = 512
WIN_TK = 256
SEL_VROWS = LANE + 16
LOG2E = math.log2(math.e)


def _proj_nsa_kernel(x_ref, w_ref, cos_ref, sin_ref, qt_o, kaug_o, vslt_o, kwn_o, vwnt_o):
    i = pl.program_id(1)
    ts = x_ref.shape[0]
    h = _dot(x_ref[...].astype(BF16), w_ref[...])
    c_ksl, c_vsl, c_kwn, c_vwn = WIDTH, WIDTH + KVW, WIDTH + 2 * KVW, WIDTH + 3 * KVW
    cosf = cos_ref[...]
    sinf = sin_ref[...]
    lane = lax.broadcasted_iota(jnp.int32, cosf.shape, 1)
    lo = lane < ROPE_HALF
    for hd in range(HEADS):
        sl = slice(hd * LANE, (hd + 1) * LANE)
        qt_o[sl, :] = (_rope(h[:, sl], cosf, sinf, lo) * (ATT_SCALE * LOG2E)).T.astype(BF16)
    tpos = i * ts + lax.broadcasted_iota(jnp.int32, cosf.shape, 0)
    onehot = jnp.where((tpos >> int(math.log2(SEL_LEN))) == lane, 1.0, 0.0).astype(BF16)
    for g in range(KV_GROUPS):
        sl = slice(g * LANE, (g + 1) * LANE)
        off = g * LANE
        kaug_o[:, 2 * g * LANE:(2 * g + 1) * LANE] = _rope(h[:, c_ksl + off:c_ksl + off + LANE], cosf, sinf, lo).astype(BF16)
        kaug_o[:, (2 * g + 1) * LANE:(2 * g + 2) * LANE] = onehot
        kwn_o[:, sl] = _rope(h[:, c_kwn + off:c_kwn + off + LANE], cosf, sinf, lo).astype(BF16)
        vs_t = h[:, c_vsl + off:c_vsl + off + LANE].T.astype(BF16)
        for c in range(ts // SEL_TK):
            ones_rows = jnp.where(lax.broadcasted_iota(jnp.int32, (SEL_VROWS - LANE, SEL_TK), 0) == 0, 1.0, 0.0)
            vslt_o[g, c] = jnp.concatenate([vs_t[:, c * SEL_TK:(c + 1) * SEL_TK], ones_rows.astype(BF16)], axis=0)
        vw_t = h[:, c_vwn + off:c_vwn + off + LANE].T.astype(BF16)
        for c in range(ts // WIN_TK):
            ones_rows = jnp.where(lax.broadcasted_iota(jnp.int32, (SEL_VROWS - LANE, WIN_TK), 0) == 0, 1.0, 0.0)
            vwnt_o[g, c] = jnp.concatenate([vw_t[:, c * WIN_TK:(c + 1) * WIN_TK], ones_rows.astype(BF16)], axis=0)


def _proj_nsa(x3, w_b, cos_tab, sin_tab):
    B, S, D = x3.shape
    ts = PROJ_TM
    tab = pl.BlockSpec((ts, LANE), lambda b, i: (i, 0))
    return pl.pallas_call(
        _proj_nsa_kernel,
        grid=(B, S // ts),
        in_specs=[pl.BlockSpec((None, ts, D), lambda b, i: (b, i, 0)),
                  pl.BlockSpec((D, NB), lambda b, i: (0, 0)), tab, tab],
        out_specs=[pl.BlockSpec((None, WIDTH, ts), lambda b, i: (b, 0, i)),
                   pl.BlockSpec((None, ts, 2 * KVW), lambda b, i: (b, i, 0)),
                   pl.BlockSpec((None, KV_GROUPS, ts // SEL_TK, SEL_VROWS, SEL_TK), lambda b, i: (b, 0, i, 0, 0)),
                   pl.BlockSpec((None, ts, KVW), lambda b, i: (b, i, 0)),
                   pl.BlockSpec((None, KV_GROUPS, ts // WIN_TK, SEL_VROWS, WIN_TK), lambda b, i: (b, 0, i, 0, 0))],
        out_shape=[jax.ShapeDtypeStruct((B, WIDTH, S), BF16),
                   jax.ShapeDtypeStruct((B, S, 2 * KVW), BF16),
                   jax.ShapeDtypeStruct((B, KV_GROUPS, S // SEL_TK, SEL_VROWS, SEL_TK), BF16),
                   jax.ShapeDtypeStruct((B, S, KVW), BF16),
                   jax.ShapeDtypeStruct((B, KV_GROUPS, S // WIN_TK, SEL_VROWS, WIN_TK), BF16)],
        compiler_params=_cparams(("parallel", "parallel")),
        name="proj_nsa",
    )(x3, w_b, cos_tab, sin_tab)


def _compress_one(t_ref, pe_ref, w1_ref, w2_ref, nc):
    half = CMP_LEN // 2
    y1 = jnp.zeros((nc, CMP_HIDDEN), F32)
    y2 = jnp.zeros((nc, CMP_HIDDEN), F32)
    for l in range(half):
        x = t_ref[pl.ds(l, nc, stride=CMP_STRIDE), :]
        y1 = y1 + _dot((x + pe_ref[l:l + 1, :]).astype(BF16), w1_ref[l * LANE:(l + 1) * LANE, :])
        y2 = y2 + _dot((x + pe_ref[half + l:half + l + 1, :]).astype(BF16),
                       w1_ref[(half + l) * LANE:(half + l + 1) * LANE, :])
    y = y1 + pltpu.roll(y2, nc - 1, 0)
    return _dot(_silu(y).astype(BF16), w2_ref[...])


def _compress_kernel(kc_ref, vc_ref, pek_ref, pev_ref, w1k_ref, w2k_ref, w1v_ref, w2v_ref,
                     cos_ref, sin_ref, kc_o, vc_o):
    nc = kc_o.shape[0]
    kc = _compress_one(kc_ref, pek_ref, w1k_ref, w2k_ref, nc)
    lo = lax.broadcasted_iota(jnp.int32, kc.shape, 1) < ROPE_HALF
    kc_o[...] = _rope(kc, cos_ref[...], sin_ref[...], lo).astype(BF16)
    vc_o[...] = _compress_one(vc_ref, pev_ref, w1v_ref, w2v_ref, nc).astype(BF16)


def _compress(h3, pek, pev, w1k, w2k, w1v, w2v, cos_c, sin_c):
    B, S, _ = h3.shape
    nc = S // CMP_STRIDE
    full = lambda shape: pl.BlockSpec(shape, lambda b, g: tuple(0 for _ in shape))
    out_spec = pl.BlockSpec((None, None, nc, LANE), lambda b, g: (b, g, 0, 0))
    out_sd = jax.ShapeDtypeStruct((B, KV_GROUPS, nc, LANE), BF16)
    cin = CMP_LEN * HEAD_DIM
    return pl.pallas_call(
        _compress_kernel,
        grid=(B, KV_GROUPS),
        in_specs=[pl.BlockSpec((None, S, LANE), lambda b, g: (b, 0, g)),
                  pl.BlockSpec((None, S, LANE), lambda b, g: (b, 0, KV_GROUPS + g)),
                  full((CMP_LEN, LANE)), full((CMP_LEN, LANE)),
                  full((cin, CMP_HIDDEN)), full((CMP_HIDDEN, LANE)),
                  full((cin, CMP_HIDDEN)), full((CMP_HIDDEN, LANE)),
                  full((nc, LANE)), full((nc, LANE))],
        out_specs=[out_spec, out_spec],
        out_shape=[out_sd, out_sd],
        compiler_params=_cparams(("parallel", "parallel")),
        name="compress",
    )(h3, h3, pek, pev, w1k, w2k, w1v, w2v, cos_c, sin_c)


def _cmp_attn_kernel(q_ref, kc_ref, vc_ref, ovt_ref, gate_ref, o_ref, bias_ref, st_ref, cnt_ref, psel_ref, *, n_sel):
    i = pl.program_id(2)
    tq = q_ref.shape[1]
    cols = HPG * tq
    nc = kc_ref.shape[0]
    t0 = i * tq
    q_g = jnp.concatenate([q_ref[hh * LANE:(hh + 1) * LANE, :] for hh in range(HPG)], axis=1)
    tpos = t0 + (lax.broadcasted_iota(jnp.int32, (1, cols), 1) & (tq - 1))
    gate = _gate_row(jnp.where(pl.program_id(1) == 0, gate_ref[0:HPG, :], gate_ref[HPG:HEADS, :]))

    def attend(rows):
        cmp_end = lax.broadcasted_iota(jnp.int32, (rows, 1), 0) * CMP_STRIDE + (CMP_LEN - 1)
        s = jnp.where(cmp_end <= tpos, _dot(kc_ref[0:rows, :], q_g), -jnp.inf)
        m = jnp.max(s, axis=0, keepdims=True)
        m = jnp.where(m > -jnp.inf, m, 0.0)
        p = jnp.exp2(s - m)
        p = p / jnp.maximum(jnp.sum(p, axis=0, keepdims=True), jnp.finfo(F32).tiny)
        vct = vc_ref[0:rows, :].astype(F32).T.astype(BF16)
        _store_heads(o_ref, _dot(vct, p.astype(BF16)) * gate, 0, tq)
        psum = p[:, 0:tq]
        for hh in range(1, HPG):
            psum = psum + p[:, hh * tq:(hh + 1) * tq]
        ovt = ovt_ref[:, 0:rows]
        p_hi = psum.astype(BF16)
        r1 = psum - p_hi.astype(F32)
        p_mid = r1.astype(BF16)
        p_lo = (r1 - p_mid.astype(F32)).astype(BF16)
        psel_ref[...] = _dot(ovt, p_hi) + (_dot(ovt, p_mid) + _dot(ovt, p_lo))

    n_part = 4
    part = nc // n_part
    n_live = (t0 + tq - CMP_LEN) // CMP_STRIDE + 1
    branch = jnp.clip((n_live - 1) // part, 0, n_part - 1)
    for r in range(n_part):
        pl.when(branch == r)(functools.partial(attend, (r + 1) * part))
    p_sel = psel_ref[...]
    jb = lax.broadcasted_iota(jnp.int32, (LANE, tq), 0)
    cur = (t0 + lax.broadcasted_iota(jnp.int32, (1, tq), 1)) >> int(math.log2(SEL_LEN))
    valid = jb <= cur
    forced = valid & ((jb == 0) | (jb == cur) | (jb == cur - 1))
    st = jnp.where(forced, jnp.inf, jnp.where(valid, p_sel, -jnp.inf))
    n_grp = n_sel // SUBLANE
    st_ref[...] = st
    cnt_ref[...] = jnp.zeros(cnt_ref.shape, F32)
    jrow = lax.broadcasted_iota(jnp.int32, (SUBLANE, tq), 0)
    cur_max = (t0 + tq - 1) >> int(math.log2(SEL_LEN))
    for gb in range(n_grp):
        @pl.when(cur_max >= gb * SUBLANE)
        def _():
            grp = [st_ref[g * SUBLANE:(g + 1) * SUBLANE, :] for g in range(n_grp)]
            cnt = [cnt_ref[g * SUBLANE:(g + 1) * SUBLANE, :] for g in range(n_grp)]
            for rb_ in range(SUBLANE):
                rb = grp[gb][rb_:rb_ + 1, :]
                for g in range(n_grp):
                    if g < gb:
                        inc = jnp.where(rb > grp[g], 1.0, 0.0)
                    elif g > gb:
                        inc = jnp.where(rb >= grp[g], 1.0, 0.0)
                    else:
                        inc = jnp.where(jrow > rb_, jnp.where(rb >= grp[g], 1.0, 0.0), jnp.where(rb > grp[g], 1.0, 0.0))
                    cnt[g] = cnt[g] + inc
            for g in range(n_grp):
                cnt_ref[g * SUBLANE:(g + 1) * SUBLANE, :] = cnt[g]

    keep = (cnt_ref[...] < float(SEL_TOPK)) & (st_ref[0:n_sel, :] > -jnp.inf)
    bias = jnp.where(keep, 0.0, NEG_BIAS)
    if n_sel < LANE:
        bias = jnp.concatenate([bias, jnp.zeros((LANE - n_sel, tq), F32)], axis=0)
    bias_ref[...] = bias.astype(BF16)


def _cmp_attn(q_t, kc, vc, overlap_t, gates_t):
    B, _, S = q_t.shape
    nc = kc.shape[2]
    n_sel = S // SEL_LEN
    tq = CMP_TQ
    gw = HPG * LANE
    return pl.pallas_call(
        functools.partial(_cmp_attn_kernel, n_sel=n_sel),
        grid=(B, KV_GROUPS, S // tq),
        in_specs=[pl.BlockSpec((None, gw, tq), lambda b, g, i: (b, g, i)),
                  pl.BlockSpec((None, None, nc, LANE), lambda b, g, i: (b, g, 0, 0)),
                  pl.BlockSpec((None, None, nc, LANE), lambda b, g, i: (b, g, 0, 0)),
                  pl.BlockSpec((LANE, nc), lambda b, g, i: (0, 0)),
                  pl.BlockSpec((None, HEADS, tq), lambda b, g, i: (b, 0, i))],
        out_specs=[pl.BlockSpec((None, tq, gw), lambda b, g, i: (b, i, g)),
                   pl.BlockSpec((None, None, LANE, tq), lambda b, g, i: (b, g, 0, i))],
        out_shape=[jax.ShapeDtypeStruct((B, S, WIDTH), BF16),
                   jax.ShapeDtypeStruct((B, KV_GROUPS, LANE, S), BF16)],
        scratch_shapes=[pltpu.VMEM((LANE, tq), F32), pltpu.VMEM((n_sel, tq), F32), pltpu.VMEM((LANE, tq), F32)],
        compiler_params=_cparams(("parallel", "parallel", "parallel")),
        name="cmp_attn",
    )(q_t, kc, vc, overlap_t, gates_t)


FLASH_TQ = 256
CMP_TQ = 1024


def _gate_row(logits):
    gt = _sigmoid(logits)
    return jnp.concatenate([gt[hh:hh + 1, :] for hh in range(HPG)], axis=1)


def _store_heads(o_ref, out_t, g, tq):
    for hh in range(HPG):
        h = g * HPG + hh
        o_ref[:, h * LANE:(h + 1) * LANE] = out_t[:, hh * tq:(hh + 1) * tq].T.astype(BF16)


def _flash_kernel(q_ref, b_ref, k_ref, vt_ref, kw_ref, vwt_ref, wb_ref, gate_ref, o_ref, ow_ref,
                  qs_ref, s_ref, m_ref, acc_ref, sw_ref, pw_ref, *, tk):
    i = pl.program_id(1)
    tq = q_ref.shape[1]
    cols = HPG * tq
    t0 = i * tq
    kw = qs_ref.shape[1]
    for g in range(KV_GROUPS):
        for hh in range(HPG):
            h = g * HPG + hh
            qs_ref[g, 0:LANE, hh * tq:(hh + 1) * tq] = q_ref[h * LANE:(h + 1) * LANE, :]
            qs_ref[g, LANE:2 * LANE, hh * tq:(hh + 1) * tq] = b_ref[g]
    m_ref[...] = jnp.full(m_ref.shape, -jnp.inf, F32)
    acc_ref[...] = jnp.zeros(acc_ref.shape, F32)
    qpos = t0 + (lax.broadcasted_iota(jnp.int32, (1, cols), 1) & (tq - 1))

    def scores(j, slot):
        k0 = pl.multiple_of(j * tk, tk)
        for g in range(KV_GROUPS):
            s_ref[slot, g] = _dot(k_ref[pl.ds(k0, tk), g * kw:(g + 1) * kw], qs_ref[g])

    def update(j, slot, masked):
        if masked:
            mask = j * tk + lax.broadcasted_iota(jnp.int32, (tk, 1), 0) <= qpos
        for g in range(KV_GROUPS):
            s = s_ref[slot, g]
            if masked:
                s = jnp.where(mask, s, -jnp.inf)
            m_old = m_ref[g]
            m_new = jnp.maximum(m_old, jnp.max(s, axis=0, keepdims=True))
            alpha = jnp.exp2(m_old - m_new)
            p = jnp.exp2(s - m_new)
            acc_ref[g] = alpha * acc_ref[g] + _dot(vt_ref[g, j], p.astype(BF16))
            m_ref[g] = m_new

    jd = t0 // tk
    last = jnp.maximum(jd - 1, 0)
    win_scores, win_softmax = _window_pass(i, kw_ref, vwt_ref, wb_ref, gate_ref, ow_ref, sw_ref, pw_ref, tq, WIN_TK)
    scores(jd, 0)
    win_scores(0, qs_ref[0, 0:LANE, :])
    update(jd, 0, True)
    win_scores(1, qs_ref[1, 0:LANE, :])
    win_softmax(0)
    scores(0, 1)
    win_softmax(1)

    def body(p, carry):
        scores(jnp.minimum(2 * p + 1, last), 0)
        update(2 * p, 1, False)
        scores(jnp.minimum(2 * p + 2, last), 1)
        update(2 * p + 1, 0, False)
        return carry

    lax.fori_loop(0, jd // 2, body, 0)

    @pl.when(jd % 2 == 1)
    def _():
        update(jd - 1, 1, False)

    for g in range(KV_GROUPS):
        out = acc_ref[g, 0:LANE] / jnp.maximum(acc_ref[g, LANE:LANE + 1], jnp.finfo(F32).tiny)
        _store_heads(o_ref, out * _gate_row(gate_ref[HEADS + g * HPG:HEADS + (g + 1) * HPG, :]), g, tq)


def _window_pass(i, k_ref, vt_ref, wb_ref, gate_ref, o_ref, s_ref, p_ref, tq, tk):
    n_t = WINDOW // tk + 1
    jt = [jnp.maximum(i - d, 0) for d in range(n_t)]
    static_bias = {0: 0, n_t - 1: 1}

    def scores(g, q_g):
        for d in range(n_t):
            k0 = pl.multiple_of(jt[d] * tk, tk)
            s_ref[g, d] = _dot(k_ref[pl.ds(k0, tk), g * LANE:(g + 1) * LANE], q_g)

    def tile(g, d):
        s = s_ref[g, d]
        return s + wb_ref[static_bias[d]] if d in static_bias else s

    def softmax(g):
        m = jnp.max(tile(g, 0), axis=0, keepdims=True)
        for d in range(1, n_t):
            m = jnp.maximum(m, jnp.where(i >= d, jnp.max(tile(g, d), axis=0, keepdims=True), -jnp.inf))
        for d in range(n_t):
            p = jnp.exp2(tile(g, d) - (m if d == 0 else jnp.where(i >= d, m, jnp.inf)))
            p_ref[g, d * tk:(d + 1) * tk, :] = p.astype(BF16)
        v_cat = jnp.concatenate([vt_ref[g, jt[d]] for d in range(n_t)], axis=1)
        acc = _dot(v_cat, p_ref[g])
        out = acc[0:LANE] / jnp.maximum(acc[LANE:LANE + 1], jnp.finfo(F32).tiny)
        _store_heads(o_ref, out * _gate_row(gate_ref[2 * HEADS + g * HPG:2 * HEADS + (g + 1) * HPG, :]), g, tq)

    return scores, softmax


def _window_bias(tq):
    r = np.arange(tq)[:, None]
    c = (np.arange(HPG * tq) % tq)[None, :]
    neg = np.float32(-np.inf)
    return jnp.asarray(np.stack([np.where(r <= c, np.float32(0), neg), np.where(c < r, np.float32(0), neg)]))


def _flash(q_t, kaug, vsl_t, bias_t, kwn, vwn_t, gates_t):
    B, _, S = q_t.shape
    tq = FLASH_TQ
    tk = SEL_TK
    tw = WIN_TK
    assert tk % tq == 0 and tw == tq and WINDOW == 2 * tw
    kw = kaug.shape[2] // KV_GROUPS
    cols = HPG * tq
    out_spec = pl.BlockSpec((None, tq, WIDTH), lambda b, i: (b, i, 0))
    out_sd = jax.ShapeDtypeStruct((B, S, WIDTH), BF16)
    return pl.pallas_call(
        functools.partial(_flash_kernel, tk=tk),
        grid=(B, S // tq),
        in_specs=[pl.BlockSpec((None, WIDTH, tq), lambda b, i: (b, 0, i)),
                  pl.BlockSpec((None, KV_GROUPS, LANE, tq), lambda b, i: (b, 0, 0, i)),
                  pl.BlockSpec((None, S, KV_GROUPS * kw), lambda b, i: (b, 0, 0)),
                  pl.BlockSpec((None, KV_GROUPS, S // tk, SEL_VROWS, tk), lambda b, i: (b, 0, 0, 0, 0)),
                  pl.BlockSpec((None, S, KVW), lambda b, i: (b, 0, 0)),
                  pl.BlockSpec((None, KV_GROUPS, S // tw, SEL_VROWS, tw), lambda b, i: (b, 0, 0, 0, 0)),
                  pl.BlockSpec((2, tw, cols), lambda b, i: (0, 0, 0)),
                  pl.BlockSpec((None, N_GATES, tq), lambda b, i: (b, 0, i))],
        out_specs=[out_spec, out_spec],
        out_shape=[out_sd, out_sd],
        scratch_shapes=[pltpu.VMEM((KV_GROUPS, kw, cols), BF16),
                        pltpu.VMEM((2, KV_GROUPS, tk, cols), F32),
                        pltpu.VMEM((KV_GROUPS, 1, cols), F32),
                        pltpu.VMEM((KV_GROUPS, SEL_VROWS, cols), F32),
                        pltpu.VMEM((KV_GROUPS, WINDOW // tw + 1, tw, cols), F32),
                        pltpu.VMEM((KV_GROUPS, (WINDOW // tw + 1) * tw, cols), BF16)],
        compiler_params=_cparams(("parallel", "parallel")),
        name="flash",
    )(q_t, bias_t, kaug, vsl_t, kwn, vwn_t, _window_bias(tq), gates_t)


def _merge_kernel(ydn_ref, oc_ref, os_ref, ow_ref, mgd_ref, mgn_ref, x_ref,
                  wdn_ref, wnsa_ref, wout_ref, g_ref, b_ref, o_ref):
    y_nsa = (oc_ref[...].astype(F32) + os_ref[...].astype(F32) + ow_ref[...].astype(F32)).astype(BF16)
    y = (_sigmoid(mgd_ref[...].astype(F32)) * _dot(ydn_ref[...], wdn_ref[...])
         + _sigmoid(mgn_ref[...].astype(F32)) * _dot(y_nsa, wnsa_ref[...]))
    z = DEEPNORM_ALPHA * x_ref[...] + _dot(y.astype(BF16), wout_ref[...])
    o_ref[...] = _layer_norm(z, g_ref[...], b_ref[...])


def _merge(y_dn2, oc2, os2, ow2, h, x2, wdn, wnsa, wout, g, b):
    T, D = x2.shape
    tm = PROJ_TM
    row = pl.BlockSpec((tm, WIDTH), lambda i: (i, 0))
    full = lambda shape: pl.BlockSpec(shape, lambda i: (0, 0))
    return pl.pallas_call(
        _merge_kernel,
        grid=(T // tm,),
        in_specs=[row, row, row, row,
                  pl.BlockSpec((tm, WIDTH), lambda i: (i, COL_MG_DN // WIDTH)),
                  pl.BlockSpec((tm, WIDTH), lambda i: (i, COL_MG_NSA // WIDTH)),
                  pl.BlockSpec((tm, D), lambda i: (i, 0)),
                  full((WIDTH, D)), full((WIDTH, D)), full((D, D)), full((1, D)), full((1, D))],
        out_specs=pl.BlockSpec((tm, D), lambda i: (i, 0)),
        out_shape=jax.ShapeDtypeStruct((T, D), F32),
        compiler_params=_cparams(("parallel",)),
        name="merge",
    )(y_dn2, oc2, os2, ow2, h, h, x2, wdn, wnsa, wout, g, b)


FFN_TM = 1024


def _ffn_up_kernel(x_ref, xp_ref, wg_ref, wv_ref, cwg_ref, cwv_ref, bg_ref, bv_ref, o_ref, xs_ref, *, tiles_per_seq):
    i = pl.program_id(1)
    tm = x_ref.shape[0]
    nv = tm // SUBLANE
    n_slab = x_ref.shape[1] // LANE
    for sp in range(SUBLANE):
        for c in range(n_slab):
            xs_ref[c, pl.ds(sp, nv, stride=SUBLANE), :] = x_ref[sp * nv:(sp + 1) * nv, c * LANE:(c + 1) * LANE]
    xb = jnp.concatenate([xs_ref[c] for c in range(n_slab)], axis=1).astype(BF16)
    notfirst = (i % tiles_per_seq != 0).astype(F32)
    xpb = (xp_ref[...] * notfirst).astype(BF16)
    sub = lax.broadcasted_iota(jnp.int32, (SUBLANE, 1), 0)

    def conv(w_ref, cw_ref, b_ref):
        u = _dot(xb, w_ref[...])
        p = _dot(xpb, w_ref[...])

        def wrap(k):
            blk = u[(nv - k) * SUBLANE:(nv - k + 1) * SUBLANE, :]
            return jnp.where(sub == 0, p[SUBLANE - k:SUBLANE - k + 1, :], pltpu.roll(blk, 1, 0))

        w1, w2 = wrap(1), wrap(2)
        u1 = jnp.concatenate([w1, u[0:(nv - 1) * SUBLANE, :]], axis=0)
        u2 = jnp.concatenate([w2, w1, u[0:(nv - 2) * SUBLANE, :]], axis=0)
        return cw_ref[0:1, :] * u2 + cw_ref[1:2, :] * u1 + cw_ref[2:3, :] * u + b_ref[...]

    o_ref[...] = (_silu(conv(wg_ref, cwg_ref, bg_ref)) * conv(wv_ref, cwv_ref, bv_ref)).astype(BF16)


def _ffn_up(x1, w_up, conv_w, conv_b, seq):
    T, D = x1.shape
    tm = FFN_TM
    nj = 2
    tn = D_FF // nj
    return pl.pallas_call(
        functools.partial(_ffn_up_kernel, tiles_per_seq=seq // tm),
        grid=(nj, T // tm),
        in_specs=[pl.BlockSpec((tm, D), lambda j, i: (i, 0)),
                  pl.BlockSpec((SUBLANE, D), lambda j, i: (jnp.maximum(i * (tm // SUBLANE) - 1, 0), 0)),
                  pl.BlockSpec((D, tn), lambda j, i: (0, j)),
                  pl.BlockSpec((D, tn), lambda j, i: (0, j + nj)),
                  pl.BlockSpec((FFN_CONV, tn), lambda j, i: (0, j)),
                  pl.BlockSpec((FFN_CONV, tn), lambda j, i: (0, j + nj)),
                  pl.BlockSpec((1, tn), lambda j, i: (0, j)),
                  pl.BlockSpec((1, tn), lambda j, i: (0, j + nj))],
        out_specs=pl.BlockSpec((tm, tn), lambda j, i: (i, j)),
        out_shape=jax.ShapeDtypeStruct((T, D_FF), BF16),
        scratch_shapes=[pltpu.VMEM((D // LANE, tm, LANE), F32)],
        compiler_params=_cparams(("parallel", "parallel")),
        name="ffn_up",
    )(x1, x1, w_up, w_up, conv_w, conv_w, conv_b, conv_b)


def _ffn_down_kernel(a_ref, w_ref, x_ref, g_ref, b_ref, o_ref, fs_ref):
    tm, d = x_ref.shape
    nv = tm // SUBLANE
    n_slab = d // LANE
    f = _dot(a_ref[...], w_ref[...])
    for c in range(n_slab):
        fs_ref[c] = f[:, c * LANE:(c + 1) * LANE]
    for sp in range(SUBLANE):
        rows = slice(sp * nv, (sp + 1) * nv)
        f_nat = jnp.concatenate([fs_ref[c, pl.ds(sp, nv, stride=SUBLANE), :] for c in range(n_slab)], axis=1)
        z = DEEPNORM_ALPHA * x_ref[rows, :] + f_nat
        o_ref[rows, :] = _layer_norm(z, g_ref[...], b_ref[...])


def _ffn_down(a, w_down, x1, g, b):
    T, D = x1.shape
    tm = FFN_TM
    full = lambda shape: pl.BlockSpec(shape, lambda i: (0, 0))
    return pl.pallas_call(
        _ffn_down_kernel,
        grid=(T // tm,),
        in_specs=[pl.BlockSpec((tm, D_FF), lambda i: (i, 0)), full((D_FF, D)),
                  pl.BlockSpec((tm, D), lambda i: (i, 0)), full((1, D)), full((1, D))],
        out_specs=pl.BlockSpec((tm, D), lambda i: (i, 0)),
        out_shape=jax.ShapeDtypeStruct((T, D), F32),
        scratch_shapes=[pltpu.VMEM((D // LANE, tm, LANE), F32)],
        compiler_params=_cparams(("parallel",)),
        name="ffn_down",
    )(a, w_down, x1, g, b)


def _split_w_in_kernel(wt_ref, a_ref, b_ref, c_ref):
    o_dz = 3 * WIDTH
    o_db = 4 * WIDTH
    o_nq = o_db + 2 * HEADS
    o_kc = o_nq + WIDTH
    o_ksl = o_kc + 2 * KVW
    o_gate = o_ksl + 4 * KVW
    o_mg = o_gate + 3 * HEADS
    slab = wt_ref.shape[1]

    def put(dst, o0, c0, c1):
        dst[:, o0:o0 + (c1 - c0)] = wt_ref[c0:c1, :].T.astype(BF16)

    put(a_ref, 0, 0, o_dz)
    small = jnp.concatenate([wt_ref[o_db:o_nq, :], wt_ref[o_gate:o_mg, :],
                             jnp.zeros((LANE - 5 * HEADS, slab), F32)], axis=0)
    a_ref[:, o_dz:NA] = small.T.astype(BF16)
    put(b_ref, 0, o_nq, o_kc)
    put(b_ref, WIDTH, o_ksl, o_gate)
    put(c_ref, 0, o_dz, o_db)
    put(c_ref, WIDTH, o_mg, o_mg + 2 * WIDTH)
    put(c_ref, 3 * WIDTH, o_kc, o_ksl)


def _split_w_in(w, layer):
    _, d, n_in = w.shape
    slab = 2 * LANE
    return pl.pallas_call(
        _split_w_in_kernel,
        grid=(d // slab,),
        in_specs=[pl.BlockSpec((None, n_in, slab), lambda i: (layer, 0, i))],
        out_specs=[pl.BlockSpec((slab, NA), lambda i: (i, 0)), pl.BlockSpec((slab, NB), lambda i: (i, 0)),
                   pl.BlockSpec((slab, NC_COLS), lambda i: (i, 0))],
        out_shape=[jax.ShapeDtypeStruct((d, NA), BF16), jax.ShapeDtypeStruct((d, NB), BF16),
                   jax.ShapeDtypeStruct((d, NC_COLS), BF16)],
        compiler_params=_cparams(("parallel",)),
        name="split_w_in",
    )(jnp.swapaxes(w, 1, 2))


def _rope_tables(pos):
    pos = np.asarray(pos)
    inv_freq = np.float32(ROPE_THETA) ** (-(np.arange(ROPE_HALF, dtype=np.float32) / np.float32(ROPE_HALF)))
    ang = pos.astype(np.float32)[:, None] * inv_freq.astype(np.float32)
    cos = np.cos(ang.astype(np.float64)).astype(np.float32)
    sin = np.sin(ang.astype(np.float64)).astype(np.float32)
    p = pos.shape[0]
    cosf = np.concatenate([cos, cos, np.ones((p, LANE - ROPE_DIM), np.float32)], axis=1)
    sinf = np.concatenate([-sin, sin, np.zeros((p, LANE - ROPE_DIM), np.float32)], axis=1)
    return jnp.asarray(cosf), jnp.asarray(sinf)


def _overlap_table(nc, n_sel):
    n = np.arange(nc)[None, :]
    j = np.arange(LANE)[:, None]
    ov = (n * CMP_STRIDE <= j * SEL_LEN + SEL_LEN - 1) & (n * CMP_STRIDE + CMP_LEN - 1 >= j * SEL_LEN)
    ov = ov & (j < n_sel) & (n < nc - 1)
    return jnp.asarray(ov.astype(np.float32)).astype(BF16)


def _lane_row(vals, offset):
    return jnp.zeros((1, LANE), F32).at[0, offset:offset + vals.shape[0]].set(vals.astype(F32))


def kernel(x, w_in, dn_conv_w, dn_a_log, dn_dt_bias, dn_norm_w, cmp_pos_k, cmp_pos_v, cmp_k_w1, cmp_k_w2,
           cmp_v_w1, cmp_v_w2, w_branch_dn, w_branch_nsa, w_out, ln1_g, ln1_b, ffn_w_up, ffn_conv_w,
           ffn_conv_b, ffn_w_down, ln2_g, ln2_b):
    B, S, D = x.shape
    T = B * S
    nc = S // CMP_STRIDE
    n_sel = S // SEL_LEN
    assert D == WIDTH and S % FFN_TM == 0 and S % PROJ_TM == 0 and n_sel <= LANE and n_sel % SUBLANE == 0
    cos_t, sin_t = _rope_tables(np.arange(S))
    cos_c, sin_c = _rope_tables(np.arange(nc) * CMP_STRIDE + (CMP_LEN - 1))
    overlap = _overlap_table(nc, n_sel)
    x2 = x.reshape(T, D)
    for layer in range(w_in.shape[0]):
        w_a, w_b, w_c = _split_w_in(w_in, layer)
        x3 = x2.reshape(B, S, D)
        h3, kv_cmp = _proj_rest(x3, w_c)

        q, k, v, bg, gates_t = _proj_gdn(x3, w_a, dn_conv_w[layer], _lane_row(dn_a_log[layer], SMALL_DECAY),
                                       _lane_row(dn_dt_bias[layer], SMALL_DECAY))
        y_dn = _gdn(q, k, v, bg, h3, dn_norm_w[layer].reshape(1, LANE))

        q_t, kaug, vsl_t, kwn_r, vwn_t = _proj_nsa(x3, w_b, cos_t, sin_t)
        kc, vc = _compress(kv_cmp, cmp_pos_k[layer], cmp_pos_v[layer],
                           cmp_k_w1[layer].astype(BF16), cmp_k_w2[layer].astype(BF16),
                           cmp_v_w1[layer].astype(BF16), cmp_v_w2[layer].astype(BF16), cos_c, sin_c)
        o_cmp, bias = _cmp_attn(q_t, kc, vc, overlap, gates_t)
        o_sel, o_win = _flash(q_t, kaug, vsl_t, bias, kwn_r, vwn_t, gates_t)

        x1 = _merge(y_dn.reshape(T, WIDTH), o_cmp.reshape(T, WIDTH), o_sel.reshape(T, WIDTH),
                    o_win.reshape(T, WIDTH), h3.reshape(T, COL_KC), x2,
                    w_branch_dn[layer].astype(BF16), w_branch_nsa[layer].astype(BF16),
                    w_out[layer].astype(BF16), ln1_g[layer].reshape(1, D), ln1_b[layer].reshape(1, D))
        a = _ffn_up(x1, ffn_w_up[layer].astype(BF16), ffn_conv_w[layer],
                    ffn_conv_b[layer].reshape(1, 2 * D_FF), S)
        x2 = _ffn_down(a, ffn_w_down[layer].astype(BF16), x1, ln2_g[layer].reshape(1, D), ln2_b[layer].reshape(1, D))
    return x2.reshape(B, S, D)
```
